```python
import math
import jax, jax.numpy as jnp
from jax import lax
import numpy as np

D_MODEL = 1024
BATCH = 2
SEQ = 8192
DEPTH = 1

HEAD_DIM = 64
D_MIX = D_MODEL
NSA_HEADS = 8
NSA_KV_HEADS = 2
NSA_Q_PER_KV = NSA_HEADS // NSA_KV_HEADS
DIL_HEADS = 8
W_NSA = NSA_HEADS * HEAD_DIM
W_KV = NSA_KV_HEADS * HEAD_DIM
W_DIL = DIL_HEADS * HEAD_DIM
CMP_LEN = 32
CMP_STRIDE = 16
CMP_HIDDEN = 256
SEL_BLOCK = 64
SEL_TOP_N = 16
SWA_WINDOW = 512
DIL_PATTERNS = ((128, 1), (512, 4), (2048, 16))
BLOCK = 128
ROPE_THETA = 500000.0
ROPE_DIMS = HEAD_DIM // 4
RMS_EPS = 1e-6
NEG_INF = -1e30
FORCE_SCORE = 1e4
IN_SPLITS = (W_NSA,
             W_KV, W_KV,
             W_KV, W_KV,
             W_KV, W_KV,
             3 * NSA_HEADS,
             W_NSA,
             W_DIL, W_DIL, W_DIL,
             W_DIL)
D_IN = sum(IN_SPLITS)

kernel_name = "hymba_nsa_dilated_hybrid"


def rmsnorm(x, g):
    xf = x.astype(jnp.float32)
    y = xf * lax.rsqrt(jnp.mean(xf * xf, axis=-1, keepdims=True) + RMS_EPS)
    return (y * g.astype(jnp.float32)).astype(x.dtype)


def rope_tables(positions):
    inv = 1.0 / (ROPE_THETA ** (jnp.arange(0, ROPE_DIMS, 2, dtype=jnp.float32) / ROPE_DIMS))
    ang = positions.astype(jnp.float32)[..., None] * inv
    return jnp.cos(ang), jnp.sin(ang)


def apply_rope(x, cos, sin):
    shape = (cos.shape[0],) + (1,) * (x.ndim - 3) + cos.shape[1:]
    c = cos.reshape(shape)
    s = sin.reshape(shape)
    half = ROPE_DIMS // 2
    xf = x.astype(jnp.float32)
    x1 = xf[..., :half]
    x2 = xf[..., half:ROPE_DIMS]
    out = jnp.concatenate([x1 * c - x2 * s, x2 * c + x1 * s, xf[..., ROPE_DIMS:]], axis=-1)
    return out.astype(x.dtype)


def banded_attention(q, k, v, window):
    B, G, R, L, D = q.shape
    nblk = -(-L // BLOCK)
    Lp = nblk * BLOCK
    nprev = -(-window // BLOCK)
    qp = jnp.pad(q, ((0, 0), (0, 0), (0, 0), (0, Lp - L), (0, 0)))
    kp = jnp.pad(k, ((0, 0), (0, 0), (nprev * BLOCK, Lp - L), (0, 0)))
    vp = jnp.pad(v, ((0, 0), (0, 0), (nprev * BLOCK, Lp - L), (0, 0)))
    qb = qp.reshape(B, G, R, nblk, BLOCK, D)
    kb = kp.reshape(B, G, nprev + nblk, BLOCK, D)
    vb = vp.reshape(B, G, nprev + nblk, BLOCK, D)
    kw = jnp.concatenate([kb[:, :, j:j + nblk] for j in range(nprev + 1)], axis=3)
    vw = jnp.concatenate([vb[:, :, j:j + nblk] for j in range(nprev + 1)], axis=3)
    i = jnp.arange(BLOCK)[:, None]
    c = jnp.arange((nprev + 1) * BLOCK)[None, :]
    dist = nprev * BLOCK + i - c
    kabs = (jnp.arange(nblk)[:, None, None] - nprev) * BLOCK + c[None]
    mask = (dist >= 0)[None] & (dist <= window)[None] & (kabs >= 0)
    s = jnp.einsum('bgrnqd,bgnkd->bgrnqk', qb, kw).astype(jnp.float32) * (HEAD_DIM ** -0.5)
    s = jnp.where(mask, s, NEG_INF)
    m = jnp.max(s, axis=-1, keepdims=True)
    e = jnp.exp(s - m)
    den = jnp.sum(e, axis=-1, keepdims=True)
    p = (e / den).astype(v.dtype)
    out = jnp.einsum('bgrnqk,bgnkd->bgrnqd', p, vw).reshape(B, G, R, Lp, D)[:, :, :, :L]
    lse = (m[..., 0] + jnp.log(den[..., 0])).reshape(B, G, R, Lp)[..., :L]
    return out, lse


def compress_blocks(k, pos_emb, w1, w2):
    B, G, S, D = k.shape
    n_cmp = (S - CMP_LEN) // CMP_STRIDE + 1
    idx = jnp.arange(n_cmp)[:, None] * CMP_STRIDE + jnp.arange(CMP_LEN)[None, :]
    blocks = k[:, :, idx] + pos_emb
    flat = blocks.reshape(B, G, n_cmp, CMP_LEN * D)
    return jax.nn.silu(flat @ w1) @ w2


def nsa_compressed_and_selected(q, k_cmp, v_cmp, k_slc, v_slc):
    B, G, R, S, D = q.shape
    n_cmp = k_cmp.shape[2]
    n_sel = S // SEL_BLOCK
    top_n = min(SEL_TOP_N, n_sel)
    scale = HEAD_DIM ** -0.5
    cmp_end = jnp.arange(n_cmp) * CMP_STRIDE + CMP_LEN - 1
    ratio = SEL_BLOCK // CMP_STRIDE
    offs = np.array([m - n for m in range(ratio) for n in range(CMP_LEN // CMP_STRIDE)], dtype=np.int32)
    cidx = jnp.arange(n_sel)[:, None] * ratio + offs[None, :]
    cvalid = (cidx >= 0) & (cidx < n_cmp)
    cidx = jnp.clip(cidx, 0, n_cmp - 1)
    kb = k_slc.reshape(B, G, n_sel, SEL_BLOCK, D)
    vb = v_slc.reshape(B, G, n_sel, SEL_BLOCK, D)
    bi = jnp.arange(B)[:, None, None, None]
    gi = jnp.arange(G)[None, :, None, None]
    blk = jnp.arange(n_sel)

    def one_block(b):
        t = b * BLOCK + jnp.arange(BLOCK)
        qb = lax.dynamic_slice_in_dim(q, b * BLOCK, BLOCK, axis=3)
        s = jnp.einsum('bgrqd,bgcd->bgrqc', qb, k_cmp).astype(jnp.float32) * scale
        valid = cmp_end[None, :] <= t[:, None]
        s = jnp.where(valid, s, NEG_INF)
        m = jnp.max(s, axis=-1, keepdims=True)
        p = jnp.where(valid, jnp.exp(s - m), 0.0)
        p = p / jnp.maximum(jnp.sum(p, axis=-1, keepdims=True), 1e-30)
        o_cmp = jnp.einsum('bgrqc,bgcd->bgrqd', p.astype(v_cmp.dtype), v_cmp)
        imp = jnp.sum(p, axis=2)
        imp_sel = jnp.sum(jnp.where(cvalid, imp[..., cidx], 0.0), axis=-1)
        cur = t // SEL_BLOCK
        blk_valid = blk[None, :] <= cur[:, None]
        forced = (blk[None, :] == 0) | (blk[None, :] == cur[:, None]) | (blk[None, :] == cur[:, None] - 1)
        score = jnp.where(forced, FORCE_SCORE, imp_sel)
        score = jnp.where(blk_valid, score, -1.0)
        _, idx = lax.top_k(score, top_n)
        sel_ok = idx <= cur[:, None]
        ksel = kb[bi, gi, idx]
        vsel = vb[bi, gi, idx]
        s2 = jnp.einsum('bgrqd,bgqnld->bgrqnl', qb, ksel).astype(jnp.float32) * scale
        kpos = idx[..., None] * SEL_BLOCK + jnp.arange(SEL_BLOCK)
        ok = sel_ok[..., None] & (kpos <= t[:, None, None])
        s2 = jnp.where(ok[:, :, None], s2, NEG_INF)
        p2 = jax.nn.softmax(s2.reshape(B, G, R, BLOCK, top_n * SEL_BLOCK), axis=-1)
        p2 = p2.reshape(B, G, R, BLOCK, top_n, SEL_BLOCK).astype(v_slc.dtype)
        o_slc = jnp.einsum('bgrqnl,bgqnld->bgrqd', p2, vsel)
        return o_cmp, o_slc

    o_cmp, o_slc = lax.map(one_block, jnp.arange(S // BLOCK))
    o_cmp = jnp.transpose(o_cmp, (1, 2, 3, 0, 4, 5)).reshape(B, G, R, S, D)
    o_slc = jnp.transpose(o_slc, (1, 2, 3, 0, 4, 5)).reshape(B, G, R, S, D)
    return o_cmp, o_slc


def dilated_mixture(q, k, v):
    B, H, S, D = q.shape
    outs, lses = [], []
    for window, dil in DIL_PATTERNS:
        L = S // dil
        qd = q.reshape(B, H, L, dil, D).transpose(0, 1, 3, 2, 4).reshape(B, H * dil, 1, L, D)
        kd = k.reshape(B, H, L, dil, D).transpose(0, 1, 3, 2, 4).reshape(B, H * dil, L, D)
        vd = v.reshape(B, H, L, dil, D).transpose(0, 1, 3, 2, 4).reshape(B, H * dil, L, D)
        o, lse = banded_attention(qd, kd, vd, window // dil)
        outs.append(o.reshape(B, H, dil, L, D).transpose(0, 1, 3, 2, 4).reshape(B, H, S, D))
        lses.append(lse.reshape(B, H, dil, L).transpose(0, 1, 3, 2).reshape(B, H, S))
    w = jax.nn.softmax(jnp.stack(lses, axis=0), axis=0)
    o = jnp.sum(w[..., None] * jnp.stack(outs, axis=0).astype(jnp.float32), axis=0)
    return o.astype(q.dtype)


def setup_inputs(seed: int = 0) -> dict:
    key = jax.random.key(seed)
    ks = jax.random.split(key, 14)
    f32 = jnp.float32
    x = jax.random.normal(ks[0], (BATCH, SEQ, D_MODEL), f32)
    offs = jax.random.randint(ks[1], (BATCH, 1), 0, 4096, dtype=jnp.int32)
    positions = offs + jnp.arange(SEQ, dtype=jnp.int32)[None, :]
    pre_norm_g = 1.0 + 0.02 * jax.random.normal(ks[2], (DEPTH, D_MODEL), f32)
    w_in = jax.random.normal(ks[3], (DEPTH, D_MODEL, D_IN), f32) * D_MODEL ** -0.5
    cmp_k_pos = 0.1 * jax.random.normal(ks[4], (DEPTH, CMP_LEN, HEAD_DIM), f32)
    cmp_k_w1 = jax.random.normal(ks[5], (DEPTH, CMP_LEN * HEAD_DIM, CMP_HIDDEN), f32) * (CMP_LEN * HEAD_DIM) ** -0.5
    cmp_k_w2 = jax.random.normal(ks[6], (DEPTH, CMP_HIDDEN, HEAD_DIM), f32) * CMP_HIDDEN ** -0.5
    cmp_v_pos = 0.1 * jax.random.normal(ks[7], (DEPTH, CMP_LEN, HEAD_DIM), f32)
    cmp_v_w1 = jax.random.normal(ks[8], (DEPTH, CMP_LEN * HEAD_DIM, CMP_HIDDEN), f32) * (CMP_LEN * HEAD_DIM) ** -0.5
    cmp_v_w2 = jax.random.normal(ks[9], (DEPTH, CMP_HIDDEN, HEAD_DIM), f32) * CMP_HIDDEN ** -0.5
    w_out = jax.random.normal(ks[10], (DEPTH, D_MIX, D_MODEL), f32) * D_MIX ** -0.5
    post_norm_g = 1.0 + 0.02 * jax.random.normal(ks[11], (DEPTH, D_MODEL), f32)
    return {"x": x, "positions": positions, "pre_norm_g": pre_norm_g, "w_in": w_in,
            "cmp_k_pos": cmp_k_pos, "cmp_k_w1": cmp_k_w1, "cmp_k_w2": cmp_k_w2,
            "cmp_v_pos": cmp_v_pos, "cmp_v_w1": cmp_v_w1, "cmp_v_w2": cmp_v_w2,
            "w_out": w_out, "post_norm_g": post_norm_g}


def reference(x, positions, pre_norm_g, w_in, cmp_k_pos, cmp_k_w1, cmp_k_w2,
              cmp_v_pos, cmp_v_w1, cmp_v_w2, w_out, post_norm_g):
    B, S, _ = x.shape
    G, R, D = NSA_KV_HEADS, NSA_Q_PER_KV, HEAD_DIM
    cos, sin = rope_tables(positions)
    for layer in range(DEPTH):
        h = rmsnorm(x, pre_norm_g[layer])
        proj = h @ w_in[layer]
        parts, start = [], 0
        for width in IN_SPLITS:
            parts.append(proj[..., start:start + width])
            start += width
        (q_a, k_cmp, v_cmp, k_slc, v_slc, k_win, v_win, gate_a, z_a,
         q_b, k_b, v_b, z_b) = parts

        qa = apply_rope(q_a.reshape(B, S, G, R, D).transpose(0, 2, 3, 1, 4), cos, sin)
        kv = lambda t: t.reshape(B, S, G, D).transpose(0, 2, 1, 3)
        kc = compress_blocks(apply_rope(kv(k_cmp), cos, sin), cmp_k_pos[layer], cmp_k_w1[layer], cmp_k_w2[layer])
        vc = compress_blocks(kv(v_cmp), cmp_v_pos[layer], cmp_v_w1[layer], cmp_v_w2[layer])
        o_cmp, o_slc = nsa_compressed_and_selected(qa, kc, vc, apply_rope(kv(k_slc), cos, sin), kv(v_slc))
        o_win, _ = banded_attention(qa, apply_rope(kv(k_win), cos, sin), kv(v_win), SWA_WINDOW - 1)
        g = jax.nn.sigmoid(gate_a.astype(jnp.float32)).reshape(B, S, 3, G, R).transpose(2, 0, 3, 4, 1)[..., None]
        o_a = (g[0] * o_cmp + g[1] * o_slc + g[2] * o_win).astype(x.dtype)
        o_a = o_a.transpose(0, 3, 1, 2, 4).reshape(B, S, W_NSA)

        heads = lambda t: t.reshape(B, S, DIL_HEADS, D).transpose(0, 2, 1, 3)
        o_b = dilated_mixture(apply_rope(heads(q_b), cos, sin), apply_rope(heads(k_b), cos, sin), heads(v_b))
        o_b = o_b.transpose(0, 2, 1, 3).reshape(B, S, W_DIL)

        mixed = jnp.concatenate([o_a * jax.nn.silu(z_a), o_b * jax.nn.silu(z_b)], axis=-1)
        x = x + rmsnorm(mixed @ w_out[layer], post_norm_g[layer])
    return x
```

```python
import functools

import jax
import jax.numpy as jnp
from jax import lax
from jax.experimental import pallas as pl
from jax.experimental.pallas import tpu as pltpu

HEAD_DIM = 64
NSA_HEADS = 8
NSA_KV_HEADS = 2
NSA_Q_PER_KV = NSA_HEADS // NSA_KV_HEADS
DIL_HEADS = 8
W_NSA = NSA_HEADS * HEAD_DIM
W_KV = NSA_KV_HEADS * HEAD_DIM
W_DIL = DIL_HEADS * HEAD_DIM
CMP_LEN = 32
CMP_STRIDE = 16
CMP_HIDDEN = 256
SEL_BLOCK = 64
SEL_TOP_N = 16
SWA_WINDOW = 512
DIL_PATTERNS = ((128, 1), (512, 4), (2048, 16))
BLOCK = 128
ROPE_THETA = 500000.0
ROPE_DIMS = HEAD_DIM // 4
RMS_EPS = 1e-6
NEG_INF = -1e30
FORCE_SCORE = 1e4

LANES = 128
VMEM_LIMIT = 56 * 1024 * 1024
MAX_SEL_BLOCKS = LANES
CMP_PER_SEL = SEL_BLOCK // CMP_STRIDE
KEY_TILE = 512
PROJ_ROWS = 512

_NT = (((1,), (1,)), ((), ()))


def _dot(a, b):
    return jnp.dot(a, b, preferred_element_type=jnp.float32)


def _dot_nt(a, b):
    return lax.dot_general(a, b, _NT, preferred_element_type=jnp.float32)


def _sigmoid(x):
    return 1.0 / (1.0 + jnp.exp(-x))


def _rope(x, c, a, b):
    width = x.shape[1]
    reps = width // LANES
    ct = jnp.tile(c, (1, reps))
    at = jnp.tile(a, (1, reps))
    bt = jnp.tile(b, (1, reps))
    half = ROPE_DIMS // 2
    return x * ct + pltpu.roll(x, width - half, 1) * at + pltpu.roll(x, half, 1) * bt


def _in_proj_kernel(x_ref, g_ref, w_ref, c_ref, a_ref, b_ref,
                    qa_ref, kva_ref, gate_ref, sza_ref, qb_ref, kb_ref, vb_ref, szb_ref):
    x = x_ref[...]
    ms = jnp.mean(x * x, axis=-1, keepdims=True)
    h = (x * lax.rsqrt(ms + RMS_EPS) * g_ref[...]).astype(jnp.bfloat16)
    c = c_ref[...]
    a = a_ref[...]
    b = b_ref[...]
    off = 0

    def proj(width):
        nonlocal off
        r = _dot(h, w_ref[:, off:off + width])
        off += width
        return r

    qa_ref[...] = _rope(proj(W_NSA), c, a, b).astype(qa_ref.dtype)
    kva = proj(6 * W_KV)
    parts = []
    for i in range(6):
        p = kva[:, i * W_KV:(i + 1) * W_KV]
        parts.append(_rope(p, c, a, b) if i % 2 == 0 else p)
    kva_ref[...] = jnp.concatenate(parts, axis=1).astype(kva_ref.dtype)
    gate_ref[...] = _sigmoid(proj(3 * W_NSA)).astype(gate_ref.dtype)
    za = proj(W_NSA)
    sza_ref[...] = (za * _sigmoid(za)).astype(sza_ref.dtype)
    qb_ref[...] = _rope(proj(W_DIL), c, a, b).astype(qb_ref.dtype)
    kb_ref[...] = _rope(proj(W_DIL), c, a, b).astype(kb_ref.dtype)
    vb_ref[...] = proj(W_DIL).astype(vb_ref.dtype)
    zb = proj(W_DIL)
    szb_ref[...] = (zb * _sigmoid(zb)).astype(szb_ref.dtype)


def _in_proj(x2, g, w, c, a, b):
    rows, d_model = x2.shape
    n_total = w.shape[1]
    tm = PROJ_ROWS
    widths = (W_NSA, 6 * W_KV, 3 * W_NSA, W_NSA, W_DIL, W_DIL, W_DIL, W_DIL)
    row_spec = lambda wd: pl.BlockSpec((tm, wd), lambda i: (i, 0))
    return pl.pallas_call(
        _in_proj_kernel,
        grid=(rows // tm,),
        in_specs=[row_spec(d_model),
                  pl.BlockSpec((1, d_model), lambda i: (0, 0)),
                  pl.BlockSpec((d_model, n_total), lambda i: (0, 0)),
                  row_spec(LANES), row_spec(LANES), row_spec(LANES)],
        out_specs=[row_spec(wd) for wd in widths],
        out_shape=[jax.ShapeDtypeStruct((rows, wd), jnp.bfloat16) for wd in widths],
        compiler_params=pltpu.CompilerParams(dimension_semantics=("arbitrary",),
                                             vmem_limit_bytes=VMEM_LIMIT),
        name="in_proj",
    )(x2, g, w, c, a, b)


def _compress_kernel(ch_ref, pos_ref, w1t_ref, w1b_ref, w2_ref, o_ref):
    ch = ch_ref[0, 0]
    n = ch.shape[0]
    half = pos_ref.shape[2] // 2
    pos = pos_ref[0]
    top = _dot(ch, w1t_ref[0])
    bot = _dot(ch, w1b_ref[0])
    bias = _dot(pos[:, :half], w1t_ref[0]) + _dot(pos[:, half:], w1b_ref[0])
    hid = top + pltpu.roll(bot, n - 1, 0) + bias[0:1, :]
    act = (hid * _sigmoid(hid)).astype(jnp.bfloat16)
    o_ref[0, 0] = _dot(act, w2_ref[0])


def _compress(chunks, pos8, w1t, w1b, w2):
    _, bg, n, width = chunks.shape
    return pl.pallas_call(
        _compress_kernel,
        grid=(2, bg),
        in_specs=[pl.BlockSpec((1, 1, n, width), lambda s, i: (s, i, 0, 0)),
                  pl.BlockSpec((1, 16, 2 * width), lambda s, i: (s, 0, 0)),
                  pl.BlockSpec((1, width, CMP_HIDDEN), lambda s, i: (s, 0, 0)),
                  pl.BlockSpec((1, width, CMP_HIDDEN), lambda s, i: (s, 0, 0)),
                  pl.BlockSpec((1, CMP_HIDDEN, HEAD_DIM), lambda s, i: (s, 0, 0))],
        out_specs=pl.BlockSpec((1, 1, n, HEAD_DIM), lambda s, i: (s, i, 0, 0)),
        out_shape=jax.ShapeDtypeStruct((2, bg, n, HEAD_DIM), jnp.float32),
        compiler_params=pltpu.CompilerParams(dimension_semantics=("arbitrary", "arbitrary")),
        name="compress",
    )(chunks, pos8, w1t, w1b, w2)


def _cmp_topk_kernel(q_ref, kc_ref, vc_ref, o_ref, mb_ref):
    qb = pl.program_id(1)
    q = q_ref[0, 0]
    rows = q.shape[0]
    ncols = kc_ref.shape[1]
    s = _dot_nt(q, kc_ref[0])
    row = lax.broadcasted_iota(jnp.int32, (rows, ncols), 0)
    col = lax.broadcasted_iota(jnp.int32, (rows, ncols), 1)
    t = qb * BLOCK + (row & (BLOCK - 1))
    cmp_end = (col & (LANES - 1)) * SEL_BLOCK + (col >> 7) * CMP_STRIDE + (CMP_LEN - 1)
    valid = cmp_end <= t
    s = jnp.where(valid, s, NEG_INF)
    m = jnp.max(s, axis=-1, keepdims=True)
    p = jnp.where(valid, jnp.exp(s - m), 0.0)
    p = p / jnp.maximum(jnp.sum(p, axis=-1, keepdims=True), 1e-30)
    o_ref[0, 0] = _dot(p.astype(jnp.bfloat16), vc_ref[0]).astype(o_ref.dtype)

    imp = p[0:BLOCK]
    for r in range(1, NSA_Q_PER_KV):
        imp = imp + p[r * BLOCK:(r + 1) * BLOCK]
    g = [imp[:, i * LANES:(i + 1) * LANES] for i in range(CMP_PER_SEL)]
    lane = lax.broadcasted_iota(jnp.int32, (BLOCK, LANES), 1)
    prev_last = jnp.where(lane == 0, 0.0, pltpu.roll(g[3], 1, 1))
    imp_sel = prev_last + 2.0 * (g[0] + g[1] + g[2]) + g[3]

    blk = lax.broadcasted_iota(jnp.int32, (LANES, BLOCK), 0)
    tq = qb * BLOCK + lax.broadcasted_iota(jnp.int32, (LANES, BLOCK), 1)
    cur = tq >> 6
    forced = (blk == 0) | (blk == cur) | (blk == cur - 1)
    score = jnp.where(forced, FORCE_SCORE, imp_sel.T)
    score = jnp.where(blk <= cur, score, -1.0)
    blk_f = blk.astype(jnp.float32)

    def pick_one(_, carry):
        sc, sel = carry
        best = jnp.max(sc, axis=0, keepdims=True)
        first = jnp.min(jnp.where(sc == best, blk_f, float(LANES)), axis=0, keepdims=True)
        pick = blk_f == first
        return jnp.where(pick, -2.0, sc), jnp.where(pick, 1.0, sel)

    _, sel = lax.fori_loop(0, SEL_TOP_N, pick_one, (score, jnp.zeros_like(score)))
    bias = jnp.where((sel > 0.0) & (blk <= cur), 0.0, NEG_INF)
    mb_ref[0, 0] = bias.T.astype(mb_ref.dtype)


def _cmp_topk(qs, kcp, vcp):
    bg, nqb, rows, _ = qs.shape
    ncols = kcp.shape[1]
    return pl.pallas_call(
        _cmp_topk_kernel,
        grid=(bg, nqb),
        in_specs=[pl.BlockSpec((1, 1, rows, HEAD_DIM), lambda i, j: (i, j, 0, 0)),
                  pl.BlockSpec((1, ncols, HEAD_DIM), lambda i, j: (i, 0, 0)),
                  pl.BlockSpec((1, ncols, HEAD_DIM), lambda i, j: (i, 0, 0))],
        out_specs=[pl.BlockSpec((1, 1, rows, HEAD_DIM), lambda i, j: (i, j, 0, 0)),
                   pl.BlockSpec((1, 1, BLOCK, LANES), lambda i, j: (i, j, 0, 0))],
        out_shape=[jax.ShapeDtypeStruct((bg, nqb, rows, HEAD_DIM), jnp.bfloat16),
                   jax.ShapeDtypeStruct((bg, nqb, BLOCK, LANES), jnp.bfloat16)],
        compiler_params=pltpu.CompilerParams(dimension_semantics=("arbitrary", "arbitrary")),
        name="cmp_topk",
    )(qs, kcp, vcp)


def _slc_attn_kernel(q_ref, mb_ref, ka_ref, v_ref, o_ref, m_sc, l_sc, acc_sc):
    qb = pl.program_id(1)
    q = q_ref[0, 0]
    rows = q.shape[0]
    mb = mb_ref[0, 0]
    qa = jnp.concatenate([jnp.concatenate([mb] * NSA_Q_PER_KV, axis=0), q], axis=1)
    kt_diag = (qb * BLOCK) // KEY_TILE

    def tile(kt):
        start = pl.multiple_of(kt * KEY_TILE, KEY_TILE)
        return ka_ref[0, pl.ds(start, KEY_TILE), :], v_ref[0, pl.ds(start, KEY_TILE), :]

    ka, v = tile(kt_diag)
    s = _dot_nt(qa, ka)
    row = lax.broadcasted_iota(jnp.int32, (rows, KEY_TILE), 0)
    col = lax.broadcasted_iota(jnp.int32, (rows, KEY_TILE), 1)
    s = jnp.where(kt_diag * KEY_TILE + col <= qb * BLOCK + (row & (BLOCK - 1)), s, NEG_INF)
    m0 = jnp.max(s, axis=-1, keepdims=True)
    p = jnp.exp(s - m0)
    m_sc[...] = m0
    l_sc[...] = jnp.sum(p, axis=-1, keepdims=True)
    acc_sc[...] = _dot(p.astype(jnp.bfloat16), v)

    def body(kt, carry):
        ka, v = tile(kt)
        s = _dot_nt(qa, ka)
        m_old = m_sc[...]
        m_new = jnp.maximum(m_old, jnp.max(s, axis=-1, keepdims=True))
        alpha = jnp.exp(m_old - m_new)
        p = jnp.exp(s - m_new)
        m_sc[...] = m_new
        l_sc[...] = alpha * l_sc[...] + jnp.sum(p, axis=-1, keepdims=True)
        acc_sc[...] = alpha * acc_sc[...] + _dot(p.astype(jnp.bfloat16), v)
        return carry

    lax.fori_loop(0, kt_diag, body, 0)
    o_ref[0, 0] = (acc_sc[...] / l_sc[...]).astype(o_ref.dtype)


def _slc_attn(qsp, mb, kaug, vs):
    bg, nqb, rows, _ = qsp.shape
    seq = kaug.shape[1]
    return pl.pallas_call(
        _slc_attn_kernel,
        grid=(bg, nqb),
        in_specs=[pl.BlockSpec((1, 1, rows, LANES), lambda i, j: (i, j, 0, 0)),
                  pl.BlockSpec((1, 1, BLOCK, LANES), lambda i, j: (i, j, 0, 0)),
                  pl.BlockSpec((1, seq, 2 * LANES), lambda i, j: (i, 0, 0)),
                  pl.BlockSpec((1, seq, HEAD_DIM), lambda i, j: (i, 0, 0))],
        out_specs=pl.BlockSpec((1, 1, rows, HEAD_DIM), lambda i, j: (i, j, 0, 0)),
        out_shape=jax.ShapeDtypeStruct((bg, nqb, rows, HEAD_DIM), jnp.bfloat16),
        scratch_shapes=[pltpu.VMEM((rows, 1), jnp.float32),
                        pltpu.VMEM((rows, 1), jnp.float32),
                        pltpu.VMEM((rows, HEAD_DIM), jnp.float32)],
        compiler_params=pltpu.CompilerParams(dimension_semantics=("arbitrary", "arbitrary"),
                                             vmem_limit_bytes=VMEM_LIMIT),
        name="slc_attn",
    )(qsp, mb, kaug, vs)


WIN_BLOCKS = -(-(SWA_WINDOW - 1) // BLOCK) + 1


def _win_attn_kernel(q_ref, k_ref, v_ref, o_ref):
    qb = pl.program_id(1)
    q = q_ref[0, 0]
    rows = q.shape[0]
    span = WIN_BLOCKS * BLOCK
    first = jnp.maximum(qb - (WIN_BLOCKS - 1), 0)
    start = pl.multiple_of(first * BLOCK, BLOCK)
    k = k_ref[0, pl.ds(start, span), :]
    v = v_ref[0, pl.ds(start, span), :]
    s = _dot_nt(q, k)
    row = lax.broadcasted_iota(jnp.int32, (rows, span), 0)
    col = lax.broadcasted_iota(jnp.int32, (rows, span), 1)
    dist = qb * BLOCK + (row & (BLOCK - 1)) - (first * BLOCK + col)
    s = jnp.where((dist >= 0) & (dist <= SWA_WINDOW - 1), s, NEG_INF)
    m = jnp.max(s, axis=-1, keepdims=True)
    e = jnp.exp(s - m)
    den = jnp.sum(e, axis=-1, keepdims=True)
    o_ref[0, 0] = (_dot(e.astype(jnp.bfloat16), v) / den).astype(o_ref.dtype)


def _win_attn(qs, kw, vw):
    bg, nqb, rows, _ = qs.shape
    seq = kw.shape[1]
    return pl.pallas_call(
        _win_attn_kernel,
        grid=(bg, nqb),
        in_specs=[pl.BlockSpec((1, 1, rows, HEAD_DIM), lambda i, j: (i, j, 0, 0)),
                  pl.BlockSpec((1, seq, HEAD_DIM), lambda i, j: (i, 0, 0)),
                  pl.BlockSpec((1, seq, HEAD_DIM), lambda i, j: (i, 0, 0))],
        out_specs=pl.BlockSpec((1, 1, rows, HEAD_DIM), lambda i, j: (i, j, 0, 0)),
        out_shape=jax.ShapeDtypeStruct((bg, nqb, rows, HEAD_DIM), jnp.bfloat16),
        compiler_params=pltpu.CompilerParams(dimension_semantics=("arbitrary", "arbitrary")),
        name="win_attn",
    )(qs, kw, vw)


def _dil_attn_kernel(q_ref, kp_ref, kc_ref, vp_ref, vc_ref, o_ref, lse_ref, *, window):
    i = pl.program_id(2)
    q = q_ref[0].astype(jnp.float32)
    k = jnp.concatenate([kp_ref[0], kc_ref[0]], axis=0).astype(jnp.float32)
    v = jnp.concatenate([vp_ref[0], vc_ref[0]], axis=0).astype(jnp.float32)
    nq, nk = q.shape[0], k.shape[0]
    row = lax.broadcasted_iota(jnp.int32, (nq, nk), 0)
    col = lax.broadcasted_iota(jnp.int32, (nq, nk), 1)
    dist = BLOCK + row - col
    valid = (dist >= 0) & (dist <= window) & ((col >= BLOCK) | (i >= 1))
    outs, lses = [], []
    for h in range(DIL_HEADS):
        sl = slice(h * HEAD_DIM, (h + 1) * HEAD_DIM)
        s = _dot_nt(q[:, sl].astype(jnp.bfloat16), k[:, sl].astype(jnp.bfloat16))
        s = jnp.where(valid, s, NEG_INF)
        m = jnp.max(s, axis=-1, keepdims=True)
        e = jnp.exp(s - m)
        den = jnp.sum(e, axis=-1, keepdims=True)
        outs.append(_dot(e.astype(jnp.bfloat16), v[:, sl].astype(jnp.bfloat16)) / den)
        lses.append(jnp.broadcast_to(m + jnp.log(den), (nq, HEAD_DIM)))
    o_ref[0] = jnp.concatenate(outs, axis=1).astype(o_ref.dtype)
    lse_ref[0] = jnp.concatenate(lses, axis=1)


def _dil_attn(qv, kv, vv, dil, window):
    b, length, _ = qv.shape
    nblk = length // BLOCK
    cur = pl.BlockSpec((1, BLOCK, W_DIL), lambda bi, c, i: (bi, i, c))
    prev = pl.BlockSpec((1, BLOCK, W_DIL), lambda bi, c, i: (bi, jnp.maximum(i - 1, 0), c))
    return pl.pallas_call(
        functools.partial(_dil_attn_kernel, window=window),
        grid=(b, dil, nblk),
        in_specs=[cur, prev, cur, prev, cur],
        out_specs=[cur, cur],
        out_shape=[jax.ShapeDtypeStruct(qv.shape, jnp.bfloat16),
                   jax.ShapeDtypeStruct(qv.shape, jnp.float32)],
        compiler_params=pltpu.CompilerParams(
            dimension_semantics=("arbitrary", "arbitrary", "arbitrary")),
        name=f"dil_attn_d{dil}",
    )(qv, kv, kv, vv, vv)


def _out_proj_kernel(x_ref, oc_ref, os_ref, ow_ref, gate_ref, sza_ref,
                     o1_ref, o2_ref, o3_ref, l1_ref, l2_ref, l3_ref, szb_ref,
                     w_ref, g_ref, out_ref):
    f32 = jnp.float32
    gate = gate_ref[...].astype(f32)
    o_a = (gate[:, 0:W_NSA] * oc_ref[...].astype(f32)
           + gate[:, W_NSA:2 * W_NSA] * os_ref[...].astype(f32)
           + gate[:, 2 * W_NSA:3 * W_NSA] * ow_ref[...].astype(f32))
    mixed_a = (o_a * sza_ref[...].astype(f32)).astype(jnp.bfloat16)
    l1, l2, l3 = l1_ref[...], l2_ref[...], l3_ref[...]
    mx = jnp.maximum(jnp.maximum(l1, l2), l3)
    e1, e2, e3 = jnp.exp(l1 - mx), jnp.exp(l2 - mx), jnp.exp(l3 - mx)
    den = e1 + e2 + e3
    o_b = (e1 / den) * o1_ref[...].astype(f32) + (e2 / den) * o2_ref[...].astype(f32) \
        + (e3 / den) * o3_ref[...].astype(f32)
    mixed_b = (o_b * szb_ref[...].astype(f32)).astype(jnp.bfloat16)
    y = _dot(mixed_a, w_ref[0:W_NSA, :]) + _dot(mixed_b, w_ref[W_NSA:W_NSA + W_DIL, :])
    ms = jnp.mean(y * y, axis=-1, keepdims=True)
    out_ref[...] = x_ref[...] + y * lax.rsqrt(ms + RMS_EPS) * g_ref[...]


def _out_proj(x2, oc, os_, ow, gate, sza, o1, o2, o3, l1, l2, l3, szb, w, g):
    rows, d_model = x2.shape
    tm = PROJ_ROWS
    row_spec = lambda wd: pl.BlockSpec((tm, wd), lambda i: (i, 0))
    return pl.pallas_call(
        _out_proj_kernel,
        grid=(rows // tm,),
        in_specs=[row_spec(d_model), row_spec(W_NSA), row_spec(W_NSA), row_spec(W_NSA),
                  row_spec(3 * W_NSA), row_spec(W_NSA),
                  row_spec(W_DIL), row_spec(W_DIL), row_spec(W_DIL),
                  row_spec(W_DIL), row_spec(W_DIL), row_spec(W_DIL), row_spec(W_DIL),
                  pl.BlockSpec(w.shape, lambda i: (0, 0)),
                  pl.BlockSpec((1, d_model), lambda i: (0, 0))],
        out_specs=row_spec(d_model),
        out_shape=jax.ShapeDtypeStruct((rows, d_model), jnp.float32),
        compiler_params=pltpu.CompilerParams(dimension_semantics=("arbitrary",),
                                             vmem_limit_bytes=VMEM_LIMIT),
        name="out_proj",
    )(x2, oc, os_, ow, gate, sza, o1, o2, o3, l1, l2, l3, szb, w, g)


def _rope_tables(positions):
    inv = 1.0 / (ROPE_THETA ** (jnp.arange(0, ROPE_DIMS, 2, dtype=jnp.float32) / ROPE_DIMS))
    ang = positions.astype(jnp.float32).reshape(-1)[:, None] * inv
    cos, sin = jnp.cos(ang), jnp.sin(ang)
    rest = HEAD_DIM - ROPE_DIMS
    one = jnp.ones((cos.shape[0], rest), jnp.float32)
    zero = jnp.zeros((cos.shape[0], rest), jnp.float32)
    zh = jnp.zeros_like(sin)
    per_head = lambda lo, hi, fill: jnp.tile(jnp.concatenate([lo, hi, fill], axis=1), (1, 2))
    return per_head(cos, cos, one), per_head(-sin, zh, zero), per_head(zh, sin, zero)


def _in_proj_weights(w_in):
    scale = HEAD_DIM ** -0.5
    o = 0
    cols = {}
    for name, width in (("qa", W_NSA), ("kva", 6 * W_KV), ("gate", 3 * NSA_HEADS), ("za", W_NSA),
                        ("qb", W_DIL), ("kb", W_DIL), ("vb", W_DIL), ("zb", W_DIL)):
        cols[name] = w_in[:, o:o + width]
        o += width
    gate_wide = jnp.repeat(cols["gate"], HEAD_DIM, axis=1)
    w = jnp.concatenate([cols["qa"] * scale, cols["kva"], gate_wide, cols["za"],
                         cols["qb"] * scale, cols["kb"], cols["vb"], cols["zb"]], axis=1)
    return w.astype(jnp.bfloat16)


def kernel(x, positions, pre_norm_g, w_in, cmp_k_pos, cmp_k_w1, cmp_k_w2,
           cmp_v_pos, cmp_v_w1, cmp_v_w2, w_out, post_norm_g):
    B, S, d_model = x.shape
    G, R, D = NSA_KV_HEADS, NSA_Q_PER_KV, HEAD_DIM
    depth = w_in.shape[0]
    n_sel = S // SEL_BLOCK
    nqb = S // BLOCK
    n_chunks = S // CMP_STRIDE
    assert S % KEY_TILE == 0 and n_sel <= MAX_SEL_BLOCKS and S >= WIN_BLOCKS * BLOCK
    assert all(S % (BLOCK * dil) == 0 and win // dil == BLOCK for win, dil in DIL_PATTERNS)
    bf16 = jnp.bfloat16

    rope_c, rope_a, rope_b = _rope_tables(positions)
    onehot = (jnp.arange(S)[:, None] // SEL_BLOCK == jnp.arange(LANES)[None, :]).astype(bf16)
    x2 = x.reshape(B * S, d_model)

    for layer in range(depth):
        qa, kva, gate, sza, qb_, kb_, vb_, szb = _in_proj(
            x2, pre_norm_g[layer][None, :], _in_proj_weights(w_in[layer]), rope_c, rope_a, rope_b)

        qs = qa.reshape(B, nqb, BLOCK, G, R, D).transpose(0, 3, 1, 4, 2, 5).reshape(B * G, nqb, R * BLOCK, D)
        qsp = jnp.pad(qs, ((0, 0), (0, 0), (0, 0), (0, LANES - D)))
        kv6 = kva.reshape(B, S, 6, G, D).transpose(2, 0, 3, 1, 4).reshape(6, B * G, S, D)

        chunks = kv6[0:2].reshape(2, B * G, n_chunks, CMP_STRIDE * D)
        pos = jnp.stack([cmp_k_pos[layer], cmp_v_pos[layer]]).reshape(2, 1, CMP_LEN * D)
        pos8 = jnp.broadcast_to(pos, (2, 16, CMP_LEN * D)).astype(bf16)
        w1 = jnp.stack([cmp_k_w1[layer], cmp_v_w1[layer]]).astype(bf16)
        w2 = jnp.stack([cmp_k_w2[layer], cmp_v_w2[layer]]).astype(bf16)
        half = CMP_STRIDE * D
        kvc = _compress(chunks, pos8, w1[:, :half], w1[:, half:], w2)
        kvc = kvc.reshape(2, B * G, n_sel, CMP_PER_SEL, D)
        kvc = jnp.pad(kvc, ((0, 0), (0, 0), (0, MAX_SEL_BLOCKS - n_sel), (0, 0), (0, 0)))
        kvc = kvc.transpose(0, 1, 3, 2, 4).reshape(2, B * G, CMP_PER_SEL * MAX_SEL_BLOCKS, D).astype(bf16)

        o_cmp, mask_bias = _cmp_topk(qs, kvc[0], kvc[1])
        kaug = jnp.concatenate([jnp.broadcast_to(onehot, (B * G, S, LANES)), kv6[2],
                                jnp.zeros((B * G, S, LANES - D), bf16)], axis=-1)
        o_slc = _slc_attn(qsp, mask_bias, kaug, kv6[3])
        o_win = _win_attn(qs, kv6[4], kv6[5])
        unhead = lambda o: o.reshape(B, G, nqb, R, BLOCK, D).transpose(0, 2, 4, 1, 3, 5).reshape(B * S, W_NSA)

        o_b, lse_b = [], []
        for window, dil in DIL_PATTERNS:
            view = lambda t: t.reshape(B, S // dil, dil * W_DIL)
            o, lse = _dil_attn(view(qb_), view(kb_), view(vb_), dil, window // dil)
            o_b.append(o.reshape(B * S, W_DIL))
            lse_b.append(lse.reshape(B * S, W_DIL))

        x2 = _out_proj(x2, unhead(o_cmp), unhead(o_slc), unhead(o_win), gate, sza,
                       o_b[0], o_b[1], o_b[2], lse_b[0], lse_b[1], lse_b[2], szb,
                       w_out[layer].astype(bf16), post_norm_g[layer][None, :])
    return x2.reshape(B, S, d_model)
```

```python
import functools

import jax
import jax.numpy as jnp
from jax import lax
from jax.experimental import pallas as pl
from jax.experimental.pallas import tpu as pltpu

HEAD_DIM = 64
NSA_HEADS = 8
NSA_KV_HEADS = 2
NSA_Q_PER_KV = NSA_HEADS // NSA_KV_HEADS
DIL_HEADS = 8
W_NSA = NSA_HEADS * HEAD_DIM
W_KV = NSA_KV_HEADS * HEAD_DIM
W_DIL = DIL_HEADS * HEAD_DIM
CMP_LEN = 32
CMP_STRIDE = 16
CMP_HIDDEN = 256
SEL_BLOCK = 64
SEL_TOP_N = 16
SWA_WINDOW = 512
DIL_PATTERNS = ((128, 1), (512, 4), (2048, 16))
BLOCK = 128
ROPE_THETA = 500000.0
ROPE_DIMS = HEAD_DIM // 4
RMS_EPS = 1e-6
NEG_INF = -1e30
FORCE_SCORE = 1e4

LANES = 128
VMEM_LIMIT = 56 * 1024 * 1024
MAX_SEL_BLOCKS = LANES
CMP_PER_SEL = SEL_BLOCK // CMP_STRIDE
KEY_TILE = 512
PROJ_ROWS = 512

_NT = (((1,), (1,)), ((), ()))


def _dot(a, b):
    return jnp.dot(a, b, preferred_element_type=jnp.float32)


def _dot_nt(a, b):
    return lax.dot_general(a, b, _NT, preferred_element_type=jnp.float32)


def _sigmoid(x):
    return 1.0 / (1.0 + jnp.exp(-x))


def _rope(x, c, a, b):
    width = x.shape[1]
    reps = width // LANES
    ct = jnp.tile(c, (1, reps))
    at = jnp.tile(a, (1, reps))
    bt = jnp.tile(b, (1, reps))
    half = ROPE_DIMS // 2
    return x * ct + pltpu.roll(x, width - half, 1) * at + pltpu.roll(x, half, 1) * bt


def _in_proj_kernel(x_ref, g_ref, w_ref, c_ref, a_ref, b_ref,
                    qa_ref, kva_ref, gate_ref, sza_ref, qb_ref, kb_ref, vb_ref, szb_ref):
    x = x_ref[...]
    ms = jnp.mean(x * x, axis=-1, keepdims=True)
    h = (x * lax.rsqrt(ms + RMS_EPS) * g_ref[...]).astype(jnp.bfloat16)
    c = c_ref[...]
    a = a_ref[...]
    b = b_ref[...]
    off = 0

    def proj(width):
        nonlocal off
        r = _dot(h, w_ref[:, off:off + width])
        off += width
        return r

    qa_ref[...] = _rope(proj(W_NSA), c, a, b).astype(qa_ref.dtype)
    kva = proj(6 * W_KV)
    parts = []
    for i in range(6):
        p = kva[:, i * W_KV:(i + 1) * W_KV]
        parts.append(_rope(p, c, a, b) if i % 2 == 0 else p)
    kva_ref[...] = jnp.concatenate(parts, axis=1).astype(kva_ref.dtype)
    gate_ref[...] = _sigmoid(proj(3 * W_NSA)).astype(gate_ref.dtype)
    za = proj(W_NSA)
    sza_ref[...] = (za * _sigmoid(za)).astype(sza_ref.dtype)
    qb_ref[...] = _rope(proj(W_DIL), c, a, b).astype(qb_ref.dtype)
    kb_ref[...] = _rope(proj(W_DIL), c, a, b).astype(kb_ref.dtype)
    vb_ref[...] = proj(W_DIL).astype(vb_ref.dtype)
    zb = proj(W_DIL)
    szb_ref[...] = (zb * _sigmoid(zb)).astype(szb_ref.dtype)


def _in_proj(x2, g, w, c, a, b):
    rows, d_model = x2.shape
    n_total = w.shape[1]
    tm = PROJ_ROWS
    widths = (W_NSA, 6 * W_KV, 3 * W_NSA, W_NSA, W_DIL, W_DIL, W_DIL, W_DIL)
    row_spec = lambda wd: pl.BlockSpec((tm, wd), lambda i: (i, 0))
    return pl.pallas_call(
        _in_proj_kernel,
        grid=(rows // tm,),
        in_specs=[row_spec(d_model),
                  pl.BlockSpec((1, d_model), lambda i: (0, 0)),
                  pl.BlockSpec((d_model, n_total), lambda i: (0, 0)),
                  row_spec(LANES), row_spec(LANES), row_spec(LANES)],
        out_specs=[row_spec(wd) for wd in widths],
        out_shape=[jax.ShapeDtypeStruct((rows, wd), jnp.bfloat16) for wd in widths],
        compiler_params=pltpu.CompilerParams(dimension_semantics=("arbitrary",),
                                             vmem_limit_bytes=VMEM_LIMIT),
        name="in_proj",
    )(x2, g, w, c, a, b)


def _compress_kernel(ch_ref, pos_ref, w1t_ref, w1b_ref, w2_ref, o_ref):
    ch = ch_ref[0, 0]
    n = ch.shape[0]
    half = pos_ref.shape[2] // 2
    pos = pos_ref[0]
    top = _dot(ch, w1t_ref[0])
    bot = _dot(ch, w1b_ref[0])
    bias = _dot(pos[:, :half], w1t_ref[0]) + _dot(pos[:, half:], w1b_ref[0])
    hid = top + pltpu.roll(bot, n - 1, 0) + bias[0:1, :]
    act = (hid * _sigmoid(hid)).astype(jnp.bfloat16)
    o_ref[0, 0] = _dot(act, w2_ref[0])


def _compress(chunks, pos8, w1t, w1b, w2):
    _, bg, n, width = chunks.shape
    return pl.pallas_call(
        _compress_kernel,
        grid=(2, bg),
        in_specs=[pl.BlockSpec((1, 1, n, width), lambda s, i: (s, i, 0, 0)),
                  pl.BlockSpec((1, 16, 2 * width), lambda s, i: (s, 0, 0)),
                  pl.BlockSpec((1, width, CMP_HIDDEN), lambda s, i: (s, 0, 0)),
                  pl.BlockSpec((1, width, CMP_HIDDEN), lambda s, i: (s, 0, 0)),
                  pl.BlockSpec((1, CMP_HIDDEN, HEAD_DIM), lambda s, i: (s, 0, 0))],
        out_specs=pl.BlockSpec((1, 1, n, HEAD_DIM), lambda s, i: (s, i, 0, 0)),
        out_shape=jax.ShapeDtypeStruct((2, bg, n, HEAD_DIM), jnp.float32),
        compiler_params=pltpu.CompilerParams(dimension_semantics=("arbitrary", "arbitrary")),
        name="compress",
    )(chunks, pos8, w1t, w1b, w2)


def _cmp_topk_kernel(q_ref, kc_ref, vc_ref, o_ref, mb_ref):
    qb = pl.program_id(1)
    q = q_ref[0, 0]
    rows = q.shape[0]
    ncols = kc_ref.shape[1]
    s = _dot_nt(q, kc_ref[0])
    row = lax.broadcasted_iota(jnp.int32, (rows, ncols), 0)
    col = lax.broadcasted_iota(jnp.int32, (rows, ncols), 1)
    t = qb * BLOCK + (row & (BLOCK - 1))
    cmp_end = (col & (LANES - 1)) * SEL_BLOCK + (col >> 7) * CMP_STRIDE + (CMP_LEN - 1)
    valid = cmp_end <= t
    s = jnp.where(valid, s, NEG_INF)
    m = jnp.max(s, axis=-1, keepdims=True)
    p = jnp.where(valid, jnp.exp(s - m), 0.0)
    p = p / jnp.maximum(jnp.sum(p, axis=-1, keepdims=True), 1e-30)
    o_ref[0, 0] = _dot(p.astype(jnp.bfloat16), vc_ref[0]).astype(o_ref.dtype)

    imp = p[0:BLOCK]
    for r in range(1, NSA_Q_PER_KV):
        imp = imp + p[r * BLOCK:(r + 1) * BLOCK]
    g = [imp[:, i * LANES:(i + 1) * LANES] for i in range(CMP_PER_SEL)]
    lane = lax.broadcasted_iota(jnp.int32, (BLOCK, LANES), 1)
    prev_last = jnp.where(lane == 0, 0.0, pltpu.roll(g[3], 1, 1))
    imp_sel = prev_last + 2.0 * (g[0] + g[1] + g[2]) + g[3]

    blk = lax.broadcasted_iota(jnp.int32, (LANES, BLOCK), 0)
    tq = qb * BLOCK + lax.broadcasted_iota(jnp.int32, (LANES, BLOCK), 1)
    cur = tq >> 6
    forced = (blk == 0) | (blk == cur) | (blk == cur - 1)
    score = jnp.where(forced, FORCE_SCORE, imp_sel.T)
    score = jnp.where(blk <= cur, score, -1.0)
    blk_f = blk.astype(jnp.float32)

    def pick_one(_, carry):
        sc, sel = carry
        best = jnp.max(sc, axis=0, keepdims=True)
        first = jnp.min(jnp.where(sc == best, blk_f, float(LANES)), axis=0, keepdims=True)
        pick = blk_f == first
        return jnp.where(pick, -2.0, sc), jnp.where(pick, 1.0, sel)

    _, sel = lax.fori_loop(0, SEL_TOP_N, pick_one, (score, jnp.zeros_like(score)))
    bias = jnp.where((sel > 0.0) & (blk <= cur), 0.0, NEG_INF)
    mb_ref[0, 0] = bias.T.astype(mb_ref.dtype)


def _cmp_topk(qs, kcp, vcp):
    bg, nqb, rows, _ = qs.shape
    ncols = kcp.shape[1]
    return pl.pallas_call(
        _cmp_topk_kernel,
        grid=(bg, nqb),
        in_specs=[pl.BlockSpec((1, 1, rows, HEAD_DIM), lambda i, j: (i, j, 0, 0)),
                  pl.BlockSpec((1, ncols, HEAD_DIM), lambda i, j: (i, 0, 0)),
                  pl.BlockSpec((1, ncols, HEAD_DIM), lambda i, j: (i, 0, 0))],
        out_specs=[pl.BlockSpec((1, 1, rows, HEAD_DIM), lambda i, j: (i, j, 0, 0)),
                   pl.BlockSpec((1, 1, BLOCK, LANES), lambda i, j: (i, j, 0, 0))],
        out_shape=[jax.ShapeDtypeStruct((bg, nqb, rows, HEAD_DIM), jnp.bfloat16),
                   jax.ShapeDtypeStruct((bg, nqb, BLOCK, LANES), jnp.bfloat16)],
        compiler_params=pltpu.CompilerParams(dimension_semantics=("arbitrary", "arbitrary")),
        name="cmp_topk",
    )(qs, kcp, vcp)


def _slc_attn_kernel(q_ref, mb_ref, ka_ref, vt_ref, o_ref):
    qb = pl.program_id(1)
    q = q_ref[0, 0]
    rows = q.shape[0]
    mb = mb_ref[0, 0]
    qa = jnp.concatenate([jnp.concatenate([mb] * NSA_Q_PER_KV, axis=0), q], axis=1)
    kt_diag = (qb * BLOCK) // KEY_TILE

    def scores(kt):
        start = pl.multiple_of(kt * KEY_TILE, KEY_TILE)
        st = _dot_nt(ka_ref[0, pl.ds(start, KEY_TILE), :], qa)
        return st, vt_ref[0, :, pl.ds(start, KEY_TILE)]

    st, vt = scores(kt_diag)
    key = kt_diag * KEY_TILE + lax.broadcasted_iota(jnp.int32, (KEY_TILE, rows), 0)
    t = qb * BLOCK + (lax.broadcasted_iota(jnp.int32, (KEY_TILE, rows), 1) & (BLOCK - 1))
    st = jnp.where(key <= t, st, NEG_INF)
    m0 = jnp.max(st, axis=0, keepdims=True)
    p = jnp.exp(st - m0)
    l0 = jnp.sum(p, axis=0, keepdims=True)
    acc0 = _dot(vt, p.astype(jnp.bfloat16))

    def body(kt, carry):
        m_old, l_old, acc = carry
        st, vt = scores(kt)
        m_new = jnp.maximum(m_old, jnp.max(st, axis=0, keepdims=True))
        alpha = jnp.exp(m_old - m_new)
        p = jnp.exp(st - m_new)
        l_new = alpha * l_old + jnp.sum(p, axis=0, keepdims=True)
        return m_new, l_new, alpha * acc + _dot(vt, p.astype(jnp.bfloat16))

    _, l, acc = lax.fori_loop(0, kt_diag, body, (m0, l0, acc0))
    o_ref[0, 0] = (acc / l).astype(o_ref.dtype)


def _slc_attn(qsp, mb, kaug, vt):
    bg, nqb, rows, _ = qsp.shape
    seq = kaug.shape[1]
    return pl.pallas_call(
        _slc_attn_kernel,
        grid=(bg, nqb),
        in_specs=[pl.BlockSpec((1, 1, rows, LANES), lambda i, j: (i, j, 0, 0)),
                  pl.BlockSpec((1, 1, BLOCK, LANES), lambda i, j: (i, j, 0, 0)),
                  pl.BlockSpec((1, seq, 2 * LANES), lambda i, j: (i, 0, 0)),
                  pl.BlockSpec((1, HEAD_DIM, seq), lambda i, j: (i, 0, 0))],
        out_specs=pl.BlockSpec((1, 1, HEAD_DIM, rows), lambda i, j: (i, j, 0, 0)),
        out_shape=jax.ShapeDtypeStruct((bg, nqb, HEAD_DIM, rows), jnp.bfloat16),
        compiler_params=pltpu.CompilerParams(dimension_semantics=("arbitrary", "arbitrary"),
                                             vmem_limit_bytes=VMEM_LIMIT),
        name="slc_attn",
    )(qsp, mb, kaug, vt)


WIN_BLOCKS = -(-(SWA_WINDOW - 1) // BLOCK) + 1


def _win_attn_kernel(q_ref, k_ref, vt_ref, o_ref):
    qb = pl.program_id(1)
    q = q_ref[0, 0]
    rows = q.shape[0]
    span = WIN_BLOCKS * BLOCK
    first = jnp.maximum(qb - (WIN_BLOCKS - 1), 0)
    start = pl.multiple_of(first * BLOCK, BLOCK)
    st = _dot_nt(k_ref[0, pl.ds(start, span), :], q)
    key = first * BLOCK + lax.broadcasted_iota(jnp.int32, (span, rows), 0)
    t = qb * BLOCK + (lax.broadcasted_iota(jnp.int32, (span, rows), 1) & (BLOCK - 1))
    dist = t - key
    st = jnp.where((dist >= 0) & (dist <= SWA_WINDOW - 1), st, NEG_INF)
    m = jnp.max(st, axis=0, keepdims=True)
    e = jnp.exp(st - m)
    den = jnp.sum(e, axis=0, keepdims=True)
    acc = _dot(vt_ref[0, :, pl.ds(start, span)], e.astype(jnp.bfloat16))
    o_ref[0, 0] = (acc / den).astype(o_ref.dtype)


def _win_attn(qs, kw, vwt):
    bg, nqb, rows, _ = qs.shape
    seq = kw.shape[1]
    return pl.pallas_call(
        _win_attn_kernel,
        grid=(bg, nqb),
        in_specs=[pl.BlockSpec((1, 1, rows, HEAD_DIM), lambda i, j: (i, j, 0, 0)),
                  pl.BlockSpec((1, seq, HEAD_DIM), lambda i, j: (i, 0, 0)),
                  pl.BlockSpec((1, HEAD_DIM, seq), lambda i, j: (i, 0, 0))],
        out_specs=pl.BlockSpec((1, 1, HEAD_DIM, rows), lambda i, j: (i, j, 0, 0)),
        out_shape=jax.ShapeDtypeStruct((bg, nqb, HEAD_DIM, rows), jnp.bfloat16),
        compiler_params=pltpu.CompilerParams(dimension_semantics=("arbitrary", "arbitrary")),
        name="win_attn",
    )(qs, kw, vwt)


def _dil_attn_kernel(q_ref, kp_ref, kc_ref, vp_ref, vc_ref, o_ref, lse_ref, *, window):
    i = pl.program_id(2)
    q = q_ref[0].astype(jnp.float32)
    k = jnp.concatenate([kp_ref[0], kc_ref[0]], axis=0).astype(jnp.float32)
    v = jnp.concatenate([vp_ref[0], vc_ref[0]], axis=0).astype(jnp.float32)
    nq, nk = q.shape[0], k.shape[0]
    row = lax.broadcasted_iota(jnp.int32, (nq, nk), 0)
    col = lax.broadcasted_iota(jnp.int32, (nq, nk), 1)
    dist = BLOCK + row - col
    valid = (dist >= 0) & (dist <= window) & ((col >= BLOCK) | (i >= 1))
    outs, lses = [], []
    for h in range(DIL_HEADS):
        sl = slice(h * HEAD_DIM, (h + 1) * HEAD_DIM)
        s = _dot_nt(q[:, sl].astype(jnp.bfloat16), k[:, sl].astype(jnp.bfloat16))
        s = jnp.where(valid, s, NEG_INF)
        m = jnp.max(s, axis=-1, keepdims=True)
        e = jnp.exp(s - m)
        den = jnp.sum(e, axis=-1, keepdims=True)
        outs.append(_dot(e.astype(jnp.bfloat16), v[:, sl].astype(jnp.bfloat16)) / den)
        lses.append(jnp.broadcast_to(m + jnp.log(den), (nq, HEAD_DIM)))
    o_ref[0] = jnp.concatenate(outs, axis=1).astype(o_ref.dtype)
    lse_ref[0] = jnp.concatenate(lses, axis=1)


def _dil_attn(qv, kv, vv, dil, window):
    b, length, _ = qv.shape
    nblk = length // BLOCK
    cur = pl.BlockSpec((1, BLOCK, W_DIL), lambda bi, c, i: (bi, i, c))
    prev = pl.BlockSpec((1, BLOCK, W_DIL), lambda bi, c, i: (bi, jnp.maximum(i - 1, 0), c))
    return pl.pallas_call(
        functools.partial(_dil_attn_kernel, window=window),
        grid=(b, dil, nblk),
        in_specs=[cur, prev, cur, prev, cur],
        out_specs=[cur, cur],
        out_shape=[jax.ShapeDtypeStruct(qv.shape, jnp.bfloat16),
                   jax.ShapeDtypeStruct(qv.shape, jnp.float32)],
        compiler_params=pltpu.CompilerParams(
            dimension_semantics=("arbitrary", "arbitrary", "arbitrary")),
        name=f"dil_attn_d{dil}",
    )(qv, kv, kv, vv, vv)


def _out_proj_kernel(x_ref, oc_ref, os_ref, ow_ref, gate_ref, sza_ref,
                     o1_ref, o2_ref, o3_ref, l1_ref, l2_ref, l3_ref, szb_ref,
                     w_ref, g_ref, out_ref):
    f32 = jnp.float32
    gate = gate_ref[...].astype(f32)
    o_a = (gate[:, 0:W_NSA] * oc_ref[...].astype(f32)
           + gate[:, W_NSA:2 * W_NSA] * os_ref[...].astype(f32)
           + gate[:, 2 * W_NSA:3 * W_NSA] * ow_ref[...].astype(f32))
    mixed_a = (o_a * sza_ref[...].astype(f32)).astype(jnp.bfloat16)
    l1, l2, l3 = l1_ref[...], l2_ref[...], l3_ref[...]
    mx = jnp.maximum(jnp.maximum(l1, l2), l3)
    e1, e2, e3 = jnp.exp(l1 - mx), jnp.exp(l2 - mx), jnp.exp(l3 - mx)
    den = e1 + e2 + e3
    o_b = (e1 / den) * o1_ref[...].astype(f32) + (e2 / den) * o2_ref[...].astype(f32) \
        + (e3 / den) * o3_ref[...].astype(f32)
    mixed_b = (o_b * szb_ref[...].astype(f32)).astype(jnp.bfloat16)
    y = _dot(mixed_a, w_ref[0:W_NSA, :]) + _dot(mixed_b, w_ref[W_NSA:W_NSA + W_DIL, :])
    ms = jnp.mean(y * y, axis=-1, keepdims=True)
    out_ref[...] = x_ref[...] + y * lax.rsqrt(ms + RMS_EPS) * g_ref[...]


def _out_proj(x2, oc, os_, ow, gate, sza, o1, o2, o3, l1, l2, l3, szb, w, g):
    rows, d_model = x2.shape
    tm = PROJ_ROWS
    row_spec = lambda wd: pl.BlockSpec((tm, wd), lambda i: (i, 0))
    return pl.pallas_call(
        _out_proj_kernel,
        grid=(rows // tm,),
        in_specs=[row_spec(d_model), row_spec(W_NSA), row_spec(W_NSA), row_spec(W_NSA),
                  row_spec(3 * W_NSA), row_spec(W_NSA),
                  row_spec(W_DIL), row_spec(W_DIL), row_spec(W_DIL),
                  row_spec(W_DIL), row_spec(W_DIL), row_spec(W_DIL), row_spec(W_DIL),
                  pl.BlockSpec(w.shape, lambda i: (0, 0)),
                  pl.BlockSpec((1, d_model), lambda i: (0, 0))],
        out_specs=row_spec(d_model),
        out_shape=jax.ShapeDtypeStruct((rows, d_model), jnp.float32),
        compiler_params=pltpu.CompilerParams(dimension_semantics=("arbitrary",),
                                             vmem_limit_bytes=VMEM_LIMIT),
        name="out_proj",
    )(x2, oc, os_, ow, gate, sza, o1, o2, o3, l1, l2, l3, szb, w, g)


def _rope_tables(positions):
    inv = 1.0 / (ROPE_THETA ** (jnp.arange(0, ROPE_DIMS, 2, dtype=jnp.float32) / ROPE_DIMS))
    ang = positions.astype(jnp.float32).reshape(-1)[:, None] * inv
    cos, sin = jnp.cos(ang), jnp.sin(ang)
    rest = HEAD_DIM - ROPE_DIMS
    one = jnp.ones((cos.shape[0], rest), jnp.float32)
    zero = jnp.zeros((cos.shape[0], rest), jnp.float32)
    zh = jnp.zeros_like(sin)
    per_head = lambda lo, hi, fill: jnp.tile(jnp.concatenate([lo, hi, fill], axis=1), (1, 2))
    return per_head(cos, cos, one), per_head(-sin, zh, zero), per_head(zh, sin, zero)


def _in_proj_weights(w_in):
    scale = HEAD_DIM ** -0.5
    o = 0
    cols = {}
    for name, width in (("qa", W_NSA), ("kva", 6 * W_KV), ("gate", 3 * NSA_HEADS), ("za", W_NSA),
                        ("qb", W_DIL), ("kb", W_DIL), ("vb", W_DIL), ("zb", W_DIL)):
        cols[name] = w_in[:, o:o + width]
        o += width
    gate_wide = jnp.repeat(cols["gate"], HEAD_DIM, axis=1)
    w = jnp.concatenate([cols["qa"] * scale, cols["kva"], gate_wide, cols["za"],
                         cols["qb"] * scale, cols["kb"], cols["vb"], cols["zb"]], axis=1)
    return w.astype(jnp.bfloat16)


def kernel(x, positions, pre_norm_g, w_in, cmp_k_pos, cmp_k_w1, cmp_k_w2,
           cmp_v_pos, cmp_v_w1, cmp_v_w2, w_out, post_norm_g):
    B, S, d_model = x.shape
    G, R, D = NSA_KV_HEADS, NSA_Q_PER_KV, HEAD_DIM
    depth = w_in.shape[0]
    n_sel = S // SEL_BLOCK
    nqb = S // BLOCK
    n_chunks = S // CMP_STRIDE
    assert S % KEY_TILE == 0 and n_sel <= MAX_SEL_BLOCKS and S >= WIN_BLOCKS * BLOCK
    assert all(S % (BLOCK * dil) == 0 and win // dil == BLOCK for win, dil in DIL_PATTERNS)
    bf16 = jnp.bfloat16

    rope_c, rope_a, rope_b = _rope_tables(positions)
    onehot = (jnp.arange(S)[:, None] // SEL_BLOCK == jnp.arange(LANES)[None, :]).astype(bf16)
    x2 = x.reshape(B * S, d_model)

    for layer in range(depth):
        qa, kva, gate, sza, qb_, kb_, vb_, szb = _in_proj(
            x2, pre_norm_g[layer][None, :], _in_proj_weights(w_in[layer]), rope_c, rope_a, rope_b)

        qs = qa.reshape(B, nqb, BLOCK, G, R, D).transpose(0, 3, 1, 4, 2, 5).reshape(B * G, nqb, R * BLOCK, D)
        qsp = jnp.pad(qs, ((0, 0), (0, 0), (0, 0), (0, LANES - D)))
        kv6 = kva.reshape(B, S, 6, G, D).transpose(2, 0, 3, 1, 4).reshape(6, B * G, S, D)

        chunks = kv6[0:2].reshape(2, B * G, n_chunks, CMP_STRIDE * D)
        pos = jnp.stack([cmp_k_pos[layer], cmp_v_pos[layer]]).reshape(2, 1, CMP_LEN * D)
        pos8 = jnp.broadcast_to(pos, (2, 16, CMP_LEN * D)).astype(bf16)
        w1 = jnp.stack([cmp_k_w1[layer], cmp_v_w1[layer]]).astype(bf16)
        w2 = jnp.stack([cmp_k_w2[layer], cmp_v_w2[layer]]).astype(bf16)
        half = CMP_STRIDE * D
        kvc = _compress(chunks, pos8, w1[:, :half], w1[:, half:], w2)
        kvc = kvc.reshape(2, B * G, n_sel, CMP_PER_SEL, D)
        kvc = jnp.pad(kvc, ((0, 0), (0, 0), (0, MAX_SEL_BLOCKS - n_sel), (0, 0), (0, 0)))
        kvc = kvc.transpose(0, 1, 3, 2, 4).reshape(2, B * G, CMP_PER_SEL * MAX_SEL_BLOCKS, D).astype(bf16)

        o_cmp, mask_bias = _cmp_topk(qs, kvc[0], kvc[1])
        kaug = jnp.concatenate([jnp.broadcast_to(onehot, (B * G, S, LANES)), kv6[2],
                                jnp.zeros((B * G, S, LANES - D), bf16)], axis=-1)
        o_slc = _slc_attn(qsp, mask_bias, kaug, kv6[3].transpose(0, 2, 1))
        o_win = _win_attn(qs, kv6[4], kv6[5].transpose(0, 2, 1))
        unhead = lambda o: o.reshape(B, G, nqb, R, BLOCK, D).transpose(0, 2, 4, 1, 3, 5).reshape(B * S, W_NSA)
        unhead_t = lambda o: o.reshape(B, G, nqb, D, R, BLOCK).transpose(0, 2, 5, 1, 4, 3).reshape(B * S, W_NSA)

        o_b, lse_b = [], []
        for window, dil in DIL_PATTERNS:
            view = lambda t: t.reshape(B, S // dil, dil * W_DIL)
            o, lse = _dil_attn(view(qb_), view(kb_), view(vb_), dil, window // dil)
            o_b.append(o.reshape(B * S, W_DIL))
            lse_b.append(lse.reshape(B * S, W_DIL))

        x2 = _out_proj(x2, unhead(o_cmp), unhead_t(o_slc), unhead_t(o_win), gate, sza,
                       o_b[0], o_b[1], o_b[2], lse_b[0], lse_b[1], lse_b[2], szb,
                       w_out[layer].astype(bf16), post_norm_g[layer][None, :])
    return x2.reshape(B, S, d_model)
```

```python
import functools

import jax
import jax.numpy as jnp
from jax import lax
from jax.experimental import pallas as pl
from jax.experimental.pallas import tpu as pltpu

HEAD_DIM = 64
NSA_HEADS = 8
NSA_KV_HEADS = 2
NSA_Q_PER_KV = NSA_HEADS // NSA_KV_HEADS
DIL_HEADS = 8
W_NSA = NSA_HEADS * HEAD_DIM
W_KV = NSA_KV_HEADS * HEAD_DIM
W_DIL = DIL_HEADS * HEAD_DIM
CMP_LEN = 32
CMP_STRIDE = 16
CMP_HIDDEN = 256
SEL_BLOCK = 64
SEL_TOP_N = 16
SWA_WINDOW = 512
DIL_PATTERNS = ((128, 1), (512, 4), (2048, 16))
BLOCK = 128
ROPE_THETA = 500000.0
ROPE_DIMS = HEAD_DIM // 4
RMS_EPS = 1e-6
NEG_INF = -1e30
FORCE_SCORE = 1e4
LOG2_E = 1.4426950408889634
LN_2 = 0.6931471805599453

LANES = 128
VMEM_LIMIT = 56 * 1024 * 1024
MAX_SEL_BLOCKS = LANES
CMP_PER_SEL = SEL_BLOCK // CMP_STRIDE
KEY_TILE = 512
PROJ_ROWS = 512

_NT = (((1,), (1,)), ((), ()))


def _dot(a, b):
    return jnp.dot(a, b, preferred_element_type=jnp.float32)


def _dot_nt(a, b):
    return lax.dot_general(a, b, _NT, preferred_element_type=jnp.float32)


def _sigmoid(x):
    return 1.0 / (1.0 + jnp.exp(-x))


def _rope(x, c, a, b):
    width = x.shape[1]
    reps = width // LANES
    ct = jnp.tile(c, (1, reps))
    at = jnp.tile(a, (1, reps))
    bt = jnp.tile(b, (1, reps))
    half = ROPE_DIMS // 2
    return x * ct + pltpu.roll(x, width - half, 1) * at + pltpu.roll(x, half, 1) * bt


def _in_proj_kernel(x_ref, g_ref, w_ref, c_ref, a_ref, b_ref,
                    qa_ref, kva_ref, gate_ref, sza_ref, qb_ref, kb_ref, vb_ref, szb_ref):
    x = x_ref[...]
    ms = jnp.mean(x * x, axis=-1, keepdims=True)
    h = (x * lax.rsqrt(ms + RMS_EPS) * g_ref[...]).astype(jnp.bfloat16)
    c = c_ref[...]
    a = a_ref[...]
    b = b_ref[...]
    off = 0

    def proj(width):
        nonlocal off
        r = _dot(h, w_ref[:, off:off + width])
        off += width
        return r

    qa_ref[...] = _rope(proj(W_NSA), c, a, b).astype(qa_ref.dtype)
    kva = proj(6 * W_KV)
    parts = []
    for i in range(6):
        p = kva[:, i * W_KV:(i + 1) * W_KV]
        parts.append(_rope(p, c, a, b) if i % 2 == 0 else p)
    kva_ref[...] = jnp.concatenate(parts, axis=1).astype(kva_ref.dtype)
    gate_ref[...] = _sigmoid(proj(3 * W_NSA)).astype(gate_ref.dtype)
    za = proj(W_NSA)
    sza_ref[...] = (za * _sigmoid(za)).astype(sza_ref.dtype)
    qb_ref[...] = _rope(proj(W_DIL), c, a, b).astype(qb_ref.dtype)
    kb_ref[...] = _rope(proj(W_DIL), c, a, b).astype(kb_ref.dtype)
    vb_ref[...] = proj(W_DIL).astype(vb_ref.dtype)
    zb = proj(W_DIL)
    szb_ref[...] = (zb * _sigmoid(zb)).astype(szb_ref.dtype)


def _in_proj(x2, g, w, c, a, b):
    rows, d_model = x2.shape
    n_total = w.shape[1]
    tm = PROJ_ROWS
    widths = (W_NSA, 6 * W_KV, 3 * W_NSA, W_NSA, W_DIL, W_DIL, W_DIL, W_DIL)
    row_spec = lambda wd: pl.BlockSpec((tm, wd), lambda i: (i, 0))
    return pl.pallas_call(
        _in_proj_kernel,
        grid=(rows // tm,),
        in_specs=[row_spec(d_model),
                  pl.BlockSpec((1, d_model), lambda i: (0, 0)),
                  pl.BlockSpec((d_model, n_total), lambda i: (0, 0)),
                  row_spec(LANES), row_spec(LANES), row_spec(LANES)],
        out_specs=[row_spec(wd) for wd in widths],
        out_shape=[jax.ShapeDtypeStruct((rows, wd), jnp.bfloat16) for wd in widths],
        compiler_params=pltpu.CompilerParams(dimension_semantics=("arbitrary",),
                                             vmem_limit_bytes=VMEM_LIMIT),
        name="in_proj",
    )(x2, g, w, c, a, b)


def _compress_kernel(ch_ref, pos_ref, w1t_ref, w1b_ref, w2_ref, o_ref):
    ch = ch_ref[0, 0]
    n = ch.shape[0]
    half = pos_ref.shape[2] // 2
    pos = pos_ref[0]
    top = _dot(ch, w1t_ref[0])
    bot = _dot(ch, w1b_ref[0])
    bias = _dot(pos[:, :half], w1t_ref[0]) + _dot(pos[:, half:], w1b_ref[0])
    hid = top + pltpu.roll(bot, n - 1, 0) + bias[0:1, :]
    act = (hid * _sigmoid(hid)).astype(jnp.bfloat16)
    o_ref[0, 0] = _dot(act, w2_ref[0])


def _compress(chunks, pos8, w1t, w1b, w2):
    _, bg, n, width = chunks.shape
    return pl.pallas_call(
        _compress_kernel,
        grid=(2, bg),
        in_specs=[pl.BlockSpec((1, 1, n, width), lambda s, i: (s, i, 0, 0)),
                  pl.BlockSpec((1, 16, 2 * width), lambda s, i: (s, 0, 0)),
                  pl.BlockSpec((1, width, CMP_HIDDEN), lambda s, i: (s, 0, 0)),
                  pl.BlockSpec((1, width, CMP_HIDDEN), lambda s, i: (s, 0, 0)),
                  pl.BlockSpec((1, CMP_HIDDEN, HEAD_DIM), lambda s, i: (s, 0, 0))],
        out_specs=pl.BlockSpec((1, 1, n, HEAD_DIM), lambda s, i: (s, i, 0, 0)),
        out_shape=jax.ShapeDtypeStruct((2, bg, n, HEAD_DIM), jnp.float32),
        compiler_params=pltpu.CompilerParams(dimension_semantics=("arbitrary", "arbitrary")),
        name="compress",
    )(chunks, pos8, w1t, w1b, w2)


def _cmp_topk_kernel(q_ref, kc_ref, vc_ref, o_ref, mb_ref):
    qb = pl.program_id(1)
    q = q_ref[0, 0]
    rows = q.shape[0]
    ncols = kc_ref.shape[1]
    s = _dot_nt(q, kc_ref[0])
    row = lax.broadcasted_iota(jnp.int32, (rows, ncols), 0)
    col = lax.broadcasted_iota(jnp.int32, (rows, ncols), 1)
    t = qb * BLOCK + (row & (BLOCK - 1))
    cmp_end = (col & (LANES - 1)) * SEL_BLOCK + (col >> 7) * CMP_STRIDE + (CMP_LEN - 1)
    valid = cmp_end <= t
    s = jnp.where(valid, s, NEG_INF)
    m = jnp.max(s, axis=-1, keepdims=True)
    p = jnp.where(valid, jnp.exp2(s - m), 0.0)
    p = p / jnp.maximum(jnp.sum(p, axis=-1, keepdims=True), 1e-30)
    o_ref[0, 0] = _dot(p.astype(jnp.bfloat16), vc_ref[0]).astype(o_ref.dtype)

    imp = p[0:BLOCK]
    for r in range(1, NSA_Q_PER_KV):
        imp = imp + p[r * BLOCK:(r + 1) * BLOCK]
    g = [imp[:, i * LANES:(i + 1) * LANES] for i in range(CMP_PER_SEL)]
    lane = lax.broadcasted_iota(jnp.int32, (BLOCK, LANES), 1)
    prev_last = jnp.where(lane == 0, 0.0, pltpu.roll(g[3], 1, 1))
    imp_sel = prev_last + 2.0 * (g[0] + g[1] + g[2]) + g[3]

    blk = lax.broadcasted_iota(jnp.int32, (LANES, BLOCK), 0)
    tq = qb * BLOCK + lax.broadcasted_iota(jnp.int32, (LANES, BLOCK), 1)
    cur = tq >> 6
    forced = (blk == 0) | (blk == cur) | (blk == cur - 1)
    score = jnp.where(forced, FORCE_SCORE, imp_sel.T)
    score = jnp.where(blk <= cur, score, -1.0)
    blk_f = blk.astype(jnp.float32)

    def pick_one(_, carry):
        sc, sel = carry
        best = jnp.max(sc, axis=0, keepdims=True)
        first = jnp.min(jnp.where(sc == best, blk_f, float(LANES)), axis=0, keepdims=True)
        pick = blk_f == first
        return jnp.where(pick, -2.0, sc), jnp.where(pick, 1.0, sel)

    _, sel = lax.fori_loop(0, SEL_TOP_N, pick_one, (score, jnp.zeros_like(score)))
    bias = jnp.where((sel > 0.0) & (blk <= cur), 0.0, NEG_INF)
    mb_ref[0, 0] = bias.T.astype(mb_ref.dtype)


def _cmp_topk(qs, kcp, vcp):
    bg, nqb, rows, _ = qs.shape
    ncols = kcp.shape[1]
    return pl.pallas_call(
        _cmp_topk_kernel,
        grid=(bg, nqb),
        in_specs=[pl.BlockSpec((1, 1, rows, HEAD_DIM), lambda i, j: (i, j, 0, 0)),
                  pl.BlockSpec((1, ncols, HEAD_DIM), lambda i, j: (i, 0, 0)),
                  pl.BlockSpec((1, ncols, HEAD_DIM), lambda i, j: (i, 0, 0))],
        out_specs=[pl.BlockSpec((1, 1, rows, HEAD_DIM), lambda i, j: (i, j, 0, 0)),
                   pl.BlockSpec((1, 1, BLOCK, LANES), lambda i, j: (i, j, 0, 0))],
        out_shape=[jax.ShapeDtypeStruct((bg, nqb, rows, HEAD_DIM), jnp.bfloat16),
                   jax.ShapeDtypeStruct((bg, nqb, BLOCK, LANES), jnp.bfloat16)],
        compiler_params=pltpu.CompilerParams(dimension_semantics=("arbitrary", "arbitrary")),
        name="cmp_topk",
    )(qs, kcp, vcp)


def _slc_attn_kernel(q_ref, mb_ref, ka_ref, vt_ref, o_ref, s_a, s_b, p_a, p_b):
    qb = pl.program_id(1)
    q = q_ref[0, 0]
    rows = q.shape[0]
    mb = mb_ref[0, 0]
    qa = jnp.concatenate([jnp.concatenate([mb] * NSA_Q_PER_KV, axis=0), q], axis=1)
    n = (qb * BLOCK) // KEY_TILE
    last = jnp.maximum(n - 1, 0)

    def qk(kt, s_ref):
        start = pl.multiple_of(kt * KEY_TILE, KEY_TILE)
        st = _dot_nt(ka_ref[0, pl.ds(start, KEY_TILE), :], qa)
        s_ref[...] = st
        return jnp.max(st, axis=0, keepdims=True)

    def pv(kt, p_ref):
        start = pl.multiple_of(kt * KEY_TILE, KEY_TILE)
        return _dot(vt_ref[0, :, pl.ds(start, KEY_TILE)], p_ref[...])

    def softmax(m_old, mx, s_ref, p_ref):
        m_new = jnp.maximum(m_old, mx)
        p_ref[...] = jnp.exp2(s_ref[...] - m_new).astype(jnp.bfloat16)
        return m_new, jnp.exp2(m_old - m_new)

    st = _dot_nt(ka_ref[0, pl.ds(pl.multiple_of(n * KEY_TILE, KEY_TILE), KEY_TILE), :], qa)
    key = n * KEY_TILE + lax.broadcasted_iota(jnp.int32, (KEY_TILE, rows), 0)
    t = qb * BLOCK + (lax.broadcasted_iota(jnp.int32, (KEY_TILE, rows), 1) & (BLOCK - 1))
    st = jnp.where(key <= t, st, NEG_INF)
    m0 = jnp.max(st, axis=0, keepdims=True)
    p_b[...] = jnp.exp2(st - m0).astype(jnp.bfloat16)
    mx0 = qk(0, s_a)
    acc0 = jnp.zeros((vt_ref.shape[1], rows), jnp.float32)
    one = jnp.ones_like(m0)

    def body(i, carry):
        m, acc, alpha_prev, w_prev, kt_prev, mx = carry
        first, second = 2 * i, 2 * i + 1
        w_second = jnp.where(second < n, 1.0, 0.0)
        kt_second = jnp.minimum(second, last)
        acc = alpha_prev * acc + w_prev * pv(kt_prev, p_b)
        m, alpha = softmax(m, mx, s_a, p_a)
        mx = qk(kt_second, s_b)
        acc = alpha * acc + pv(first, p_a)
        m, alpha = softmax(m, mx, s_b, p_b)
        mx = qk(jnp.minimum(second + 1, last), s_a)
        return m, acc, alpha, w_second, kt_second, mx

    init = (m0, acc0, one, jnp.float32(1.0), n, mx0)
    _, acc, alpha_prev, w_prev, kt_prev, _ = lax.fori_loop(0, (n + 1) // 2, body, init)
    acc = alpha_prev * acc + w_prev * pv(kt_prev, p_b)
    o_ref[0, 0] = (acc[0:HEAD_DIM] / acc[HEAD_DIM:HEAD_DIM + 1]).astype(o_ref.dtype)


def _slc_attn(qsp, mb, kaug, vt):
    bg, nqb, rows, _ = qsp.shape
    seq = kaug.shape[1]
    return pl.pallas_call(
        _slc_attn_kernel,
        grid=(bg, nqb),
        in_specs=[pl.BlockSpec((1, 1, rows, LANES), lambda i, j: (i, j, 0, 0)),
                  pl.BlockSpec((1, 1, BLOCK, LANES), lambda i, j: (i, j, 0, 0)),
                  pl.BlockSpec((1, seq, 2 * LANES), lambda i, j: (i, 0, 0)),
                  pl.BlockSpec((1, vt.shape[1], seq), lambda i, j: (i, 0, 0))],
        out_specs=pl.BlockSpec((1, 1, HEAD_DIM, rows), lambda i, j: (i, j, 0, 0)),
        out_shape=jax.ShapeDtypeStruct((bg, nqb, HEAD_DIM, rows), jnp.bfloat16),
        scratch_shapes=[pltpu.VMEM((KEY_TILE, rows), jnp.float32),
                        pltpu.VMEM((KEY_TILE, rows), jnp.float32),
                        pltpu.VMEM((KEY_TILE, rows), jnp.bfloat16),
                        pltpu.VMEM((KEY_TILE, rows), jnp.bfloat16)],
        compiler_params=pltpu.CompilerParams(dimension_semantics=("arbitrary", "arbitrary"),
                                             vmem_limit_bytes=VMEM_LIMIT),
        name="slc_attn",
    )(qsp, mb, kaug, vt)


WIN_BLOCKS = -(-(SWA_WINDOW - 1) // BLOCK) + 1


def _win_attn_kernel(q_ref, k_ref, vt_ref, o_ref):
    qb = pl.program_id(1)
    q = q_ref[0, 0]
    rows = q.shape[0]
    span = WIN_BLOCKS * BLOCK
    first = jnp.maximum(qb - (WIN_BLOCKS - 1), 0)
    start = pl.multiple_of(first * BLOCK, BLOCK)
    st = _dot_nt(k_ref[0, pl.ds(start, span), :], q)
    key = first * BLOCK + lax.broadcasted_iota(jnp.int32, (span, rows), 0)
    t = qb * BLOCK + (lax.broadcasted_iota(jnp.int32, (span, rows), 1) & (BLOCK - 1))
    dist = t - key
    st = jnp.where((dist >= 0) & (dist <= SWA_WINDOW - 1), st, NEG_INF)
    m = jnp.max(st, axis=0, keepdims=True)
    e = jnp.exp2(st - m)
    den = jnp.sum(e, axis=0, keepdims=True)
    acc = _dot(vt_ref[0, :, pl.ds(start, span)], e.astype(jnp.bfloat16))
    o_ref[0, 0] = (acc / den).astype(o_ref.dtype)


def _win_attn(qs, kw, vwt):
    bg, nqb, rows, _ = qs.shape
    seq = kw.shape[1]
    return pl.pallas_call(
        _win_attn_kernel,
        grid=(bg, nqb),
        in_specs=[pl.BlockSpec((1, 1, rows, HEAD_DIM), lambda i, j: (i, j, 0, 0)),
                  pl.BlockSpec((1, seq, HEAD_DIM), lambda i, j: (i, 0, 0)),
                  pl.BlockSpec((1, HEAD_DIM, seq), lambda i, j: (i, 0, 0))],
        out_specs=pl.BlockSpec((1, 1, HEAD_DIM, rows), lambda i, j: (i, j, 0, 0)),
        out_shape=jax.ShapeDtypeStruct((bg, nqb, HEAD_DIM, rows), jnp.bfloat16),
        compiler_params=pltpu.CompilerParams(dimension_semantics=("arbitrary", "arbitrary")),
        name="win_attn",
    )(qs, kw, vwt)


def _dil_attn_kernel(q_ref, kp_ref, kc_ref, vp_ref, vc_ref, o_ref, lse_ref, *, window):
    i = pl.program_id(2)
    q = q_ref[0].astype(jnp.float32)
    k = jnp.concatenate([kp_ref[0], kc_ref[0]], axis=0).astype(jnp.float32)
    v = jnp.concatenate([vp_ref[0], vc_ref[0]], axis=0).astype(jnp.float32)
    nq, nk = q.shape[0], k.shape[0]
    row = lax.broadcasted_iota(jnp.int32, (nq, nk), 0)
    col = lax.broadcasted_iota(jnp.int32, (nq, nk), 1)
    dist = BLOCK + row - col
    valid = (dist >= 0) & (dist <= window) & ((col >= BLOCK) | (i >= 1))
    outs, lses = [], []
    for h in range(DIL_HEADS):
        sl = slice(h * HEAD_DIM, (h + 1) * HEAD_DIM)
        s = _dot_nt(q[:, sl].astype(jnp.bfloat16), k[:, sl].astype(jnp.bfloat16))
        s = jnp.where(valid, s, NEG_INF)
        m = jnp.max(s, axis=-1, keepdims=True)
        e = jnp.exp2(s - m)
        den = jnp.sum(e, axis=-1, keepdims=True)
        outs.append(_dot(e.astype(jnp.bfloat16), v[:, sl].astype(jnp.bfloat16)) / den)
        lses.append(jnp.broadcast_to(m * LN_2 + jnp.log(den), (nq, HEAD_DIM)))
    o_ref[0] = jnp.concatenate(outs, axis=1).astype(o_ref.dtype)
    lse_ref[0] = jnp.concatenate(lses, axis=1)


def _dil_attn(qv, kv, vv, dil, window):
    b, length, _ = qv.shape
    nblk = length // BLOCK
    cur = pl.BlockSpec((1, BLOCK, W_DIL), lambda bi, c, i: (bi, i, c))
    prev = pl.BlockSpec((1, BLOCK, W_DIL), lambda bi, c, i: (bi, jnp.maximum(i - 1, 0), c))
    return pl.pallas_call(
        functools.partial(_dil_attn_kernel, window=window),
        grid=(b, dil, nblk),
        in_specs=[cur, prev, cur, prev, cur],
        out_specs=[cur, cur],
        out_shape=[jax.ShapeDtypeStruct(qv.shape, jnp.bfloat16),
                   jax.ShapeDtypeStruct(qv.shape, jnp.float32)],
        compiler_params=pltpu.CompilerParams(
            dimension_semantics=("arbitrary", "arbitrary", "arbitrary")),
        name=f"dil_attn_d{dil}",
    )(qv, kv, kv, vv, vv)


def _out_proj_kernel(x_ref, oc_ref, os_ref, ow_ref, gate_ref, sza_ref,
                     o1_ref, o2_ref, o3_ref, l1_ref, l2_ref, l3_ref, szb_ref,
                     w_ref, g_ref, out_ref):
    f32 = jnp.float32
    gate = gate_ref[...].astype(f32)
    o_a = (gate[:, 0:W_NSA] * oc_ref[...].astype(f32)
           + gate[:, W_NSA:2 * W_NSA] * os_ref[...].astype(f32)
           + gate[:, 2 * W_NSA:3 * W_NSA] * ow_ref[...].astype(f32))
    mixed_a = (o_a * sza_ref[...].astype(f32)).astype(jnp.bfloat16)
    l1, l2, l3 = l1_ref[...], l2_ref[...], l3_ref[...]
    mx = jnp.maximum(jnp.maximum(l1, l2), l3)
    e1, e2, e3 = jnp.exp(l1 - mx), jnp.exp(l2 - mx), jnp.exp(l3 - mx)
    den = e1 + e2 + e3
    o_b = (e1 / den) * o1_ref[...].astype(f32) + (e2 / den) * o2_ref[...].astype(f32) \
        + (e3 / den) * o3_ref[...].astype(f32)
    mixed_b = (o_b * szb_ref[...].astype(f32)).astype(jnp.bfloat16)
    y = _dot(mixed_a, w_ref[0:W_NSA, :]) + _dot(mixed_b, w_ref[W_NSA:W_NSA + W_DIL, :])
    ms = jnp.mean(y * y, axis=-1, keepdims=True)
    out_ref[...] = x_ref[...] + y * lax.rsqrt(ms + RMS_EPS) * g_ref[...]


def _out_proj(x2, oc, os_, ow, gate, sza, o1, o2, o3, l1, l2, l3, szb, w, g):
    rows, d_model = x2.shape
    tm = PROJ_ROWS
    row_spec = lambda wd: pl.BlockSpec((tm, wd), lambda i: (i, 0))
    return pl.pallas_call(
        _out_proj_kernel,
        grid=(rows // tm,),
        in_specs=[row_spec(d_model), row_spec(W_NSA), row_spec(W_NSA), row_spec(W_NSA),
                  row_spec(3 * W_NSA), row_spec(W_NSA),
                  row_spec(W_DIL), row_spec(W_DIL), row_spec(W_DIL),
                  row_spec(W_DIL), row_spec(W_DIL), row_spec(W_DIL), row_spec(W_DIL),
                  pl.BlockSpec(w.shape, lambda i: (0, 0)),
                  pl.BlockSpec((1, d_model), lambda i: (0, 0))],
        out_specs=row_spec(d_model),
        out_shape=jax.ShapeDtypeStruct((rows, d_model), jnp.float32),
        compiler_params=pltpu.CompilerParams(dimension_semantics=("arbitrary",),
                                             vmem_limit_bytes=VMEM_LIMIT),
        name="out_proj",
    )(x2, oc, os_, ow, gate, sza, o1, o2, o3, l1, l2, l3, szb, w, g)


def _rope_tables(positions):
    inv = 1.0 / (ROPE_THETA ** (jnp.arange(0, ROPE_DIMS, 2, dtype=jnp.float32) / ROPE_DIMS))
    ang = positions.astype(jnp.float32).reshape(-1)[:, None] * inv
    cos, sin = jnp.cos(ang), jnp.sin(ang)
    rest = HEAD_DIM - ROPE_DIMS
    one = jnp.ones((cos.shape[0], rest), jnp.float32)
    zero = jnp.zeros((cos.shape[0], rest), jnp.float32)
    zh = jnp.zeros_like(sin)
    per_head = lambda lo, hi, fill: jnp.tile(jnp.concatenate([lo, hi, fill], axis=1), (1, 2))
    return per_head(cos, cos, one), per_head(-sin, zh, zero), per_head(zh, sin, zero)


def _in_proj_weights(w_in):
    scale = HEAD_DIM ** -0.5 * LOG2_E
    o = 0
    cols = {}
    for name, width in (("qa", W_NSA), ("kva", 6 * W_KV), ("gate", 3 * NSA_HEADS), ("za", W_NSA),
                        ("qb", W_DIL), ("kb", W_DIL), ("vb", W_DIL), ("zb", W_DIL)):
        cols[name] = w_in[:, o:o + width]
        o += width
    gate_wide = jnp.repeat(cols["gate"], HEAD_DIM, axis=1)
    w = jnp.concatenate([cols["qa"] * scale, cols["kva"], gate_wide, cols["za"],
                         cols["qb"] * scale, cols["kb"], cols["vb"], cols["zb"]], axis=1)
    return w.astype(jnp.bfloat16)


def kernel(x, positions, pre_norm_g, w_in, cmp_k_pos, cmp_k_w1, cmp_k_w2,
           cmp_v_pos, cmp_v_w1, cmp_v_w2, w_out, post_norm_g):
    B, S, d_model = x.shape
    G, R, D = NSA_KV_HEADS, NSA_Q_PER_KV, HEAD_DIM
    depth = w_in.shape[0]
    n_sel = S // SEL_BLOCK
    nqb = S // BLOCK
    n_chunks = S // CMP_STRIDE
    assert S % KEY_TILE == 0 and n_sel <= MAX_SEL_BLOCKS and S >= WIN_BLOCKS * BLOCK
    assert all(S % (BLOCK * dil) == 0 and win // dil == BLOCK for win, dil in DIL_PATTERNS)
    bf16 = jnp.bfloat16

    rope_c, rope_a, rope_b = _rope_tables(positions)
    onehot = (jnp.arange(S)[:, None] // SEL_BLOCK == jnp.arange(LANES)[None, :]).astype(bf16)
    x2 = x.reshape(B * S, d_model)

    for layer in range(depth):
        qa, kva, gate, sza, qb_, kb_, vb_, szb = _in_proj(
            x2, pre_norm_g[layer][None, :], _in_proj_weights(w_in[layer]), rope_c, rope_a, rope_b)

        qs = qa.reshape(B, nqb, BLOCK, G, R, D).transpose(0, 3, 1, 4, 2, 5).reshape(B * G, nqb, R * BLOCK, D)
        qsp = jnp.pad(qs, ((0, 0), (0, 0), (0, 0), (0, LANES - D)))
        kv6 = kva.reshape(B, S, 6, G, D).transpose(2, 0, 3, 1, 4).reshape(6, B * G, S, D)

        chunks = kv6[0:2].reshape(2, B * G, n_chunks, CMP_STRIDE * D)
        pos = jnp.stack([cmp_k_pos[layer], cmp_v_pos[layer]]).reshape(2, 1, CMP_LEN * D)
        pos8 = jnp.broadcast_to(pos, (2, 16, CMP_LEN * D)).astype(bf16)
        w1 = jnp.stack([cmp_k_w1[layer], cmp_v_w1[layer]]).astype(bf16)
        w2 = jnp.stack([cmp_k_w2[layer], cmp_v_w2[layer]]).astype(bf16)
        half = CMP_STRIDE * D
        kvc = _compress(chunks, pos8, w1[:, :half], w1[:, half:], w2)
        kvc = kvc.reshape(2, B * G, n_sel, CMP_PER_SEL, D)
        kvc = jnp.pad(kvc, ((0, 0), (0, 0), (0, MAX_SEL_BLOCKS - n_sel), (0, 0), (0, 0)))
        kvc = kvc.transpose(0, 1, 3, 2, 4).reshape(2, B * G, CMP_PER_SEL * MAX_SEL_BLOCKS, D).astype(bf16)

        o_cmp, mask_bias = _cmp_topk(qs, kvc[0], kvc[1])
        kaug = jnp.concatenate([jnp.broadcast_to(onehot, (B * G, S, LANES)), kv6[2],
                                jnp.zeros((B * G, S, LANES - D), bf16)], axis=-1)
        vt_aug = jnp.concatenate([kv6[3].transpose(0, 2, 1), jnp.ones((B * G, 1, S), bf16),
                                  jnp.zeros((B * G, 15, S), bf16)], axis=1)
        o_slc = _slc_attn(qsp, mask_bias, kaug, vt_aug)
        o_win = _win_attn(qs, kv6[4], kv6[5].transpose(0, 2, 1))
        unhead = lambda o: o.reshape(B, G, nqb, R, BLOCK, D).transpose(0, 2, 4, 1, 3, 5).reshape(B * S, W_NSA)
        unhead_t = lambda o: o.reshape(B, G, nqb, D, R, BLOCK).transpose(0, 2, 5, 1, 4, 3).reshape(B * S, W_NSA)

        o_b, lse_b = [], []
        for window, dil in DIL_PATTERNS:
            view = lambda t: t.reshape(B, S // dil, dil * W_DIL)
            o, lse = _dil_attn(view(qb_), view(kb_), view(vb_), dil, window // dil)
            o_b.append(o.reshape(B * S, W_DIL))
            lse_b.append(lse.reshape(B * S, W_DIL))

        x2 = _out_proj(x2, unhead(o_cmp), unhead_t(o_slc), unhead_t(o_win), gate, sza,
                       o_b[0], o_b[1], o_b[2], lse_b[0], lse_b[1], lse_b[2], szb,
                       w_out[layer].astype(bf16), post_norm_g[layer][None, :])
    return x2.reshape(B, S, d_model)
```

```python
import jax
import jax.numpy as jnp
from jax import lax
from jax.experimental import pallas as pl
from jax.experimental.pallas import tpu as pltpu

HEAD_DIM = 64
NSA_HEADS = 8
NSA_KV_HEADS = 2
NSA_Q_PER_KV = NSA_HEADS // NSA_KV_HEADS
DIL_HEADS = 8
W_NSA = NSA_HEADS * HEAD_DIM
W_KV = NSA_KV_HEADS * HEAD_DIM
W_DIL = DIL_HEADS * HEAD_DIM
CMP_LEN = 32
CMP_STRIDE = 16
CMP_HIDDEN = 256
SEL_BLOCK = 64
SEL_TOP_N = 16
SWA_WINDOW = 512
DIL_PATTERNS = ((128, 1), (512, 4), (2048, 16))
BLOCK = 128
ROPE_THETA = 500000.0
ROPE_DIMS = HEAD_DIM // 4
RMS_EPS = 1e-6
NEG_INF = -1e30
FORCE_SCORE = 1e4
LOG2_E = 1.4426950408889634
LN_2 = 0.6931471805599453

LANES = 128
VMEM_LIMIT = 56 * 1024 * 1024
MAX_SEL_BLOCKS = LANES
CMP_PER_SEL = SEL_BLOCK // CMP_STRIDE
KEY_TILE = 512
PROJ_ROWS = 512

_NT = (((1,), (1,)), ((), ()))


def _dot(a, b):
    return jnp.dot(a, b, preferred_element_type=jnp.float32)


def _dot_nt(a, b):
    return lax.dot_general(a, b, _NT, preferred_element_type=jnp.float32)


def _sigmoid(x):
    return 1.0 / (1.0 + jnp.exp(-x))


def _rope(x, c, a, b):
    width = x.shape[1]
    reps = width // LANES
    ct = jnp.tile(c, (1, reps))
    at = jnp.tile(a, (1, reps))
    bt = jnp.tile(b, (1, reps))
    half = ROPE_DIMS // 2
    return x * ct + pltpu.roll(x, width - half, 1) * at + pltpu.roll(x, half, 1) * bt


def _in_proj_kernel(x_ref, g_ref, w_ref, c_ref, a_ref, b_ref,
                    qa_ref, kva_ref, gate_ref, sza_ref, qb_ref, kb_ref, vb_ref, szb_ref):
    x = x_ref[...]
    ms = jnp.mean(x * x, axis=-1, keepdims=True)
    h = (x * lax.rsqrt(ms + RMS_EPS) * g_ref[...]).astype(jnp.bfloat16)
    c = c_ref[...]
    a = a_ref[...]
    b = b_ref[...]
    off = 0

    def proj(width):
        nonlocal off
        r = _dot(h, w_ref[:, off:off + width])
        off += width
        return r

    qa_ref[...] = _rope(proj(W_NSA), c, a, b).astype(qa_ref.dtype)
    kva = proj(6 * W_KV)
    parts = []
    for i in range(6):
        p = kva[:, i * W_KV:(i + 1) * W_KV]
        parts.append(_rope(p, c, a, b) if i % 2 == 0 else p)
    kva_ref[...] = jnp.concatenate(parts, axis=1).astype(kva_ref.dtype)
    gate_ref[...] = _sigmoid(proj(3 * W_NSA)).astype(gate_ref.dtype)
    za = proj(W_NSA)
    sza_ref[...] = (za * _sigmoid(za)).astype(sza_ref.dtype)
    qb_ref[...] = _rope(proj(W_DIL), c, a, b).astype(qb_ref.dtype)
    kb_ref[...] = _rope(proj(W_DIL), c, a, b).astype(kb_ref.dtype)
    vb_ref[...] = proj(W_DIL).astype(vb_ref.dtype)
    zb = proj(W_DIL)
    szb_ref[...] = (zb * _sigmoid(zb)).astype(szb_ref.dtype)


def _in_proj(x2, g, w, c, a, b):
    rows, d_model = x2.shape
    n_total = w.shape[1]
    tm = PROJ_ROWS
    widths = (W_NSA, 6 * W_KV, 3 * W_NSA, W_NSA, W_DIL, W_DIL, W_DIL, W_DIL)
    row_spec = lambda wd: pl.BlockSpec((tm, wd), lambda i: (i, 0))
    return pl.pallas_call(
        _in_proj_kernel,
        grid=(rows // tm,),
        in_specs=[row_spec(d_model),
                  pl.BlockSpec((1, d_model), lambda i: (0, 0)),
                  pl.BlockSpec((d_model, n_total), lambda i: (0, 0)),
                  row_spec(LANES), row_spec(LANES), row_spec(LANES)],
        out_specs=[row_spec(wd) for wd in widths],
        out_shape=[jax.ShapeDtypeStruct((rows, wd), jnp.bfloat16) for wd in widths],
        compiler_params=pltpu.CompilerParams(dimension_semantics=("arbitrary",),
                                             vmem_limit_bytes=VMEM_LIMIT),
        name="in_proj",
    )(x2, g, w, c, a, b)


def _compress_kernel(ch_ref, pos_ref, w1t_ref, w1b_ref, w2_ref, o_ref):
    ch = ch_ref[0, 0]
    n = ch.shape[0]
    half = pos_ref.shape[2] // 2
    pos = pos_ref[0]
    top = _dot(ch, w1t_ref[0])
    bot = _dot(ch, w1b_ref[0])
    bias = _dot(pos[:, :half], w1t_ref[0]) + _dot(pos[:, half:], w1b_ref[0])
    hid = top + pltpu.roll(bot, n - 1, 0) + bias[0:1, :]
    act = (hid * _sigmoid(hid)).astype(jnp.bfloat16)
    o_ref[0, 0] = _dot(act, w2_ref[0])


def _compress(chunks, pos8, w1t, w1b, w2):
    _, bg, n, width = chunks.shape
    return pl.pallas_call(
        _compress_kernel,
        grid=(2, bg),
        in_specs=[pl.BlockSpec((1, 1, n, width), lambda s, i: (s, i, 0, 0)),
                  pl.BlockSpec((1, 16, 2 * width), lambda s, i: (s, 0, 0)),
                  pl.BlockSpec((1, width, CMP_HIDDEN), lambda s, i: (s, 0, 0)),
                  pl.BlockSpec((1, width, CMP_HIDDEN), lambda s, i: (s, 0, 0)),
                  pl.BlockSpec((1, CMP_HIDDEN, HEAD_DIM), lambda s, i: (s, 0, 0))],
        out_specs=pl.BlockSpec((1, 1, n, HEAD_DIM), lambda s, i: (s, i, 0, 0)),
        out_shape=jax.ShapeDtypeStruct((2, bg, n, HEAD_DIM), jnp.float32),
        compiler_params=pltpu.CompilerParams(dimension_semantics=("arbitrary", "arbitrary")),
        name="compress",
    )(chunks, pos8, w1t, w1b, w2)


def _cmp_topk_kernel(q_ref, kc_ref, vc_ref, o_ref, mb_ref):
    qb = pl.program_id(1)
    q = q_ref[0, 0]
    rows = q.shape[0]
    ncols = kc_ref.shape[1]
    s = _dot_nt(q, kc_ref[0])
    row = lax.broadcasted_iota(jnp.int32, (rows, ncols), 0)
    col = lax.broadcasted_iota(jnp.int32, (rows, ncols), 1)
    t = qb * BLOCK + (row & (BLOCK - 1))
    cmp_end = (col & (LANES - 1)) * SEL_BLOCK + (col >> 7) * CMP_STRIDE + (CMP_LEN - 1)
    valid = cmp_end <= t
    s = jnp.where(valid, s, NEG_INF)
    m = jnp.max(s, axis=-1, keepdims=True)
    p = jnp.where(valid, jnp.exp2(s - m), 0.0)
    p = p / jnp.maximum(jnp.sum(p, axis=-1, keepdims=True), 1e-30)
    o_ref[0, 0] = _dot(p.astype(jnp.bfloat16), vc_ref[0]).astype(o_ref.dtype)

    imp = p[0:BLOCK]
    for r in range(1, NSA_Q_PER_KV):
        imp = imp + p[r * BLOCK:(r + 1) * BLOCK]
    g = [imp[:, i * LANES:(i + 1) * LANES] for i in range(CMP_PER_SEL)]
    lane = lax.broadcasted_iota(jnp.int32, (BLOCK, LANES), 1)
    prev_last = jnp.where(lane == 0, 0.0, pltpu.roll(g[3], 1, 1))
    imp_sel = prev_last + 2.0 * (g[0] + g[1] + g[2]) + g[3]

    blk = lax.broadcasted_iota(jnp.int32, (LANES, BLOCK), 0)
    tq = qb * BLOCK + lax.broadcasted_iota(jnp.int32, (LANES, BLOCK), 1)
    cur = tq >> 6
    forced = (blk == 0) | (blk == cur) | (blk == cur - 1)
    score = jnp.where(forced, FORCE_SCORE, imp_sel.T)
    score = jnp.where(blk <= cur, score, -1.0)
    blk_f = blk.astype(jnp.float32)

    def pick_one(_, carry):
        sc, sel = carry
        best = jnp.max(sc, axis=0, keepdims=True)
        first = jnp.min(jnp.where(sc == best, blk_f, float(LANES)), axis=0, keepdims=True)
        pick = blk_f == first
        return jnp.where(pick, -2.0, sc), jnp.where(pick, 1.0, sel)

    _, sel = lax.fori_loop(0, SEL_TOP_N, pick_one, (score, jnp.zeros_like(score)))
    bias = jnp.where((sel > 0.0) & (blk <= cur), 0.0, NEG_INF)
    mb_ref[0, 0] = bias.T.astype(mb_ref.dtype)


def _cmp_topk(qs, kcp, vcp):
    bg, nqb, rows, _ = qs.shape
    ncols = kcp.shape[1]
    return pl.pallas_call(
        _cmp_topk_kernel,
        grid=(bg, nqb),
        in_specs=[pl.BlockSpec((1, 1, rows, HEAD_DIM), lambda i, j: (i, j, 0, 0)),
                  pl.BlockSpec((1, ncols, HEAD_DIM), lambda i, j: (i, 0, 0)),
                  pl.BlockSpec((1, ncols, HEAD_DIM), lambda i, j: (i, 0, 0))],
        out_specs=[pl.BlockSpec((1, 1, rows, HEAD_DIM), lambda i, j: (i, j, 0, 0)),
                   pl.BlockSpec((1, 1, BLOCK, LANES), lambda i, j: (i, j, 0, 0))],
        out_shape=[jax.ShapeDtypeStruct((bg, nqb, rows, HEAD_DIM), jnp.bfloat16),
                   jax.ShapeDtypeStruct((bg, nqb, BLOCK, LANES), jnp.bfloat16)],
        compiler_params=pltpu.CompilerParams(dimension_semantics=("arbitrary", "arbitrary")),
        name="cmp_topk",
    )(qs, kcp, vcp)


def _slc_attn_kernel(q_ref, mb_ref, ka_ref, vt_ref, o_ref, s_a, s_b, p_a, p_b):
    qb = pl.program_id(1)
    q = q_ref[0, 0]
    rows = q.shape[0]
    mb = mb_ref[0, 0]
    qa = jnp.concatenate([jnp.concatenate([mb] * NSA_Q_PER_KV, axis=0), q], axis=1)
    n = (qb * BLOCK) // KEY_TILE
    last = jnp.maximum(n - 1, 0)

    def qk(kt, s_ref):
        start = pl.multiple_of(kt * KEY_TILE, KEY_TILE)
        st = _dot_nt(ka_ref[0, pl.ds(start, KEY_TILE), :], qa)
        s_ref[...] = st
        return jnp.max(st, axis=0, keepdims=True)

    def pv(kt, p_ref):
        start = pl.multiple_of(kt * KEY_TILE, KEY_TILE)
        return _dot(vt_ref[0, :, pl.ds(start, KEY_TILE)], p_ref[...])

    def softmax(m_old, mx, s_ref, p_ref):
        m_new = jnp.maximum(m_old, mx)
        p_ref[...] = jnp.exp2(s_ref[...] - m_new).astype(jnp.bfloat16)
        return m_new, jnp.exp2(m_old - m_new)

    st = _dot_nt(ka_ref[0, pl.ds(pl.multiple_of(n * KEY_TILE, KEY_TILE), KEY_TILE), :], qa)
    key = n * KEY_TILE + lax.broadcasted_iota(jnp.int32, (KEY_TILE, rows), 0)
    t = qb * BLOCK + (lax.broadcasted_iota(jnp.int32, (KEY_TILE, rows), 1) & (BLOCK - 1))
    st = jnp.where(key <= t, st, NEG_INF)
    m0 = jnp.max(st, axis=0, keepdims=True)
    p_b[...] = jnp.exp2(st - m0).astype(jnp.bfloat16)
    mx0 = qk(0, s_a)
    acc0 = jnp.zeros((vt_ref.shape[1], rows), jnp.float32)
    one = jnp.ones_like(m0)

    def body(i, carry):
        m, acc, alpha_prev, w_prev, kt_prev, mx = carry
        first, second = 2 * i, 2 * i + 1
        w_second = jnp.where(second < n, 1.0, 0.0)
        kt_second = jnp.minimum(second, last)
        acc = alpha_prev * acc + w_prev * pv(kt_prev, p_b)
        m, alpha = softmax(m, mx, s_a, p_a)
        mx = qk(kt_second, s_b)
        acc = alpha * acc + pv(first, p_a)
        m, alpha = softmax(m, mx, s_b, p_b)
        mx = qk(jnp.minimum(second + 1, last), s_a)
        return m, acc, alpha, w_second, kt_second, mx

    init = (m0, acc0, one, jnp.float32(1.0), n, mx0)
    _, acc, alpha_prev, w_prev, kt_prev, _ = lax.fori_loop(0, (n + 1) // 2, body, init)
    acc = alpha_prev * acc + w_prev * pv(kt_prev, p_b)
    o_ref[0, 0] = (acc[0:HEAD_DIM] / acc[HEAD_DIM:HEAD_DIM + 1]).astype(o_ref.dtype)


def _slc_attn(qsp, mb, kaug, vt):
    bg, nqb, rows, _ = qsp.shape
    seq = kaug.shape[1]
    return pl.pallas_call(
        _slc_attn_kernel,
        grid=(bg, nqb),
        in_specs=[pl.BlockSpec((1, 1, rows, LANES), lambda i, j: (i, j, 0, 0)),
                  pl.BlockSpec((1, 1, BLOCK, LANES), lambda i, j: (i, j, 0, 0)),
                  pl.BlockSpec((1, seq, 2 * LANES), lambda i, j: (i, 0, 0)),
                  pl.BlockSpec((1, vt.shape[1], seq), lambda i, j: (i, 0, 0))],
        out_specs=pl.BlockSpec((1, 1, HEAD_DIM, rows), lambda i, j: (i, j, 0, 0)),
        out_shape=jax.ShapeDtypeStruct((bg, nqb, HEAD_DIM, rows), jnp.bfloat16),
        scratch_shapes=[pltpu.VMEM((KEY_TILE, rows), jnp.float32),
                        pltpu.VMEM((KEY_TILE, rows), jnp.float32),
                        pltpu.VMEM((KEY_TILE, rows), jnp.bfloat16),
                        pltpu.VMEM((KEY_TILE, rows), jnp.bfloat16)],
        compiler_params=pltpu.CompilerParams(dimension_semantics=("arbitrary", "arbitrary"),
                                             vmem_limit_bytes=VMEM_LIMIT),
        name="slc_attn",
    )(qsp, mb, kaug, vt)


WIN_BLOCKS = -(-(SWA_WINDOW - 1) // BLOCK) + 1


def _win_attn_kernel(q_ref, k_ref, vt_ref, o_ref):
    qb = pl.program_id(1)
    q = q_ref[0, 0]
    rows = q.shape[0]
    span = WIN_BLOCKS * BLOCK
    first = jnp.maximum(qb - (WIN_BLOCKS - 1), 0)
    start = pl.multiple_of(first * BLOCK, BLOCK)
    st = _dot_nt(k_ref[0, pl.ds(start, span), :], q)
    key = first * BLOCK + lax.broadcasted_iota(jnp.int32, (span, rows), 0)
    t = qb * BLOCK + (lax.broadcasted_iota(jnp.int32, (span, rows), 1) & (BLOCK - 1))
    dist = t - key
    st = jnp.where((dist >= 0) & (dist <= SWA_WINDOW - 1), st, NEG_INF)
    m = jnp.max(st, axis=0, keepdims=True)
    e = jnp.exp2(st - m)
    den = jnp.sum(e, axis=0, keepdims=True)
    acc = _dot(vt_ref[0, :, pl.ds(start, span)], e.astype(jnp.bfloat16))
    o_ref[0, 0] = (acc / den).astype(o_ref.dtype)


def _win_attn(qs, kw, vwt):
    bg, nqb, rows, _ = qs.shape
    seq = kw.shape[1]
    return pl.pallas_call(
        _win_attn_kernel,
        grid=(bg, nqb),
        in_specs=[pl.BlockSpec((1, 1, rows, HEAD_DIM), lambda i, j: (i, j, 0, 0)),
                  pl.BlockSpec((1, seq, HEAD_DIM), lambda i, j: (i, 0, 0)),
                  pl.BlockSpec((1, HEAD_DIM, seq), lambda i, j: (i, 0, 0))],
        out_specs=pl.BlockSpec((1, 1, HEAD_DIM, rows), lambda i, j: (i, j, 0, 0)),
        out_shape=jax.ShapeDtypeStruct((bg, nqb, HEAD_DIM, rows), jnp.bfloat16),
        compiler_params=pltpu.CompilerParams(dimension_semantics=("arbitrary", "arbitrary")),
        name="win_attn",
    )(qs, kw, vwt)


DIL_MAX = max(d for _, d in DIL_PATTERNS)
DIL_SUPER = BLOCK * DIL_MAX
DIL_UNITS = DIL_SUPER // BLOCK
DIL_UNROLL = 4
HEAD_PAIR = 2 * HEAD_DIM
MIX_ROWS = 256


def _dil_mix_kernel(q_ref, kp_ref, kc_ref, vp_ref, vc_ref, z_ref, o_ref,
                    qf, kf, vf, o_scr, lse_scr, bias_scr):
    f32, bf16 = jnp.float32, jnp.bfloat16
    sb = pl.program_id(1)
    qf[...] = q_ref[...].astype(f32)
    kf[0:DIL_SUPER] = kp_ref[...].astype(f32)
    kf[DIL_SUPER:2 * DIL_SUPER] = kc_ref[...].astype(f32)
    vf[0:DIL_SUPER] = vp_ref[...].astype(f32)
    vf[DIL_SUPER:2 * DIL_SUPER] = vc_ref[...].astype(f32)

    row = lax.broadcasted_iota(jnp.int32, (2 * BLOCK, 2 * BLOCK), 0)
    col = lax.broadcasted_iota(jnp.int32, (2 * BLOCK, 2 * BLOCK), 1)
    dist = BLOCK + (row & (BLOCK - 1)) - col
    band = (dist >= 0) & (dist <= BLOCK)
    bias_scr[0] = jnp.where(band, 0.0, NEG_INF)
    bias_scr[1] = jnp.where(band & (col >= BLOCK), 0.0, NEG_INF)
    first_head = lax.broadcasted_iota(jnp.int32, (BLOCK, HEAD_PAIR), 1) < HEAD_DIM
    ones = jnp.ones((2 * BLOCK, HEAD_PAIR), bf16)

    for pat, (window, dil) in enumerate(DIL_PATTERNS):
        shift = dil.bit_length() - 1

        def unit(u, pat=pat, dil=dil, shift=shift):
            cls = u & (dil - 1)
            blk = u >> shift
            q_start = cls + blk * (BLOCK * dil)
            k_start = DIL_SUPER + q_start - BLOCK * dil
            q2 = qf[pl.ds(q_start, BLOCK, stride=dil), :]
            k2 = kf[pl.ds(k_start, 2 * BLOCK, stride=dil), :]
            v2 = vf[pl.ds(k_start, 2 * BLOCK, stride=dil), :]
            qm = jnp.concatenate([jnp.where(first_head, q2, 0.0),
                                  jnp.where(first_head, 0.0, q2)], axis=0).astype(bf16)
            s = _dot_nt(qm, k2.astype(bf16))
            no_prev = jnp.where((sb == 0) & (blk == 0), 1, 0)
            s = s + bias_scr[no_prev]
            m = jnp.max(s, axis=1, keepdims=True)
            e = jnp.exp2(s - m).astype(bf16)
            r = _dot(e, jnp.concatenate([v2.astype(bf16), ones], axis=1))
            mb = jnp.broadcast_to(m, (2 * BLOCK, HEAD_PAIR)) * LN_2
            den_a, den_b = r[0:BLOCK, HEAD_PAIR:], r[BLOCK:, HEAD_PAIR:]
            o2 = jnp.where(first_head, r[0:BLOCK, 0:HEAD_PAIR] / den_a, r[BLOCK:, 0:HEAD_PAIR] / den_b)
            lse2 = jnp.where(first_head, mb[0:BLOCK] + jnp.log(den_a), mb[BLOCK:] + jnp.log(den_b))
            o_scr[pat, pl.ds(q_start, BLOCK, stride=dil), :] = o2
            lse_scr[pat, pl.ds(q_start, BLOCK, stride=dil), :] = lse2

        def trip(it, carry, unit=unit):
            for j in range(DIL_UNROLL):
                unit(it * DIL_UNROLL + j)
            return carry

        lax.fori_loop(0, DIL_UNITS // DIL_UNROLL, trip, 0)

    def mix(ci, carry):
        rows = pl.ds(pl.multiple_of(ci * MIX_ROWS, MIX_ROWS), MIX_ROWS)
        l1, l2, l3 = lse_scr[0, rows, :], lse_scr[1, rows, :], lse_scr[2, rows, :]
        mx = jnp.maximum(jnp.maximum(l1, l2), l3)
        e1, e2, e3 = jnp.exp(l1 - mx), jnp.exp(l2 - mx), jnp.exp(l3 - mx)
        den = e1 + e2 + e3
        o = (e1 / den) * o_scr[0, rows, :] + (e2 / den) * o_scr[1, rows, :] + (e3 / den) * o_scr[2, rows, :]
        o_ref[rows, :] = (o * z_ref[rows, :].astype(f32)).astype(o_ref.dtype)
        return carry

    lax.fori_loop(0, DIL_SUPER // MIX_ROWS, mix, 0)


def _dil_mix(qb, kb, vb, szb, batch):
    rows = qb.shape[0]
    nsb = rows // batch // DIL_SUPER
    cur = pl.BlockSpec((DIL_SUPER, HEAD_PAIR), lambda b, s, h: (b * nsb + s, h))
    prev = pl.BlockSpec((DIL_SUPER, HEAD_PAIR), lambda b, s, h: (b * nsb + jnp.maximum(s - 1, 0), h))
    f32 = jnp.float32
    return pl.pallas_call(
        _dil_mix_kernel,
        grid=(batch, nsb, W_DIL // HEAD_PAIR),
        in_specs=[cur, prev, cur, prev, cur, cur],
        out_specs=cur,
        out_shape=jax.ShapeDtypeStruct((rows, W_DIL), jnp.bfloat16),
        scratch_shapes=[pltpu.VMEM((DIL_SUPER, HEAD_PAIR), f32),
                        pltpu.VMEM((2 * DIL_SUPER, HEAD_PAIR), f32),
                        pltpu.VMEM((2 * DIL_SUPER, HEAD_PAIR), f32),
                        pltpu.VMEM((len(DIL_PATTERNS), DIL_SUPER, HEAD_PAIR), f32),
                        pltpu.VMEM((len(DIL_PATTERNS), DIL_SUPER, HEAD_PAIR), f32),
                        pltpu.VMEM((2, 2 * BLOCK, 2 * BLOCK), f32)],
        compiler_params=pltpu.CompilerParams(
            dimension_semantics=("arbitrary", "arbitrary", "arbitrary"),
            vmem_limit_bytes=VMEM_LIMIT),
        name="dil_mix",
    )(qb, kb, kb, vb, vb, szb)


def _out_proj_kernel(x_ref, oc_ref, os_ref, ow_ref, gate_ref, sza_ref, mb_ref, w_ref, g_ref, out_ref):
    f32 = jnp.float32
    gate = gate_ref[...].astype(f32)
    o_a = (gate[:, 0:W_NSA] * oc_ref[...].astype(f32)
           + gate[:, W_NSA:2 * W_NSA] * os_ref[...].astype(f32)
           + gate[:, 2 * W_NSA:3 * W_NSA] * ow_ref[...].astype(f32))
    mixed_a = (o_a * sza_ref[...].astype(f32)).astype(jnp.bfloat16)
    y = _dot(mixed_a, w_ref[0:W_NSA, :]) + _dot(mb_ref[...], w_ref[W_NSA:W_NSA + W_DIL, :])
    ms = jnp.mean(y * y, axis=-1, keepdims=True)
    out_ref[...] = x_ref[...] + y * lax.rsqrt(ms + RMS_EPS) * g_ref[...]


def _out_proj(x2, oc, os_, ow, gate, sza, mixed_b, w, g):
    rows, d_model = x2.shape
    tm = PROJ_ROWS
    row_spec = lambda wd: pl.BlockSpec((tm, wd), lambda i: (i, 0))
    return pl.pallas_call(
        _out_proj_kernel,
        grid=(rows // tm,),
        in_specs=[row_spec(d_model), row_spec(W_NSA), row_spec(W_NSA), row_spec(W_NSA),
                  row_spec(3 * W_NSA), row_spec(W_NSA), row_spec(W_DIL),
                  pl.BlockSpec(w.shape, lambda i: (0, 0)),
                  pl.BlockSpec((1, d_model), lambda i: (0, 0))],
        out_specs=row_spec(d_model),
        out_shape=jax.ShapeDtypeStruct((rows, d_model), jnp.float32),
        compiler_params=pltpu.CompilerParams(dimension_semantics=("arbitrary",),
                                             vmem_limit_bytes=VMEM_LIMIT),
        name="out_proj",
    )(x2, oc, os_, ow, gate, sza, mixed_b, w, g)


def _rope_tables(positions):
    inv = 1.0 / (ROPE_THETA ** (jnp.arange(0, ROPE_DIMS, 2, dtype=jnp.float32) / ROPE_DIMS))
    ang = positions.astype(jnp.float32).reshape(-1)[:, None] * inv
    cos, sin = jnp.cos(ang), jnp.sin(ang)
    rest = HEAD_DIM - ROPE_DIMS
    one = jnp.ones((cos.shape[0], rest), jnp.float32)
    zero = jnp.zeros((cos.shape[0], rest), jnp.float32)
    zh = jnp.zeros_like(sin)
    per_head = lambda lo, hi, fill: jnp.tile(jnp.concatenate([lo, hi, fill], axis=1), (1, 2))
    return per_head(cos, cos, one), per_head(-sin, zh, zero), per_head(zh, sin, zero)


def _in_proj_weights(w_in):
    scale = HEAD_DIM ** -0.5 * LOG2_E
    o = 0
    cols = {}
    for name, width in (("qa", W_NSA), ("kva", 6 * W_KV), ("gate", 3 * NSA_HEADS), ("za", W_NSA),
                        ("qb", W_DIL), ("kb", W_DIL), ("vb", W_DIL), ("zb", W_DIL)):
        cols[name] = w_in[:, o:o + width]
        o += width
    gate_wide = jnp.repeat(cols["gate"], HEAD_DIM, axis=1)
    w = jnp.concatenate([cols["qa"] * scale, cols["kva"], gate_wide, cols["za"],
                         cols["qb"] * scale, cols["kb"], cols["vb"], cols["zb"]], axis=1)
    return w.astype(jnp.bfloat16)


def kernel(x, positions, pre_norm_g, w_in, cmp_k_pos, cmp_k_w1, cmp_k_w2,
           cmp_v_pos, cmp_v_w1, cmp_v_w2, w_out, post_norm_g):
    B, S, d_model = x.shape
    G, R, D = NSA_KV_HEADS, NSA_Q_PER_KV, HEAD_DIM
    depth = w_in.shape[0]
    n_sel = S // SEL_BLOCK
    nqb = S // BLOCK
    n_chunks = S // CMP_STRIDE
    assert S % KEY_TILE == 0 and n_sel <= MAX_SEL_BLOCKS and S >= WIN_BLOCKS * BLOCK
    assert S % DIL_SUPER == 0 and all(win // dil == BLOCK for win, dil in DIL_PATTERNS)
    bf16 = jnp.bfloat16

    rope_c, rope_a, rope_b = _rope_tables(positions)
    onehot = (jnp.arange(S)[:, None] // SEL_BLOCK == jnp.arange(LANES)[None, :]).astype(bf16)
    x2 = x.reshape(B * S, d_model)

    for layer in range(depth):
        qa, kva, gate, sza, qb_, kb_, vb_, szb = _in_proj(
            x2, pre_norm_g[layer][None, :], _in_proj_weights(w_in[layer]), rope_c, rope_a, rope_b)

        qs = qa.reshape(B, nqb, BLOCK, G, R, D).transpose(0, 3, 1, 4, 2, 5).reshape(B * G, nqb, R * BLOCK, D)
        qsp = jnp.pad(qs, ((0, 0), (0, 0), (0, 0), (0, LANES - D)))
        kv6 = kva.reshape(B, S, 6, G, D).transpose(2, 0, 3, 1, 4).reshape(6, B * G, S, D)

        chunks = kv6[0:2].reshape(2, B * G, n_chunks, CMP_STRIDE * D)
        pos = jnp.stack([cmp_k_pos[layer], cmp_v_pos[layer]]).reshape(2, 1, CMP_LEN * D)
        pos8 = jnp.broadcast_to(pos, (2, 16, CMP_LEN * D)).astype(bf16)
        w1 = jnp.stack([cmp_k_w1[layer], cmp_v_w1[layer]]).astype(bf16)
        w2 = jnp.stack([cmp_k_w2[layer], cmp_v_w2[layer]]).astype(bf16)
        half = CMP_STRIDE * D
        kvc = _compress(chunks, pos8, w1[:, :half], w1[:, half:], w2)
        kvc = kvc.reshape(2, B * G, n_sel, CMP_PER_SEL, D)
        kvc = jnp.pad(kvc, ((0, 0), (0, 0), (0, MAX_SEL_BLOCKS - n_sel), (0, 0), (0, 0)))
        kvc = kvc.transpose(0, 1, 3, 2, 4).reshape(2, B * G, CMP_PER_SEL * MAX_SEL_BLOCKS, D).astype(bf16)

        o_cmp, mask_bias = _cmp_topk(qs, kvc[0], kvc[1])
        kaug = jnp.concatenate([jnp.broadcast_to(onehot, (B * G, S, LANES)), kv6[2],
                                jnp.zeros((B * G, S, LANES - D), bf16)], axis=-1)
        vt_aug = jnp.concatenate([kv6[3].transpose(0, 2, 1), jnp.ones((B * G, 1, S), bf16),
                                  jnp.zeros((B * G, 15, S), bf16)], axis=1)
        o_slc = _slc_attn(qsp, mask_bias, kaug, vt_aug)
        o_win = _win_attn(qs, kv6[4], kv6[5].transpose(0, 2, 1))
        unhead = lambda o: o.reshape(B, G, nqb, R, BLOCK, D).transpose(0, 2, 4, 1, 3, 5).reshape(B * S, W_NSA)
        unhead_t = lambda o: o.reshape(B, G, nqb, D, R, BLOCK).transpose(0, 2, 5, 1, 4, 3).reshape(B * S, W_NSA)

        mixed_b = _dil_mix(qb_, kb_, vb_, szb, B)

        x2 = _out_proj(x2, unhead(o_cmp), unhead_t(o_slc), unhead_t(o_win), gate, sza, mixed_b,
                       w_out[layer].astype(bf16), post_norm_g[layer][None, :])
    return x2.reshape(B, S, d_model)
```

```python
import jax
import jax.numpy as jnp
from jax import lax
from jax.experimental import pallas as pl
from jax.experimental.pallas import tpu as pltpu

HEAD_DIM = 64
NSA_HEADS = 8
NSA_KV_HEADS = 2
NSA_Q_PER_KV = NSA_HEADS // NSA_KV_HEADS
DIL_HEADS = 8
W_NSA = NSA_HEADS * HEAD_DIM
W_KV = NSA_KV_HEADS * HEAD_DIM
W_DIL = DIL_HEADS * HEAD_DIM
CMP_LEN = 32
CMP_STRIDE = 16
CMP_HIDDEN = 256
SEL_BLOCK = 64
SEL_TOP_N = 16
SWA_WINDOW = 512
DIL_PATTERNS = ((128, 1), (512, 4), (2048, 16))
BLOCK = 128
ROPE_THETA = 500000.0
ROPE_DIMS = HEAD_DIM // 4
RMS_EPS = 1e-6
NEG_INF = -1e30
FORCE_SCORE = 1e4
LOG2_E = 1.4426950408889634
LN_2 = 0.6931471805599453

LANES = 128
VMEM_LIMIT = 56 * 1024 * 1024
MAX_SEL_BLOCKS = LANES
CMP_PER_SEL = SEL_BLOCK // CMP_STRIDE
KEY_TILE = 512
PROJ_ROWS = 512
ONES_ROWS = 16
GROUP_W = NSA_Q_PER_KV * HEAD_DIM

_NT = (((1,), (1,)), ((), ()))


def _dot(a, b):
    return jnp.dot(a, b, preferred_element_type=jnp.float32)


def _dot_nt(a, b):
    return lax.dot_general(a, b, _NT, preferred_element_type=jnp.float32)


def _sigmoid(x):
    return 1.0 / (1.0 + jnp.exp(-x))


def _rope(x, c, a, b):
    width = x.shape[1]
    reps = width // LANES
    ct = jnp.tile(c, (1, reps))
    at = jnp.tile(a, (1, reps))
    bt = jnp.tile(b, (1, reps))
    half = ROPE_DIMS // 2
    return x * ct + pltpu.roll(x, width - half, 1) * at + pltpu.roll(x, half, 1) * bt


def _in_proj_kernel(x_ref, g_ref, w_ref, c_ref, a_ref, b_ref,
                    qa_ref, kvc_ref, ks_ref, kw_ref, vst_ref, vwt_ref, gate_ref, sza_ref,
                    qb_ref, kb_ref, vb_ref, szb_ref):
    x = x_ref[...]
    ms = jnp.mean(x * x, axis=-1, keepdims=True)
    h = (x * lax.rsqrt(ms + RMS_EPS) * g_ref[...]).astype(jnp.bfloat16)
    c = c_ref[...]
    a = a_ref[...]
    b = b_ref[...]
    off = 0

    def proj(width):
        nonlocal off
        r = _dot(h, w_ref[:, off:off + width])
        off += width
        return r

    qa_ref[...] = _rope(proj(W_NSA), c, a, b).astype(qa_ref.dtype)
    kva = proj(6 * W_KV)
    part = lambda i: kva[:, i * W_KV:(i + 1) * W_KV]
    kvc_ref[...] = jnp.concatenate([_rope(part(0), c, a, b), part(1)], axis=1).astype(kvc_ref.dtype)
    ks_ref[...] = _rope(part(2), c, a, b).astype(ks_ref.dtype)
    kw_ref[...] = _rope(part(4), c, a, b).astype(kw_ref.dtype)
    vst_ref[0] = part(3).T.astype(vst_ref.dtype)
    vwt_ref[0] = part(5).T.astype(vwt_ref.dtype)
    gate_ref[...] = _sigmoid(proj(LANES))
    za = proj(W_NSA)
    sza_ref[...] = (za * _sigmoid(za)).astype(sza_ref.dtype)
    qb_ref[...] = _rope(proj(W_DIL), c, a, b).astype(qb_ref.dtype)
    kb_ref[...] = _rope(proj(W_DIL), c, a, b).astype(kb_ref.dtype)
    vb_ref[...] = proj(W_DIL).astype(vb_ref.dtype)
    zb = proj(W_DIL)
    szb_ref[...] = (zb * _sigmoid(zb)).astype(szb_ref.dtype)


def _in_proj(x2, g, w, c, a, b, batch):
    rows, d_model = x2.shape
    n_total = w.shape[1]
    tm = PROJ_ROWS
    seq = rows // batch
    per_batch = seq // tm
    bf16 = jnp.bfloat16
    row_spec = lambda wd: pl.BlockSpec((tm, wd), lambda i: (i, 0))
    t_spec = pl.BlockSpec((1, W_KV, tm), lambda i: (i // per_batch, 0, i % per_batch))
    row_out = lambda wd, dt=bf16: (row_spec(wd), jax.ShapeDtypeStruct((rows, wd), dt))
    t_out = (t_spec, jax.ShapeDtypeStruct((batch, W_KV, seq), bf16))
    outs = [row_out(W_NSA), row_out(2 * W_KV), row_out(W_KV), row_out(W_KV), t_out, t_out,
            row_out(LANES, jnp.float32), row_out(W_NSA),
            row_out(W_DIL), row_out(W_DIL), row_out(W_DIL), row_out(W_DIL)]
    return pl.pallas_call(
        _in_proj_kernel,
        grid=(rows // tm,),
        in_specs=[row_spec(d_model),
                  pl.BlockSpec((1, d_model), lambda i: (0, 0)),
                  pl.BlockSpec((d_model, n_total), lambda i: (0, 0)),
                  row_spec(LANES), row_spec(LANES), row_spec(LANES)],
        out_specs=[o[0] for o in outs],
        out_shape=[o[1] for o in outs],
        compiler_params=pltpu.CompilerParams(dimension_semantics=("arbitrary",),
                                             vmem_limit_bytes=VMEM_LIMIT),
        name="in_proj",
    )(x2, g, w, c, a, b)


def _compress_kernel(ch_ref, pos_ref, w1t_ref, w1b_ref, w2_ref, o_ref):
    ch = ch_ref[0, 0]
    n = ch.shape[0]
    half = pos_ref.shape[2] // 2
    pos = pos_ref[0]
    top = _dot(ch, w1t_ref[0])
    bot = _dot(ch, w1b_ref[0])
    bias = _dot(pos[:, :half], w1t_ref[0]) + _dot(pos[:, half:], w1b_ref[0])
    hid = top + pltpu.roll(bot, n - 1, 0) + bias[0:1, :]
    act = (hid * _sigmoid(hid)).astype(jnp.bfloat16)
    o_ref[0, 0] = _dot(act, w2_ref[0])


def _compress(chunks, pos8, w1t, w1b, w2):
    _, bg, n, width = chunks.shape
    return pl.pallas_call(
        _compress_kernel,
        grid=(2, bg),
        in_specs=[pl.BlockSpec((1, 1, n, width), lambda s, i: (s, i, 0, 0)),
                  pl.BlockSpec((1, 16, 2 * width), lambda s, i: (s, 0, 0)),
                  pl.BlockSpec((1, width, CMP_HIDDEN), lambda s, i: (s, 0, 0)),
                  pl.BlockSpec((1, width, CMP_HIDDEN), lambda s, i: (s, 0, 0)),
                  pl.BlockSpec((1, CMP_HIDDEN, HEAD_DIM), lambda s, i: (s, 0, 0))],
        out_specs=pl.BlockSpec((1, 1, n, HEAD_DIM), lambda s, i: (s, i, 0, 0)),
        out_shape=jax.ShapeDtypeStruct((2, bg, n, HEAD_DIM), jnp.float32),
        compiler_params=pltpu.CompilerParams(dimension_semantics=("arbitrary", "arbitrary")),
        name="compress",
    )(chunks, pos8, w1t, w1b, w2)


WIN_BLOCKS = -(-(SWA_WINDOW - 1) // BLOCK) + 1


def _nsa_attn_kernel(q_ref, kc_ref, vct_ref, et_ref, ks_ref, vst_ref, kw_ref, vwt_ref,
                     gate_ref, sza_ref, o_ref, s_a, s_b, p_a, p_b, gate_t):
    f32, bf16 = jnp.float32, jnp.bfloat16
    grp = pl.program_id(1)
    qb = pl.program_id(2)
    rows = NSA_Q_PER_KV * BLOCK
    qf = q_ref[...].astype(f32)
    heads = [qf[:, r * HEAD_DIM:(r + 1) * HEAD_DIM] for r in range(NSA_Q_PER_KV)]
    q_cmp = jnp.concatenate(heads, axis=0).astype(bf16)
    own = (lax.broadcasted_iota(jnp.int32, (BLOCK, LANES), 1) >> 6) == grp
    q_kv = jnp.concatenate([jnp.where(own, jnp.concatenate([h, h], axis=1), 0.0) for h in heads],
                           axis=0).astype(bf16)
    t_q = qb * BLOCK + (lax.broadcasted_iota(jnp.int32, (1, rows), 1) & (BLOCK - 1))

    ncmp = kc_ref.shape[1]
    st = _dot_nt(kc_ref[0], q_cmp)
    pos = lax.broadcasted_iota(jnp.int32, (ncmp, rows), 0)
    cmp_end = (pos & (LANES - 1)) * SEL_BLOCK + (pos >> 7) * CMP_STRIDE + (CMP_LEN - 1)
    valid = cmp_end <= t_q
    st = jnp.where(valid, st, NEG_INF)
    mx = jnp.max(st, axis=0, keepdims=True)
    p = jnp.where(valid, jnp.exp2(st - mx), 0.0)
    p = p / jnp.maximum(jnp.sum(p, axis=0, keepdims=True), 1e-30)
    o_cmp = _dot(vct_ref[0], p.astype(bf16))

    imp = p[:, 0:BLOCK]
    for r in range(1, NSA_Q_PER_KV):
        imp = imp + p[:, r * BLOCK:(r + 1) * BLOCK]
    g = [imp[i * LANES:(i + 1) * LANES] for i in range(CMP_PER_SEL)]
    blk = lax.broadcasted_iota(jnp.int32, (LANES, BLOCK), 0)
    prev_last = jnp.where(blk == 0, 0.0, pltpu.roll(g[3], 1, 0))
    imp_sel = prev_last + 2.0 * (g[0] + g[1] + g[2]) + g[3]

    cur = (qb * BLOCK + lax.broadcasted_iota(jnp.int32, (LANES, BLOCK), 1)) >> 6
    forced = (blk == 0) | (blk == cur) | (blk == cur - 1)
    score = jnp.where(forced, FORCE_SCORE, imp_sel)
    score = jnp.where(blk <= cur, score, -1.0)
    blk_f = blk.astype(f32)

    def pick_one(_, carry):
        sc, sel = carry
        best = jnp.max(sc, axis=0, keepdims=True)
        first = jnp.min(jnp.where(sc == best, blk_f, float(LANES)), axis=0, keepdims=True)
        pick = blk_f == first
        return jnp.where(pick, -2.0, sc), jnp.where(pick, 1.0, sel)

    _, sel = lax.fori_loop(0, SEL_TOP_N, pick_one, (score, jnp.zeros_like(score)))
    bias = jnp.where((sel > 0.0) & (blk <= cur), 0.0, NEG_INF).T.astype(bf16)

    qa = jnp.concatenate([jnp.concatenate([bias] * NSA_Q_PER_KV, axis=0), q_kv], axis=1)
    n = (qb * BLOCK) // KEY_TILE
    last = jnp.maximum(n - 1, 0)
    ones = jnp.ones((ONES_ROWS, KEY_TILE), bf16)

    def keys_aug(kt):
        tile = pl.ds(pl.multiple_of(kt * KEY_TILE, KEY_TILE), KEY_TILE)
        return jnp.concatenate([et_ref[tile, :], ks_ref[tile, :]], axis=1)

    def qk(kt, s_ref):
        st = _dot_nt(keys_aug(kt), qa)
        s_ref[...] = st
        return jnp.max(st, axis=0, keepdims=True)

    def pv(kt, p_ref):
        tile = pl.ds(pl.multiple_of(kt * KEY_TILE, KEY_TILE), KEY_TILE)
        return _dot(jnp.concatenate([vst_ref[0, :, tile], ones], axis=0), p_ref[...])

    def softmax(m_old, mx, s_ref, p_ref):
        m_new = jnp.maximum(m_old, mx)
        p_ref[...] = jnp.exp2(s_ref[...] - m_new).astype(bf16)
        return m_new, jnp.exp2(m_old - m_new)

    st = _dot_nt(keys_aug(n), qa)
    key = n * KEY_TILE + lax.broadcasted_iota(jnp.int32, (KEY_TILE, rows), 0)
    st = jnp.where(key <= t_q, st, NEG_INF)
    m0 = jnp.max(st, axis=0, keepdims=True)
    p_b[...] = jnp.exp2(st - m0).astype(bf16)
    mx0 = qk(0, s_a)
    acc0 = jnp.zeros((HEAD_DIM + ONES_ROWS, rows), f32)
    one = jnp.ones_like(m0)

    def body(i, carry):
        m, acc, alpha_prev, w_prev, kt_prev, mx = carry
        first, second = 2 * i, 2 * i + 1
        w_second = jnp.where(second < n, 1.0, 0.0)
        kt_second = jnp.minimum(second, last)
        acc = alpha_prev * acc + w_prev * pv(kt_prev, p_b)
        m, alpha = softmax(m, mx, s_a, p_a)
        mx = qk(kt_second, s_b)
        acc = alpha * acc + pv(first, p_a)
        m, alpha = softmax(m, mx, s_b, p_b)
        mx = qk(jnp.minimum(second + 1, last), s_a)
        return m, acc, alpha, w_second, kt_second, mx

    init = (m0, acc0, one, jnp.float32(1.0), n, mx0)
    _, acc, alpha_prev, w_prev, kt_prev, _ = lax.fori_loop(0, (n + 1) // 2, body, init)
    acc = alpha_prev * acc + w_prev * pv(kt_prev, p_b)
    o_slc = acc[0:HEAD_DIM] / acc[HEAD_DIM:HEAD_DIM + 1]

    span = WIN_BLOCKS * BLOCK
    first_blk = jnp.maximum(qb - (WIN_BLOCKS - 1), 0)
    win = pl.ds(pl.multiple_of(first_blk * BLOCK, BLOCK), span)
    st = _dot_nt(kw_ref[win, :], q_kv)
    dist = t_q - (first_blk * BLOCK + lax.broadcasted_iota(jnp.int32, (span, rows), 0))
    st = jnp.where((dist >= 0) & (dist <= SWA_WINDOW - 1), st, NEG_INF)
    mw = jnp.max(st, axis=0, keepdims=True)
    e = jnp.exp2(st - mw)
    den = jnp.sum(e, axis=0, keepdims=True)
    o_win = _dot(vwt_ref[0, :, win], e.astype(bf16)) / den

    gate_t[...] = gate_ref[...].T
    zeros = jnp.zeros((LANES - HEAD_DIM, BLOCK), f32)
    outs = []
    for r in range(NSA_Q_PER_KV):
        cols = slice(r * BLOCK, (r + 1) * BLOCK)
        gate_row = lambda branch: gate_t[pl.ds(branch * NSA_HEADS + grp * NSA_Q_PER_KV + r, 1), :]
        o_t = gate_row(0) * o_cmp[:, cols] + gate_row(1) * o_slc[:, cols] + gate_row(2) * o_win[:, cols]
        outs.append(jnp.concatenate([o_t, zeros], axis=0).T[:, 0:HEAD_DIM])
    o_a = jnp.concatenate(outs, axis=1)
    o_ref[...] = (o_a * sza_ref[...].astype(f32)).astype(o_ref.dtype)


def _nsa_attn(qa, kcp, vcpt, onehot, ks, vst, kw, vwt, gate, sza, batch):
    rows_total = qa.shape[0]
    seq = rows_total // batch
    nqb = seq // BLOCK
    ncmp = kcp.shape[1]
    rows = NSA_Q_PER_KV * BLOCK
    G = NSA_KV_HEADS
    q_rows = lambda b, g, j: (b * nqb + j, g)
    return pl.pallas_call(
        _nsa_attn_kernel,
        grid=(batch, G, nqb),
        in_specs=[pl.BlockSpec((BLOCK, GROUP_W), q_rows),
                  pl.BlockSpec((1, ncmp, HEAD_DIM), lambda b, g, j: (b * G + g, 0, 0)),
                  pl.BlockSpec((1, HEAD_DIM, ncmp), lambda b, g, j: (b * G + g, 0, 0)),
                  pl.BlockSpec((seq, LANES), lambda b, g, j: (0, 0)),
                  pl.BlockSpec((seq, W_KV), lambda b, g, j: (b, 0)),
                  pl.BlockSpec((1, HEAD_DIM, seq), lambda b, g, j: (b, g, 0)),
                  pl.BlockSpec((seq, W_KV), lambda b, g, j: (b, 0)),
                  pl.BlockSpec((1, HEAD_DIM, seq), lambda b, g, j: (b, g, 0)),
                  pl.BlockSpec((BLOCK, LANES), lambda b, g, j: (b * nqb + j, 0)),
                  pl.BlockSpec((BLOCK, GROUP_W), q_rows)],
        out_specs=pl.BlockSpec((BLOCK, GROUP_W), q_rows),
        out_shape=jax.ShapeDtypeStruct((rows_total, W_NSA), jnp.bfloat16),
        scratch_shapes=[pltpu.VMEM((KEY_TILE, rows), jnp.float32),
                        pltpu.VMEM((KEY_TILE, rows), jnp.float32),
                        pltpu.VMEM((KEY_TILE, rows), jnp.bfloat16),
                        pltpu.VMEM((KEY_TILE, rows), jnp.bfloat16),
                        pltpu.VMEM((LANES, BLOCK), jnp.float32)],
        compiler_params=pltpu.CompilerParams(
            dimension_semantics=("arbitrary", "arbitrary", "arbitrary"),
            vmem_limit_bytes=VMEM_LIMIT),
        name="nsa_attn",
    )(qa, kcp, vcpt, onehot, ks, vst, kw, vwt, gate, sza)


DIL_MAX = max(d for _, d in DIL_PATTERNS)
DIL_SUPER = BLOCK * DIL_MAX
DIL_UNITS = DIL_SUPER // BLOCK
DIL_UNROLL = 4
HEAD_PAIR = 2 * HEAD_DIM
MIX_ROWS = 256


def _dil_mix_kernel(q_ref, kp_ref, kc_ref, vp_ref, vc_ref, z_ref, o_ref,
                    qf, kf, vf, o_scr, lse_scr, bias_scr):
    f32, bf16 = jnp.float32, jnp.bfloat16
    sb = pl.program_id(1)
    qf[...] = q_ref[...].astype(f32)
    kf[0:DIL_SUPER] = kp_ref[...].astype(f32)
    kf[DIL_SUPER:2 * DIL_SUPER] = kc_ref[...].astype(f32)
    vf[0:DIL_SUPER] = vp_ref[...].astype(f32)
    vf[DIL_SUPER:2 * DIL_SUPER] = vc_ref[...].astype(f32)

    row = lax.broadcasted_iota(jnp.int32, (2 * BLOCK, 2 * BLOCK), 0)
    col = lax.broadcasted_iota(jnp.int32, (2 * BLOCK, 2 * BLOCK), 1)
    dist = BLOCK + (row & (BLOCK - 1)) - col
    band = (dist >= 0) & (dist <= BLOCK)
    bias_scr[0] = jnp.where(band, 0.0, NEG_INF)
    bias_scr[1] = jnp.where(band & (col >= BLOCK), 0.0, NEG_INF)
    first_head = lax.broadcasted_iota(jnp.int32, (BLOCK, HEAD_PAIR), 1) < HEAD_DIM
    ones = jnp.ones((2 * BLOCK, HEAD_PAIR), bf16)

    for pat, (window, dil) in enumerate(DIL_PATTERNS):
        shift = dil.bit_length() - 1

        def unit(u, pat=pat, dil=dil, shift=shift):
            cls = u & (dil - 1)
            blk = u >> shift
            q_start = cls + blk * (BLOCK * dil)
            k_start = DIL_SUPER + q_start - BLOCK * dil
            q2 = qf[pl.ds(q_start, BLOCK, stride=dil), :]
            k2 = kf[pl.ds(k_start, 2 * BLOCK, stride=dil), :]
            v2 = vf[pl.ds(k_start, 2 * BLOCK, stride=dil), :]
            qm = jnp.concatenate([jnp.where(first_head, q2, 0.0),
                                  jnp.where(first_head, 0.0, q2)], axis=0).astype(bf16)
            s = _dot_nt(qm, k2.astype(bf16))
            no_prev = jnp.where((sb == 0) & (blk == 0), 1, 0)
            s = s + bias_scr[no_prev]
            m = jnp.max(s, axis=1, keepdims=True)
            e = jnp.exp2(s - m).astype(bf16)
            r = _dot(e, jnp.concatenate([v2.astype(bf16), ones], axis=1))
            mb = jnp.broadcast_to(m, (2 * BLOCK, HEAD_PAIR)) * LN_2
            den_a, den_b = r[0:BLOCK, HEAD_PAIR:], r[BLOCK:, HEAD_PAIR:]
            o2 = jnp.where(first_head, r[0:BLOCK, 0:HEAD_PAIR] / den_a, r[BLOCK:, 0:HEAD_PAIR] / den_b)
            lse2 = jnp.where(first_head, mb[0:BLOCK] + jnp.log(den_a), mb[BLOCK:] + jnp.log(den_b))
            o_scr[pat, pl.ds(q_start, BLOCK, stride=dil), :] = o2
            lse_scr[pat, pl.ds(q_start, BLOCK, stride=dil), :] = lse2

        def trip(it, carry, unit=unit):
            for j in range(DIL_UNROLL):
                unit(it * DIL_UNROLL + j)
            return carry

        lax.fori_loop(0, DIL_UNITS // DIL_UNROLL, trip, 0)

    def mix(ci, carry):
        rows = pl.ds(pl.multiple_of(ci * MIX_ROWS, MIX_ROWS), MIX_ROWS)
        l1, l2, l3 = lse_scr[0, rows, :], lse_scr[1, rows, :], lse_scr[2, rows, :]
        mx = jnp.maximum(jnp.maximum(l1, l2), l3)
        e1, e2, e3 = jnp.exp(l1 - mx), jnp.exp(l2 - mx), jnp.exp(l3 - mx)
        den = e1 + e2 + e3
        o = (e1 / den) * o_scr[0, rows, :] + (e2 / den) * o_scr[1, rows, :] + (e3 / den) * o_scr[2, rows, :]
        o_ref[rows, :] = (o * z_ref[rows, :].astype(f32)).astype(o_ref.dtype)
        return carry

    lax.fori_loop(0, DIL_SUPER // MIX_ROWS, mix, 0)


def _dil_mix(qb, kb, vb, szb, batch):
    rows = qb.shape[0]
    nsb = rows // batch // DIL_SUPER
    cur = pl.BlockSpec((DIL_SUPER, HEAD_PAIR), lambda b, s, h: (b * nsb + s, h))
    prev = pl.BlockSpec((DIL_SUPER, HEAD_PAIR), lambda b, s, h: (b * nsb + jnp.maximum(s - 1, 0), h))
    f32 = jnp.float32
    return pl.pallas_call(
        _dil_mix_kernel,
        grid=(batch, nsb, W_DIL // HEAD_PAIR),
        in_specs=[cur, prev, cur, prev, cur, cur],
        out_specs=cur,
        out_shape=jax.ShapeDtypeStruct((rows, W_DIL), jnp.bfloat16),
        scratch_shapes=[pltpu.VMEM((DIL_SUPER, HEAD_PAIR), f32),
                        pltpu.VMEM((2 * DIL_SUPER, HEAD_PAIR), f32),
                        pltpu.VMEM((2 * DIL_SUPER, HEAD_PAIR), f32),
                        pltpu.VMEM((len(DIL_PATTERNS), DIL_SUPER, HEAD_PAIR), f32),
                        pltpu.VMEM((len(DIL_PATTERNS), DIL_SUPER, HEAD_PAIR), f32),
                        pltpu.VMEM((2, 2 * BLOCK, 2 * BLOCK), f32)],
        compiler_params=pltpu.CompilerParams(
            dimension_semantics=("arbitrary", "arbitrary", "arbitrary"),
            vmem_limit_bytes=VMEM_LIMIT),
        name="dil_mix",
    )(qb, kb, kb, vb, vb, szb)


def _out_proj_kernel(x_ref, ma_ref, mb_ref, w_ref, g_ref, out_ref):
    y = _dot(ma_ref[...], w_ref[0:W_NSA, :]) + _dot(mb_ref[...], w_ref[W_NSA:W_NSA + W_DIL, :])
    ms = jnp.mean(y * y, axis=-1, keepdims=True)
    out_ref[...] = x_ref[...] + y * lax.rsqrt(ms + RMS_EPS) * g_ref[...]


def _out_proj(x2, mixed_a, mixed_b, w, g):
    rows, d_model = x2.shape
    tm = PROJ_ROWS
    row_spec = lambda wd: pl.BlockSpec((tm, wd), lambda i: (i, 0))
    return pl.pallas_call(
        _out_proj_kernel,
        grid=(rows // tm,),
        in_specs=[row_spec(d_model), row_spec(W_NSA), row_spec(W_DIL),
                  pl.BlockSpec(w.shape, lambda i: (0, 0)),
                  pl.BlockSpec((1, d_model), lambda i: (0, 0))],
        out_specs=row_spec(d_model),
        out_shape=jax.ShapeDtypeStruct((rows, d_model), jnp.float32),
        compiler_params=pltpu.CompilerParams(dimension_semantics=("arbitrary",)),
        name="out_proj",
    )(x2, mixed_a, mixed_b, w, g)


def _rope_tables(positions):
    inv = 1.0 / (ROPE_THETA ** (jnp.arange(0, ROPE_DIMS, 2, dtype=jnp.float32) / ROPE_DIMS))
    ang = positions.astype(jnp.float32).reshape(-1)[:, None] * inv
    cos, sin = jnp.cos(ang), jnp.sin(ang)
    rest = HEAD_DIM - ROPE_DIMS
    one = jnp.ones((cos.shape[0], rest), jnp.float32)
    zero = jnp.zeros((cos.shape[0], rest), jnp.float32)
    zh = jnp.zeros_like(sin)
    per_head = lambda lo, hi, fill: jnp.tile(jnp.concatenate([lo, hi, fill], axis=1), (1, 2))
    return per_head(cos, cos, one), per_head(-sin, zh, zero), per_head(zh, sin, zero)


def _in_proj_weights(w_in):
    scale = HEAD_DIM ** -0.5 * LOG2_E
    o = 0
    cols = {}
    for name, width in (("qa", W_NSA), ("kva", 6 * W_KV), ("gate", 3 * NSA_HEADS), ("za", W_NSA),
                        ("qb", W_DIL), ("kb", W_DIL), ("vb", W_DIL), ("zb", W_DIL)):
        cols[name] = w_in[:, o:o + width]
        o += width
    gate = jnp.pad(cols["gate"], ((0, 0), (0, LANES - 3 * NSA_HEADS)))
    w = jnp.concatenate([cols["qa"] * scale, cols["kva"], gate, cols["za"],
                         cols["qb"] * scale, cols["kb"], cols["vb"], cols["zb"]], axis=1)
    return w.astype(jnp.bfloat16)


def kernel(x, positions, pre_norm_g, w_in, cmp_k_pos, cmp_k_w1, cmp_k_w2,
           cmp_v_pos, cmp_v_w1, cmp_v_w2, w_out, post_norm_g):
    B, S, d_model = x.shape
    G, D = NSA_KV_HEADS, HEAD_DIM
    depth = w_in.shape[0]
    n_sel = S // SEL_BLOCK
    n_chunks = S // CMP_STRIDE
    assert S % KEY_TILE == 0 and n_sel <= MAX_SEL_BLOCKS and S >= WIN_BLOCKS * BLOCK
    assert S % DIL_SUPER == 0 and all(win // dil == BLOCK for win, dil in DIL_PATTERNS)
    bf16 = jnp.bfloat16

    rope_c, rope_a, rope_b = _rope_tables(positions)
    onehot = (jnp.arange(S)[:, None] // SEL_BLOCK == jnp.arange(LANES)[None, :]).astype(bf16)
    x2 = x.reshape(B * S, d_model)

    for layer in range(depth):
        qa, kvc, ks, kw, vst, vwt, gate, sza, qb_, kb_, vb_, szb = _in_proj(
            x2, pre_norm_g[layer][None, :], _in_proj_weights(w_in[layer]), rope_c, rope_a, rope_b, B)

        chunks = kvc.reshape(B, S, 2, G, D).transpose(2, 0, 3, 1, 4).reshape(2, B * G, n_chunks, CMP_STRIDE * D)
        pos = jnp.stack([cmp_k_pos[layer], cmp_v_pos[layer]]).reshape(2, 1, CMP_LEN * D)
        pos8 = jnp.broadcast_to(pos, (2, 16, CMP_LEN * D)).astype(bf16)
        w1 = jnp.stack([cmp_k_w1[layer], cmp_v_w1[layer]]).astype(bf16)
        w2 = jnp.stack([cmp_k_w2[layer], cmp_v_w2[layer]]).astype(bf16)
        half = CMP_STRIDE * D
        kvc = _compress(chunks, pos8, w1[:, :half], w1[:, half:], w2)
        kvc = kvc.reshape(2, B * G, n_sel, CMP_PER_SEL, D)
        kvc = jnp.pad(kvc, ((0, 0), (0, 0), (0, MAX_SEL_BLOCKS - n_sel), (0, 0), (0, 0)))
        kvc = kvc.transpose(0, 1, 3, 2, 4).reshape(2, B * G, CMP_PER_SEL * MAX_SEL_BLOCKS, D).astype(bf16)

        mixed_a = _nsa_attn(qa, kvc[0], kvc[1].transpose(0, 2, 1), onehot, ks, vst, kw, vwt, gate, sza, B)
        mixed_b = _dil_mix(qb_, kb_, vb_, szb, B)
        x2 = _out_proj(x2, mixed_a, mixed_b, w_out[layer].astype(bf16), post_norm_g[layer][None, :])
    return x2.reshape(B, S, d_model)
```

```python
import jax
import jax.numpy as jnp
from jax import lax
from jax.experimental import pallas as pl
from jax.experimental.pallas import tpu as pltpu

HEAD_DIM = 64
NSA_HEADS = 8
NSA_KV_HEADS = 2
NSA_Q_PER_KV = NSA_HEADS // NSA_KV_HEADS
DIL_HEADS = 8
W_NSA = NSA_HEADS * HEAD_DIM
W_KV = NSA_KV_HEADS * HEAD_DIM
W_DIL = DIL_HEADS * HEAD_DIM
CMP_LEN = 32
CMP_STRIDE = 16
CMP_HIDDEN = 256
SEL_BLOCK = 64
SEL_TOP_N = 16
N_FORCED = 3
SWA_WINDOW = 512
DIL_PATTERNS = ((128, 1), (512, 4), (2048, 16))
BLOCK = 128
ROPE_THETA = 500000.0
ROPE_DIMS = HEAD_DIM // 4
RMS_EPS = 1e-6
NEG_INF = -1e30
FORCE_SCORE = 1e4
LOG2_E = 1.4426950408889634
LN_2 = 0.6931471805599453

LANES = 128
VMEM_LIMIT = 56 * 1024 * 1024
MAX_SEL_BLOCKS = LANES
CMP_PER_SEL = SEL_BLOCK // CMP_STRIDE
KEY_TILE = 512
PROJ_ROWS = 512
ONES_ROWS = 16
GROUP_W = NSA_Q_PER_KV * HEAD_DIM

_NT = (((1,), (1,)), ((), ()))


def _dot(a, b):
    return jnp.dot(a, b, preferred_element_type=jnp.float32)


def _dot_nt(a, b):
    return lax.dot_general(a, b, _NT, preferred_element_type=jnp.float32)


def _sigmoid(x):
    return 1.0 / (1.0 + jnp.exp(-x))


def _rope(x, c, a, b):
    width = x.shape[1]
    reps = width // LANES
    ct = jnp.tile(c, (1, reps))
    at = jnp.tile(a, (1, reps))
    bt = jnp.tile(b, (1, reps))
    half = ROPE_DIMS // 2
    return x * ct + pltpu.roll(x, width - half, 1) * at + pltpu.roll(x, half, 1) * bt


def _in_proj_kernel(x_ref, g_ref, w_ref, c_ref, a_ref, b_ref,
                    qa_ref, kvc_ref, ks_ref, kw_ref, vst_ref, vwt_ref, gate_ref, sza_ref,
                    qb_ref, kb_ref, vb_ref, szb_ref):
    x = x_ref[...]
    ms = jnp.mean(x * x, axis=-1, keepdims=True)
    h = (x * lax.rsqrt(ms + RMS_EPS) * g_ref[...]).astype(jnp.bfloat16)
    c = c_ref[...]
    a = a_ref[...]
    b = b_ref[...]
    off = 0

    def proj(width):
        nonlocal off
        r = _dot(h, w_ref[:, off:off + width])
        off += width
        return r

    qa_ref[...] = _rope(proj(W_NSA), c, a, b).astype(qa_ref.dtype)
    kva = proj(6 * W_KV)
    part = lambda i: kva[:, i * W_KV:(i + 1) * W_KV]
    kvc_ref[...] = jnp.concatenate([_rope(part(0), c, a, b), part(1)], axis=1).astype(kvc_ref.dtype)
    ks_ref[...] = _rope(part(2), c, a, b).astype(ks_ref.dtype)
    kw_ref[...] = _rope(part(4), c, a, b).astype(kw_ref.dtype)
    vst_ref[0] = part(3).T.astype(vst_ref.dtype)
    vwt_ref[0] = part(5).T.astype(vwt_ref.dtype)
    gate_ref[...] = _sigmoid(proj(LANES))
    za = proj(W_NSA)
    sza_ref[...] = (za * _sigmoid(za)).astype(sza_ref.dtype)
    qb_ref[...] = _rope(proj(W_DIL), c, a, b).astype(qb_ref.dtype)
    kb_ref[...] = _rope(proj(W_DIL), c, a, b).astype(kb_ref.dtype)
    vb_ref[...] = proj(W_DIL).astype(vb_ref.dtype)
    zb = proj(W_DIL)
    szb_ref[...] = (zb * _sigmoid(zb)).astype(szb_ref.dtype)


def _in_proj(x2, g, w, c, a, b, batch):
    rows, d_model = x2.shape
    n_total = w.shape[1]
    tm = PROJ_ROWS
    seq = rows // batch
    per_batch = seq // tm
    bf16 = jnp.bfloat16
    row_spec = lambda wd: pl.BlockSpec((tm, wd), lambda i: (i, 0))
    t_spec = pl.BlockSpec((1, W_KV, tm), lambda i: (i // per_batch, 0, i % per_batch))
    row_out = lambda wd, dt=bf16: (row_spec(wd), jax.ShapeDtypeStruct((rows, wd), dt))
    t_out = (t_spec, jax.ShapeDtypeStruct((batch, W_KV, seq), bf16))
    outs = [row_out(W_NSA), row_out(2 * W_KV), row_out(W_KV), row_out(W_KV), t_out, t_out,
            row_out(LANES, jnp.float32), row_out(W_NSA),
            row_out(W_DIL), row_out(W_DIL), row_out(W_DIL), row_out(W_DIL)]
    return pl.pallas_call(
        _in_proj_kernel,
        grid=(rows // tm,),
        in_specs=[row_spec(d_model),
                  pl.BlockSpec((1, d_model), lambda i: (0, 0)),
                  pl.BlockSpec((d_model, n_total), lambda i: (0, 0)),
                  row_spec(LANES), row_spec(LANES), row_spec(LANES)],
        out_specs=[o[0] for o in outs],
        out_shape=[o[1] for o in outs],
        compiler_params=pltpu.CompilerParams(dimension_semantics=("arbitrary",),
                                             vmem_limit_bytes=VMEM_LIMIT),
        name="in_proj",
    )(x2, g, w, c, a, b)


def _compress_kernel(ch_ref, pos_ref, w1t_ref, w1b_ref, w2_ref, o_ref):
    ch = ch_ref[0, 0]
    n = ch.shape[0]
    half = pos_ref.shape[2] // 2
    pos = pos_ref[0]
    top = _dot(ch, w1t_ref[0])
    bot = _dot(ch, w1b_ref[0])
    bias = _dot(pos[:, :half], w1t_ref[0]) + _dot(pos[:, half:], w1b_ref[0])
    hid = top + pltpu.roll(bot, n - 1, 0) + bias[0:1, :]
    act = (hid * _sigmoid(hid)).astype(jnp.bfloat16)
    o_ref[0, 0] = _dot(act, w2_ref[0])


def _compress(chunks, pos8, w1t, w1b, w2):
    _, bg, n, width = chunks.shape
    return pl.pallas_call(
        _compress_kernel,
        grid=(2, bg),
        in_specs=[pl.BlockSpec((1, 1, n, width), lambda s, i: (s, i, 0, 0)),
                  pl.BlockSpec((1, 16, 2 * width), lambda s, i: (s, 0, 0)),
                  pl.BlockSpec((1, width, CMP_HIDDEN), lambda s, i: (s, 0, 0)),
                  pl.BlockSpec((1, width, CMP_HIDDEN), lambda s, i: (s, 0, 0)),
                  pl.BlockSpec((1, CMP_HIDDEN, HEAD_DIM), lambda s, i: (s, 0, 0))],
        out_specs=pl.BlockSpec((1, 1, n, HEAD_DIM), lambda s, i: (s, i, 0, 0)),
        out_shape=jax.ShapeDtypeStruct((2, bg, n, HEAD_DIM), jnp.float32),
        compiler_params=pltpu.CompilerParams(dimension_semantics=("arbitrary", "arbitrary")),
        name="compress",
    )(chunks, pos8, w1t, w1b, w2)


WIN_BLOCKS = -(-(SWA_WINDOW - 1) // BLOCK) + 1


def _nsa_attn_kernel(q_ref, kc_ref, vct_ref, et_ref, ks_ref, vst_ref, kw_ref, vwt_ref,
                     gate_ref, sza_ref, o_ref, s_a, s_b, p_a, p_b, gate_t):
    f32, bf16 = jnp.float32, jnp.bfloat16
    grp = pl.program_id(1)
    qb = pl.program_id(2)
    rows = NSA_Q_PER_KV * BLOCK
    qf = q_ref[...].astype(f32)
    heads = [qf[:, r * HEAD_DIM:(r + 1) * HEAD_DIM] for r in range(NSA_Q_PER_KV)]
    q_cmp = jnp.concatenate(heads, axis=0).astype(bf16)
    own = (lax.broadcasted_iota(jnp.int32, (BLOCK, LANES), 1) >> 6) == grp
    q_kv = jnp.concatenate([jnp.where(own, jnp.concatenate([h, h], axis=1), 0.0) for h in heads],
                           axis=0).astype(bf16)
    t_1 = qb * BLOCK + lax.broadcasted_iota(jnp.int32, (1, BLOCK), 1)
    head_bias = lambda ok: jnp.tile(jnp.where(ok, 0.0, NEG_INF), (1, NSA_Q_PER_KV))

    ncmp = kc_ref.shape[1]
    st = _dot_nt(kc_ref[0], q_cmp)
    pos = lax.broadcasted_iota(jnp.int32, (ncmp, BLOCK), 0)
    cmp_end = (pos & (LANES - 1)) * SEL_BLOCK + (pos >> 7) * CMP_STRIDE + (CMP_LEN - 1)
    st = st + head_bias(cmp_end <= t_1)
    mx = jnp.max(st, axis=0, keepdims=True)
    e = jnp.exp2(st - mx)
    den = jnp.maximum(jnp.sum(e, axis=0, keepdims=True), 1e-30)
    p = e * (jnp.tile(jnp.where(t_1 >= CMP_LEN - 1, 1.0, 0.0), (1, NSA_Q_PER_KV)) / den)
    o_cmp = _dot(vct_ref[0], p.astype(bf16))

    imp = p[:, 0:BLOCK]
    for r in range(1, NSA_Q_PER_KV):
        imp = imp + p[:, r * BLOCK:(r + 1) * BLOCK]
    g = [imp[i * LANES:(i + 1) * LANES] for i in range(CMP_PER_SEL)]
    blk = lax.broadcasted_iota(jnp.int32, (LANES, BLOCK), 0)
    prev_last = jnp.where(blk == 0, 0.0, pltpu.roll(g[3], 1, 0))
    imp_sel = prev_last + 2.0 * (g[0] + g[1] + g[2]) + g[3]

    span = WIN_BLOCKS * BLOCK
    first_blk = jnp.maximum(qb - (WIN_BLOCKS - 1), 0)
    win = pl.ds(pl.multiple_of(first_blk * BLOCK, BLOCK), span)
    st = _dot_nt(kw_ref[win, :], q_kv)
    dist = t_1 - (first_blk * BLOCK + lax.broadcasted_iota(jnp.int32, (span, BLOCK), 0))
    st = st + head_bias((dist >= 0) & (dist <= SWA_WINDOW - 1))
    mw = jnp.max(st, axis=0, keepdims=True)
    e = jnp.exp2(st - mw)
    den = jnp.sum(e, axis=0, keepdims=True)
    o_win = _dot(vwt_ref[0, :, win], e.astype(bf16)) * (1.0 / den)

    cur = (qb * BLOCK + lax.broadcasted_iota(jnp.int32, (LANES, BLOCK), 1)) >> 6
    forced = (blk == 0) | (blk == cur) | (blk == cur - 1)
    sc = jnp.where(forced, -2.0, jnp.where(blk <= cur, imp_sel, -1.0))
    sel = forced
    blk_f = blk.astype(f32)
    for _ in range(SEL_TOP_N - N_FORCED):
        best = jnp.max(sc, axis=0, keepdims=True)
        first = jnp.min(jnp.where(sc == best, blk_f, float(LANES)), axis=0, keepdims=True)
        pick = blk_f == first
        sc = jnp.where(pick, -2.0, sc)
        sel = sel | pick
    bias = jnp.where(sel & (blk <= cur), 0.0, NEG_INF).T.astype(bf16)

    qa = jnp.concatenate([jnp.concatenate([bias] * NSA_Q_PER_KV, axis=0), q_kv], axis=1)
    n = (qb * BLOCK) // KEY_TILE
    last = jnp.maximum(n - 1, 0)
    ones = jnp.ones((ONES_ROWS, KEY_TILE), bf16)

    def keys_aug(kt):
        tile = pl.ds(pl.multiple_of(kt * KEY_TILE, KEY_TILE), KEY_TILE)
        return jnp.concatenate([et_ref[tile, :], ks_ref[tile, :]], axis=1)

    def qk(kt, s_ref):
        st = _dot_nt(keys_aug(kt), qa)
        s_ref[...] = st
        return jnp.max(st, axis=0, keepdims=True)

    def pv(kt, p_ref):
        tile = pl.ds(pl.multiple_of(kt * KEY_TILE, KEY_TILE), KEY_TILE)
        return _dot(jnp.concatenate([vst_ref[0, :, tile], ones], axis=0), p_ref[...])

    def softmax(m_old, mx, s_ref, p_ref):
        m_new = jnp.maximum(m_old, mx)
        p_ref[...] = jnp.exp2(s_ref[...] - m_new).astype(bf16)
        return m_new, jnp.exp2(m_old - m_new)

    st = _dot_nt(keys_aug(n), qa)
    key = n * KEY_TILE + lax.broadcasted_iota(jnp.int32, (KEY_TILE, BLOCK), 0)
    st = st + head_bias(key <= t_1)
    m0 = jnp.max(st, axis=0, keepdims=True)
    p_b[...] = jnp.exp2(st - m0).astype(bf16)
    mx0 = qk(0, s_a)
    acc0 = jnp.zeros((HEAD_DIM + ONES_ROWS, rows), f32)
    one = jnp.ones_like(m0)

    def body(i, carry):
        m, acc, alpha_prev, w_prev, kt_prev, mx = carry
        first, second = 2 * i, 2 * i + 1
        w_second = jnp.where(second < n, 1.0, 0.0)
        kt_second = jnp.minimum(second, last)
        acc = alpha_prev * acc + w_prev * pv(kt_prev, p_b)
        m, alpha = softmax(m, mx, s_a, p_a)
        mx = qk(kt_second, s_b)
        acc = alpha * acc + pv(first, p_a)
        m, alpha = softmax(m, mx, s_b, p_b)
        mx = qk(jnp.minimum(second + 1, last), s_a)
        return m, acc, alpha, w_second, kt_second, mx

    init = (m0, acc0, one, jnp.float32(1.0), n, mx0)
    _, acc, alpha_prev, w_prev, kt_prev, _ = lax.fori_loop(0, (n + 1) // 2, body, init)
    acc = alpha_prev * acc + w_prev * pv(kt_prev, p_b)
    o_slc = acc[0:HEAD_DIM] * (1.0 / acc[HEAD_DIM:HEAD_DIM + 1])

    gate_t[...] = gate_ref[...].T
    zeros = jnp.zeros((LANES - HEAD_DIM, BLOCK), f32)
    outs = []
    for r in range(NSA_Q_PER_KV):
        cols = slice(r * BLOCK, (r + 1) * BLOCK)
        gate_row = lambda branch: gate_t[pl.ds(branch * NSA_HEADS + grp * NSA_Q_PER_KV + r, 1), :]
        o_t = gate_row(0) * o_cmp[:, cols] + gate_row(1) * o_slc[:, cols] + gate_row(2) * o_win[:, cols]
        outs.append(jnp.concatenate([o_t, zeros], axis=0).T[:, 0:HEAD_DIM])
    o_a = jnp.concatenate(outs, axis=1)
    o_ref[...] = (o_a * sza_ref[...].astype(f32)).astype(o_ref.dtype)


def _nsa_attn(qa, kcp, vcpt, onehot, ks, vst, kw, vwt, gate, sza, batch):
    rows_total = qa.shape[0]
    seq = rows_total // batch
    nqb = seq // BLOCK
    ncmp = kcp.shape[1]
    rows = NSA_Q_PER_KV * BLOCK
    G = NSA_KV_HEADS
    q_rows = lambda b, g, j: (b * nqb + j, g)
    return pl.pallas_call(
        _nsa_attn_kernel,
        grid=(batch, G, nqb),
        in_specs=[pl.BlockSpec((BLOCK, GROUP_W), q_rows),
                  pl.BlockSpec((1, ncmp, HEAD_DIM), lambda b, g, j: (b * G + g, 0, 0)),
                  pl.BlockSpec((1, HEAD_DIM, ncmp), lambda b, g, j: (b * G + g, 0, 0)),
                  pl.BlockSpec((seq, LANES), lambda b, g, j: (0, 0)),
                  pl.BlockSpec((seq, W_KV), lambda b, g, j: (b, 0)),
                  pl.BlockSpec((1, HEAD_DIM, seq), lambda b, g, j: (b, g, 0)),
                  pl.BlockSpec((seq, W_KV), lambda b, g, j: (b, 0)),
                  pl.BlockSpec((1, HEAD_DIM, seq), lambda b, g, j: (b, g, 0)),
                  pl.BlockSpec((BLOCK, LANES), lambda b, g, j: (b * nqb + j, 0)),
                  pl.BlockSpec((BLOCK, GROUP_W), q_rows)],
        out_specs=pl.BlockSpec((BLOCK, GROUP_W), q_rows),
        out_shape=jax.ShapeDtypeStruct((rows_total, W_NSA), jnp.bfloat16),
        scratch_shapes=[pltpu.VMEM((KEY_TILE, rows), jnp.float32),
                        pltpu.VMEM((KEY_TILE, rows), jnp.float32),
                        pltpu.VMEM((KEY_TILE, rows), jnp.bfloat16),
                        pltpu.VMEM((KEY_TILE, rows), jnp.bfloat16),
                        pltpu.VMEM((LANES, BLOCK), jnp.float32)],
        compiler_params=pltpu.CompilerParams(
            dimension_semantics=("arbitrary", "arbitrary", "arbitrary"),
            vmem_limit_bytes=VMEM_LIMIT),
        name="nsa_attn",
    )(qa, kcp, vcpt, onehot, ks, vst, kw, vwt, gate, sza)


DIL_MAX = max(d for _, d in DIL_PATTERNS)
DIL_SUPER = BLOCK * DIL_MAX
DIL_UNITS = DIL_SUPER // BLOCK
DIL_UNROLL = 4
HEAD_PAIR = 2 * HEAD_DIM
MIX_ROWS = 256


def _dil_mix_kernel(q_ref, kp_ref, kc_ref, vp_ref, vc_ref, z_ref, o_ref,
                    qf, kf, vf, o_scr, lse_scr, bias_scr):
    f32, bf16 = jnp.float32, jnp.bfloat16
    sb = pl.program_id(1)
    qf[...] = q_ref[...].astype(f32)
    kf[0:DIL_SUPER] = kp_ref[...].astype(f32)
    kf[DIL_SUPER:2 * DIL_SUPER] = kc_ref[...].astype(f32)
    vf[0:DIL_SUPER] = vp_ref[...].astype(f32)
    vf[DIL_SUPER:2 * DIL_SUPER] = vc_ref[...].astype(f32)

    row = lax.broadcasted_iota(jnp.int32, (2 * BLOCK, 2 * BLOCK), 0)
    col = lax.broadcasted_iota(jnp.int32, (2 * BLOCK, 2 * BLOCK), 1)
    dist = BLOCK + (row & (BLOCK - 1)) - col
    band = (dist >= 0) & (dist <= BLOCK)
    bias_scr[0] = jnp.where(band, 0.0, NEG_INF)
    bias_scr[1] = jnp.where(band & (col >= BLOCK), 0.0, NEG_INF)
    first_head = lax.broadcasted_iota(jnp.int32, (BLOCK, HEAD_PAIR), 1) < HEAD_DIM
    ones = jnp.ones((2 * BLOCK, HEAD_PAIR), bf16)

    for pat, (window, dil) in enumerate(DIL_PATTERNS):
        shift = dil.bit_length() - 1

        def unit(u, pat=pat, dil=dil, shift=shift):
            cls = u & (dil - 1)
            blk = u >> shift
            q_start = cls + blk * (BLOCK * dil)
            k_start = DIL_SUPER + q_start - BLOCK * dil
            q2 = qf[pl.ds(q_start, BLOCK, stride=dil), :]
            k2 = kf[pl.ds(k_start, 2 * BLOCK, stride=dil), :]
            v2 = vf[pl.ds(k_start, 2 * BLOCK, stride=dil), :]
            qm = jnp.concatenate([jnp.where(first_head, q2, 0.0),
                                  jnp.where(first_head, 0.0, q2)], axis=0).astype(bf16)
            s = _dot_nt(qm, k2.astype(bf16))
            no_prev = jnp.where((sb == 0) & (blk == 0), 1, 0)
            s = s + bias_scr[no_prev]
            m = jnp.max(s, axis=1, keepdims=True)
            e = jnp.exp2(s - m).astype(bf16)
            r = _dot(e, jnp.concatenate([v2.astype(bf16), ones], axis=1))
            mb = jnp.broadcast_to(m, (2 * BLOCK, HEAD_PAIR)) * LN_2
            den_a, den_b = r[0:BLOCK, HEAD_PAIR:], r[BLOCK:, HEAD_PAIR:]
            o2 = jnp.where(first_head, r[0:BLOCK, 0:HEAD_PAIR] / den_a, r[BLOCK:, 0:HEAD_PAIR] / den_b)
            lse2 = jnp.where(first_head, mb[0:BLOCK] + jnp.log(den_a), mb[BLOCK:] + jnp.log(den_b))
            o_scr[pat, pl.ds(q_start, BLOCK, stride=dil), :] = o2
            lse_scr[pat, pl.ds(q_start, BLOCK, stride=dil), :] = lse2

        def trip(it, carry, unit=unit):
            for j in range(DIL_UNROLL):
                unit(it * DIL_UNROLL + j)
            return carry

        lax.fori_loop(0, DIL_UNITS // DIL_UNROLL, trip, 0)

    def mix(ci, carry):
        rows = pl.ds(pl.multiple_of(ci * MIX_ROWS, MIX_ROWS), MIX_ROWS)
        l1, l2, l3 = lse_scr[0, rows, :], lse_scr[1, rows, :], lse_scr[2, rows, :]
        mx = jnp.maximum(jnp.maximum(l1, l2), l3)
        e1, e2, e3 = jnp.exp(l1 - mx), jnp.exp(l2 - mx), jnp.exp(l3 - mx)
        den = e1 + e2 + e3
        o = (e1 / den) * o_scr[0, rows, :] + (e2 / den) * o_scr[1, rows, :] + (e3 / den) * o_scr[2, rows, :]
        o_ref[rows, :] = (o * z_ref[rows, :].astype(f32)).astype(o_ref.dtype)
        return carry

    lax.fori_loop(0, DIL_SUPER // MIX_ROWS, mix, 0)


def _dil_mix(qb, kb, vb, szb, batch):
    rows = qb.shape[0]
    nsb = rows // batch // DIL_SUPER
    cur = pl.BlockSpec((DIL_SUPER, HEAD_PAIR), lambda b, s, h: (b * nsb + s, h))
    prev = pl.BlockSpec((DIL_SUPER, HEAD_PAIR), lambda b, s, h: (b * nsb + jnp.maximum(s - 1, 0), h))
    f32 = jnp.float32
    return pl.pallas_call(
        _dil_mix_kernel,
        grid=(batch, nsb, W_DIL // HEAD_PAIR),
        in_specs=[cur, prev, cur, prev, cur, cur],
        out_specs=cur,
        out_shape=jax.ShapeDtypeStruct((rows, W_DIL), jnp.bfloat16),
        scratch_shapes=[pltpu.VMEM((DIL_SUPER, HEAD_PAIR), f32),
                        pltpu.VMEM((2 * DIL_SUPER, HEAD_PAIR), f32),
                        pltpu.VMEM((2 * DIL_SUPER, HEAD_PAIR), f32),
                        pltpu.VMEM((len(DIL_PATTERNS), DIL_SUPER, HEAD_PAIR), f32),
                        pltpu.VMEM((len(DIL_PATTERNS), DIL_SUPER, HEAD_PAIR), f32),
                        pltpu.VMEM((2, 2 * BLOCK, 2 * BLOCK), f32)],
        compiler_params=pltpu.CompilerParams(
            dimension_semantics=("arbitrary", "arbitrary", "arbitrary"),
            vmem_limit_bytes=VMEM_LIMIT),
        name="dil_mix",
    )(qb, kb, kb, vb, vb, szb)


def _out_proj_kernel(x_ref, ma_ref, mb_ref, w_ref, g_ref, out_ref):
    y = _dot(ma_ref[...], w_ref[0:W_NSA, :]) + _dot(mb_ref[...], w_ref[W_NSA:W_NSA + W_DIL, :])
    ms = jnp.mean(y * y, axis=-1, keepdims=True)
    out_ref[...] = x_ref[...] + y * lax.rsqrt(ms + RMS_EPS) * g_ref[...]


def _out_proj(x2, mixed_a, mixed_b, w, g):
    rows, d_model = x2.shape
    tm = PROJ_ROWS
    row_spec = lambda wd: pl.BlockSpec((tm, wd), lambda i: (i, 0))
    return pl.pallas_call(
        _out_proj_kernel,
        grid=(rows // tm,),
        in_specs=[row_spec(d_model), row_spec(W_NSA), row_spec(W_DIL),
                  pl.BlockSpec(w.shape, lambda i: (0, 0)),
                  pl.BlockSpec((1, d_model), lambda i: (0, 0))],
        out_specs=row_spec(d_model),
        out_shape=jax.ShapeDtypeStruct((rows, d_model), jnp.float32),
        compiler_params=pltpu.CompilerParams(dimension_semantics=("arbitrary",)),
        name="out_proj",
    )(x2, mixed_a, mixed_b, w, g)


def _rope_tables(positions):
    inv = 1.0 / (ROPE_THETA ** (jnp.arange(0, ROPE_DIMS, 2, dtype=jnp.float32) / ROPE_DIMS))
    ang = positions.astype(jnp.float32).reshape(-1)[:, None] * inv
    cos, sin = jnp.cos(ang), jnp.sin(ang)
    rest = HEAD_DIM - ROPE_DIMS
    one = jnp.ones((cos.shape[0], rest), jnp.float32)
    zero = jnp.zeros((cos.shape[0], rest), jnp.float32)
    zh = jnp.zeros_like(sin)
    per_head = lambda lo, hi, fill: jnp.tile(jnp.concatenate([lo, hi, fill], axis=1), (1, 2))
    return per_head(cos, cos, one), per_head(-sin, zh, zero), per_head(zh, sin, zero)


def _in_proj_weights(w_in):
    scale = HEAD_DIM ** -0.5 * LOG2_E
    o = 0
    cols = {}
    for name, width in (("qa", W_NSA), ("kva", 6 * W_KV), ("gate", 3 * NSA_HEADS), ("za", W_NSA),
                        ("qb", W_DIL), ("kb", W_DIL), ("vb", W_DIL), ("zb", W_DIL)):
        cols[name] = w_in[:, o:o + width]
        o += width
    gate = jnp.pad(cols["gate"], ((0, 0), (0, LANES - 3 * NSA_HEADS)))
    w = jnp.concatenate([cols["qa"] * scale, cols["kva"], gate, cols["za"],
                         cols["qb"] * scale, cols["kb"], cols["vb"], cols["zb"]], axis=1)
    return w.astype(jnp.bfloat16)


def kernel(x, positions, pre_norm_g, w_in, cmp_k_pos, cmp_k_w1, cmp_k_w2,
           cmp_v_pos, cmp_v_w1, cmp_v_w2, w_out, post_norm_g):
    B, S, d_model = x.shape
    G, D = NSA_KV_HEADS, HEAD_DIM
    depth = w_in.shape[0]
    n_sel = S // SEL_BLOCK
    n_chunks = S // CMP_STRIDE
    assert S % KEY_TILE == 0 and n_sel <= MAX_SEL_BLOCKS and S >= WIN_BLOCKS * BLOCK
    assert S % DIL_SUPER == 0 and all(win // dil == BLOCK for win, dil in DIL_PATTERNS)
    bf16 = jnp.bfloat16

    rope_c, rope_a, rope_b = _rope_tables(positions)
    onehot = (jnp.arange(S)[:, None] // SEL_BLOCK == jnp.arange(LANES)[None, :]).astype(bf16)
    x2 = x.reshape(B * S, d_model)

    for layer in range(depth):
        qa, kvc, ks, kw, vst, vwt, gate, sza, qb_, kb_, vb_, szb = _in_proj(
            x2, pre_norm_g[layer][None, :], _in_proj_weights(w_in[layer]), rope_c, rope_a, rope_b, B)

        chunks = kvc.reshape(B, S, 2, G, D).transpose(2, 0, 3, 1, 4).reshape(2, B * G, n_chunks, CMP_STRIDE * D)
        pos = jnp.stack([cmp_k_pos[layer], cmp_v_pos[layer]]).reshape(2, 1, CMP_LEN * D)
        pos8 = jnp.broadcast_to(pos, (2, 16, CMP_LEN * D)).astype(bf16)
        w1 = jnp.stack([cmp_k_w1[layer], cmp_v_w1[layer]]).astype(bf16)
        w2 = jnp.stack([cmp_k_w2[layer], cmp_v_w2[layer]]).astype(bf16)
        half = CMP_STRIDE * D
        kvc = _compress(chunks, pos8, w1[:, :half], w1[:, half:], w2)
        kvc = kvc.reshape(2, B * G, n_sel, CMP_PER_SEL, D)
        kvc = jnp.pad(kvc, ((0, 0), (0, 0), (0, MAX_SEL_BLOCKS - n_sel), (0, 0), (0, 0)))
        kvc = kvc.transpose(0, 1, 3, 2, 4).reshape(2, B * G, CMP_PER_SEL * MAX_SEL_BLOCKS, D).astype(bf16)

        mixed_a = _nsa_attn(qa, kvc[0], kvc[1].transpose(0, 2, 1), onehot, ks, vst, kw, vwt, gate, sza, B)
        mixed_b = _dil_mix(qb_, kb_, vb_, szb, B)
        x2 = _out_proj(x2, mixed_a, mixed_b, w_out[layer].astype(bf16), post_norm_g[layer][None, :])
    return x2.reshape(B, S, d_model)
```

```python
import jax
import jax.numpy as jnp
from jax import lax
from jax.experimental import pallas as pl
from jax.experimental.pallas import tpu as pltpu

HEAD_DIM = 64
NSA_HEADS = 8
NSA_KV_HEADS = 2
NSA_Q_PER_KV = NSA_HEADS // NSA_KV_HEADS
DIL_HEADS = 8
W_NSA = NSA_HEADS * HEAD_DIM
W_KV = NSA_KV_HEADS * HEAD_DIM
W_DIL = DIL_HEADS * HEAD_DIM
CMP_LEN = 32
CMP_STRIDE = 16
CMP_HIDDEN = 256
SEL_BLOCK = 64
SEL_TOP_N = 16
N_FORCED = 3
SWA_WINDOW = 512
DIL_PATTERNS = ((128, 1), (512, 4), (2048, 16))
BLOCK = 128
ROPE_THETA = 500000.0
ROPE_DIMS = HEAD_DIM // 4
RMS_EPS = 1e-6
NEG_INF = -1e30
FORCE_SCORE = 1e4
LOG2_E = 1.4426950408889634
LN_2 = 0.6931471805599453

LANES = 128
VMEM_LIMIT = 56 * 1024 * 1024
MAX_SEL_BLOCKS = LANES
CMP_PER_SEL = SEL_BLOCK // CMP_STRIDE
KEY_TILE = 512
PROJ_ROWS = 512
ONES_ROWS = 16
GROUP_W = NSA_Q_PER_KV * HEAD_DIM

_NT = (((1,), (1,)), ((), ()))


def _dot(a, b):
    return jnp.dot(a, b, preferred_element_type=jnp.float32)


def _dot_nt(a, b):
    return lax.dot_general(a, b, _NT, preferred_element_type=jnp.float32)


def _sigmoid(x):
    return 1.0 / (1.0 + jnp.exp(-x))


def _rope(x, c, a, b):
    width = x.shape[1]
    reps = width // LANES
    ct = jnp.tile(c, (1, reps))
    at = jnp.tile(a, (1, reps))
    bt = jnp.tile(b, (1, reps))
    half = ROPE_DIMS // 2
    return x * ct + pltpu.roll(x, width - half, 1) * at + pltpu.roll(x, half, 1) * bt


def _in_proj_kernel(x_ref, g_ref, w_ref, c_ref, a_ref, b_ref,
                    qa_ref, kvc_ref, ks_ref, kw_ref, vst_ref, vwt_ref, gate_ref, sza_ref,
                    qb_ref, kb_ref, vb_ref, szb_ref):
    x = x_ref[...]
    ms = jnp.mean(x * x, axis=-1, keepdims=True)
    h = (x * lax.rsqrt(ms + RMS_EPS) * g_ref[...]).astype(jnp.bfloat16)
    c = c_ref[...]
    a = a_ref[...]
    b = b_ref[...]
    off = 0

    def proj(width):
        nonlocal off
        r = _dot(h, w_ref[:, off:off + width])
        off += width
        return r

    qa_ref[...] = _rope(proj(W_NSA), c, a, b).astype(qa_ref.dtype)
    kva = proj(6 * W_KV)
    part = lambda i: kva[:, i * W_KV:(i + 1) * W_KV]
    kvc_ref[...] = jnp.concatenate([_rope(part(0), c, a, b), part(1)], axis=1).astype(kvc_ref.dtype)
    ks_ref[...] = _rope(part(2), c, a, b).astype(ks_ref.dtype)
    kw_ref[...] = _rope(part(4), c, a, b).astype(kw_ref.dtype)
    vst_ref[0] = part(3).T.astype(vst_ref.dtype)
    vwt_ref[0] = part(5).T.astype(vwt_ref.dtype)
    gate_ref[...] = _sigmoid(proj(LANES))
    za = proj(W_NSA)
    sza_ref[...] = (za * _sigmoid(za)).astype(sza_ref.dtype)
    qb_ref[...] = _rope(proj(W_DIL), c, a, b).astype(qb_ref.dtype)
    kb_ref[...] = _rope(proj(W_DIL), c, a, b).astype(kb_ref.dtype)
    vb_ref[...] = proj(W_DIL).astype(vb_ref.dtype)
    zb = proj(W_DIL)
    szb_ref[...] = (zb * _sigmoid(zb)).astype(szb_ref.dtype)


def _in_proj(x2, g, w, c, a, b, batch):
    rows, d_model = x2.shape
    n_total = w.shape[1]
    tm = PROJ_ROWS
    seq = rows // batch
    per_batch = seq // tm
    bf16 = jnp.bfloat16
    row_spec = lambda wd: pl.BlockSpec((tm, wd), lambda i: (i, 0))
    t_spec = pl.BlockSpec((1, W_KV, tm), lambda i: (i // per_batch, 0, i % per_batch))
    row_out = lambda wd, dt=bf16: (row_spec(wd), jax.ShapeDtypeStruct((rows, wd), dt))
    t_out = (t_spec, jax.ShapeDtypeStruct((batch, W_KV, seq), bf16))
    outs = [row_out(W_NSA), row_out(2 * W_KV), row_out(W_KV), row_out(W_KV), t_out, t_out,
            row_out(LANES, jnp.float32), row_out(W_NSA),
            row_out(W_DIL), row_out(W_DIL), row_out(W_DIL), row_out(W_DIL)]
    return pl.pallas_call(
        _in_proj_kernel,
        grid=(rows // tm,),
        in_specs=[row_spec(d_model),
                  pl.BlockSpec((1, d_model), lambda i: (0, 0)),
                  pl.BlockSpec((d_model, n_total), lambda i: (0, 0)),
                  row_spec(LANES), row_spec(LANES), row_spec(LANES)],
        out_specs=[o[0] for o in outs],
        out_shape=[o[1] for o in outs],
        compiler_params=pltpu.CompilerParams(dimension_semantics=("arbitrary",),
                                             vmem_limit_bytes=VMEM_LIMIT),
        name="in_proj",
    )(x2, g, w, c, a, b)


def _compress_kernel(ch_ref, pos_ref, w1t_ref, w1b_ref, w2_ref, o_ref):
    ch = ch_ref[0, 0]
    n = ch.shape[0]
    half = pos_ref.shape[2] // 2
    pos = pos_ref[0]
    top = _dot(ch, w1t_ref[0])
    bot = _dot(ch, w1b_ref[0])
    bias = _dot(pos[:, :half], w1t_ref[0]) + _dot(pos[:, half:], w1b_ref[0])
    hid = top + pltpu.roll(bot, n - 1, 0) + bias[0:1, :]
    act = (hid * _sigmoid(hid)).astype(jnp.bfloat16)
    o_ref[0, 0] = _dot(act, w2_ref[0])


def _compress(chunks, pos8, w1t, w1b, w2):
    _, bg, n, width = chunks.shape
    return pl.pallas_call(
        _compress_kernel,
        grid=(2, bg),
        in_specs=[pl.BlockSpec((1, 1, n, width), lambda s, i: (s, i, 0, 0)),
                  pl.BlockSpec((1, 16, 2 * width), lambda s, i: (s, 0, 0)),
                  pl.BlockSpec((1, width, CMP_HIDDEN), lambda s, i: (s, 0, 0)),
                  pl.BlockSpec((1, width, CMP_HIDDEN), lambda s, i: (s, 0, 0)),
                  pl.BlockSpec((1, CMP_HIDDEN, HEAD_DIM), lambda s, i: (s, 0, 0))],
        out_specs=pl.BlockSpec((1, 1, n, HEAD_DIM), lambda s, i: (s, i, 0, 0)),
        out_shape=jax.ShapeDtypeStruct((2, bg, n, HEAD_DIM), jnp.float32),
        compiler_params=pltpu.CompilerParams(dimension_semantics=("arbitrary", "arbitrary")),
        name="compress",
    )(chunks, pos8, w1t, w1b, w2)


WIN_BLOCKS = -(-(SWA_WINDOW - 1) // BLOCK) + 1


def _nsa_attn_kernel(q_ref, kc_ref, vct_ref, et_ref, ks_ref, vst_ref, kw_ref, vwt_ref,
                     gate_ref, sza_ref, o_ref, s_a, s_b, p_a, p_b, gate_t):
    f32, bf16 = jnp.float32, jnp.bfloat16
    grp = pl.program_id(1)
    qb = pl.program_id(2)
    rows = NSA_Q_PER_KV * BLOCK
    qf = q_ref[...].astype(f32)
    heads_t = []
    for r in range(NSA_Q_PER_KV):
        h = qf[:, r * HEAD_DIM:(r + 1) * HEAD_DIM]
        heads_t.append(jnp.concatenate([h, h], axis=1).T)
    q_t = jnp.concatenate(heads_t, axis=1)
    q_cmp_t = q_t[0:HEAD_DIM].astype(bf16)
    own = (lax.broadcasted_iota(jnp.int32, (LANES, rows), 0) >> 6) == grp
    q_kv_t = jnp.where(own, q_t, 0.0).astype(bf16)
    t_1 = qb * BLOCK + lax.broadcasted_iota(jnp.int32, (1, BLOCK), 1)
    head_bias = lambda ok: jnp.tile(jnp.where(ok, 0.0, NEG_INF), (1, NSA_Q_PER_KV))

    ncmp = kc_ref.shape[1]
    st = _dot(kc_ref[0], q_cmp_t)
    pos = lax.broadcasted_iota(jnp.int32, (ncmp, BLOCK), 0)
    cmp_end = (pos & (LANES - 1)) * SEL_BLOCK + (pos >> 7) * CMP_STRIDE + (CMP_LEN - 1)
    st = st + head_bias(cmp_end <= t_1)
    mx = jnp.max(st, axis=0, keepdims=True)
    e = jnp.exp2(st - mx)
    den = jnp.maximum(jnp.sum(e, axis=0, keepdims=True), 1e-30)
    p = e * (jnp.tile(jnp.where(t_1 >= CMP_LEN - 1, 1.0, 0.0), (1, NSA_Q_PER_KV)) / den)
    o_cmp = _dot(vct_ref[0], p.astype(bf16))

    imp = p[:, 0:BLOCK]
    for r in range(1, NSA_Q_PER_KV):
        imp = imp + p[:, r * BLOCK:(r + 1) * BLOCK]
    g = [imp[i * LANES:(i + 1) * LANES] for i in range(CMP_PER_SEL)]
    blk = lax.broadcasted_iota(jnp.int32, (LANES, BLOCK), 0)
    prev_last = jnp.where(blk == 0, 0.0, pltpu.roll(g[3], 1, 0))
    imp_sel = prev_last + 2.0 * (g[0] + g[1] + g[2]) + g[3]

    span = WIN_BLOCKS * BLOCK
    first_blk = jnp.maximum(qb - (WIN_BLOCKS - 1), 0)
    win = pl.ds(pl.multiple_of(first_blk * BLOCK, BLOCK), span)
    st = _dot(kw_ref[win, :], q_kv_t)
    dist = t_1 - (first_blk * BLOCK + lax.broadcasted_iota(jnp.int32, (span, BLOCK), 0))
    st = st + head_bias((dist >= 0) & (dist <= SWA_WINDOW - 1))
    mw = jnp.max(st, axis=0, keepdims=True)
    e = jnp.exp2(st - mw)
    den = jnp.sum(e, axis=0, keepdims=True)
    o_win = _dot(vwt_ref[0, :, win], e.astype(bf16)) * (1.0 / den)

    cur = (qb * BLOCK + lax.broadcasted_iota(jnp.int32, (LANES, BLOCK), 1)) >> 6
    forced = (blk == 0) | (blk == cur) | (blk == cur - 1)
    sc = jnp.where(forced, -2.0, jnp.where(blk <= cur, imp_sel, -1.0))
    sel = forced
    blk_f = blk.astype(f32)
    for _ in range(SEL_TOP_N - N_FORCED):
        best = jnp.max(sc, axis=0, keepdims=True)
        first = jnp.min(jnp.where(sc == best, blk_f, float(LANES)), axis=0, keepdims=True)
        pick = blk_f == first
        sc = jnp.where(pick, -2.0, sc)
        sel = sel | pick
    bias = jnp.where(sel & (blk <= cur), 0.0, NEG_INF).astype(bf16)

    qa_t = jnp.concatenate([jnp.tile(bias, (1, NSA_Q_PER_KV)), q_kv_t], axis=0)
    n = (qb * BLOCK) // KEY_TILE
    last = jnp.maximum(n - 1, 0)
    ones = jnp.ones((ONES_ROWS, KEY_TILE), bf16)

    def keys_aug(kt):
        tile = pl.ds(pl.multiple_of(kt * KEY_TILE, KEY_TILE), KEY_TILE)
        return jnp.concatenate([et_ref[tile, :], ks_ref[tile, :]], axis=1)

    def qk(kt, s_ref):
        st = _dot(keys_aug(kt), qa_t)
        s_ref[...] = st
        return jnp.max(st, axis=0, keepdims=True)

    def pv(kt, p_ref):
        tile = pl.ds(pl.multiple_of(kt * KEY_TILE, KEY_TILE), KEY_TILE)
        return _dot(jnp.concatenate([vst_ref[0, :, tile], ones], axis=0), p_ref[...])

    def softmax(m_old, mx, s_ref, p_ref):
        m_new = jnp.maximum(m_old, mx)
        p_ref[...] = jnp.exp2(s_ref[...] - m_new).astype(bf16)
        return m_new, jnp.exp2(m_old - m_new)

    st = _dot(keys_aug(n), qa_t)
    key = n * KEY_TILE + lax.broadcasted_iota(jnp.int32, (KEY_TILE, BLOCK), 0)
    st = st + head_bias(key <= t_1)
    m0 = jnp.max(st, axis=0, keepdims=True)
    p_b[...] = jnp.exp2(st - m0).astype(bf16)
    mx0 = qk(0, s_a)
    acc0 = jnp.zeros((HEAD_DIM + ONES_ROWS, rows), f32)
    one = jnp.ones_like(m0)

    def body(i, carry):
        m, acc, alpha_prev, w_prev, kt_prev, mx = carry
        first, second = 2 * i, 2 * i + 1
        w_second = jnp.where(second < n, 1.0, 0.0)
        kt_second = jnp.minimum(second, last)
        acc = alpha_prev * acc + w_prev * pv(kt_prev, p_b)
        m, alpha = softmax(m, mx, s_a, p_a)
        mx = qk(kt_second, s_b)
        acc = alpha * acc + pv(first, p_a)
        m, alpha = softmax(m, mx, s_b, p_b)
        mx = qk(jnp.minimum(second + 1, last), s_a)
        return m, acc, alpha, w_second, kt_second, mx

    init = (m0, acc0, one, jnp.float32(1.0), n, mx0)
    _, acc, alpha_prev, w_prev, kt_prev, _ = lax.fori_loop(0, (n + 1) // 2, body, init)
    acc = alpha_prev * acc + w_prev * pv(kt_prev, p_b)
    o_slc = acc[0:HEAD_DIM] * (1.0 / acc[HEAD_DIM:HEAD_DIM + 1])

    gate_t[...] = gate_ref[...].T
    zeros = jnp.zeros((LANES - HEAD_DIM, BLOCK), f32)
    outs = []
    for r in range(NSA_Q_PER_KV):
        cols = slice(r * BLOCK, (r + 1) * BLOCK)
        gate_row = lambda branch: gate_t[pl.ds(branch * NSA_HEADS + grp * NSA_Q_PER_KV + r, 1), :]
        o_t = gate_row(0) * o_cmp[:, cols] + gate_row(1) * o_slc[:, cols] + gate_row(2) * o_win[:, cols]
        outs.append(jnp.concatenate([o_t, zeros], axis=0).T[:, 0:HEAD_DIM])
    o_a = jnp.concatenate(outs, axis=1)
    o_ref[...] = (o_a * sza_ref[...].astype(f32)).astype(o_ref.dtype)


def _nsa_attn(qa, kcp, vcpt, onehot, ks, vst, kw, vwt, gate, sza, batch):
    rows_total = qa.shape[0]
    seq = rows_total // batch
    nqb = seq // BLOCK
    ncmp = kcp.shape[1]
    rows = NSA_Q_PER_KV * BLOCK
    G = NSA_KV_HEADS
    q_rows = lambda b, g, j: (b * nqb + j, g)
    return pl.pallas_call(
        _nsa_attn_kernel,
        grid=(batch, G, nqb),
        in_specs=[pl.BlockSpec((BLOCK, GROUP_W), q_rows),
                  pl.BlockSpec((1, ncmp, HEAD_DIM), lambda b, g, j: (b * G + g, 0, 0)),
                  pl.BlockSpec((1, HEAD_DIM, ncmp), lambda b, g, j: (b * G + g, 0, 0)),
                  pl.BlockSpec((seq, LANES), lambda b, g, j: (0, 0)),
                  pl.BlockSpec((seq, W_KV), lambda b, g, j: (b, 0)),
                  pl.BlockSpec((1, HEAD_DIM, seq), lambda b, g, j: (b, g, 0)),
                  pl.BlockSpec((seq, W_KV), lambda b, g, j: (b, 0)),
                  pl.BlockSpec((1, HEAD_DIM, seq), lambda b, g, j: (b, g, 0)),
                  pl.BlockSpec((BLOCK, LANES), lambda b, g, j: (b * nqb + j, 0)),
                  pl.BlockSpec((BLOCK, GROUP_W), q_rows)],
        out_specs=pl.BlockSpec((BLOCK, GROUP_W), q_rows),
        out_shape=jax.ShapeDtypeStruct((rows_total, W_NSA), jnp.bfloat16),
        scratch_shapes=[pltpu.VMEM((KEY_TILE, rows), jnp.float32),
                        pltpu.VMEM((KEY_TILE, rows), jnp.float32),
                        pltpu.VMEM((KEY_TILE, rows), jnp.bfloat16),
                        pltpu.VMEM((KEY_TILE, rows), jnp.bfloat16),
                        pltpu.VMEM((LANES, BLOCK), jnp.float32)],
        compiler_params=pltpu.CompilerParams(
            dimension_semantics=("arbitrary", "arbitrary", "arbitrary"),
            vmem_limit_bytes=VMEM_LIMIT),
        name="nsa_attn",
    )(qa, kcp, vcpt, onehot, ks, vst, kw, vwt, gate, sza)


DIL_MAX = max(d for _, d in DIL_PATTERNS)
DIL_SUPER = BLOCK * DIL_MAX
DIL_UNITS = DIL_SUPER // BLOCK
DIL_UNROLL = 4
HEAD_PAIR = 2 * HEAD_DIM
MIX_ROWS = 256


def _dil_mix_kernel(q_ref, kp_ref, kc_ref, vp_ref, vc_ref, z_ref, o_ref,
                    qf, kf, vf, o_scr, lse_scr, bias_scr):
    f32, bf16 = jnp.float32, jnp.bfloat16
    sb = pl.program_id(1)
    qf[...] = q_ref[...].astype(f32)
    kf[0:DIL_SUPER] = kp_ref[...].astype(f32)
    kf[DIL_SUPER:2 * DIL_SUPER] = kc_ref[...].astype(f32)
    vf[0:DIL_SUPER] = vp_ref[...].astype(f32)
    vf[DIL_SUPER:2 * DIL_SUPER] = vc_ref[...].astype(f32)

    row = lax.broadcasted_iota(jnp.int32, (2 * BLOCK, 2 * BLOCK), 0)
    col = lax.broadcasted_iota(jnp.int32, (2 * BLOCK, 2 * BLOCK), 1)
    dist = BLOCK + (row & (BLOCK - 1)) - col
    band = (dist >= 0) & (dist <= BLOCK)
    bias_scr[0] = jnp.where(band, 0.0, NEG_INF)
    bias_scr[1] = jnp.where(band & (col >= BLOCK), 0.0, NEG_INF)
    first_head = lax.broadcasted_iota(jnp.int32, (BLOCK, HEAD_PAIR), 1) < HEAD_DIM
    ones = jnp.ones((2 * BLOCK, HEAD_PAIR), bf16)

    for pat, (window, dil) in enumerate(DIL_PATTERNS):
        shift = dil.bit_length() - 1

        def unit(u, pat=pat, dil=dil, shift=shift):
            cls = u & (dil - 1)
            blk = u >> shift
            q_start = cls + blk * (BLOCK * dil)
            k_start = DIL_SUPER + q_start - BLOCK * dil
            q2 = qf[pl.ds(q_start, BLOCK, stride=dil), :]
            k2 = kf[pl.ds(k_start, 2 * BLOCK, stride=dil), :]
            v2 = vf[pl.ds(k_start, 2 * BLOCK, stride=dil), :]
            qm = jnp.concatenate([jnp.where(first_head, q2, 0.0),
                                  jnp.where(first_head, 0.0, q2)], axis=0).astype(bf16)
            s = _dot_nt(qm, k2.astype(bf16))
            no_prev = jnp.where((sb == 0) & (blk == 0), 1, 0)
            s = s + bias_scr[no_prev]
            m = jnp.max(s, axis=1, keepdims=True)
            e = jnp.exp2(s - m).astype(bf16)
            r = _dot(e, jnp.concatenate([v2.astype(bf16), ones], axis=1))
            mb = jnp.broadcast_to(m, (2 * BLOCK, HEAD_PAIR)) * LN_2
            den_a, den_b = r[0:BLOCK, HEAD_PAIR:], r[BLOCK:, HEAD_PAIR:]
            o2 = jnp.where(first_head, r[0:BLOCK, 0:HEAD_PAIR] / den_a, r[BLOCK:, 0:HEAD_PAIR] / den_b)
            lse2 = jnp.where(first_head, mb[0:BLOCK] + jnp.log(den_a), mb[BLOCK:] + jnp.log(den_b))
            o_scr[pat, pl.ds(q_start, BLOCK, stride=dil), :] = o2
            lse_scr[pat, pl.ds(q_start, BLOCK, stride=dil), :] = lse2

        def trip(it, carry, unit=unit):
            for j in range(DIL_UNROLL):
                unit(it * DIL_UNROLL + j)
            return carry

        lax.fori_loop(0, DIL_UNITS // DIL_UNROLL, trip, 0)

    def mix(ci, carry):
        rows = pl.ds(pl.multiple_of(ci * MIX_ROWS, MIX_ROWS), MIX_ROWS)
        l1, l2, l3 = lse_scr[0, rows, :], lse_scr[1, rows, :], lse_scr[2, rows, :]
        mx = jnp.maximum(jnp.maximum(l1, l2), l3)
        e1, e2, e3 = jnp.exp(l1 - mx), jnp.exp(l2 - mx), jnp.exp(l3 - mx)
        den = e1 + e2 + e3
        o = (e1 / den) * o_scr[0, rows, :] + (e2 / den) * o_scr[1, rows, :] + (e3 / den) * o_scr[2, rows, :]
        o_ref[rows, :] = (o * z_ref[rows, :].astype(f32)).astype(o_ref.dtype)
        return carry

    lax.fori_loop(0, DIL_SUPER // MIX_ROWS, mix, 0)


def _dil_mix(qb, kb, vb, szb, batch):
    rows = qb.shape[0]
    nsb = rows // batch // DIL_SUPER
    cur = pl.BlockSpec((DIL_SUPER, HEAD_PAIR), lambda b, s, h: (b * nsb + s, h))
    prev = pl.BlockSpec((DIL_SUPER, HEAD_PAIR), lambda b, s, h: (b * nsb + jnp.maximum(s - 1, 0), h))
    f32 = jnp.float32
    return pl.pallas_call(
        _dil_mix_kernel,
        grid=(batch, nsb, W_DIL // HEAD_PAIR),
        in_specs=[cur, prev, cur, prev, cur, cur],
        out_specs=cur,
        out_shape=jax.ShapeDtypeStruct((rows, W_DIL), jnp.bfloat16),
        scratch_shapes=[pltpu.VMEM((DIL_SUPER, HEAD_PAIR), f32),
                        pltpu.VMEM((2 * DIL_SUPER, HEAD_PAIR), f32),
                        pltpu.VMEM((2 * DIL_SUPER, HEAD_PAIR), f32),
                        pltpu.VMEM((len(DIL_PATTERNS), DIL_SUPER, HEAD_PAIR), f32),
                        pltpu.VMEM((len(DIL_PATTERNS), DIL_SUPER, HEAD_PAIR), f32),
                        pltpu.VMEM((2, 2 * BLOCK, 2 * BLOCK), f32)],
        compiler_params=pltpu.CompilerParams(
            dimension_semantics=("arbitrary", "arbitrary", "arbitrary"),
            vmem_limit_bytes=VMEM_LIMIT),
        name="dil_mix",
    )(qb, kb, kb, vb, vb, szb)


def _out_proj_kernel(x_ref, ma_ref, mb_ref, w_ref, g_ref, out_ref):
    y = _dot(ma_ref[...], w_ref[0:W_NSA, :]) + _dot(mb_ref[...], w_ref[W_NSA:W_NSA + W_DIL, :])
    ms = jnp.mean(y * y, axis=-1, keepdims=True)
    out_ref[...] = x_ref[...] + y * lax.rsqrt(ms + RMS_EPS) * g_ref[...]


def _out_proj(x2, mixed_a, mixed_b, w, g):
    rows, d_model = x2.shape
    tm = PROJ_ROWS
    row_spec = lambda wd: pl.BlockSpec((tm, wd), lambda i: (i, 0))
    return pl.pallas_call(
        _out_proj_kernel,
        grid=(rows // tm,),
        in_specs=[row_spec(d_model), row_spec(W_NSA), row_spec(W_DIL),
                  pl.BlockSpec(w.shape, lambda i: (0, 0)),
                  pl.BlockSpec((1, d_model), lambda i: (0, 0))],
        out_specs=row_spec(d_model),
        out_shape=jax.ShapeDtypeStruct((rows, d_model), jnp.float32),
        compiler_params=pltpu.CompilerParams(dimension_semantics=("arbitrary",)),
        name="out_proj",
    )(x2, mixed_a, mixed_b, w, g)


def _rope_tables(positions):
    inv = 1.0 / (ROPE_THETA ** (jnp.arange(0, ROPE_DIMS, 2, dtype=jnp.float32) / ROPE_DIMS))
    ang = positions.astype(jnp.float32).reshape(-1)[:, None] * inv
    cos, sin = jnp.cos(ang), jnp.sin(ang)
    rest = HEAD_DIM - ROPE_DIMS
    one = jnp.ones((cos.shape[0], rest), jnp.float32)
    zero = jnp.zeros((cos.shape[0], rest), jnp.float32)
    zh = jnp.zeros_like(sin)
    per_head = lambda lo, hi, fill: jnp.tile(jnp.concatenate([lo, hi, fill], axis=1), (1, 2))
    return per_head(cos, cos, one), per_head(-sin, zh, zero), per_head(zh, sin, zero)


def _in_proj_weights(w_in):
    scale = HEAD_DIM ** -0.5 * LOG2_E
    o = 0
    cols = {}
    for name, width in (("qa", W_NSA), ("kva", 6 * W_KV), ("gate", 3 * NSA_HEADS), ("za", W_NSA),
                        ("qb", W_DIL), ("kb", W_DIL), ("vb", W_DIL), ("zb", W_DIL)):
        cols[name] = w_in[:, o:o + width]
        o += width
    gate = jnp.pad(cols["gate"], ((0, 0), (0, LANES - 3 * NSA_HEADS)))
    w = jnp.concatenate([cols["qa"] * scale, cols["kva"], gate, cols["za"],
                         cols["qb"] * scale, cols["kb"], cols["vb"], cols["zb"]], axis=1)
    return w.astype(jnp.bfloat16)


def kernel(x, positions, pre_norm_g, w_in, cmp_k_pos, cmp_k_w1, cmp_k_w2,
           cmp_v_pos, cmp_v_w1, cmp_v_w2, w_out, post_norm_g):
    B, S, d_model = x.shape
    G, D = NSA_KV_HEADS, HEAD_DIM
    depth = w_in.shape[0]
    n_sel = S // SEL_BLOCK
    n_chunks = S // CMP_STRIDE
    assert S % KEY_TILE == 0 and n_sel <= MAX_SEL_BLOCKS and S >= WIN_BLOCKS * BLOCK
    assert S % DIL_SUPER == 0 and all(win // dil == BLOCK for win, dil in DIL_PATTERNS)
    bf16 = jnp.bfloat16

    rope_c, rope_a, rope_b = _rope_tables(positions)
    onehot = (jnp.arange(S)[:, None] // SEL_BLOCK == jnp.arange(LANES)[None, :]).astype(bf16)
    x2 = x.reshape(B * S, d_model)

    for layer in range(depth):
        qa, kvc, ks, kw, vst, vwt, gate, sza, qb_, kb_, vb_, szb = _in_proj(
            x2, pre_norm_g[layer][None, :], _in_proj_weights(w_in[layer]), rope_c, rope_a, rope_b, B)

        chunks = kvc.reshape(B, S, 2, G, D).transpose(2, 0, 3, 1, 4).reshape(2, B * G, n_chunks, CMP_STRIDE * D)
        pos = jnp.stack([cmp_k_pos[layer], cmp_v_pos[layer]]).reshape(2, 1, CMP_LEN * D)
        pos8 = jnp.broadcast_to(pos, (2, 16, CMP_LEN * D)).astype(bf16)
        w1 = jnp.stack([cmp_k_w1[layer], cmp_v_w1[layer]]).astype(bf16)
        w2 = jnp.stack([cmp_k_w2[layer], cmp_v_w2[layer]]).astype(bf16)
        half = CMP_STRIDE * D
        kvc = _compress(chunks, pos8, w1[:, :half], w1[:, half:], w2)
        kvc = kvc.reshape(2, B * G, n_sel, CMP_PER_SEL, D)
        kvc = jnp.pad(kvc, ((0, 0), (0, 0), (0, MAX_SEL_BLOCKS - n_sel), (0, 0), (0, 0)))
        kvc = kvc.transpose(0, 1, 3, 2, 4).reshape(2, B * G, CMP_PER_SEL * MAX_SEL_BLOCKS, D).astype(bf16)

        mixed_a = _nsa_attn(qa, kvc[0], kvc[1].transpose(0, 2, 1), onehot, ks, vst, kw, vwt, gate, sza, B)
        mixed_b = _dil_mix(qb_, kb_, vb_, szb, B)
        x2 = _out_proj(x2, mixed_a, mixed_b, w_out[layer].astype(bf16), post_norm_g[layer][None, :])
    return x2.reshape(B, S, d_model)
```

```python
import jax
import jax.numpy as jnp
from jax import lax
from jax.experimental import pallas as pl
from jax.experimental.pallas import tpu as pltpu

HEAD_DIM = 64
NSA_HEADS = 8
NSA_KV_HEADS = 2
NSA_Q_PER_KV = NSA_HEADS // NSA_KV_HEADS
DIL_HEADS = 8
W_NSA = NSA_HEADS * HEAD_DIM
W_KV = NSA_KV_HEADS * HEAD_DIM
W_DIL = DIL_HEADS * HEAD_DIM
CMP_LEN = 32
CMP_STRIDE = 16
CMP_HIDDEN = 256
SEL_BLOCK = 64
SEL_TOP_N = 16
N_FORCED = 3
SWA_WINDOW = 512
DIL_PATTERNS = ((128, 1), (512, 4), (2048, 16))
BLOCK = 128
ROPE_THETA = 500000.0
ROPE_DIMS = HEAD_DIM // 4
RMS_EPS = 1e-6
NEG_INF = -1e30
FORCE_SCORE = 1e4
LOG2_E = 1.4426950408889634
LN_2 = 0.6931471805599453

LANES = 128
VMEM_LIMIT = 56 * 1024 * 1024
MAX_SEL_BLOCKS = LANES
CMP_PER_SEL = SEL_BLOCK // CMP_STRIDE
KEY_TILE = 512
PROJ_ROWS = 512
ONES_ROWS = 16

_NT = (((1,), (1,)), ((), ()))


def _dot(a, b):
    return jnp.dot(a, b, preferred_element_type=jnp.float32)


def _dot_nt(a, b):
    return lax.dot_general(a, b, _NT, preferred_element_type=jnp.float32)


def _sigmoid(x):
    return 1.0 / (1.0 + jnp.exp(-x))


def _rope(x, c, a, b):
    width = x.shape[1]
    reps = width // LANES
    ct = jnp.tile(c, (1, reps))
    at = jnp.tile(a, (1, reps))
    bt = jnp.tile(b, (1, reps))
    half = ROPE_DIMS // 2
    return x * ct + pltpu.roll(x, width - half, 1) * at + pltpu.roll(x, half, 1) * bt


def _in_proj_kernel(x_ref, g_ref, w_ref, c_ref, a_ref, b_ref,
                    qa_ref, kvc_ref, ks_ref, kw_ref, vst_ref, vwt_ref, gate_ref, sza_ref,
                    qb_ref, kb_ref, vb_ref, szb_ref):
    x = x_ref[...]
    ms = jnp.mean(x * x, axis=-1, keepdims=True)
    h = (x * lax.rsqrt(ms + RMS_EPS) * g_ref[...]).astype(jnp.bfloat16)
    c = c_ref[...]
    a = a_ref[...]
    b = b_ref[...]
    off = 0

    def proj(width):
        nonlocal off
        r = _dot(h, w_ref[:, off:off + width])
        off += width
        return r

    qa_ref[...] = _rope(proj(W_NSA), c, a, b).astype(qa_ref.dtype)
    kva = proj(6 * W_KV)
    part = lambda i: kva[:, i * W_KV:(i + 1) * W_KV]
    kvc_ref[...] = jnp.concatenate([_rope(part(0), c, a, b), part(1)], axis=1).astype(kvc_ref.dtype)
    ks_ref[...] = _rope(part(2), c, a, b).astype(ks_ref.dtype)
    kw_ref[...] = _rope(part(4), c, a, b).astype(kw_ref.dtype)
    vst_ref[0] = part(3).T.astype(vst_ref.dtype)
    vwt_ref[0] = part(5).T.astype(vwt_ref.dtype)
    gate_ref[...] = _sigmoid(proj(LANES))
    za = proj(W_NSA)
    sza_ref[...] = (za * _sigmoid(za)).astype(sza_ref.dtype)
    qb_ref[...] = _rope(proj(W_DIL), c, a, b).astype(qb_ref.dtype)
    kb_ref[...] = _rope(proj(W_DIL), c, a, b).astype(kb_ref.dtype)
    vb_ref[...] = proj(W_DIL).astype(vb_ref.dtype)
    zb = proj(W_DIL)
    szb_ref[...] = (zb * _sigmoid(zb)).astype(szb_ref.dtype)


def _in_proj(x2, g, w, c, a, b, batch):
    rows, d_model = x2.shape
    n_total = w.shape[1]
    tm = PROJ_ROWS
    seq = rows // batch
    per_batch = seq // tm
    bf16 = jnp.bfloat16
    row_spec = lambda wd: pl.BlockSpec((tm, wd), lambda i: (i, 0))
    t_spec = pl.BlockSpec((1, W_KV, tm), lambda i: (i // per_batch, 0, i % per_batch))
    row_out = lambda wd, dt=bf16: (row_spec(wd), jax.ShapeDtypeStruct((rows, wd), dt))
    t_out = (t_spec, jax.ShapeDtypeStruct((batch, W_KV, seq), bf16))
    outs = [row_out(W_NSA), row_out(2 * W_KV), row_out(W_KV), row_out(W_KV), t_out, t_out,
            row_out(LANES, jnp.float32), row_out(W_NSA),
            row_out(W_DIL), row_out(W_DIL), row_out(W_DIL), row_out(W_DIL)]
    return pl.pallas_call(
        _in_proj_kernel,
        grid=(rows // tm,),
        in_specs=[row_spec(d_model),
                  pl.BlockSpec((1, d_model), lambda i: (0, 0)),
                  pl.BlockSpec((d_model, n_total), lambda i: (0, 0)),
                  row_spec(LANES), row_spec(LANES), row_spec(LANES)],
        out_specs=[o[0] for o in outs],
        out_shape=[o[1] for o in outs],
        compiler_params=pltpu.CompilerParams(dimension_semantics=("arbitrary",),
                                             vmem_limit_bytes=VMEM_LIMIT),
        name="in_proj",
    )(x2, g, w, c, a, b)


def _compress_kernel(ch_ref, pos_ref, w1t_ref, w1b_ref, w2_ref, o_ref):
    ch = ch_ref[0, 0]
    n = ch.shape[0]
    half = pos_ref.shape[2] // 2
    pos = pos_ref[0]
    top = _dot(ch, w1t_ref[0])
    bot = _dot(ch, w1b_ref[0])
    bias = _dot(pos[:, :half], w1t_ref[0]) + _dot(pos[:, half:], w1b_ref[0])
    hid = top + pltpu.roll(bot, n - 1, 0) + bias[0:1, :]
    act = (hid * _sigmoid(hid)).astype(jnp.bfloat16)
    o_ref[0, 0] = _dot(act, w2_ref[0])


def _compress(chunks, pos8, w1t, w1b, w2):
    _, bg, n, width = chunks.shape
    return pl.pallas_call(
        _compress_kernel,
        grid=(2, bg),
        in_specs=[pl.BlockSpec((1, 1, n, width), lambda s, i: (s, i, 0, 0)),
                  pl.BlockSpec((1, 16, 2 * width), lambda s, i: (s, 0, 0)),
                  pl.BlockSpec((1, width, CMP_HIDDEN), lambda s, i: (s, 0, 0)),
                  pl.BlockSpec((1, width, CMP_HIDDEN), lambda s, i: (s, 0, 0)),
                  pl.BlockSpec((1, CMP_HIDDEN, HEAD_DIM), lambda s, i: (s, 0, 0))],
        out_specs=pl.BlockSpec((1, 1, n, HEAD_DIM), lambda s, i: (s, i, 0, 0)),
        out_shape=jax.ShapeDtypeStruct((2, bg, n, HEAD_DIM), jnp.float32),
        compiler_params=pltpu.CompilerParams(dimension_semantics=("arbitrary", "arbitrary")),
        name="compress",
    )(chunks, pos8, w1t, w1b, w2)


WIN_BLOCKS = -(-(SWA_WINDOW - 1) // BLOCK) + 1


def _nsa_attn_kernel(q_ref, kc_ref, vct_ref, et_ref, ks_ref, vst_ref, kw_ref, vwt_ref,
                     gate_ref, sza_ref, o_ref, s_a, s_b, p_a, p_b, gate_t):
    f32, bf16 = jnp.float32, jnp.bfloat16
    qb = pl.program_id(1)
    G, R = NSA_KV_HEADS, NSA_Q_PER_KV
    grows = R * BLOCK
    rows = G * grows
    gcols = lambda g: slice(g * grows, (g + 1) * grows)
    qf = q_ref[...].astype(f32)
    heads_t = []
    for h in range(NSA_HEADS):
        qh = qf[:, h * HEAD_DIM:(h + 1) * HEAD_DIM]
        heads_t.append(jnp.concatenate([qh, qh], axis=1).T)
    q_t = jnp.concatenate(heads_t, axis=1)
    q_cmp_t = q_t[0:HEAD_DIM].astype(bf16)
    own = ((lax.broadcasted_iota(jnp.int32, (LANES, rows), 0) >> 6)
           == (lax.broadcasted_iota(jnp.int32, (LANES, rows), 1) >> 9))
    q_kv_t = jnp.where(own, q_t, 0.0).astype(bf16)
    t_1 = qb * BLOCK + lax.broadcasted_iota(jnp.int32, (1, BLOCK), 1)
    head_bias = lambda ok, reps: jnp.tile(jnp.where(ok, 0.0, NEG_INF), (1, reps))

    ncmp = kc_ref.shape[1]
    pos = lax.broadcasted_iota(jnp.int32, (ncmp, BLOCK), 0)
    cmp_end = (pos & (LANES - 1)) * SEL_BLOCK + (pos >> 7) * CMP_STRIDE + (CMP_LEN - 1)
    cmp_bias = head_bias(cmp_end <= t_1, R)
    seen = jnp.tile(jnp.where(t_1 >= CMP_LEN - 1, 1.0, 0.0), (1, R))
    blk = lax.broadcasted_iota(jnp.int32, (LANES, BLOCK), 0)
    o_cmp, imp_sel = [], []
    for g in range(G):
        st = _dot(kc_ref[g], q_cmp_t[:, gcols(g)]) + cmp_bias
        mx = jnp.max(st, axis=0, keepdims=True)
        e = jnp.exp2(st - mx)
        den = jnp.maximum(jnp.sum(e, axis=0, keepdims=True), 1e-30)
        p = e * (seen / den)
        o_cmp.append(_dot(vct_ref[g], p.astype(bf16)))
        imp = p[:, 0:BLOCK]
        for r in range(1, R):
            imp = imp + p[:, r * BLOCK:(r + 1) * BLOCK]
        q4 = [imp[i * LANES:(i + 1) * LANES] for i in range(CMP_PER_SEL)]
        prev_last = jnp.where(blk == 0, 0.0, pltpu.roll(q4[3], 1, 0))
        imp_sel.append(prev_last + 2.0 * (q4[0] + q4[1] + q4[2]) + q4[3])

    span = WIN_BLOCKS * BLOCK
    first_blk = jnp.maximum(qb - (WIN_BLOCKS - 1), 0)
    win = pl.ds(pl.multiple_of(first_blk * BLOCK, BLOCK), span)
    dist = t_1 - (first_blk * BLOCK + lax.broadcasted_iota(jnp.int32, (span, BLOCK), 0))
    st = _dot(kw_ref[win, :], q_kv_t) + head_bias((dist >= 0) & (dist <= SWA_WINDOW - 1), G * R)
    mw = jnp.max(st, axis=0, keepdims=True)
    e = jnp.exp2(st - mw)
    inv = 1.0 / jnp.sum(e, axis=0, keepdims=True)
    e = e.astype(bf16)
    o_win = [_dot(vwt_ref[0, g * HEAD_DIM:(g + 1) * HEAD_DIM, win], e[:, gcols(g)]) * inv[:, gcols(g)]
             for g in range(G)]

    cur = (qb * BLOCK + lax.broadcasted_iota(jnp.int32, (LANES, BLOCK), 1)) >> 6
    forced = (blk == 0) | (blk == cur) | (blk == cur - 1)
    blk_f = blk.astype(f32)
    bias = []
    for g in range(G):
        sc = jnp.where(forced, -2.0, jnp.where(blk <= cur, imp_sel[g], -1.0))
        sel = forced
        for _ in range(SEL_TOP_N - N_FORCED):
            best = jnp.max(sc, axis=0, keepdims=True)
            first = jnp.min(jnp.where(sc == best, blk_f, float(LANES)), axis=0, keepdims=True)
            pick = blk_f == first
            sc = jnp.where(pick, -2.0, sc)
            sel = sel | pick
        bias.append(jnp.tile(jnp.where(sel & (blk <= cur), 0.0, NEG_INF).astype(bf16), (1, R)))

    qa_t = jnp.concatenate([jnp.concatenate(bias, axis=1), q_kv_t], axis=0)
    n = (qb * BLOCK) // KEY_TILE
    last = jnp.maximum(n - 1, 0)
    ones = jnp.ones((ONES_ROWS, KEY_TILE), bf16)

    def keys_aug(kt):
        tile = pl.ds(pl.multiple_of(kt * KEY_TILE, KEY_TILE), KEY_TILE)
        return jnp.concatenate([et_ref[tile, :], ks_ref[tile, :]], axis=1)

    def qk(kt, s_ref):
        st = _dot(keys_aug(kt), qa_t)
        s_ref[...] = st
        return jnp.max(st, axis=0, keepdims=True)

    def pv(kt, p_ref):
        tile = pl.ds(pl.multiple_of(kt * KEY_TILE, KEY_TILE), KEY_TILE)
        return tuple(_dot(jnp.concatenate([vst_ref[0, g * HEAD_DIM:(g + 1) * HEAD_DIM, tile], ones], axis=0),
                          p_ref[:, gcols(g)]) for g in range(G))

    def softmax(m_old, mx, s_ref, p_ref):
        m_new = jnp.maximum(m_old, mx)
        p_ref[...] = jnp.exp2(s_ref[...] - m_new).astype(bf16)
        return m_new, jnp.exp2(m_old - m_new)

    def accumulate(acc, alpha, weight, contrib):
        return tuple(alpha[:, gcols(g)] * acc[g] + weight * contrib[g] for g in range(G))

    key = n * KEY_TILE + lax.broadcasted_iota(jnp.int32, (KEY_TILE, BLOCK), 0)
    st = _dot(keys_aug(n), qa_t) + head_bias(key <= t_1, G * R)
    m0 = jnp.max(st, axis=0, keepdims=True)
    p_b[...] = jnp.exp2(st - m0).astype(bf16)
    mx0 = qk(0, s_a)
    acc0 = tuple(jnp.zeros((HEAD_DIM + ONES_ROWS, grows), f32) for _ in range(G))
    one = jnp.ones_like(m0)

    def body(i, carry):
        m, acc, alpha_prev, w_prev, kt_prev, mx = carry
        first, second = 2 * i, 2 * i + 1
        w_second = jnp.where(second < n, 1.0, 0.0)
        kt_second = jnp.minimum(second, last)
        acc = accumulate(acc, alpha_prev, w_prev, pv(kt_prev, p_b))
        m, alpha = softmax(m, mx, s_a, p_a)
        mx = qk(kt_second, s_b)
        acc = accumulate(acc, alpha, 1.0, pv(first, p_a))
        m, alpha = softmax(m, mx, s_b, p_b)
        mx = qk(jnp.minimum(second + 1, last), s_a)
        return m, acc, alpha, w_second, kt_second, mx

    init = (m0, acc0, one, jnp.float32(1.0), n, mx0)
    _, acc, alpha_prev, w_prev, kt_prev, _ = lax.fori_loop(0, (n + 1) // 2, body, init)
    acc = accumulate(acc, alpha_prev, w_prev, pv(kt_prev, p_b))
    o_slc = [a[0:HEAD_DIM] * (1.0 / a[HEAD_DIM:HEAD_DIM + 1]) for a in acc]

    gate_t[...] = gate_ref[...].T
    zeros = jnp.zeros((LANES - HEAD_DIM, BLOCK), f32)
    outs = []
    for g in range(G):
        for r in range(R):
            cols = slice(r * BLOCK, (r + 1) * BLOCK)
            gate_row = lambda branch: gate_t[branch * NSA_HEADS + g * R + r:branch * NSA_HEADS + g * R + r + 1, :]
            o_t = (gate_row(0) * o_cmp[g][:, cols] + gate_row(1) * o_slc[g][:, cols]
                   + gate_row(2) * o_win[g][:, cols])
            outs.append(jnp.concatenate([o_t, zeros], axis=0).T[:, 0:HEAD_DIM])
    o_a = jnp.concatenate(outs, axis=1)
    o_ref[...] = (o_a * sza_ref[...].astype(f32)).astype(o_ref.dtype)


def _nsa_attn(qa, kcp, vcpt, onehot, ks, vst, kw, vwt, gate, sza, batch):
    rows_total = qa.shape[0]
    seq = rows_total // batch
    nqb = seq // BLOCK
    ncmp = kcp.shape[1]
    G = NSA_KV_HEADS
    rows = NSA_HEADS * BLOCK
    q_rows = lambda b, j: (b * nqb + j, 0)
    return pl.pallas_call(
        _nsa_attn_kernel,
        grid=(batch, nqb),
        in_specs=[pl.BlockSpec((BLOCK, W_NSA), q_rows),
                  pl.BlockSpec((G, ncmp, HEAD_DIM), lambda b, j: (b, 0, 0)),
                  pl.BlockSpec((G, HEAD_DIM, ncmp), lambda b, j: (b, 0, 0)),
                  pl.BlockSpec((seq, LANES), lambda b, j: (0, 0)),
                  pl.BlockSpec((seq, W_KV), lambda b, j: (b, 0)),
                  pl.BlockSpec((1, W_KV, seq), lambda b, j: (b, 0, 0)),
                  pl.BlockSpec((seq, W_KV), lambda b, j: (b, 0)),
                  pl.BlockSpec((1, W_KV, seq), lambda b, j: (b, 0, 0)),
                  pl.BlockSpec((BLOCK, LANES), q_rows),
                  pl.BlockSpec((BLOCK, W_NSA), q_rows)],
        out_specs=pl.BlockSpec((BLOCK, W_NSA), q_rows),
        out_shape=jax.ShapeDtypeStruct((rows_total, W_NSA), jnp.bfloat16),
        scratch_shapes=[pltpu.VMEM((KEY_TILE, rows), jnp.float32),
                        pltpu.VMEM((KEY_TILE, rows), jnp.float32),
                        pltpu.VMEM((KEY_TILE, rows), jnp.bfloat16),
                        pltpu.VMEM((KEY_TILE, rows), jnp.bfloat16),
                        pltpu.VMEM((LANES, BLOCK), jnp.float32)],
        compiler_params=pltpu.CompilerParams(dimension_semantics=("arbitrary", "arbitrary"),
                                             vmem_limit_bytes=VMEM_LIMIT),
        name="nsa_attn",
    )(qa, kcp, vcpt, onehot, ks, vst, kw, vwt, gate, sza)


DIL_MAX = max(d for _, d in DIL_PATTERNS)
DIL_SUPER = BLOCK * DIL_MAX
DIL_UNITS = DIL_SUPER // BLOCK
DIL_UNROLL = 4
HEAD_PAIR = 2 * HEAD_DIM
MIX_ROWS = 256


def _dil_mix_kernel(q_ref, kp_ref, kc_ref, vp_ref, vc_ref, z_ref, o_ref,
                    qf, kf, vf, o_scr, lse_scr, bias_scr):
    f32, bf16 = jnp.float32, jnp.bfloat16
    sb = pl.program_id(1)
    qf[...] = q_ref[...].astype(f32)
    kf[0:DIL_SUPER] = kp_ref[...].astype(f32)
    kf[DIL_SUPER:2 * DIL_SUPER] = kc_ref[...].astype(f32)
    vf[0:DIL_SUPER] = vp_ref[...].astype(f32)
    vf[DIL_SUPER:2 * DIL_SUPER] = vc_ref[...].astype(f32)

    row = lax.broadcasted_iota(jnp.int32, (2 * BLOCK, 2 * BLOCK), 0)
    col = lax.broadcasted_iota(jnp.int32, (2 * BLOCK, 2 * BLOCK), 1)
    dist = BLOCK + (row & (BLOCK - 1)) - col
    band = (dist >= 0) & (dist <= BLOCK)
    bias_scr[0] = jnp.where(band, 0.0, NEG_INF)
    bias_scr[1] = jnp.where(band & (col >= BLOCK), 0.0, NEG_INF)
    first_head = lax.broadcasted_iota(jnp.int32, (BLOCK, HEAD_PAIR), 1) < HEAD_DIM
    ones = jnp.ones((2 * BLOCK, HEAD_PAIR), bf16)

    for pat, (window, dil) in enumerate(DIL_PATTERNS):
        shift = dil.bit_length() - 1

        def unit(u, pat=pat, dil=dil, shift=shift):
            cls = u & (dil - 1)
            blk = u >> shift
            q_start = cls + blk * (BLOCK * dil)
            k_start = DIL_SUPER + q_start - BLOCK * dil
            q2 = qf[pl.ds(q_start, BLOCK, stride=dil), :]
            k2 = kf[pl.ds(k_start, 2 * BLOCK, stride=dil), :]
            v2 = vf[pl.ds(k_start, 2 * BLOCK, stride=dil), :]
            qm = jnp.concatenate([jnp.where(first_head, q2, 0.0),
                                  jnp.where(first_head, 0.0, q2)], axis=0).astype(bf16)
            s = _dot_nt(qm, k2.astype(bf16))
            no_prev = jnp.where((sb == 0) & (blk == 0), 1, 0)
            s = s + bias_scr[no_prev]
            m = jnp.max(s, axis=1, keepdims=True)
            e = jnp.exp2(s - m).astype(bf16)
            r = _dot(e, jnp.concatenate([v2.astype(bf16), ones], axis=1))
            mb = jnp.broadcast_to(m, (2 * BLOCK, HEAD_PAIR)) * LN_2
            den_a, den_b = r[0:BLOCK, HEAD_PAIR:], r[BLOCK:, HEAD_PAIR:]
            o2 = jnp.where(first_head, r[0:BLOCK, 0:HEAD_PAIR] / den_a, r[BLOCK:, 0:HEAD_PAIR] / den_b)
            lse2 = jnp.where(first_head, mb[0:BLOCK] + jnp.log(den_a), mb[BLOCK:] + jnp.log(den_b))
            o_scr[pat, pl.ds(q_start, BLOCK, stride=dil), :] = o2
            lse_scr[pat, pl.ds(q_start, BLOCK, stride=dil), :] = lse2

        def trip(it, carry, unit=unit):
            for j in range(DIL_UNROLL):
                unit(it * DIL_UNROLL + j)
            return carry

        lax.fori_loop(0, DIL_UNITS // DIL_UNROLL, trip, 0)

    def mix(ci, carry):
        rows = pl.ds(pl.multiple_of(ci * MIX_ROWS, MIX_ROWS), MIX_ROWS)
        l1, l2, l3 = lse_scr[0, rows, :], lse_scr[1, rows, :], lse_scr[2, rows, :]
        mx = jnp.maximum(jnp.maximum(l1, l2), l3)
        e1, e2, e3 = jnp.exp(l1 - mx), jnp.exp(l2 - mx), jnp.exp(l3 - mx)
        den = e1 + e2 + e3
        o = (e1 / den) * o_scr[0, rows, :] + (e2 / den) * o_scr[1, rows, :] + (e3 / den) * o_scr[2, rows, :]
        o_ref[rows, :] = (o * z_ref[rows, :].astype(f32)).astype(o_ref.dtype)
        return carry

    lax.fori_loop(0, DIL_SUPER // MIX_ROWS, mix, 0)


def _dil_mix(qb, kb, vb, szb, batch):
    rows = qb.shape[0]
    nsb = rows // batch // DIL_SUPER
    cur = pl.BlockSpec((DIL_SUPER, HEAD_PAIR), lambda b, s, h: (b * nsb + s, h))
    prev = pl.BlockSpec((DIL_SUPER, HEAD_PAIR), lambda b, s, h: (b * nsb + jnp.maximum(s - 1, 0), h))
    f32 = jnp.float32
    return pl.pallas_call(
        _dil_mix_kernel,
        grid=(batch, nsb, W_DIL // HEAD_PAIR),
        in_specs=[cur, prev, cur, prev, cur, cur],
        out_specs=cur,
        out_shape=jax.ShapeDtypeStruct((rows, W_DIL), jnp.bfloat16),
        scratch_shapes=[pltpu.VMEM((DIL_SUPER, HEAD_PAIR), f32),
                        pltpu.VMEM((2 * DIL_SUPER, HEAD_PAIR), f32),
                        pltpu.VMEM((2 * DIL_SUPER, HEAD_PAIR), f32),
                        pltpu.VMEM((len(DIL_PATTERNS), DIL_SUPER, HEAD_PAIR), f32),
                        pltpu.VMEM((len(DIL_PATTERNS), DIL_SUPER, HEAD_PAIR), f32),
                        pltpu.VMEM((2, 2 * BLOCK, 2 * BLOCK), f32)],
        compiler_params=pltpu.CompilerParams(
            dimension_semantics=("arbitrary", "arbitrary", "arbitrary"),
            vmem_limit_bytes=VMEM_LIMIT),
        name="dil_mix",
    )(qb, kb, kb, vb, vb, szb)


def _out_proj_kernel(x_ref, ma_ref, mb_ref, w_ref, g_ref, out_ref):
    y = _dot(ma_ref[...], w_ref[0:W_NSA, :]) + _dot(mb_ref[...], w_ref[W_NSA:W_NSA + W_DIL, :])
    ms = jnp.mean(y * y, axis=-1, keepdims=True)
    out_ref[...] = x_ref[...] + y * lax.rsqrt(ms + RMS_EPS) * g_ref[...]


def _out_proj(x2, mixed_a, mixed_b, w, g):
    rows, d_model = x2.shape
    tm = PROJ_ROWS
    row_spec = lambda wd: pl.BlockSpec((tm, wd), lambda i: (i, 0))
    return pl.pallas_call(
        _out_proj_kernel,
        grid=(rows // tm,),
        in_specs=[row_spec(d_model), row_spec(W_NSA), row_spec(W_DIL),
                  pl.BlockSpec(w.shape, lambda i: (0, 0)),
                  pl.BlockSpec((1, d_model), lambda i: (0, 0))],
        out_specs=row_spec(d_model),
        out_shape=jax.ShapeDtypeStruct((rows, d_model), jnp.float32),
        compiler_params=pltpu.CompilerParams(dimension_semantics=("arbitrary",)),
        name="out_proj",
    )(x2, mixed_a, mixed_b, w, g)


def _rope_tables(positions):
    inv = 1.0 / (ROPE_THETA ** (jnp.arange(0, ROPE_DIMS, 2, dtype=jnp.float32) / ROPE_DIMS))
    ang = positions.astype(jnp.float32).reshape(-1)[:, None] * inv
    cos, sin = jnp.cos(ang), jnp.sin(ang)
    rest = HEAD_DIM - ROPE_DIMS
    one = jnp.ones((cos.shape[0], rest), jnp.float32)
    zero = jnp.zeros((cos.shape[0], rest), jnp.float32)
    zh = jnp.zeros_like(sin)
    per_head = lambda lo, hi, fill: jnp.tile(jnp.concatenate([lo, hi, fill], axis=1), (1, 2))
    return per_head(cos, cos, one), per_head(-sin, zh, zero), per_head(zh, sin, zero)


def _in_proj_weights(w_in):
    scale = HEAD_DIM ** -0.5 * LOG2_E
    o = 0
    cols = {}
    for name, width in (("qa", W_NSA), ("kva", 6 * W_KV), ("gate", 3 * NSA_HEADS), ("za", W_NSA),
                        ("qb", W_DIL), ("kb", W_DIL), ("vb", W_DIL), ("zb", W_DIL)):
        cols[name] = w_in[:, o:o + width]
        o += width
    gate = jnp.pad(cols["gate"], ((0, 0), (0, LANES - 3 * NSA_HEADS)))
    w = jnp.concatenate([cols["qa"] * scale, cols["kva"], gate, cols["za"],
                         cols["qb"] * scale, cols["kb"], cols["vb"], cols["zb"]], axis=1)
    return w.astype(jnp.bfloat16)


def kernel(x, positions, pre_norm_g, w_in, cmp_k_pos, cmp_k_w1, cmp_k_w2,
           cmp_v_pos, cmp_v_w1, cmp_v_w2, w_out, post_norm_g):
    B, S, d_model = x.shape
    G, D = NSA_KV_HEADS, HEAD_DIM
    depth = w_in.shape[0]
    n_sel = S // SEL_BLOCK
    n_chunks = S // CMP_STRIDE
    assert S % KEY_TILE == 0 and n_sel <= MAX_SEL_BLOCKS and S >= WIN_BLOCKS * BLOCK
    assert S % DIL_SUPER == 0 and all(win // dil == BLOCK for win, dil in DIL_PATTERNS)
    bf16 = jnp.bfloat16

    rope_c, rope_a, rope_b = _rope_tables(positions)
    onehot = (jnp.arange(S)[:, None] // SEL_BLOCK == jnp.arange(LANES)[None, :]).astype(bf16)
    x2 = x.reshape(B * S, d_model)

    for layer in range(depth):
        qa, kvc, ks, kw, vst, vwt, gate, sza, qb_, kb_, vb_, szb = _in_proj(
            x2, pre_norm_g[layer][None, :], _in_proj_weights(w_in[layer]), rope_c, rope_a, rope_b, B)

        chunks = kvc.reshape(B, S, 2, G, D).transpose(2, 0, 3, 1, 4).reshape(2, B * G, n_chunks, CMP_STRIDE * D)
        pos = jnp.stack([cmp_k_pos[layer], cmp_v_pos[layer]]).reshape(2, 1, CMP_LEN * D)
        pos8 = jnp.broadcast_to(pos, (2, 16, CMP_LEN * D)).astype(bf16)
        w1 = jnp.stack([cmp_k_w1[layer], cmp_v_w1[layer]]).astype(bf16)
        w2 = jnp.stack([cmp_k_w2[layer], cmp_v_w2[layer]]).astype(bf16)
        half = CMP_STRIDE * D
        kvc = _compress(chunks, pos8, w1[:, :half], w1[:, half:], w2)
        kvc = kvc.reshape(2, B * G, n_sel, CMP_PER_SEL, D)
        kvc = jnp.pad(kvc, ((0, 0), (0, 0), (0, MAX_SEL_BLOCKS - n_sel), (0, 0), (0, 0)))
        kvc = kvc.transpose(0, 1, 3, 2, 4).reshape(2, B * G, CMP_PER_SEL * MAX_SEL_BLOCKS, D).astype(bf16)

        mixed_a = _nsa_attn(qa, kvc[0], kvc[1].transpose(0, 2, 1), onehot, ks, vst, kw, vwt, gate, sza, B)
        mixed_b = _dil_mix(qb_, kb_, vb_, szb, B)
        x2 = _out_proj(x2, mixed_a, mixed_b, w_out[layer].astype(bf16), post_norm_g[layer][None, :])
    return x2.reshape(B, S, d_model)
```

```python
import jax
import jax.numpy as jnp
from jax import lax
from jax.experimental import pallas as pl
from jax.experimental.pallas import tpu as pltpu

HEAD_DIM = 64
NSA_HEADS = 8
NSA_KV_HEADS = 2
NSA_Q_PER_KV = NSA_HEADS // NSA_KV_HEADS
DIL_HEADS = 8
W_NSA = NSA_HEADS * HEAD_DIM
W_KV = NSA_KV_HEADS * HEAD_DIM
W_DIL = DIL_HEADS * HEAD_DIM
CMP_LEN = 32
CMP_STRIDE = 16
CMP_HIDDEN = 256
SEL_BLOCK = 64
SEL_TOP_N = 16
N_FORCED = 3
SWA_WINDOW = 512
DIL_PATTERNS = ((128, 1), (512, 4), (2048, 16))
BLOCK = 128
ROPE_THETA = 500000.0
ROPE_DIMS = HEAD_DIM // 4
RMS_EPS = 1e-6
NEG_INF = -1e30
FORCE_SCORE = 1e4
LOG2_E = 1.4426950408889634

LANES = 128
VMEM_LIMIT = 56 * 1024 * 1024
MAX_SEL_BLOCKS = LANES
CMP_PER_SEL = SEL_BLOCK // CMP_STRIDE
KEY_TILE = 512
PROJ_ROWS = 512
ONES_ROWS = 16

_NT = (((1,), (1,)), ((), ()))


def _dot(a, b):
    return jnp.dot(a, b, preferred_element_type=jnp.float32)


def _dot_nt(a, b):
    return lax.dot_general(a, b, _NT, preferred_element_type=jnp.float32)


def _sigmoid(x):
    return 1.0 / (1.0 + jnp.exp(-x))


def _rope(x, c, a, b):
    width = x.shape[1]
    reps = width // LANES
    ct = jnp.tile(c, (1, reps))
    at = jnp.tile(a, (1, reps))
    bt = jnp.tile(b, (1, reps))
    half = ROPE_DIMS // 2
    return x * ct + pltpu.roll(x, width - half, 1) * at + pltpu.roll(x, half, 1) * bt


def _in_proj_kernel(x_ref, g_ref, w_ref, c_ref, a_ref, b_ref,
                    qa_ref, kvc_ref, ks_ref, kw_ref, vst_ref, vwt_ref, gate_ref, sza_ref,
                    qb_ref, kb_ref, vb_ref, szb_ref):
    x = x_ref[...]
    ms = jnp.mean(x * x, axis=-1, keepdims=True)
    h = (x * lax.rsqrt(ms + RMS_EPS) * g_ref[...]).astype(jnp.bfloat16)
    c = c_ref[...]
    a = a_ref[...]
    b = b_ref[...]
    off = 0

    def proj(width):
        nonlocal off
        r = _dot(h, w_ref[:, off:off + width])
        off += width
        return r

    qa_ref[...] = _rope(proj(W_NSA), c, a, b).astype(qa_ref.dtype)
    kva = proj(6 * W_KV)
    part = lambda i: kva[:, i * W_KV:(i + 1) * W_KV]
    kvc_ref[...] = jnp.concatenate([_rope(part(0), c, a, b), part(1)], axis=1).astype(kvc_ref.dtype)
    ks_ref[...] = _rope(part(2), c, a, b).astype(ks_ref.dtype)
    kw_ref[...] = _rope(part(4), c, a, b).astype(kw_ref.dtype)
    vst_ref[0] = part(3).T.astype(vst_ref.dtype)
    vwt_ref[0] = part(5).T.astype(vwt_ref.dtype)
    gate_ref[...] = _sigmoid(proj(LANES))
    za = proj(W_NSA)
    sza_ref[...] = (za * _sigmoid(za)).astype(sza_ref.dtype)
    qb_ref[...] = _rope(proj(W_DIL), c, a, b).astype(qb_ref.dtype)
    kb_ref[...] = _rope(proj(W_DIL), c, a, b).astype(kb_ref.dtype)
    vb_ref[...] = proj(W_DIL).astype(vb_ref.dtype)
    zb = proj(W_DIL)
    szb_ref[...] = (zb * _sigmoid(zb)).astype(szb_ref.dtype)


def _in_proj(x2, g, w, c, a, b, batch):
    rows, d_model = x2.shape
    n_total = w.shape[1]
    tm = PROJ_ROWS
    seq = rows // batch
    per_batch = seq // tm
    bf16 = jnp.bfloat16
    row_spec = lambda wd: pl.BlockSpec((tm, wd), lambda i: (i, 0))
    t_spec = pl.BlockSpec((1, W_KV, tm), lambda i: (i // per_batch, 0, i % per_batch))
    row_out = lambda wd, dt=bf16: (row_spec(wd), jax.ShapeDtypeStruct((rows, wd), dt))
    t_out = (t_spec, jax.ShapeDtypeStruct((batch, W_KV, seq), bf16))
    outs = [row_out(W_NSA), row_out(2 * W_KV), row_out(W_KV), row_out(W_KV), t_out, t_out,
            row_out(LANES, jnp.float32), row_out(W_NSA),
            row_out(W_DIL), row_out(W_DIL), row_out(W_DIL), row_out(W_DIL)]
    return pl.pallas_call(
        _in_proj_kernel,
        grid=(rows // tm,),
        in_specs=[row_spec(d_model),
                  pl.BlockSpec((1, d_model), lambda i: (0, 0)),
                  pl.BlockSpec((d_model, n_total), lambda i: (0, 0)),
                  row_spec(LANES), row_spec(LANES), row_spec(LANES)],
        out_specs=[o[0] for o in outs],
        out_shape=[o[1] for o in outs],
        compiler_params=pltpu.CompilerParams(dimension_semantics=("arbitrary",),
                                             vmem_limit_bytes=VMEM_LIMIT),
        name="in_proj",
    )(x2, g, w, c, a, b)


def _compress_kernel(ch_ref, pos_ref, w1t_ref, w1b_ref, w2_ref, o_ref):
    ch = ch_ref[0, 0]
    n = ch.shape[0]
    half = pos_ref.shape[2] // 2
    pos = pos_ref[0]
    top = _dot(ch, w1t_ref[0])
    bot = _dot(ch, w1b_ref[0])
    bias = _dot(pos[:, :half], w1t_ref[0]) + _dot(pos[:, half:], w1b_ref[0])
    hid = top + pltpu.roll(bot, n - 1, 0) + bias[0:1, :]
    act = (hid * _sigmoid(hid)).astype(jnp.bfloat16)
    o_ref[0, 0] = _dot(act, w2_ref[0])


def _compress(chunks, pos8, w1t, w1b, w2):
    _, bg, n, width = chunks.shape
    return pl.pallas_call(
        _compress_kernel,
        grid=(2, bg),
        in_specs=[pl.BlockSpec((1, 1, n, width), lambda s, i: (s, i, 0, 0)),
                  pl.BlockSpec((1, 16, 2 * width), lambda s, i: (s, 0, 0)),
                  pl.BlockSpec((1, width, CMP_HIDDEN), lambda s, i: (s, 0, 0)),
                  pl.BlockSpec((1, width, CMP_HIDDEN), lambda s, i: (s, 0, 0)),
                  pl.BlockSpec((1, CMP_HIDDEN, HEAD_DIM), lambda s, i: (s, 0, 0))],
        out_specs=pl.BlockSpec((1, 1, n, HEAD_DIM), lambda s, i: (s, i, 0, 0)),
        out_shape=jax.ShapeDtypeStruct((2, bg, n, HEAD_DIM), jnp.float32),
        compiler_params=pltpu.CompilerParams(dimension_semantics=("arbitrary", "arbitrary")),
        name="compress",
    )(chunks, pos8, w1t, w1b, w2)


WIN_BLOCKS = -(-(SWA_WINDOW - 1) // BLOCK) + 1


def _nsa_attn_kernel(q_ref, kc_ref, vct_ref, et_ref, ks_ref, vst_ref, kw_ref, vwt_ref,
                     gate_ref, sza_ref, o_ref, s_a, s_b, p_a, p_b, gate_t):
    f32, bf16 = jnp.float32, jnp.bfloat16
    qb = pl.program_id(1)
    G, R = NSA_KV_HEADS, NSA_Q_PER_KV
    grows = R * BLOCK
    rows = G * grows
    gcols = lambda g: slice(g * grows, (g + 1) * grows)
    qf = q_ref[...].astype(f32)
    heads_t = []
    for h in range(NSA_HEADS):
        qh = qf[:, h * HEAD_DIM:(h + 1) * HEAD_DIM]
        heads_t.append(jnp.concatenate([qh, qh], axis=1).T)
    q_t = jnp.concatenate(heads_t, axis=1)
    q_cmp_t = q_t[0:HEAD_DIM].astype(bf16)
    own = ((lax.broadcasted_iota(jnp.int32, (LANES, rows), 0) >> 6)
           == (lax.broadcasted_iota(jnp.int32, (LANES, rows), 1) >> 9))
    q_kv_t = jnp.where(own, q_t, 0.0).astype(bf16)
    t_1 = qb * BLOCK + lax.broadcasted_iota(jnp.int32, (1, BLOCK), 1)
    head_bias = lambda ok, reps: jnp.tile(jnp.where(ok, 0.0, NEG_INF), (1, reps))

    ncmp = kc_ref.shape[1]
    pos = lax.broadcasted_iota(jnp.int32, (ncmp, BLOCK), 0)
    cmp_end = (pos & (LANES - 1)) * SEL_BLOCK + (pos >> 7) * CMP_STRIDE + (CMP_LEN - 1)
    cmp_bias = head_bias(cmp_end <= t_1, R)
    seen = jnp.tile(jnp.where(t_1 >= CMP_LEN - 1, 1.0, 0.0), (1, R))
    blk = lax.broadcasted_iota(jnp.int32, (LANES, BLOCK), 0)
    o_cmp, imp_sel = [], []
    for g in range(G):
        st = _dot(kc_ref[g], q_cmp_t[:, gcols(g)]) + cmp_bias
        mx = jnp.max(st, axis=0, keepdims=True)
        e = jnp.exp2(st - mx)
        den = jnp.maximum(jnp.sum(e, axis=0, keepdims=True), 1e-30)
        p = e * (seen / den)
        o_cmp.append(_dot(vct_ref[g], p.astype(bf16)))
        imp = p[:, 0:BLOCK]
        for r in range(1, R):
            imp = imp + p[:, r * BLOCK:(r + 1) * BLOCK]
        q4 = [imp[i * LANES:(i + 1) * LANES] for i in range(CMP_PER_SEL)]
        prev_last = jnp.where(blk == 0, 0.0, pltpu.roll(q4[3], 1, 0))
        imp_sel.append(prev_last + 2.0 * (q4[0] + q4[1] + q4[2]) + q4[3])

    span = WIN_BLOCKS * BLOCK
    first_blk = jnp.maximum(qb - (WIN_BLOCKS - 1), 0)
    win = pl.ds(pl.multiple_of(first_blk * BLOCK, BLOCK), span)
    dist = t_1 - (first_blk * BLOCK + lax.broadcasted_iota(jnp.int32, (span, BLOCK), 0))
    st = _dot(kw_ref[win, :], q_kv_t) + head_bias((dist >= 0) & (dist <= SWA_WINDOW - 1), G * R)
    mw = jnp.max(st, axis=0, keepdims=True)
    e = jnp.exp2(st - mw).astype(bf16)
    ones_win = jnp.ones((ONES_ROWS, span), bf16)
    o_win = []
    for g in range(G):
        r = _dot(jnp.concatenate([vwt_ref[0, g * HEAD_DIM:(g + 1) * HEAD_DIM, win], ones_win], axis=0),
                 e[:, gcols(g)])
        o_win.append(r[0:HEAD_DIM] * (1.0 / r[HEAD_DIM:HEAD_DIM + 1]))

    cur = (qb * BLOCK + lax.broadcasted_iota(jnp.int32, (LANES, BLOCK), 1)) >> 6
    forced = (blk == 0) | (blk == cur) | (blk == cur - 1)
    blk_f = blk.astype(f32)
    bias = []
    for g in range(G):
        sc = jnp.where(forced, -2.0, jnp.where(blk <= cur, imp_sel[g], -1.0))
        for _ in range(SEL_TOP_N - N_FORCED):
            best = jnp.max(sc, axis=0, keepdims=True)
            first = jnp.min(jnp.where(sc == best, blk_f, float(LANES)), axis=0, keepdims=True)
            sc = jnp.where(blk_f == first, -2.0, sc)
        taken = (sc == -2.0) & (blk <= cur)
        bias.append(jnp.tile(jnp.where(taken, 0.0, NEG_INF).astype(bf16), (1, R)))

    qa_t = jnp.concatenate([jnp.concatenate(bias, axis=1), q_kv_t], axis=0)
    n = (qb * BLOCK) // KEY_TILE
    last = jnp.maximum(n - 1, 0)
    ones = jnp.ones((ONES_ROWS, KEY_TILE), bf16)

    def keys_aug(kt):
        tile = pl.ds(pl.multiple_of(kt * KEY_TILE, KEY_TILE), KEY_TILE)
        return jnp.concatenate([et_ref[tile, :], ks_ref[tile, :]], axis=1)

    def qk(kt, s_ref):
        st = _dot(keys_aug(kt), qa_t)
        s_ref[...] = st
        return jnp.max(st, axis=0, keepdims=True)

    def pv(kt, p_ref):
        tile = pl.ds(pl.multiple_of(kt * KEY_TILE, KEY_TILE), KEY_TILE)
        return tuple(_dot(jnp.concatenate([vst_ref[0, g * HEAD_DIM:(g + 1) * HEAD_DIM, tile], ones], axis=0),
                          p_ref[:, gcols(g)]) for g in range(G))

    def softmax(m_old, mx, s_ref, p_ref):
        m_new = jnp.maximum(m_old, mx)
        p_ref[...] = jnp.exp2(s_ref[...] - m_new).astype(bf16)
        return m_new, jnp.exp2(m_old - m_new)

    def accumulate(acc, alpha, weight, contrib):
        return tuple(alpha[:, gcols(g)] * acc[g] + weight * contrib[g] for g in range(G))

    key = n * KEY_TILE + lax.broadcasted_iota(jnp.int32, (KEY_TILE, BLOCK), 0)
    st = _dot(keys_aug(n), qa_t) + head_bias(key <= t_1, G * R)
    m0 = jnp.max(st, axis=0, keepdims=True)
    p_b[...] = jnp.exp2(st - m0).astype(bf16)
    mx0 = qk(0, s_a)
    acc0 = tuple(jnp.zeros((HEAD_DIM + ONES_ROWS, grows), f32) for _ in range(G))
    one = jnp.ones_like(m0)

    def body(i, carry):
        m, acc, alpha_prev, w_prev, kt_prev, mx = carry
        first, second = 2 * i, 2 * i + 1
        w_second = jnp.where(second < n, 1.0, 0.0)
        kt_second = jnp.minimum(second, last)
        acc = accumulate(acc, alpha_prev, w_prev, pv(kt_prev, p_b))
        m, alpha = softmax(m, mx, s_a, p_a)
        mx = qk(kt_second, s_b)
        acc = accumulate(acc, alpha, 1.0, pv(first, p_a))
        m, alpha = softmax(m, mx, s_b, p_b)
        mx = qk(jnp.minimum(second + 1, last), s_a)
        return m, acc, alpha, w_second, kt_second, mx

    init = (m0, acc0, one, jnp.float32(1.0), n, mx0)
    _, acc, alpha_prev, w_prev, kt_prev, _ = lax.fori_loop(0, (n + 1) // 2, body, init)
    acc = accumulate(acc, alpha_prev, w_prev, pv(kt_prev, p_b))
    o_slc = [a[0:HEAD_DIM] * (1.0 / a[HEAD_DIM:HEAD_DIM + 1]) for a in acc]

    gate_t[...] = gate_ref[...].T
    zeros = jnp.zeros((LANES - HEAD_DIM, BLOCK), f32)
    outs = []
    for g in range(G):
        for r in range(R):
            cols = slice(r * BLOCK, (r + 1) * BLOCK)
            gate_row = lambda branch: gate_t[branch * NSA_HEADS + g * R + r:branch * NSA_HEADS + g * R + r + 1, :]
            o_t = (gate_row(0) * o_cmp[g][:, cols] + gate_row(1) * o_slc[g][:, cols]
                   + gate_row(2) * o_win[g][:, cols])
            outs.append(jnp.concatenate([o_t, zeros], axis=0).T[:, 0:HEAD_DIM])
    o_a = jnp.concatenate(outs, axis=1)
    o_ref[...] = (o_a * sza_ref[...].astype(f32)).astype(o_ref.dtype)


def _nsa_attn(qa, kcp, vcpt, onehot, ks, vst, kw, vwt, gate, sza, batch):
    rows_total = qa.shape[0]
    seq = rows_total // batch
    nqb = seq // BLOCK
    ncmp = kcp.shape[1]
    G = NSA_KV_HEADS
    rows = NSA_HEADS * BLOCK
    q_rows = lambda b, j: (b * nqb + j, 0)
    return pl.pallas_call(
        _nsa_attn_kernel,
        grid=(batch, nqb),
        in_specs=[pl.BlockSpec((BLOCK, W_NSA), q_rows),
                  pl.BlockSpec((G, ncmp, HEAD_DIM), lambda b, j: (b, 0, 0)),
                  pl.BlockSpec((G, HEAD_DIM, ncmp), lambda b, j: (b, 0, 0)),
                  pl.BlockSpec((seq, LANES), lambda b, j: (0, 0)),
                  pl.BlockSpec((seq, W_KV), lambda b, j: (b, 0)),
                  pl.BlockSpec((1, W_KV, seq), lambda b, j: (b, 0, 0)),
                  pl.BlockSpec((seq, W_KV), lambda b, j: (b, 0)),
                  pl.BlockSpec((1, W_KV, seq), lambda b, j: (b, 0, 0)),
                  pl.BlockSpec((BLOCK, LANES), q_rows),
                  pl.BlockSpec((BLOCK, W_NSA), q_rows)],
        out_specs=pl.BlockSpec((BLOCK, W_NSA), q_rows),
        out_shape=jax.ShapeDtypeStruct((rows_total, W_NSA), jnp.bfloat16),
        scratch_shapes=[pltpu.VMEM((KEY_TILE, rows), jnp.float32),
                        pltpu.VMEM((KEY_TILE, rows), jnp.float32),
                        pltpu.VMEM((KEY_TILE, rows), jnp.bfloat16),
                        pltpu.VMEM((KEY_TILE, rows), jnp.bfloat16),
                        pltpu.VMEM((LANES, BLOCK), jnp.float32)],
        compiler_params=pltpu.CompilerParams(dimension_semantics=("arbitrary", "arbitrary"),
                                             vmem_limit_bytes=VMEM_LIMIT),
        name="nsa_attn",
    )(qa, kcp, vcpt, onehot, ks, vst, kw, vwt, gate, sza)


DIL_MAX = max(d for _, d in DIL_PATTERNS)
DIL_SUPER = BLOCK * DIL_MAX
DIL_UNITS = DIL_SUPER // BLOCK
DIL_UNROLL = 16
HEAD_PAIR = 2 * HEAD_DIM
MIX_ROWS = 256


def _dil_mix_kernel(q_ref, kp_ref, kc_ref, vp_ref, vc_ref, z_ref, o_ref,
                    qf, kf, vf, num_scr, den_scr, max_scr, bias_scr):
    f32, bf16 = jnp.float32, jnp.bfloat16
    sb = pl.program_id(1)
    qf[...] = q_ref[...].astype(f32)
    kf[0:DIL_SUPER] = kp_ref[...].astype(f32)
    kf[DIL_SUPER:2 * DIL_SUPER] = kc_ref[...].astype(f32)
    vf[0:DIL_SUPER] = vp_ref[...].astype(f32)
    vf[DIL_SUPER:2 * DIL_SUPER] = vc_ref[...].astype(f32)

    row = lax.broadcasted_iota(jnp.int32, (2 * BLOCK, 2 * BLOCK), 0)
    col = lax.broadcasted_iota(jnp.int32, (2 * BLOCK, 2 * BLOCK), 1)
    dist = BLOCK + (row & (BLOCK - 1)) - col
    band = (dist >= 0) & (dist <= BLOCK)
    bias_scr[0] = jnp.where(band, 0.0, NEG_INF)
    bias_scr[1] = jnp.where(band & (col >= BLOCK), 0.0, NEG_INF)
    first_head = lax.broadcasted_iota(jnp.int32, (BLOCK, HEAD_PAIR), 1) < HEAD_DIM
    ones = jnp.ones((2 * BLOCK, HEAD_PAIR), bf16)

    for pat, (window, dil) in enumerate(DIL_PATTERNS):
        shift = dil.bit_length() - 1

        def unit(u, pat=pat, dil=dil, shift=shift):
            cls = u & (dil - 1)
            blk = u >> shift
            q_start = cls + blk * (BLOCK * dil)
            k_start = DIL_SUPER + q_start - BLOCK * dil
            q2 = qf[pl.ds(q_start, BLOCK, stride=dil), :]
            k2 = kf[pl.ds(k_start, 2 * BLOCK, stride=dil), :]
            v2 = vf[pl.ds(k_start, 2 * BLOCK, stride=dil), :]
            qm = jnp.concatenate([jnp.where(first_head, q2, 0.0),
                                  jnp.where(first_head, 0.0, q2)], axis=0).astype(bf16)
            s = _dot_nt(qm, k2.astype(bf16))
            no_prev = jnp.where((sb == 0) & (blk == 0), 1, 0)
            s = s + bias_scr[no_prev]
            m = jnp.max(s, axis=1, keepdims=True)
            e = jnp.exp2(s - m).astype(bf16)
            r = _dot(e, jnp.concatenate([v2.astype(bf16), ones], axis=1))
            mb = jnp.broadcast_to(m, (2 * BLOCK, HEAD_PAIR))
            out_rows = pl.ds(q_start, BLOCK, stride=dil)
            num_scr[pat, out_rows, :] = jnp.where(first_head, r[0:BLOCK, 0:HEAD_PAIR], r[BLOCK:, 0:HEAD_PAIR])
            den_scr[pat, out_rows, :] = jnp.where(first_head, r[0:BLOCK, HEAD_PAIR:], r[BLOCK:, HEAD_PAIR:])
            max_scr[pat, out_rows, :] = jnp.where(first_head, mb[0:BLOCK], mb[BLOCK:])

        def trip(it, carry, unit=unit):
            for j in range(DIL_UNROLL):
                unit(it * DIL_UNROLL + j)
            return carry

        lax.fori_loop(0, DIL_UNITS // DIL_UNROLL, trip, 0)

    def mix(ci, carry):
        rows = pl.ds(pl.multiple_of(ci * MIX_ROWS, MIX_ROWS), MIX_ROWS)
        ms = [max_scr[p, rows, :] for p in range(len(DIL_PATTERNS))]
        mx = jnp.maximum(jnp.maximum(ms[0], ms[1]), ms[2])
        cs = [jnp.exp2(m - mx) for m in ms]
        num = cs[0] * num_scr[0, rows, :] + cs[1] * num_scr[1, rows, :] + cs[2] * num_scr[2, rows, :]
        den = cs[0] * den_scr[0, rows, :] + cs[1] * den_scr[1, rows, :] + cs[2] * den_scr[2, rows, :]
        o_ref[rows, :] = (num / den * z_ref[rows, :].astype(f32)).astype(o_ref.dtype)
        return carry

    lax.fori_loop(0, DIL_SUPER // MIX_ROWS, mix, 0)


def _dil_mix(qb, kb, vb, szb, batch):
    rows = qb.shape[0]
    nsb = rows // batch // DIL_SUPER
    cur = pl.BlockSpec((DIL_SUPER, HEAD_PAIR), lambda b, s, h: (b * nsb + s, h))
    prev = pl.BlockSpec((DIL_SUPER, HEAD_PAIR), lambda b, s, h: (b * nsb + jnp.maximum(s - 1, 0), h))
    f32 = jnp.float32
    return pl.pallas_call(
        _dil_mix_kernel,
        grid=(batch, nsb, W_DIL // HEAD_PAIR),
        in_specs=[cur, prev, cur, prev, cur, cur],
        out_specs=cur,
        out_shape=jax.ShapeDtypeStruct((rows, W_DIL), jnp.bfloat16),
        scratch_shapes=[pltpu.VMEM((DIL_SUPER, HEAD_PAIR), f32),
                        pltpu.VMEM((2 * DIL_SUPER, HEAD_PAIR), f32),
                        pltpu.VMEM((2 * DIL_SUPER, HEAD_PAIR), f32),
                        pltpu.VMEM((len(DIL_PATTERNS), DIL_SUPER, HEAD_PAIR), f32),
                        pltpu.VMEM((len(DIL_PATTERNS), DIL_SUPER, HEAD_PAIR), f32),
                        pltpu.VMEM((len(DIL_PATTERNS), DIL_SUPER, HEAD_PAIR), f32),
                        pltpu.VMEM((2, 2 * BLOCK, 2 * BLOCK), f32)],
        compiler_params=pltpu.CompilerParams(
            dimension_semantics=("arbitrary", "arbitrary", "arbitrary"),
            vmem_limit_bytes=VMEM_LIMIT),
        name="dil_mix",
    )(qb, kb, kb, vb, vb, szb)


def _out_proj_kernel(x_ref, ma_ref, mb_ref, w_ref, g_ref, out_ref):
    y = _dot(ma_ref[...], w_ref[0:W_NSA, :]) + _dot(mb_ref[...], w_ref[W_NSA:W_NSA + W_DIL, :])
    ms = jnp.mean(y * y, axis=-1, keepdims=True)
    out_ref[...] = x_ref[...] + y * lax.rsqrt(ms + RMS_EPS) * g_ref[...]


def _out_proj(x2, mixed_a, mixed_b, w, g):
    rows, d_model = x2.shape
    tm = PROJ_ROWS
    row_spec = lambda wd: pl.BlockSpec((tm, wd), lambda i: (i, 0))
    return pl.pallas_call(
        _out_proj_kernel,
        grid=(rows // tm,),
        in_specs=[row_spec(d_model), row_spec(W_NSA), row_spec(W_DIL),
                  pl.BlockSpec(w.shape, lambda i: (0, 0)),
                  pl.BlockSpec((1, d_model), lambda i: (0, 0))],
        out_specs=row_spec(d_model),
        out_shape=jax.ShapeDtypeStruct((rows, d_model), jnp.float32),
        compiler_params=pltpu.CompilerParams(dimension_semantics=("arbitrary",)),
        name="out_proj",
    )(x2, mixed_a, mixed_b, w, g)


def _rope_tables(positions):
    inv = 1.0 / (ROPE_THETA ** (jnp.arange(0, ROPE_DIMS, 2, dtype=jnp.float32) / ROPE_DIMS))
    ang = positions.astype(jnp.float32).reshape(-1)[:, None] * inv
    cos, sin = jnp.cos(ang), jnp.sin(ang)
    rest = HEAD_DIM - ROPE_DIMS
    one = jnp.ones((cos.shape[0], rest), jnp.float32)
    zero = jnp.zeros((cos.shape[0], rest), jnp.float32)
    zh = jnp.zeros_like(sin)
    per_head = lambda lo, hi, fill: jnp.tile(jnp.concatenate([lo, hi, fill], axis=1), (1, 2))
    return per_head(cos, cos, one), per_head(-sin, zh, zero), per_head(zh, sin, zero)


def _in_proj_weights(w_in):
    scale = HEAD_DIM ** -0.5 * LOG2_E
    o = 0
    cols = {}
    for name, width in (("qa", W_NSA), ("kva", 6 * W_KV), ("gate", 3 * NSA_HEADS), ("za", W_NSA),
                        ("qb", W_DIL), ("kb", W_DIL), ("vb", W_DIL), ("zb", W_DIL)):
        cols[name] = w_in[:, o:o + width]
        o += width
    gate = jnp.pad(cols["gate"], ((0, 0), (0, LANES - 3 * NSA_HEADS)))
    w = jnp.concatenate([cols["qa"] * scale, cols["kva"], gate, cols["za"],
                         cols["qb"] * scale, cols["kb"], cols["vb"], cols["zb"]], axis=1)
    return w.astype(jnp.bfloat16)


def kernel(x, positions, pre_norm_g, w_in, cmp_k_pos, cmp_k_w1, cmp_k_w2,
           cmp_v_pos, cmp_v_w1, cmp_v_w2, w_out, post_norm_g):
    B, S, d_model = x.shape
    G, D = NSA_KV_HEADS, HEAD_DIM
    depth = w_in.shape[0]
    n_sel = S // SEL_BLOCK
    n_chunks = S // CMP_STRIDE
    assert S % KEY_TILE == 0 and n_sel <= MAX_SEL_BLOCKS and S >= WIN_BLOCKS * BLOCK
    assert S % DIL_SUPER == 0 and all(win // dil == BLOCK for win, dil in DIL_PATTERNS)
    bf16 = jnp.bfloat16

    rope_c, rope_a, rope_b = _rope_tables(positions)
    onehot = (jnp.arange(S)[:, None] // SEL_BLOCK == jnp.arange(LANES)[None, :]).astype(bf16)
    x2 = x.reshape(B * S, d_model)

    for layer in range(depth):
        qa, kvc, ks, kw, vst, vwt, gate, sza, qb_, kb_, vb_, szb = _in_proj(
            x2, pre_norm_g[layer][None, :], _in_proj_weights(w_in[layer]), rope_c, rope_a, rope_b, B)

        chunks = kvc.reshape(B, S, 2, G, D).transpose(2, 0, 3, 1, 4).reshape(2, B * G, n_chunks, CMP_STRIDE * D)
        pos = jnp.stack([cmp_k_pos[layer], cmp_v_pos[layer]]).reshape(2, 1, CMP_LEN * D)
        pos8 = jnp.broadcast_to(pos, (2, 16, CMP_LEN * D)).astype(bf16)
        w1 = jnp.stack([cmp_k_w1[layer], cmp_v_w1[layer]]).astype(bf16)
        w2 = jnp.stack([cmp_k_w2[layer], cmp_v_w2[layer]]).astype(bf16)
        half = CMP_STRIDE * D
        kvc = _compress(chunks, pos8, w1[:, :half], w1[:, half:], w2)
        kvc = kvc.reshape(2, B * G, n_sel, CMP_PER_SEL, D)
        kvc = jnp.pad(kvc, ((0, 0), (0, 0), (0, MAX_SEL_BLOCKS - n_sel), (0, 0), (0, 0)))
        kvc = kvc.transpose(0, 1, 3, 2, 4).reshape(2, B * G, CMP_PER_SEL * MAX_SEL_BLOCKS, D).astype(bf16)

        mixed_a = _nsa_attn(qa, kvc[0], kvc[1].transpose(0, 2, 1), onehot, ks, vst, kw, vwt, gate, sza, B)
        mixed_b = _dil_mix(qb_, kb_, vb_, szb, B)
        x2 = _out_proj(x2, mixed_a, mixed_b, w_out[layer].astype(bf16), post_norm_g[layer][None, :])
    return x2.reshape(B, S, d_model)
```

```python
import jax
import jax.numpy as jnp
from jax import lax
from jax.experimental import pallas as pl
from jax.experimental.pallas import tpu as pltpu

HEAD_DIM = 64
NSA_HEADS = 8
NSA_KV_HEADS = 2
NSA_Q_PER_KV = NSA_HEADS // NSA_KV_HEADS
DIL_HEADS = 8
W_NSA = NSA_HEADS * HEAD_DIM
W_KV = NSA_KV_HEADS * HEAD_DIM
W_DIL = DIL_HEADS * HEAD_DIM
CMP_LEN = 32
CMP_STRIDE = 16
CMP_HIDDEN = 256
SEL_BLOCK = 64
SEL_TOP_N = 16
N_FORCED = 3
SWA_WINDOW = 512
DIL_PATTERNS = ((128, 1), (512, 4), (2048, 16))
BLOCK = 128
ROPE_THETA = 500000.0
ROPE_DIMS = HEAD_DIM // 4
RMS_EPS = 1e-6
NEG_INF = -1e30
FORCE_SCORE = 1e4
LOG2_E = 1.4426950408889634

LANES = 128
VMEM_LIMIT = 56 * 1024 * 1024
MAX_SEL_BLOCKS = LANES
CMP_PER_SEL = SEL_BLOCK // CMP_STRIDE
KEY_TILE = 512
PROJ_ROWS = 512
ONES_ROWS = 16

_NT = (((1,), (1,)), ((), ()))


def _dot(a, b):
    return jnp.dot(a, b, preferred_element_type=jnp.float32)


def _dot_nt(a, b):
    return lax.dot_general(a, b, _NT, preferred_element_type=jnp.float32)


def _sigmoid(x):
    return 1.0 / (1.0 + jnp.exp(-x))


def _rope(x, c, a, b):
    width = x.shape[1]
    reps = width // LANES
    ct = jnp.tile(c, (1, reps))
    at = jnp.tile(a, (1, reps))
    bt = jnp.tile(b, (1, reps))
    half = ROPE_DIMS // 2
    return x * ct + pltpu.roll(x, width - half, 1) * at + pltpu.roll(x, half, 1) * bt


def _in_proj_kernel(x_ref, g_ref, w_ref, c_ref, a_ref, b_ref,
                    qa_ref, kvc_ref, ks_ref, kw_ref, vst_ref, vwt_ref, gate_ref, sza_ref,
                    qb_ref, kb_ref, vb_ref, szb_ref):
    x = x_ref[...]
    ms = jnp.mean(x * x, axis=-1, keepdims=True)
    h = (x * lax.rsqrt(ms + RMS_EPS) * g_ref[...]).astype(jnp.bfloat16)
    c = c_ref[...]
    a = a_ref[...]
    b = b_ref[...]
    off = 0

    def proj(width):
        nonlocal off
        r = _dot(h, w_ref[:, off:off + width])
        off += width
        return r

    qa_ref[0] = _rope(proj(W_NSA), c, a, b).T.astype(qa_ref.dtype)
    kva = proj(6 * W_KV)
    part = lambda i: kva[:, i * W_KV:(i + 1) * W_KV]
    kvc_ref[...] = jnp.concatenate([_rope(part(0), c, a, b), part(1)], axis=1).astype(kvc_ref.dtype)
    ks_ref[...] = _rope(part(2), c, a, b).astype(ks_ref.dtype)
    kw_ref[...] = _rope(part(4), c, a, b).astype(kw_ref.dtype)
    vst_ref[0] = part(3).T.astype(vst_ref.dtype)
    vwt_ref[0] = part(5).T.astype(vwt_ref.dtype)
    gate_ref[0] = _sigmoid(proj(LANES)).T
    za = proj(W_NSA)
    sza_ref[0] = (za * _sigmoid(za)).T.astype(sza_ref.dtype)
    qb_ref[...] = _rope(proj(W_DIL), c, a, b).astype(qb_ref.dtype)
    kb_ref[...] = _rope(proj(W_DIL), c, a, b).astype(kb_ref.dtype)
    vb_ref[...] = proj(W_DIL).astype(vb_ref.dtype)
    zb = proj(W_DIL)
    szb_ref[...] = (zb * _sigmoid(zb)).astype(szb_ref.dtype)


def _in_proj(x2, g, w, c, a, b, batch):
    rows, d_model = x2.shape
    n_total = w.shape[1]
    tm = PROJ_ROWS
    seq = rows // batch
    per_batch = seq // tm
    bf16 = jnp.bfloat16
    row_spec = lambda wd: pl.BlockSpec((tm, wd), lambda i: (i, 0))
    row_out = lambda wd: (row_spec(wd), jax.ShapeDtypeStruct((rows, wd), bf16))
    t_out = lambda wd, dt=bf16: (pl.BlockSpec((1, wd, tm), lambda i: (i // per_batch, 0, i % per_batch)),
                                 jax.ShapeDtypeStruct((batch, wd, seq), dt))
    outs = [t_out(W_NSA), row_out(2 * W_KV), row_out(W_KV), row_out(W_KV), t_out(W_KV), t_out(W_KV),
            t_out(LANES, jnp.float32), t_out(W_NSA),
            row_out(W_DIL), row_out(W_DIL), row_out(W_DIL), row_out(W_DIL)]
    return pl.pallas_call(
        _in_proj_kernel,
        grid=(rows // tm,),
        in_specs=[row_spec(d_model),
                  pl.BlockSpec((1, d_model), lambda i: (0, 0)),
                  pl.BlockSpec((d_model, n_total), lambda i: (0, 0)),
                  row_spec(LANES), row_spec(LANES), row_spec(LANES)],
        out_specs=[o[0] for o in outs],
        out_shape=[o[1] for o in outs],
        compiler_params=pltpu.CompilerParams(dimension_semantics=("arbitrary",),
                                             vmem_limit_bytes=VMEM_LIMIT),
        name="in_proj",
    )(x2, g, w, c, a, b)


def _compress_kernel(ch_ref, pos_ref, w1t_ref, w1b_ref, w2_ref, o_ref):
    ch = ch_ref[0, 0]
    n = ch.shape[0]
    half = pos_ref.shape[2] // 2
    pos = pos_ref[0]
    top = _dot(ch, w1t_ref[0])
    bot = _dot(ch, w1b_ref[0])
    bias = _dot(pos[:, :half], w1t_ref[0]) + _dot(pos[:, half:], w1b_ref[0])
    hid = top + pltpu.roll(bot, n - 1, 0) + bias[0:1, :]
    act = (hid * _sigmoid(hid)).astype(jnp.bfloat16)
    o_ref[0, 0] = _dot(act, w2_ref[0])


def _compress(chunks, pos8, w1t, w1b, w2):
    _, bg, n, width = chunks.shape
    return pl.pallas_call(
        _compress_kernel,
        grid=(2, bg),
        in_specs=[pl.BlockSpec((1, 1, n, width), lambda s, i: (s, i, 0, 0)),
                  pl.BlockSpec((1, 16, 2 * width), lambda s, i: (s, 0, 0)),
                  pl.BlockSpec((1, width, CMP_HIDDEN), lambda s, i: (s, 0, 0)),
                  pl.BlockSpec((1, width, CMP_HIDDEN), lambda s, i: (s, 0, 0)),
                  pl.BlockSpec((1, CMP_HIDDEN, HEAD_DIM), lambda s, i: (s, 0, 0))],
        out_specs=pl.BlockSpec((1, 1, n, HEAD_DIM), lambda s, i: (s, i, 0, 0)),
        out_shape=jax.ShapeDtypeStruct((2, bg, n, HEAD_DIM), jnp.float32),
        compiler_params=pltpu.CompilerParams(dimension_semantics=("arbitrary", "arbitrary")),
        name="compress",
    )(chunks, pos8, w1t, w1b, w2)


WIN_BLOCKS = -(-(SWA_WINDOW - 1) // BLOCK) + 1


def _nsa_attn_kernel(q_ref, kc_ref, vct_ref, et_ref, ks_ref, vst_ref, kw_ref, vwt_ref,
                     gate_ref, sza_ref, o_ref, s_a, s_b, p_a, p_b):
    f32, bf16 = jnp.float32, jnp.bfloat16
    qb = pl.program_id(1)
    G, R = NSA_KV_HEADS, NSA_Q_PER_KV
    grows = R * BLOCK
    rows = G * grows
    gcols = lambda g: slice(g * grows, (g + 1) * grows)
    heads_t = [q_ref[0, h * HEAD_DIM:(h + 1) * HEAD_DIM, :] for h in range(NSA_HEADS)]
    q_cmp_t = jnp.concatenate(heads_t, axis=1)
    zero_t = jnp.zeros((HEAD_DIM, BLOCK), bf16)
    q_kv_t = jnp.concatenate([jnp.concatenate([qh, zero_t] if h < R else [zero_t, qh], axis=0)
                              for h, qh in enumerate(heads_t)], axis=1)
    t_1 = qb * BLOCK + lax.broadcasted_iota(jnp.int32, (1, BLOCK), 1)
    head_bias = lambda ok, reps: jnp.tile(jnp.where(ok, 0.0, NEG_INF), (1, reps))

    ncmp = kc_ref.shape[1]
    pos = lax.broadcasted_iota(jnp.int32, (ncmp, BLOCK), 0)
    cmp_end = (pos & (LANES - 1)) * SEL_BLOCK + (pos >> 7) * CMP_STRIDE + (CMP_LEN - 1)
    cmp_bias = head_bias(cmp_end <= t_1, R)
    seen = jnp.tile(jnp.where(t_1 >= CMP_LEN - 1, 1.0, 0.0), (1, R))
    blk = lax.broadcasted_iota(jnp.int32, (LANES, BLOCK), 0)
    o_cmp, imp_sel = [], []
    for g in range(G):
        st = _dot(kc_ref[g], q_cmp_t[:, gcols(g)]) + cmp_bias
        mx = jnp.max(st, axis=0, keepdims=True)
        e = jnp.exp2(st - mx)
        den = jnp.maximum(jnp.sum(e, axis=0, keepdims=True), 1e-30)
        p = e * (seen / den)
        o_cmp.append(_dot(vct_ref[g], p.astype(bf16)))
        imp = p[:, 0:BLOCK]
        for r in range(1, R):
            imp = imp + p[:, r * BLOCK:(r + 1) * BLOCK]
        q4 = [imp[i * LANES:(i + 1) * LANES] for i in range(CMP_PER_SEL)]
        prev_last = jnp.where(blk == 0, 0.0, pltpu.roll(q4[3], 1, 0))
        imp_sel.append(prev_last + 2.0 * (q4[0] + q4[1] + q4[2]) + q4[3])

    span = WIN_BLOCKS * BLOCK
    first_blk = jnp.maximum(qb - (WIN_BLOCKS - 1), 0)
    win = pl.ds(pl.multiple_of(first_blk * BLOCK, BLOCK), span)
    dist = t_1 - (first_blk * BLOCK + lax.broadcasted_iota(jnp.int32, (span, BLOCK), 0))
    st = _dot(kw_ref[win, :], q_kv_t) + head_bias((dist >= 0) & (dist <= SWA_WINDOW - 1), G * R)
    mw = jnp.max(st, axis=0, keepdims=True)
    e = jnp.exp2(st - mw).astype(bf16)
    ones_win = jnp.ones((ONES_ROWS, span), bf16)
    o_win = []
    for g in range(G):
        r = _dot(jnp.concatenate([vwt_ref[0, g * HEAD_DIM:(g + 1) * HEAD_DIM, win], ones_win], axis=0),
                 e[:, gcols(g)])
        o_win.append(r[0:HEAD_DIM] * (1.0 / r[HEAD_DIM:HEAD_DIM + 1]))

    cur = (qb * BLOCK + lax.broadcasted_iota(jnp.int32, (LANES, BLOCK), 1)) >> 6
    forced = (blk == 0) | (blk == cur) | (blk == cur - 1)
    blk_f = blk.astype(f32)
    bias = []
    for g in range(G):
        sc = jnp.where(forced, -2.0, jnp.where(blk <= cur, imp_sel[g], -1.0))
        for _ in range(SEL_TOP_N - N_FORCED):
            best = jnp.max(sc, axis=0, keepdims=True)
            first = jnp.min(jnp.where(sc == best, blk_f, float(LANES)), axis=0, keepdims=True)
            sc = jnp.where(blk_f == first, -2.0, sc)
        taken = (sc == -2.0) & (blk <= cur)
        bias.append(jnp.tile(jnp.where(taken, 0.0, NEG_INF).astype(bf16), (1, R)))

    qa_t = jnp.concatenate([jnp.concatenate(bias, axis=1), q_kv_t], axis=0)
    n = (qb * BLOCK) // KEY_TILE
    last = jnp.maximum(n - 1, 0)
    ones = jnp.ones((ONES_ROWS, KEY_TILE), bf16)

    def keys_aug(kt):
        tile = pl.ds(pl.multiple_of(kt * KEY_TILE, KEY_TILE), KEY_TILE)
        return jnp.concatenate([et_ref[tile, :], ks_ref[tile, :]], axis=1)

    def qk(kt, s_ref):
        st = _dot(keys_aug(kt), qa_t)
        s_ref[...] = st
        return jnp.max(st, axis=0, keepdims=True)

    def pv(kt, p_ref):
        tile = pl.ds(pl.multiple_of(kt * KEY_TILE, KEY_TILE), KEY_TILE)
        return tuple(_dot(jnp.concatenate([vst_ref[0, g * HEAD_DIM:(g + 1) * HEAD_DIM, tile], ones], axis=0),
                          p_ref[:, gcols(g)]) for g in range(G))

    def softmax(m_old, mx, s_ref, p_ref):
        m_new = jnp.maximum(m_old, mx)
        p_ref[...] = jnp.exp2(s_ref[...] - m_new).astype(bf16)
        return m_new, jnp.exp2(m_old - m_new)

    def accumulate(acc, alpha, weight, contrib):
        return tuple(alpha[:, gcols(g)] * acc[g] + weight * contrib[g] for g in range(G))

    key = n * KEY_TILE + lax.broadcasted_iota(jnp.int32, (KEY_TILE, BLOCK), 0)
    st = _dot(keys_aug(n), qa_t) + head_bias(key <= t_1, G * R)
    m0 = jnp.max(st, axis=0, keepdims=True)
    p_b[...] = jnp.exp2(st - m0).astype(bf16)
    mx0 = qk(0, s_a)
    acc0 = tuple(jnp.zeros((HEAD_DIM + ONES_ROWS, grows), f32) for _ in range(G))
    one = jnp.ones_like(m0)

    def body(i, carry):
        m, acc, alpha_prev, w_prev, kt_prev, mx = carry
        first, second = 2 * i, 2 * i + 1
        w_second = jnp.where(second < n, 1.0, 0.0)
        kt_second = jnp.minimum(second, last)
        acc = accumulate(acc, alpha_prev, w_prev, pv(kt_prev, p_b))
        m, alpha = softmax(m, mx, s_a, p_a)
        mx = qk(kt_second, s_b)
        acc = accumulate(acc, alpha, 1.0, pv(first, p_a))
        m, alpha = softmax(m, mx, s_b, p_b)
        mx = qk(jnp.minimum(second + 1, last), s_a)
        return m, acc, alpha, w_second, kt_second, mx

    init = (m0, acc0, one, jnp.float32(1.0), n, mx0)
    _, acc, alpha_prev, w_prev, kt_prev, _ = lax.fori_loop(0, (n + 1) // 2, body, init)
    acc = accumulate(acc, alpha_prev, w_prev, pv(kt_prev, p_b))
    o_slc = [a[0:HEAD_DIM] * (1.0 / a[HEAD_DIM:HEAD_DIM + 1]) for a in acc]

    for g in range(G):
        for r in range(R):
            h = g * R + r
            cols = slice(r * BLOCK, (r + 1) * BLOCK)
            gate_row = lambda branch: gate_ref[0, branch * NSA_HEADS + h:branch * NSA_HEADS + h + 1, :]
            o_t = (gate_row(0) * o_cmp[g][:, cols] + gate_row(1) * o_slc[g][:, cols]
                   + gate_row(2) * o_win[g][:, cols])
            dims = slice(h * HEAD_DIM, (h + 1) * HEAD_DIM)
            o_ref[0, dims, :] = (o_t * sza_ref[0, dims, :].astype(f32)).astype(o_ref.dtype)


def _nsa_attn(qa, kcp, vcpt, onehot, ks, vst, kw, vwt, gate, sza, batch):
    seq = qa.shape[2]
    nqb = seq // BLOCK
    ncmp = kcp.shape[1]
    G = NSA_KV_HEADS
    rows = NSA_HEADS * BLOCK
    t_block = lambda wd: pl.BlockSpec((1, wd, BLOCK), lambda b, j: (b, 0, j))
    return pl.pallas_call(
        _nsa_attn_kernel,
        grid=(batch, nqb),
        in_specs=[t_block(W_NSA),
                  pl.BlockSpec((G, ncmp, HEAD_DIM), lambda b, j: (b, 0, 0)),
                  pl.BlockSpec((G, HEAD_DIM, ncmp), lambda b, j: (b, 0, 0)),
                  pl.BlockSpec((seq, LANES), lambda b, j: (0, 0)),
                  pl.BlockSpec((seq, W_KV), lambda b, j: (b, 0)),
                  pl.BlockSpec((1, W_KV, seq), lambda b, j: (b, 0, 0)),
                  pl.BlockSpec((seq, W_KV), lambda b, j: (b, 0)),
                  pl.BlockSpec((1, W_KV, seq), lambda b, j: (b, 0, 0)),
                  t_block(LANES),
                  t_block(W_NSA)],
        out_specs=t_block(W_NSA),
        out_shape=jax.ShapeDtypeStruct((batch, W_NSA, seq), jnp.bfloat16),
        scratch_shapes=[pltpu.VMEM((KEY_TILE, rows), jnp.float32),
                        pltpu.VMEM((KEY_TILE, rows), jnp.float32),
                        pltpu.VMEM((KEY_TILE, rows), jnp.bfloat16),
                        pltpu.VMEM((KEY_TILE, rows), jnp.bfloat16)],
        compiler_params=pltpu.CompilerParams(dimension_semantics=("arbitrary", "arbitrary"),
                                             vmem_limit_bytes=VMEM_LIMIT),
        name="nsa_attn",
    )(qa, kcp, vcpt, onehot, ks, vst, kw, vwt, gate, sza)


DIL_MAX = max(d for _, d in DIL_PATTERNS)
DIL_SUPER = BLOCK * DIL_MAX
DIL_UNITS = DIL_SUPER // BLOCK
DIL_UNROLL = 16
HEAD_PAIR = 2 * HEAD_DIM
MIX_ROWS = 256


def _dil_mix_kernel(q_ref, kp_ref, kc_ref, vp_ref, vc_ref, z_ref, o_ref,
                    qf, kf, vf, num_scr, den_scr, max_scr, bias_scr):
    f32, bf16 = jnp.float32, jnp.bfloat16
    sb = pl.program_id(1)
    qf[...] = q_ref[...].astype(f32)
    kf[0:DIL_SUPER] = kp_ref[...].astype(f32)
    kf[DIL_SUPER:2 * DIL_SUPER] = kc_ref[...].astype(f32)
    vf[0:DIL_SUPER] = vp_ref[...].astype(f32)
    vf[DIL_SUPER:2 * DIL_SUPER] = vc_ref[...].astype(f32)

    row = lax.broadcasted_iota(jnp.int32, (2 * BLOCK, 2 * BLOCK), 0)
    col = lax.broadcasted_iota(jnp.int32, (2 * BLOCK, 2 * BLOCK), 1)
    dist = BLOCK + (row & (BLOCK - 1)) - col
    band = (dist >= 0) & (dist <= BLOCK)
    bias_scr[0] = jnp.where(band, 0.0, NEG_INF)
    bias_scr[1] = jnp.where(band & (col >= BLOCK), 0.0, NEG_INF)
    first_head = lax.broadcasted_iota(jnp.int32, (BLOCK, HEAD_PAIR), 1) < HEAD_DIM
    ones = jnp.ones((2 * BLOCK, HEAD_PAIR), bf16)

    for pat, (window, dil) in enumerate(DIL_PATTERNS):
        shift = dil.bit_length() - 1

        def unit(u, pat=pat, dil=dil, shift=shift):
            cls = u & (dil - 1)
            blk = u >> shift
            q_start = cls + blk * (BLOCK * dil)
            k_start = DIL_SUPER + q_start - BLOCK * dil
            q2 = qf[pl.ds(q_start, BLOCK, stride=dil), :]
            k2 = kf[pl.ds(k_start, 2 * BLOCK, stride=dil), :]
            v2 = vf[pl.ds(k_start, 2 * BLOCK, stride=dil), :]
            qm = jnp.concatenate([jnp.where(first_head, q2, 0.0),
                                  jnp.where(first_head, 0.0, q2)], axis=0).astype(bf16)
            s = _dot_nt(qm, k2.astype(bf16))
            no_prev = jnp.where((sb == 0) & (blk == 0), 1, 0)
            s = s + bias_scr[no_prev]
            m = jnp.max(s, axis=1, keepdims=True)
            e = jnp.exp2(s - m).astype(bf16)
            r = _dot(e, jnp.concatenate([v2.astype(bf16), ones], axis=1))
            mb = jnp.broadcast_to(m, (2 * BLOCK, HEAD_PAIR))
            out_rows = pl.ds(q_start, BLOCK, stride=dil)
            num_scr[pat, out_rows, :] = jnp.where(first_head, r[0:BLOCK, 0:HEAD_PAIR], r[BLOCK:, 0:HEAD_PAIR])
            den_scr[pat, out_rows, :] = jnp.where(first_head, r[0:BLOCK, HEAD_PAIR:], r[BLOCK:, HEAD_PAIR:])
            max_scr[pat, out_rows, :] = jnp.where(first_head, mb[0:BLOCK], mb[BLOCK:])

        def trip(it, carry, unit=unit):
            for j in range(DIL_UNROLL):
                unit(it * DIL_UNROLL + j)
            return carry

        lax.fori_loop(0, DIL_UNITS // DIL_UNROLL, trip, 0)

    def mix(ci, carry):
        rows = pl.ds(pl.multiple_of(ci * MIX_ROWS, MIX_ROWS), MIX_ROWS)
        ms = [max_scr[p, rows, :] for p in range(len(DIL_PATTERNS))]
        mx = jnp.maximum(jnp.maximum(ms[0], ms[1]), ms[2])
        cs = [jnp.exp2(m - mx) for m in ms]
        num = cs[0] * num_scr[0, rows, :] + cs[1] * num_scr[1, rows, :] + cs[2] * num_scr[2, rows, :]
        den = cs[0] * den_scr[0, rows, :] + cs[1] * den_scr[1, rows, :] + cs[2] * den_scr[2, rows, :]
        o_ref[rows, :] = (num / den * z_ref[rows, :].astype(f32)).astype(o_ref.dtype)
        return carry

    lax.fori_loop(0, DIL_SUPER // MIX_ROWS, mix, 0)


def _dil_mix(qb, kb, vb, szb, batch):
    rows = qb.shape[0]
    nsb = rows // batch // DIL_SUPER
    cur = pl.BlockSpec((DIL_SUPER, HEAD_PAIR), lambda b, s, h: (b * nsb + s, h))
    prev = pl.BlockSpec((DIL_SUPER, HEAD_PAIR), lambda b, s, h: (b * nsb + jnp.maximum(s - 1, 0), h))
    f32 = jnp.float32
    return pl.pallas_call(
        _dil_mix_kernel,
        grid=(batch, nsb, W_DIL // HEAD_PAIR),
        in_specs=[cur, prev, cur, prev, cur, cur],
        out_specs=cur,
        out_shape=jax.ShapeDtypeStruct((rows, W_DIL), jnp.bfloat16),
        scratch_shapes=[pltpu.VMEM((DIL_SUPER, HEAD_PAIR), f32),
                        pltpu.VMEM((2 * DIL_SUPER, HEAD_PAIR), f32),
                        pltpu.VMEM((2 * DIL_SUPER, HEAD_PAIR), f32),
                        pltpu.VMEM((len(DIL_PATTERNS), DIL_SUPER, HEAD_PAIR), f32),
                        pltpu.VMEM((len(DIL_PATTERNS), DIL_SUPER, HEAD_PAIR), f32),
                        pltpu.VMEM((len(DIL_PATTERNS), DIL_SUPER, HEAD_PAIR), f32),
                        pltpu.VMEM((2, 2 * BLOCK, 2 * BLOCK), f32)],
        compiler_params=pltpu.CompilerParams(
            dimension_semantics=("arbitrary", "arbitrary", "arbitrary"),
            vmem_limit_bytes=VMEM_LIMIT),
        name="dil_mix",
    )(qb, kb, kb, vb, vb, szb)


def _out_proj_kernel(x_ref, ma_ref, mb_ref, w_ref, g_ref, out_ref):
    y = lax.dot_general(ma_ref[0], w_ref[0:W_NSA, :], (((0,), (0,)), ((), ())),
                        preferred_element_type=jnp.float32)
    y = y + _dot(mb_ref[...], w_ref[W_NSA:W_NSA + W_DIL, :])
    ms = jnp.mean(y * y, axis=-1, keepdims=True)
    out_ref[...] = x_ref[...] + y * lax.rsqrt(ms + RMS_EPS) * g_ref[...]


def _out_proj(x2, mixed_a_t, mixed_b, w, g):
    rows, d_model = x2.shape
    tm = PROJ_ROWS
    per_batch = mixed_a_t.shape[2] // tm
    row_spec = lambda wd: pl.BlockSpec((tm, wd), lambda i: (i, 0))
    return pl.pallas_call(
        _out_proj_kernel,
        grid=(rows // tm,),
        in_specs=[row_spec(d_model),
                  pl.BlockSpec((1, W_NSA, tm), lambda i: (i // per_batch, 0, i % per_batch)),
                  row_spec(W_DIL),
                  pl.BlockSpec(w.shape, lambda i: (0, 0)),
                  pl.BlockSpec((1, d_model), lambda i: (0, 0))],
        out_specs=row_spec(d_model),
        out_shape=jax.ShapeDtypeStruct((rows, d_model), jnp.float32),
        compiler_params=pltpu.CompilerParams(dimension_semantics=("arbitrary",)),
        name="out_proj",
    )(x2, mixed_a_t, mixed_b, w, g)


def _rope_tables(positions):
    inv = 1.0 / (ROPE_THETA ** (jnp.arange(0, ROPE_DIMS, 2, dtype=jnp.float32) / ROPE_DIMS))
    ang = positions.astype(jnp.float32).reshape(-1)[:, None] * inv
    cos, sin = jnp.cos(ang), jnp.sin(ang)
    rest = HEAD_DIM - ROPE_DIMS
    one = jnp.ones((cos.shape[0], rest), jnp.float32)
    zero = jnp.zeros((cos.shape[0], rest), jnp.float32)
    zh = jnp.zeros_like(sin)
    per_head = lambda lo, hi, fill: jnp.tile(jnp.concatenate([lo, hi, fill], axis=1), (1, 2))
    return per_head(cos, cos, one), per_head(-sin, zh, zero), per_head(zh, sin, zero)


def _in_proj_weights(w_in):
    scale = HEAD_DIM ** -0.5 * LOG2_E
    o = 0
    cols = {}
    for name, width in (("qa", W_NSA), ("kva", 6 * W_KV), ("gate", 3 * NSA_HEADS), ("za", W_NSA),
                        ("qb", W_DIL), ("kb", W_DIL), ("vb", W_DIL), ("zb", W_DIL)):
        cols[name] = w_in[:, o:o + width]
        o += width
    gate = jnp.pad(cols["gate"], ((0, 0), (0, LANES - 3 * NSA_HEADS)))
    w = jnp.concatenate([cols["qa"] * scale, cols["kva"], gate, cols["za"],
                         cols["qb"] * scale, cols["kb"], cols["vb"], cols["zb"]], axis=1)
    return w.astype(jnp.bfloat16)


def kernel(x, positions, pre_norm_g, w_in, cmp_k_pos, cmp_k_w1, cmp_k_w2,
           cmp_v_pos, cmp_v_w1, cmp_v_w2, w_out, post_norm_g):
    B, S, d_model = x.shape
    G, D = NSA_KV_HEADS, HEAD_DIM
    depth = w_in.shape[0]
    n_sel = S // SEL_BLOCK
    n_chunks = S // CMP_STRIDE
    assert S % KEY_TILE == 0 and n_sel <= MAX_SEL_BLOCKS and S >= WIN_BLOCKS * BLOCK
    assert S % DIL_SUPER == 0 and all(win // dil == BLOCK for win, dil in DIL_PATTERNS)
    bf16 = jnp.bfloat16

    rope_c, rope_a, rope_b = _rope_tables(positions)
    onehot = (jnp.arange(S)[:, None] // SEL_BLOCK == jnp.arange(LANES)[None, :]).astype(bf16)
    x2 = x.reshape(B * S, d_model)

    for layer in range(depth):
        qa, kvc, ks, kw, vst, vwt, gate, sza, qb_, kb_, vb_, szb = _in_proj(
            x2, pre_norm_g[layer][None, :], _in_proj_weights(w_in[layer]), rope_c, rope_a, rope_b, B)

        chunks = kvc.reshape(B, S, 2, G, D).transpose(2, 0, 3, 1, 4).reshape(2, B * G, n_chunks, CMP_STRIDE * D)
        pos = jnp.stack([cmp_k_pos[layer], cmp_v_pos[layer]]).reshape(2, 1, CMP_LEN * D)
        pos8 = jnp.broadcast_to(pos, (2, 16, CMP_LEN * D)).astype(bf16)
        w1 = jnp.stack([cmp_k_w1[layer], cmp_v_w1[layer]]).astype(bf16)
        w2 = jnp.stack([cmp_k_w2[layer], cmp_v_w2[layer]]).astype(bf16)
        half = CMP_STRIDE * D
        kvc = _compress(chunks, pos8, w1[:, :half], w1[:, half:], w2)
        kvc = kvc.reshape(2, B * G, n_sel, CMP_PER_SEL, D)
        kvc = jnp.pad(kvc, ((0, 0), (0, 0), (0, MAX_SEL_BLOCKS - n_sel), (0, 0), (0, 0)))
        kvc = kvc.transpose(0, 1, 3, 2, 4).reshape(2, B * G, CMP_PER_SEL * MAX_SEL_BLOCKS, D).astype(bf16)

        mixed_a = _nsa_attn(qa, kvc[0], kvc[1].transpose(0, 2, 1), onehot, ks, vst, kw, vwt, gate, sza, B)
        mixed_b = _dil_mix(qb_, kb_, vb_, szb, B)
        x2 = _out_proj(x2, mixed_a, mixed_b, w_out[layer].astype(bf16), post_norm_g[layer][None, :])
    return x2.reshape(B, S, d_model)
```

```python
import jax
import jax.numpy as jnp
from jax import lax
from jax.experimental import pallas as pl
from jax.experimental.pallas import tpu as pltpu

HEAD_DIM = 64
NSA_HEADS = 8
NSA_KV_HEADS = 2
NSA_Q_PER_KV = NSA_HEADS // NSA_KV_HEADS
DIL_HEADS = 8
W_NSA = NSA_HEADS * HEAD_DIM
W_KV = NSA_KV_HEADS * HEAD_DIM
W_DIL = DIL_HEADS * HEAD_DIM
CMP_LEN = 32
CMP_STRIDE = 16
CMP_HIDDEN = 256
SEL_BLOCK = 64
SEL_TOP_N = 16
N_FORCED = 3
SWA_WINDOW = 512
DIL_PATTERNS = ((128, 1), (512, 4), (2048, 16))
BLOCK = 128
ROPE_THETA = 500000.0
ROPE_DIMS = HEAD_DIM // 4
RMS_EPS = 1e-6
NEG_INF = -1e30
FORCE_SCORE = 1e4
LOG2_E = 1.4426950408889634

LANES = 128
VMEM_LIMIT = 56 * 1024 * 1024
MAX_SEL_BLOCKS = LANES
CMP_PER_SEL = SEL_BLOCK // CMP_STRIDE
KEY_TILE = 512
PROJ_ROWS = 512
ONES_ROWS = 16

_NT = (((1,), (1,)), ((), ()))
_TN = (((0,), (0,)), ((), ()))


def _dot(a, b):
    return jnp.dot(a, b, preferred_element_type=jnp.float32)


def _dot_nt(a, b):
    return lax.dot_general(a, b, _NT, preferred_element_type=jnp.float32)


def _sigmoid(x):
    return 1.0 / (1.0 + jnp.exp(-x))


def _rope(x, c, a, b):
    width = x.shape[1]
    reps = width // LANES
    ct = jnp.tile(c, (1, reps))
    at = jnp.tile(a, (1, reps))
    bt = jnp.tile(b, (1, reps))
    half = ROPE_DIMS // 2
    return x * ct + pltpu.roll(x, width - half, 1) * at + pltpu.roll(x, half, 1) * bt


def _in_proj_kernel(x_ref, g_ref, w_ref, c_ref, a_ref, b_ref,
                    qa_ref, kvc_ref, ks_ref, kw_ref, vst_ref, vwt_ref, gate_ref, sza_ref,
                    qb_ref, kb_ref, vb_ref, szb_ref):
    x = x_ref[...]
    ms = jnp.mean(x * x, axis=-1, keepdims=True)
    h = (x * lax.rsqrt(ms + RMS_EPS) * g_ref[...]).astype(jnp.bfloat16)
    c = c_ref[...]
    a = a_ref[...]
    b = b_ref[...]
    off = 0

    def proj(width):
        nonlocal off
        r = _dot(h, w_ref[:, off:off + width])
        off += width
        return r

    qa_ref[0] = _rope(proj(W_NSA), c, a, b).T.astype(qa_ref.dtype)
    kva = proj(6 * W_KV)
    part = lambda i: kva[:, i * W_KV:(i + 1) * W_KV]
    kvc_ref[...] = jnp.concatenate([_rope(part(0), c, a, b), part(1)], axis=1).astype(kvc_ref.dtype)
    ks_ref[...] = _rope(part(2), c, a, b).astype(ks_ref.dtype)
    kw_ref[...] = _rope(part(4), c, a, b).astype(kw_ref.dtype)
    vst_ref[0] = part(3).T.astype(vst_ref.dtype)
    vwt_ref[0] = part(5).T.astype(vwt_ref.dtype)
    gate_ref[0] = _sigmoid(proj(LANES)).T
    za = proj(W_NSA)
    sza_ref[0] = (za * _sigmoid(za)).T.astype(sza_ref.dtype)
    qb_ref[...] = _rope(proj(W_DIL), c, a, b).astype(qb_ref.dtype)
    kb_ref[...] = _rope(proj(W_DIL), c, a, b).astype(kb_ref.dtype)
    vb_ref[...] = proj(W_DIL).astype(vb_ref.dtype)
    zb = proj(W_DIL)
    szb_ref[...] = (zb * _sigmoid(zb)).astype(szb_ref.dtype)


def _in_proj(x2, g, w, c, a, b, batch):
    rows, d_model = x2.shape
    n_total = w.shape[1]
    tm = PROJ_ROWS
    seq = rows // batch
    per_batch = seq // tm
    bf16 = jnp.bfloat16
    row_spec = lambda wd: pl.BlockSpec((tm, wd), lambda i: (i, 0))
    row_out = lambda wd: (row_spec(wd), jax.ShapeDtypeStruct((rows, wd), bf16))
    t_out = lambda wd, dt=bf16: (pl.BlockSpec((1, wd, tm), lambda i: (i // per_batch, 0, i % per_batch)),
                                 jax.ShapeDtypeStruct((batch, wd, seq), dt))
    outs = [t_out(W_NSA), row_out(2 * W_KV), row_out(W_KV), row_out(W_KV), t_out(W_KV), t_out(W_KV),
            t_out(LANES, jnp.float32), t_out(W_NSA),
            row_out(W_DIL), row_out(W_DIL), row_out(W_DIL), row_out(W_DIL)]
    return pl.pallas_call(
        _in_proj_kernel,
        grid=(rows // tm,),
        in_specs=[row_spec(d_model),
                  pl.BlockSpec((1, d_model), lambda i: (0, 0)),
                  pl.BlockSpec((d_model, n_total), lambda i: (0, 0)),
                  row_spec(LANES), row_spec(LANES), row_spec(LANES)],
        out_specs=[o[0] for o in outs],
        out_shape=[o[1] for o in outs],
        compiler_params=pltpu.CompilerParams(dimension_semantics=("arbitrary",),
                                             vmem_limit_bytes=VMEM_LIMIT),
        name="in_proj",
    )(x2, g, w, c, a, b)


CMP_PAIRS = CMP_STRIDE // 2


def _compress_kernel(x_ref, pos_ref, wt_ref, wb_ref, w2_ref, o_ref, stage):
    f32, bf16 = jnp.float32, jnp.bfloat16
    stage[...] = x_ref[...].astype(f32)
    nj = MAX_SEL_BLOCKS
    hidden = 2 * CMP_HIDDEN

    def offset_rows(l):
        return jnp.concatenate([stage[pl.ds(CMP_STRIDE * m + l, nj, stride=SEL_BLOCK), :]
                                for m in range(CMP_PER_SEL)], axis=0)

    top = jnp.zeros((CMP_PER_SEL * nj, hidden), f32)
    bot = jnp.zeros((CMP_PER_SEL * nj, hidden), f32)
    bias = jnp.zeros((16, hidden), f32)
    for i in range(CMP_PAIRS):
        x = jnp.concatenate([offset_rows(2 * i), offset_rows(2 * i + 1)], axis=1).astype(bf16)
        top = top + _dot(x, wt_ref[0, i])
        bot = bot + _dot(x, wb_ref[0, i])
        p_top = jnp.concatenate([pos_ref[0, 2 * i], pos_ref[0, 2 * i + 1]], axis=1)
        p_bot = jnp.concatenate([pos_ref[0, CMP_STRIDE + 2 * i], pos_ref[0, CMP_STRIDE + 2 * i + 1]], axis=1)
        bias = bias + _dot(p_top, wt_ref[0, i]) + _dot(p_bot, wb_ref[0, i])
    nxt = jnp.concatenate([bot[nj:], pltpu.roll(bot[0:nj], nj - 1, 0)], axis=0)
    hid = top + nxt + bias[0:1, :]
    act = (hid * _sigmoid(hid)).astype(bf16)
    for g in range(NSA_KV_HEADS):
        o_ref[0, g] = _dot(act[:, g * CMP_HIDDEN:(g + 1) * CMP_HIDDEN], w2_ref[0]).astype(o_ref.dtype)


def _compress(kvc, pos, wt, wb, w2, batch):
    seq = kvc.shape[0] // batch
    G = NSA_KV_HEADS
    ncmp = CMP_PER_SEL * MAX_SEL_BLOCKS
    whole = lambda arr: pl.BlockSpec((1,) + arr.shape[1:], lambda s, b: (s,) + (0,) * (arr.ndim - 1))
    return pl.pallas_call(
        _compress_kernel,
        grid=(2, batch),
        in_specs=[pl.BlockSpec((seq, W_KV), lambda s, b: (b, s)), whole(pos), whole(wt), whole(wb), whole(w2)],
        out_specs=pl.BlockSpec((1, G, ncmp, HEAD_DIM), lambda s, b: (s, b, 0, 0)),
        out_shape=jax.ShapeDtypeStruct((2, batch * G, ncmp, HEAD_DIM), jnp.bfloat16),
        scratch_shapes=[pltpu.VMEM((seq, W_KV), jnp.float32)],
        compiler_params=pltpu.CompilerParams(dimension_semantics=("arbitrary", "arbitrary"),
                                             vmem_limit_bytes=VMEM_LIMIT),
        name="compress",
    )(kvc, pos, wt, wb, w2)


def _compress_weights(pos, w1, w2):
    bf16 = jnp.bfloat16
    pos2 = jnp.broadcast_to(jnp.tile(pos, (1, 1, 2))[:, :, None, :], (2, CMP_LEN, 16, W_KV)).astype(bf16)
    w = w1.reshape(2, 2, CMP_STRIDE, HEAD_DIM, CMP_HIDDEN)
    z = jnp.zeros_like(w)
    bd = jnp.concatenate([jnp.concatenate([w, z], axis=-1), jnp.concatenate([z, w], axis=-1)], axis=-2)
    bd = bd.reshape(2, 2, CMP_PAIRS, 2 * W_KV, 2 * CMP_HIDDEN).astype(bf16)
    return pos2, bd[:, 0], bd[:, 1], w2.astype(bf16)


WIN_BLOCKS = -(-(SWA_WINDOW - 1) // BLOCK) + 1


def _nsa_attn_kernel(q_ref, kc_ref, vc_ref, et_ref, ks_ref, vst_ref, kw_ref, vwt_ref,
                     gate_ref, sza_ref, o_ref, s_a, s_b, p_a, p_b):
    f32, bf16 = jnp.float32, jnp.bfloat16
    qb = pl.program_id(1)
    G, R = NSA_KV_HEADS, NSA_Q_PER_KV
    grows = R * BLOCK
    rows = G * grows
    gcols = lambda g: slice(g * grows, (g + 1) * grows)
    heads_t = [q_ref[0, h * HEAD_DIM:(h + 1) * HEAD_DIM, :] for h in range(NSA_HEADS)]
    q_cmp_t = jnp.concatenate(heads_t, axis=1)
    zero_t = jnp.zeros((HEAD_DIM, BLOCK), bf16)
    q_kv_t = jnp.concatenate([jnp.concatenate([qh, zero_t] if h < R else [zero_t, qh], axis=0)
                              for h, qh in enumerate(heads_t)], axis=1)
    t_1 = qb * BLOCK + lax.broadcasted_iota(jnp.int32, (1, BLOCK), 1)
    head_bias = lambda ok, reps: jnp.tile(jnp.where(ok, 0.0, NEG_INF), (1, reps))

    ncmp = kc_ref.shape[1]
    pos = lax.broadcasted_iota(jnp.int32, (ncmp, BLOCK), 0)
    cmp_end = (pos & (LANES - 1)) * SEL_BLOCK + (pos >> 7) * CMP_STRIDE + (CMP_LEN - 1)
    cmp_bias = head_bias(cmp_end <= t_1, R)
    seen = jnp.tile(jnp.where(t_1 >= CMP_LEN - 1, 1.0, 0.0), (1, R))
    blk = lax.broadcasted_iota(jnp.int32, (LANES, BLOCK), 0)
    o_cmp, imp_sel = [], []
    for g in range(G):
        st = _dot(kc_ref[g], q_cmp_t[:, gcols(g)]) + cmp_bias
        mx = jnp.max(st, axis=0, keepdims=True)
        e = jnp.exp2(st - mx)
        den = jnp.maximum(jnp.sum(e, axis=0, keepdims=True), 1e-30)
        p = e * (seen / den)
        o_cmp.append(lax.dot_general(vc_ref[g], p.astype(bf16), _TN,
                                     preferred_element_type=f32))
        imp = p[:, 0:BLOCK]
        for r in range(1, R):
            imp = imp + p[:, r * BLOCK:(r + 1) * BLOCK]
        q4 = [imp[i * LANES:(i + 1) * LANES] for i in range(CMP_PER_SEL)]
        prev_last = jnp.where(blk == 0, 0.0, pltpu.roll(q4[3], 1, 0))
        imp_sel.append(prev_last + 2.0 * (q4[0] + q4[1] + q4[2]) + q4[3])

    span = WIN_BLOCKS * BLOCK
    first_blk = jnp.maximum(qb - (WIN_BLOCKS - 1), 0)
    win = pl.ds(pl.multiple_of(first_blk * BLOCK, BLOCK), span)
    dist = t_1 - (first_blk * BLOCK + lax.broadcasted_iota(jnp.int32, (span, BLOCK), 0))
    st = _dot(kw_ref[win, :], q_kv_t) + head_bias((dist >= 0) & (dist <= SWA_WINDOW - 1), G * R)
    mw = jnp.max(st, axis=0, keepdims=True)
    e = jnp.exp2(st - mw).astype(bf16)
    ones_win = jnp.ones((ONES_ROWS, span), bf16)
    o_win = []
    for g in range(G):
        r = _dot(jnp.concatenate([vwt_ref[0, g * HEAD_DIM:(g + 1) * HEAD_DIM, win], ones_win], axis=0),
                 e[:, gcols(g)])
        o_win.append(r[0:HEAD_DIM] * (1.0 / r[HEAD_DIM:HEAD_DIM + 1]))

    cur = (qb * BLOCK + lax.broadcasted_iota(jnp.int32, (LANES, BLOCK), 1)) >> 6
    forced = (blk == 0) | (blk == cur) | (blk == cur - 1)
    blk_f = blk.astype(f32)
    bias = []
    for g in range(G):
        sc = jnp.where(forced, -2.0, jnp.where(blk <= cur, imp_sel[g], -1.0))
        for _ in range(SEL_TOP_N - N_FORCED):
            best = jnp.max(sc, axis=0, keepdims=True)
            first = jnp.min(jnp.where(sc == best, blk_f, float(LANES)), axis=0, keepdims=True)
            sc = jnp.where(blk_f == first, -2.0, sc)
        taken = (sc == -2.0) & (blk <= cur)
        bias.append(jnp.tile(jnp.where(taken, 0.0, NEG_INF).astype(bf16), (1, R)))

    qa_t = jnp.concatenate([jnp.concatenate(bias, axis=1), q_kv_t], axis=0)
    n = (qb * BLOCK) // KEY_TILE
    last = jnp.maximum(n - 1, 0)
    ones = jnp.ones((ONES_ROWS, KEY_TILE), bf16)

    def keys_aug(kt):
        tile = pl.ds(pl.multiple_of(kt * KEY_TILE, KEY_TILE), KEY_TILE)
        return jnp.concatenate([et_ref[tile, :], ks_ref[tile, :]], axis=1)

    def qk(kt, s_ref):
        st = _dot(keys_aug(kt), qa_t)
        s_ref[...] = st
        return jnp.max(st, axis=0, keepdims=True)

    def pv(kt, p_ref):
        tile = pl.ds(pl.multiple_of(kt * KEY_TILE, KEY_TILE), KEY_TILE)
        return tuple(_dot(jnp.concatenate([vst_ref[0, g * HEAD_DIM:(g + 1) * HEAD_DIM, tile], ones], axis=0),
                          p_ref[:, gcols(g)]) for g in range(G))

    def softmax(m_old, mx, s_ref, p_ref):
        m_new = jnp.maximum(m_old, mx)
        p_ref[...] = jnp.exp2(s_ref[...] - m_new).astype(bf16)
        return m_new, jnp.exp2(m_old - m_new)

    def accumulate(acc, alpha, weight, contrib):
        return tuple(alpha[:, gcols(g)] * acc[g] + weight * contrib[g] for g in range(G))

    key = n * KEY_TILE + lax.broadcasted_iota(jnp.int32, (KEY_TILE, BLOCK), 0)
    st = _dot(keys_aug(n), qa_t) + head_bias(key <= t_1, G * R)
    m0 = jnp.max(st, axis=0, keepdims=True)
    p_b[...] = jnp.exp2(st - m0).astype(bf16)
    mx0 = qk(0, s_a)
    acc0 = tuple(jnp.zeros((HEAD_DIM + ONES_ROWS, grows), f32) for _ in range(G))
    one = jnp.ones_like(m0)

    def body(i, carry):
        m, acc, alpha_prev, w_prev, kt_prev, mx = carry
        first, second = 2 * i, 2 * i + 1
        w_second = jnp.where(second < n, 1.0, 0.0)
        kt_second = jnp.minimum(second, last)
        acc = accumulate(acc, alpha_prev, w_prev, pv(kt_prev, p_b))
        m, alpha = softmax(m, mx, s_a, p_a)
        mx = qk(kt_second, s_b)
        acc = accumulate(acc, alpha, 1.0, pv(first, p_a))
        m, alpha = softmax(m, mx, s_b, p_b)
        mx = qk(jnp.minimum(second + 1, last), s_a)
        return m, acc, alpha, w_second, kt_second, mx

    init = (m0, acc0, one, jnp.float32(1.0), n, mx0)
    _, acc, alpha_prev, w_prev, kt_prev, _ = lax.fori_loop(0, (n + 1) // 2, body, init)
    acc = accumulate(acc, alpha_prev, w_prev, pv(kt_prev, p_b))
    o_slc = [a[0:HEAD_DIM] * (1.0 / a[HEAD_DIM:HEAD_DIM + 1]) for a in acc]

    for g in range(G):
        for r in range(R):
            h = g * R + r
            cols = slice(r * BLOCK, (r + 1) * BLOCK)
            gate_row = lambda branch: gate_ref[0, branch * NSA_HEADS + h:branch * NSA_HEADS + h + 1, :]
            o_t = (gate_row(0) * o_cmp[g][:, cols] + gate_row(1) * o_slc[g][:, cols]
                   + gate_row(2) * o_win[g][:, cols])
            dims = slice(h * HEAD_DIM, (h + 1) * HEAD_DIM)
            o_ref[0, dims, :] = (o_t * sza_ref[0, dims, :].astype(f32)).astype(o_ref.dtype)


def _nsa_attn(qa, kcp, vcp, onehot, ks, vst, kw, vwt, gate, sza, batch):
    seq = qa.shape[2]
    nqb = seq // BLOCK
    ncmp = kcp.shape[1]
    G = NSA_KV_HEADS
    rows = NSA_HEADS * BLOCK
    t_block = lambda wd: pl.BlockSpec((1, wd, BLOCK), lambda b, j: (b, 0, j))
    return pl.pallas_call(
        _nsa_attn_kernel,
        grid=(batch, nqb),
        in_specs=[t_block(W_NSA),
                  pl.BlockSpec((G, ncmp, HEAD_DIM), lambda b, j: (b, 0, 0)),
                  pl.BlockSpec((G, ncmp, HEAD_DIM), lambda b, j: (b, 0, 0)),
                  pl.BlockSpec((seq, LANES), lambda b, j: (0, 0)),
                  pl.BlockSpec((seq, W_KV), lambda b, j: (b, 0)),
                  pl.BlockSpec((1, W_KV, seq), lambda b, j: (b, 0, 0)),
                  pl.BlockSpec((seq, W_KV), lambda b, j: (b, 0)),
                  pl.BlockSpec((1, W_KV, seq), lambda b, j: (b, 0, 0)),
                  t_block(LANES),
                  t_block(W_NSA)],
        out_specs=t_block(W_NSA),
        out_shape=jax.ShapeDtypeStruct((batch, W_NSA, seq), jnp.bfloat16),
        scratch_shapes=[pltpu.VMEM((KEY_TILE, rows), jnp.float32),
                        pltpu.VMEM((KEY_TILE, rows), jnp.float32),
                        pltpu.VMEM((KEY_TILE, rows), jnp.bfloat16),
                        pltpu.VMEM((KEY_TILE, rows), jnp.bfloat16)],
        compiler_params=pltpu.CompilerParams(dimension_semantics=("arbitrary", "arbitrary"),
                                             vmem_limit_bytes=VMEM_LIMIT),
        name="nsa_attn",
    )(qa, kcp, vcp, onehot, ks, vst, kw, vwt, gate, sza)


DIL_MAX = max(d for _, d in DIL_PATTERNS)
DIL_SUPER = BLOCK * DIL_MAX
DIL_UNITS = DIL_SUPER // BLOCK
DIL_UNROLL = 16
HEAD_PAIR = 2 * HEAD_DIM
MIX_ROWS = 256


def _dil_mix_kernel(q_ref, kp_ref, kc_ref, vp_ref, vc_ref, z_ref, o_ref,
                    qf, kf, vf, num_scr, den_scr, max_scr, bias_scr):
    f32, bf16 = jnp.float32, jnp.bfloat16
    sb = pl.program_id(1)
    qf[...] = q_ref[...].astype(f32)
    kf[0:DIL_SUPER] = kp_ref[...].astype(f32)
    kf[DIL_SUPER:2 * DIL_SUPER] = kc_ref[...].astype(f32)
    vf[0:DIL_SUPER] = vp_ref[...].astype(f32)
    vf[DIL_SUPER:2 * DIL_SUPER] = vc_ref[...].astype(f32)

    row = lax.broadcasted_iota(jnp.int32, (2 * BLOCK, 2 * BLOCK), 0)
    col = lax.broadcasted_iota(jnp.int32, (2 * BLOCK, 2 * BLOCK), 1)
    dist = BLOCK + (row & (BLOCK - 1)) - col
    band = (dist >= 0) & (dist <= BLOCK)
    bias_scr[0] = jnp.where(band, 0.0, NEG_INF)
    bias_scr[1] = jnp.where(band & (col >= BLOCK), 0.0, NEG_INF)
    first_head = lax.broadcasted_iota(jnp.int32, (BLOCK, HEAD_PAIR), 1) < HEAD_DIM
    ones = jnp.ones((2 * BLOCK, HEAD_PAIR), bf16)

    for pat, (window, dil) in enumerate(DIL_PATTERNS):
        shift = dil.bit_length() - 1

        def unit(u, pat=pat, dil=dil, shift=shift):
            cls = u & (dil - 1)
            blk = u >> shift
            q_start = cls + blk * (BLOCK * dil)
            k_start = DIL_SUPER + q_start - BLOCK * dil
            q2 = qf[pl.ds(q_start, BLOCK, stride=dil), :]
            k2 = kf[pl.ds(k_start, 2 * BLOCK, stride=dil), :]
            v2 = vf[pl.ds(k_start, 2 * BLOCK, stride=dil), :]
            qm = jnp.concatenate([jnp.where(first_head, q2, 0.0),
                                  jnp.where(first_head, 0.0, q2)], axis=0).astype(bf16)
            s = _dot_nt(qm, k2.astype(bf16))
            no_prev = jnp.where((sb == 0) & (blk == 0), 1, 0)
            s = s + bias_scr[no_prev]
            m = jnp.max(s, axis=1, keepdims=True)
            e = jnp.exp2(s - m).astype(bf16)
            r = _dot(e, jnp.concatenate([v2.astype(bf16), ones], axis=1))
            mb = jnp.broadcast_to(m, (2 * BLOCK, HEAD_PAIR))
            out_rows = pl.ds(q_start, BLOCK, stride=dil)
            num_scr[pat, out_rows, :] = jnp.where(first_head, r[0:BLOCK, 0:HEAD_PAIR], r[BLOCK:, 0:HEAD_PAIR])
            den_scr[pat, out_rows, :] = jnp.where(first_head, r[0:BLOCK, HEAD_PAIR:], r[BLOCK:, HEAD_PAIR:])
            max_scr[pat, out_rows, :] = jnp.where(first_head, mb[0:BLOCK], mb[BLOCK:])

        def trip(it, carry, unit=unit):
            for j in range(DIL_UNROLL):
                unit(it * DIL_UNROLL + j)
            return carry

        lax.fori_loop(0, DIL_UNITS // DIL_UNROLL, trip, 0)

    def mix(ci, carry):
        rows = pl.ds(pl.multiple_of(ci * MIX_ROWS, MIX_ROWS), MIX_ROWS)
        ms = [max_scr[p, rows, :] for p in range(len(DIL_PATTERNS))]
        mx = jnp.maximum(jnp.maximum(ms[0], ms[1]), ms[2])
        cs = [jnp.exp2(m - mx) for m in ms]
        num = cs[0] * num_scr[0, rows, :] + cs[1] * num_scr[1, rows, :] + cs[2] * num_scr[2, rows, :]
        den = cs[0] * den_scr[0, rows, :] + cs[1] * den_scr[1, rows, :] + cs[2] * den_scr[2, rows, :]
        o_ref[rows, :] = (num / den * z_ref[rows, :].astype(f32)).astype(o_ref.dtype)
        return carry

    lax.fori_loop(0, DIL_SUPER // MIX_ROWS, mix, 0)


def _dil_mix(qb, kb, vb, szb, batch):
    rows = qb.shape[0]
    nsb = rows // batch // DIL_SUPER
    cur = pl.BlockSpec((DIL_SUPER, HEAD_PAIR), lambda b, s, h: (b * nsb + s, h))
    prev = pl.BlockSpec((DIL_SUPER, HEAD_PAIR), lambda b, s, h: (b * nsb + jnp.maximum(s - 1, 0), h))
    f32 = jnp.float32
    return pl.pallas_call(
        _dil_mix_kernel,
        grid=(batch, nsb, W_DIL // HEAD_PAIR),
        in_specs=[cur, prev, cur, prev, cur, cur],
        out_specs=cur,
        out_shape=jax.ShapeDtypeStruct((rows, W_DIL), jnp.bfloat16),
        scratch_shapes=[pltpu.VMEM((DIL_SUPER, HEAD_PAIR), f32),
                        pltpu.VMEM((2 * DIL_SUPER, HEAD_PAIR), f32),
                        pltpu.VMEM((2 * DIL_SUPER, HEAD_PAIR), f32),
                        pltpu.VMEM((len(DIL_PATTERNS), DIL_SUPER, HEAD_PAIR), f32),
                        pltpu.VMEM((len(DIL_PATTERNS), DIL_SUPER, HEAD_PAIR), f32),
                        pltpu.VMEM((len(DIL_PATTERNS), DIL_SUPER, HEAD_PAIR), f32),
                        pltpu.VMEM((2, 2 * BLOCK, 2 * BLOCK), f32)],
        compiler_params=pltpu.CompilerParams(
            dimension_semantics=("arbitrary", "arbitrary", "arbitrary"),
            vmem_limit_bytes=VMEM_LIMIT),
        name="dil_mix",
    )(qb, kb, kb, vb, vb, szb)


def _out_proj_kernel(x_ref, ma_ref, mb_ref, w_ref, g_ref, out_ref):
    y = lax.dot_general(ma_ref[0], w_ref[0:W_NSA, :], _TN, preferred_element_type=jnp.float32)
    y = y + _dot(mb_ref[...], w_ref[W_NSA:W_NSA + W_DIL, :])
    ms = jnp.mean(y * y, axis=-1, keepdims=True)
    out_ref[...] = x_ref[...] + y * lax.rsqrt(ms + RMS_EPS) * g_ref[...]


def _out_proj(x2, mixed_a_t, mixed_b, w, g):
    rows, d_model = x2.shape
    tm = PROJ_ROWS
    per_batch = mixed_a_t.shape[2] // tm
    row_spec = lambda wd: pl.BlockSpec((tm, wd), lambda i: (i, 0))
    return pl.pallas_call(
        _out_proj_kernel,
        grid=(rows // tm,),
        in_specs=[row_spec(d_model),
                  pl.BlockSpec((1, W_NSA, tm), lambda i: (i // per_batch, 0, i % per_batch)),
                  row_spec(W_DIL),
                  pl.BlockSpec(w.shape, lambda i: (0, 0)),
                  pl.BlockSpec((1, d_model), lambda i: (0, 0))],
        out_specs=row_spec(d_model),
        out_shape=jax.ShapeDtypeStruct((rows, d_model), jnp.float32),
        compiler_params=pltpu.CompilerParams(dimension_semantics=("arbitrary",)),
        name="out_proj",
    )(x2, mixed_a_t, mixed_b, w, g)


def _rope_tables(positions):
    inv = 1.0 / (ROPE_THETA ** (jnp.arange(0, ROPE_DIMS, 2, dtype=jnp.float32) / ROPE_DIMS))
    ang = positions.astype(jnp.float32).reshape(-1)[:, None] * inv
    cos, sin = jnp.cos(ang), jnp.sin(ang)
    rest = HEAD_DIM - ROPE_DIMS
    one = jnp.ones((cos.shape[0], rest), jnp.float32)
    zero = jnp.zeros((cos.shape[0], rest), jnp.float32)
    zh = jnp.zeros_like(sin)
    per_head = lambda lo, hi, fill: jnp.tile(jnp.concatenate([lo, hi, fill], axis=1), (1, 2))
    return per_head(cos, cos, one), per_head(-sin, zh, zero), per_head(zh, sin, zero)


def _in_proj_weights(w_in):
    scale = HEAD_DIM ** -0.5 * LOG2_E
    o = 0
    cols = {}
    for name, width in (("qa", W_NSA), ("kva", 6 * W_KV), ("gate", 3 * NSA_HEADS), ("za", W_NSA),
                        ("qb", W_DIL), ("kb", W_DIL), ("vb", W_DIL), ("zb", W_DIL)):
        cols[name] = w_in[:, o:o + width]
        o += width
    gate = jnp.pad(cols["gate"], ((0, 0), (0, LANES - 3 * NSA_HEADS)))
    w = jnp.concatenate([cols["qa"] * scale, cols["kva"], gate, cols["za"],
                         cols["qb"] * scale, cols["kb"], cols["vb"], cols["zb"]], axis=1)
    return w.astype(jnp.bfloat16)


def kernel(x, positions, pre_norm_g, w_in, cmp_k_pos, cmp_k_w1, cmp_k_w2,
           cmp_v_pos, cmp_v_w1, cmp_v_w2, w_out, post_norm_g):
    B, S, d_model = x.shape
    depth = w_in.shape[0]
    n_sel = S // SEL_BLOCK
    assert S % KEY_TILE == 0 and n_sel == MAX_SEL_BLOCKS and S >= WIN_BLOCKS * BLOCK
    assert S % DIL_SUPER == 0 and all(win // dil == BLOCK for win, dil in DIL_PATTERNS)
    bf16 = jnp.bfloat16

    rope_c, rope_a, rope_b = _rope_tables(positions)
    onehot = (jnp.arange(S)[:, None] // SEL_BLOCK == jnp.arange(LANES)[None, :]).astype(bf16)
    x2 = x.reshape(B * S, d_model)

    for layer in range(depth):
        qa, kvc, ks, kw, vst, vwt, gate, sza, qb_, kb_, vb_, szb = _in_proj(
            x2, pre_norm_g[layer][None, :], _in_proj_weights(w_in[layer]), rope_c, rope_a, rope_b, B)

        kvc = _compress(kvc, *_compress_weights(jnp.stack([cmp_k_pos[layer], cmp_v_pos[layer]]),
                                                jnp.stack([cmp_k_w1[layer], cmp_v_w1[layer]]),
                                                jnp.stack([cmp_k_w2[layer], cmp_v_w2[layer]])), B)
        mixed_a = _nsa_attn(qa, kvc[0], kvc[1], onehot, ks, vst, kw, vwt, gate, sza, B)
        mixed_b = _dil_mix(qb_, kb_, vb_, szb, B)
        x2 = _out_proj(x2, mixed_a, mixed_b, w_out[layer].astype(bf16), post_norm_g[layer][None, :])
    return x2.reshape(B, S, d_model)
```

```python
import jax
import jax.numpy as jnp
from jax import lax
from jax.experimental import pallas as pl
from jax.experimental.pallas import tpu as pltpu

HEAD_DIM = 64
NSA_HEADS = 8
NSA_KV_HEADS = 2
NSA_Q_PER_KV = NSA_HEADS // NSA_KV_HEADS
DIL_HEADS = 8
W_NSA = NSA_HEADS * HEAD_DIM
W_KV = NSA_KV_HEADS * HEAD_DIM
W_DIL = DIL_HEADS * HEAD_DIM
CMP_LEN = 32
CMP_STRIDE = 16
CMP_HIDDEN = 256
SEL_BLOCK = 64
SEL_TOP_N = 16
N_FORCED = 3
SWA_WINDOW = 512
DIL_PATTERNS = ((128, 1), (512, 4), (2048, 16))
BLOCK = 128
ROPE_THETA = 500000.0
ROPE_DIMS = HEAD_DIM // 4
RMS_EPS = 1e-6
NEG_INF = -1e30
FORCE_SCORE = 1e4
LOG2_E = 1.4426950408889634

LANES = 128
VMEM_LIMIT = 56 * 1024 * 1024
MAX_SEL_BLOCKS = LANES
CMP_PER_SEL = SEL_BLOCK // CMP_STRIDE
KEY_TILE = 512
PROJ_ROWS = 512
ONES_ROWS = 16

_NT = (((1,), (1,)), ((), ()))
_TN = (((0,), (0,)), ((), ()))


def _dot(a, b):
    return jnp.dot(a, b, preferred_element_type=jnp.float32)


def _dot_nt(a, b):
    return lax.dot_general(a, b, _NT, preferred_element_type=jnp.float32)


def _sigmoid(x):
    return 1.0 / (1.0 + jnp.exp(-x))


def _rope(x, c, a, b):
    width = x.shape[1]
    reps = width // LANES
    ct = jnp.tile(c, (1, reps))
    at = jnp.tile(a, (1, reps))
    bt = jnp.tile(b, (1, reps))
    half = ROPE_DIMS // 2
    return x * ct + pltpu.roll(x, width - half, 1) * at + pltpu.roll(x, half, 1) * bt


def _in_proj_kernel(x_ref, g_ref, w_head_ref, w_gate_ref, w_tail_ref, c_ref, a_ref, b_ref,
                    qa_ref, kvc_ref, ks_ref, kw_ref, vst_ref, vwt_ref, gate_ref, sza_ref,
                    qb_ref, kb_ref, vb_ref, szb_ref):
    x = x_ref[...]
    ms = jnp.mean(x * x, axis=-1, keepdims=True)
    h = (x * lax.rsqrt(ms + RMS_EPS) * g_ref[...]).astype(jnp.bfloat16)
    c = c_ref[...]
    a = a_ref[...]
    b = b_ref[...]
    def columns_of(w_ref):
        off = 0

        def take(width):
            nonlocal off
            off += width
            return _dot(h, w_ref[:, off - width:off])
        return take

    head, gates, proj = columns_of(w_head_ref), columns_of(w_gate_ref), columns_of(w_tail_ref)
    qa_ref[0] = _rope(head(W_NSA), c, a, b).T.astype(qa_ref.dtype)
    kva = head(6 * W_KV)
    part = lambda i: kva[:, i * W_KV:(i + 1) * W_KV]
    kvc_ref[...] = jnp.concatenate([_rope(part(0), c, a, b), part(1)], axis=1).astype(kvc_ref.dtype)
    ks_ref[...] = _rope(part(2), c, a, b).astype(ks_ref.dtype)
    kw_ref[...] = _rope(part(4), c, a, b).astype(kw_ref.dtype)
    vst_ref[0] = part(3).T.astype(vst_ref.dtype)
    vwt_ref[0] = part(5).T.astype(vwt_ref.dtype)
    gate_ref[0] = _sigmoid(gates(LANES)).T
    za = proj(W_NSA)
    sza_ref[0] = (za * _sigmoid(za)).T.astype(sza_ref.dtype)
    qb_ref[...] = _rope(proj(W_DIL), c, a, b).astype(qb_ref.dtype)
    kb_ref[...] = _rope(proj(W_DIL), c, a, b).astype(kb_ref.dtype)
    vb_ref[...] = proj(W_DIL).astype(vb_ref.dtype)
    zb = proj(W_DIL)
    szb_ref[...] = (zb * _sigmoid(zb)).astype(szb_ref.dtype)


def _in_proj(x2, g, weights, c, a, b, batch):
    rows, d_model = x2.shape
    tm = PROJ_ROWS
    seq = rows // batch
    per_batch = seq // tm
    bf16 = jnp.bfloat16
    row_spec = lambda wd: pl.BlockSpec((tm, wd), lambda i: (i, 0))
    row_out = lambda wd: (row_spec(wd), jax.ShapeDtypeStruct((rows, wd), bf16))
    t_out = lambda wd, dt=bf16: (pl.BlockSpec((1, wd, tm), lambda i: (i // per_batch, 0, i % per_batch)),
                                 jax.ShapeDtypeStruct((batch, wd, seq), dt))
    outs = [t_out(W_NSA), row_out(2 * W_KV), row_out(W_KV), row_out(W_KV), t_out(W_KV), t_out(W_KV),
            t_out(LANES, jnp.float32), t_out(W_NSA),
            row_out(W_DIL), row_out(W_DIL), row_out(W_DIL), row_out(W_DIL)]
    return pl.pallas_call(
        _in_proj_kernel,
        grid=(rows // tm,),
        in_specs=[row_spec(d_model),
                  pl.BlockSpec((1, d_model), lambda i: (0, 0))]
                 + [pl.BlockSpec(w.shape, lambda i: (0, 0)) for w in weights]
                 + [row_spec(LANES), row_spec(LANES), row_spec(LANES)],
        out_specs=[o[0] for o in outs],
        out_shape=[o[1] for o in outs],
        compiler_params=pltpu.CompilerParams(dimension_semantics=("arbitrary",),
                                             vmem_limit_bytes=VMEM_LIMIT),
        name="in_proj",
    )(x2, g, *weights, c, a, b)


CMP_PAIRS = CMP_STRIDE // 2


def _compress_kernel(x_ref, pos_ref, wt_ref, wb_ref, w2_ref, o_ref, stage):
    f32, bf16 = jnp.float32, jnp.bfloat16
    stage[...] = x_ref[...].astype(f32)
    nj = MAX_SEL_BLOCKS
    hidden = 2 * CMP_HIDDEN

    def offset_rows(l):
        return jnp.concatenate([stage[pl.ds(CMP_STRIDE * m + l, nj, stride=SEL_BLOCK), :]
                                for m in range(CMP_PER_SEL)], axis=0)

    top = jnp.zeros((CMP_PER_SEL * nj, hidden), f32)
    bot = jnp.zeros((CMP_PER_SEL * nj, hidden), f32)
    bias = jnp.zeros((16, hidden), f32)
    for i in range(CMP_PAIRS):
        x = jnp.concatenate([offset_rows(2 * i), offset_rows(2 * i + 1)], axis=1).astype(bf16)
        top = top + _dot(x, wt_ref[0, i])
        bot = bot + _dot(x, wb_ref[0, i])
        p_top = jnp.concatenate([pos_ref[0, 2 * i], pos_ref[0, 2 * i + 1]], axis=1)
        p_bot = jnp.concatenate([pos_ref[0, CMP_STRIDE + 2 * i], pos_ref[0, CMP_STRIDE + 2 * i + 1]], axis=1)
        bias = bias + _dot(p_top, wt_ref[0, i]) + _dot(p_bot, wb_ref[0, i])
    nxt = jnp.concatenate([bot[nj:], pltpu.roll(bot[0:nj], nj - 1, 0)], axis=0)
    hid = top + nxt + bias[0:1, :]
    act = (hid * _sigmoid(hid)).astype(bf16)
    for g in range(NSA_KV_HEADS):
        o_ref[0, g] = _dot(act[:, g * CMP_HIDDEN:(g + 1) * CMP_HIDDEN], w2_ref[0]).astype(o_ref.dtype)


def _compress(kvc, pos, wt, wb, w2, batch):
    seq = kvc.shape[0] // batch
    G = NSA_KV_HEADS
    ncmp = CMP_PER_SEL * MAX_SEL_BLOCKS
    whole = lambda arr: pl.BlockSpec((1,) + arr.shape[1:], lambda s, b: (s,) + (0,) * (arr.ndim - 1))
    return pl.pallas_call(
        _compress_kernel,
        grid=(2, batch),
        in_specs=[pl.BlockSpec((seq, W_KV), lambda s, b: (b, s)), whole(pos), whole(wt), whole(wb), whole(w2)],
        out_specs=pl.BlockSpec((1, G, ncmp, HEAD_DIM), lambda s, b: (s, b, 0, 0)),
        out_shape=jax.ShapeDtypeStruct((2, batch * G, ncmp, HEAD_DIM), jnp.bfloat16),
        scratch_shapes=[pltpu.VMEM((seq, W_KV), jnp.float32)],
        compiler_params=pltpu.CompilerParams(dimension_semantics=("arbitrary", "arbitrary"),
                                             vmem_limit_bytes=VMEM_LIMIT),
        name="compress",
    )(kvc, pos, wt, wb, w2)


def _compress_weights(pos, w1, w2):
    bf16 = jnp.bfloat16
    pos2 = jnp.broadcast_to(jnp.tile(pos, (1, 1, 2))[:, :, None, :], (2, CMP_LEN, 16, W_KV)).astype(bf16)
    w = w1.astype(bf16).reshape(2, 2, CMP_STRIDE, HEAD_DIM, CMP_HIDDEN)
    lead = ((0, 0),) * 3
    bd = (jnp.pad(w, lead + ((0, HEAD_DIM), (0, CMP_HIDDEN)))
          + jnp.pad(w, lead + ((HEAD_DIM, 0), (CMP_HIDDEN, 0))))
    bd = bd.reshape(2, 2, CMP_PAIRS, 2 * W_KV, 2 * CMP_HIDDEN)
    return pos2, bd[:, 0], bd[:, 1], w2.astype(bf16)


WIN_BLOCKS = -(-(SWA_WINDOW - 1) // BLOCK) + 1


def _nsa_attn_kernel(q_ref, kc_ref, vc_ref, et_ref, ks_ref, vst_ref, kw_ref, vwt_ref,
                     gate_ref, sza_ref, o_ref, s_a, s_b, p_a, p_b):
    f32, bf16 = jnp.float32, jnp.bfloat16
    qb = pl.program_id(1)
    G, R = NSA_KV_HEADS, NSA_Q_PER_KV
    grows = R * BLOCK
    rows = G * grows
    gcols = lambda g: slice(g * grows, (g + 1) * grows)
    heads_t = [q_ref[0, h * HEAD_DIM:(h + 1) * HEAD_DIM, :] for h in range(NSA_HEADS)]
    q_cmp_t = jnp.concatenate(heads_t, axis=1)
    zero_t = jnp.zeros((HEAD_DIM, BLOCK), bf16)
    q_kv_t = jnp.concatenate([jnp.concatenate([qh, zero_t] if h < R else [zero_t, qh], axis=0)
                              for h, qh in enumerate(heads_t)], axis=1)
    t_1 = qb * BLOCK + lax.broadcasted_iota(jnp.int32, (1, BLOCK), 1)
    head_bias = lambda ok, reps: jnp.tile(jnp.where(ok, 0.0, NEG_INF), (1, reps))

    ncmp = kc_ref.shape[1]
    pos = lax.broadcasted_iota(jnp.int32, (ncmp, BLOCK), 0)
    cmp_end = (pos & (LANES - 1)) * SEL_BLOCK + (pos >> 7) * CMP_STRIDE + (CMP_LEN - 1)
    cmp_bias = head_bias(cmp_end <= t_1, R)
    seen = jnp.tile(jnp.where(t_1 >= CMP_LEN - 1, 1.0, 0.0), (1, R))
    blk = lax.broadcasted_iota(jnp.int32, (LANES, BLOCK), 0)
    o_cmp, imp_sel = [], []
    for g in range(G):
        st = _dot(kc_ref[g], q_cmp_t[:, gcols(g)]) + cmp_bias
        mx = jnp.max(st, axis=0, keepdims=True)
        e = jnp.exp2(st - mx)
        den = jnp.maximum(jnp.sum(e, axis=0, keepdims=True), 1e-30)
        p = e * (seen / den)
        o_cmp.append(lax.dot_general(vc_ref[g], p.astype(bf16), _TN,
                                     preferred_element_type=f32))
        imp = p[:, 0:BLOCK]
        for r in range(1, R):
            imp = imp + p[:, r * BLOCK:(r + 1) * BLOCK]
        q4 = [imp[i * LANES:(i + 1) * LANES] for i in range(CMP_PER_SEL)]
        prev_last = jnp.where(blk == 0, 0.0, pltpu.roll(q4[3], 1, 0))
        imp_sel.append(prev_last + 2.0 * (q4[0] + q4[1] + q4[2]) + q4[3])

    span = WIN_BLOCKS * BLOCK
    first_blk = jnp.maximum(qb - (WIN_BLOCKS - 1), 0)
    win = pl.ds(pl.multiple_of(first_blk * BLOCK, BLOCK), span)
    dist = t_1 - (first_blk * BLOCK + lax.broadcasted_iota(jnp.int32, (span, BLOCK), 0))
    st = _dot(kw_ref[win, :], q_kv_t) + head_bias((dist >= 0) & (dist <= SWA_WINDOW - 1), G * R)
    mw = jnp.max(st, axis=0, keepdims=True)
    e = jnp.exp2(st - mw).astype(bf16)
    ones_win = jnp.ones((ONES_ROWS, span), bf16)
    o_win = []
    for g in range(G):
        r = _dot(jnp.concatenate([vwt_ref[0, g * HEAD_DIM:(g + 1) * HEAD_DIM, win], ones_win], axis=0),
                 e[:, gcols(g)])
        o_win.append(r[0:HEAD_DIM] * (1.0 / r[HEAD_DIM:HEAD_DIM + 1]))

    cur = (qb * BLOCK + lax.broadcasted_iota(jnp.int32, (LANES, BLOCK), 1)) >> 6
    forced = (blk == 0) | (blk == cur) | (blk == cur - 1)
    blk_f = blk.astype(f32)
    bias = []
    for g in range(G):
        sc = jnp.where(forced, -2.0, jnp.where(blk <= cur, imp_sel[g], -1.0))
        for _ in range(SEL_TOP_N - N_FORCED):
            best = jnp.max(sc, axis=0, keepdims=True)
            first = jnp.min(jnp.where(sc == best, blk_f, float(LANES)), axis=0, keepdims=True)
            sc = jnp.where(blk_f == first, -2.0, sc)
        taken = (sc == -2.0) & (blk <= cur)
        bias.append(jnp.tile(jnp.where(taken, 0.0, NEG_INF).astype(bf16), (1, R)))

    qa_t = jnp.concatenate([jnp.concatenate(bias, axis=1), q_kv_t], axis=0)
    n = (qb * BLOCK) // KEY_TILE
    last = jnp.maximum(n - 1, 0)
    ones = jnp.ones((ONES_ROWS, KEY_TILE), bf16)

    def keys_aug(kt):
        tile = pl.ds(pl.multiple_of(kt * KEY_TILE, KEY_TILE), KEY_TILE)
        return jnp.concatenate([et_ref[tile, :], ks_ref[tile, :]], axis=1)

    def qk(kt, s_ref):
        st = _dot(keys_aug(kt), qa_t)
        s_ref[...] = st
        return jnp.max(st, axis=0, keepdims=True)

    def pv(kt, p_ref):
        tile = pl.ds(pl.multiple_of(kt * KEY_TILE, KEY_TILE), KEY_TILE)
        return tuple(_dot(jnp.concatenate([vst_ref[0, g * HEAD_DIM:(g + 1) * HEAD_DIM, tile], ones], axis=0),
                          p_ref[:, gcols(g)]) for g in range(G))

    def softmax(m_old, mx, s_ref, p_ref):
        m_new = jnp.maximum(m_old, mx)
        p_ref[...] = jnp.exp2(s_ref[...] - m_new).astype(bf16)
        return m_new, jnp.exp2(m_old - m_new)

    def accumulate(acc, alpha, weight, contrib):
        return tuple(alpha[:, gcols(g)] * acc[g] + weight * contrib[g] for g in range(G))

    key = n * KEY_TILE + lax.broadcasted_iota(jnp.int32, (KEY_TILE, BLOCK), 0)
    st = _dot(keys_aug(n), qa_t) + head_bias(key <= t_1, G * R)
    m0 = jnp.max(st, axis=0, keepdims=True)
    p_b[...] = jnp.exp2(st - m0).astype(bf16)
    mx0 = qk(0, s_a)
    acc0 = tuple(jnp.zeros((HEAD_DIM + ONES_ROWS, grows), f32) for _ in range(G))
    one = jnp.ones_like(m0)

    def body(i, carry):
        m, acc, alpha_prev, w_prev, kt_prev, mx = carry
        first, second = 2 * i, 2 * i + 1
        w_second = jnp.where(second < n, 1.0, 0.0)
        kt_second = jnp.minimum(second, last)
        acc = accumulate(acc, alpha_prev, w_prev, pv(kt_prev, p_b))
        m, alpha = softmax(m, mx, s_a, p_a)
        mx = qk(kt_second, s_b)
        acc = accumulate(acc, alpha, 1.0, pv(first, p_a))
        m, alpha = softmax(m, mx, s_b, p_b)
        mx = qk(jnp.minimum(second + 1, last), s_a)
        return m, acc, alpha, w_second, kt_second, mx

    init = (m0, acc0, one, jnp.float32(1.0), n, mx0)
    _, acc, alpha_prev, w_prev, kt_prev, _ = lax.fori_loop(0, (n + 1) // 2, body, init)
    acc = accumulate(acc, alpha_prev, w_prev, pv(kt_prev, p_b))
    o_slc = [a[0:HEAD_DIM] * (1.0 / a[HEAD_DIM:HEAD_DIM + 1]) for a in acc]

    for g in range(G):
        for r in range(R):
            h = g * R + r
            cols = slice(r * BLOCK, (r + 1) * BLOCK)
            gate_row = lambda branch: gate_ref[0, branch * NSA_HEADS + h:branch * NSA_HEADS + h + 1, :]
            o_t = (gate_row(0) * o_cmp[g][:, cols] + gate_row(1) * o_slc[g][:, cols]
                   + gate_row(2) * o_win[g][:, cols])
            dims = slice(h * HEAD_DIM, (h + 1) * HEAD_DIM)
            o_ref[0, dims, :] = (o_t * sza_ref[0, dims, :].astype(f32)).astype(o_ref.dtype)


def _nsa_attn(qa, kcp, vcp, onehot, ks, vst, kw, vwt, gate, sza, batch):
    seq = qa.shape[2]
    nqb = seq // BLOCK
    ncmp = kcp.shape[1]
    G = NSA_KV_HEADS
    rows = NSA_HEADS * BLOCK
    t_block = lambda wd: pl.BlockSpec((1, wd, BLOCK), lambda b, j: (b, 0, j))
    return pl.pallas_call(
        _nsa_attn_kernel,
        grid=(batch, nqb),
        in_specs=[t_block(W_NSA),
                  pl.BlockSpec((G, ncmp, HEAD_DIM), lambda b, j: (b, 0, 0)),
                  pl.BlockSpec((G, ncmp, HEAD_DIM), lambda b, j: (b, 0, 0)),
                  pl.BlockSpec((seq, LANES), lambda b, j: (0, 0)),
                  pl.BlockSpec((seq, W_KV), lambda b, j: (b, 0)),
                  pl.BlockSpec((1, W_KV, seq), lambda b, j: (b, 0, 0)),
                  pl.BlockSpec((seq, W_KV), lambda b, j: (b, 0)),
                  pl.BlockSpec((1, W_KV, seq), lambda b, j: (b, 0, 0)),
                  t_block(LANES),
                  t_block(W_NSA)],
        out_specs=t_block(W_NSA),
        out_shape=jax.ShapeDtypeStruct((batch, W_NSA, seq), jnp.bfloat16),
        scratch_shapes=[pltpu.VMEM((KEY_TILE, rows), jnp.float32),
                        pltpu.VMEM((KEY_TILE, rows), jnp.float32),
                        pltpu.VMEM((KEY_TILE, rows), jnp.bfloat16),
                        pltpu.VMEM((KEY_TILE, rows), jnp.bfloat16)],
        compiler_params=pltpu.CompilerParams(dimension_semantics=("arbitrary", "arbitrary"),
                                             vmem_limit_bytes=VMEM_LIMIT),
        name="nsa_attn",
    )(qa, kcp, vcp, onehot, ks, vst, kw, vwt, gate, sza)


DIL_MAX = max(d for _, d in DIL_PATTERNS)
DIL_SUPER = BLOCK * DIL_MAX
DIL_UNITS = DIL_SUPER // BLOCK
DIL_UNROLL = 16
HEAD_PAIR = 2 * HEAD_DIM
MIX_ROWS = 256


def _dil_mix_kernel(q_ref, kp_ref, kc_ref, vp_ref, vc_ref, z_ref, o_ref,
                    qf, kf, vf, num_scr, den_scr, max_scr, bias_scr):
    f32, bf16 = jnp.float32, jnp.bfloat16
    sb = pl.program_id(1)
    qf[...] = q_ref[...].astype(f32)
    kf[0:DIL_SUPER] = kp_ref[...].astype(f32)
    kf[DIL_SUPER:2 * DIL_SUPER] = kc_ref[...].astype(f32)
    vf[0:DIL_SUPER] = vp_ref[...].astype(f32)
    vf[DIL_SUPER:2 * DIL_SUPER] = vc_ref[...].astype(f32)

    row = lax.broadcasted_iota(jnp.int32, (2 * BLOCK, 2 * BLOCK), 0)
    col = lax.broadcasted_iota(jnp.int32, (2 * BLOCK, 2 * BLOCK), 1)
    dist = BLOCK + (row & (BLOCK - 1)) - col
    band = (dist >= 0) & (dist <= BLOCK)
    bias_scr[0] = jnp.where(band, 0.0, NEG_INF)
    bias_scr[1] = jnp.where(band & (col >= BLOCK), 0.0, NEG_INF)
    first_head = lax.broadcasted_iota(jnp.int32, (BLOCK, HEAD_PAIR), 1) < HEAD_DIM
    ones = jnp.ones((2 * BLOCK, HEAD_PAIR), bf16)

    for pat, (window, dil) in enumerate(DIL_PATTERNS):
        shift = dil.bit_length() - 1

        def unit(u, pat=pat, dil=dil, shift=shift):
            cls = u & (dil - 1)
            blk = u >> shift
            q_start = cls + blk * (BLOCK * dil)
            k_start = DIL_SUPER + q_start - BLOCK * dil
            q2 = qf[pl.ds(q_start, BLOCK, stride=dil), :]
            k2 = kf[pl.ds(k_start, 2 * BLOCK, stride=dil), :]
            v2 = vf[pl.ds(k_start, 2 * BLOCK, stride=dil), :]
            qm = jnp.concatenate([jnp.where(first_head, q2, 0.0),
                                  jnp.where(first_head, 0.0, q2)], axis=0).astype(bf16)
            s = _dot_nt(qm, k2.astype(bf16))
            no_prev = jnp.where((sb == 0) & (blk == 0), 1, 0)
            s = s + bias_scr[no_prev]
            m = jnp.max(s, axis=1, keepdims=True)
            e = jnp.exp2(s - m).astype(bf16)
            r = _dot(e, jnp.concatenate([v2.astype(bf16), ones], axis=1))
            mb = jnp.broadcast_to(m, (2 * BLOCK, HEAD_PAIR))
            out_rows = pl.ds(q_start, BLOCK, stride=dil)
            num_scr[pat, out_rows, :] = jnp.where(first_head, r[0:BLOCK, 0:HEAD_PAIR], r[BLOCK:, 0:HEAD_PAIR])
            den_scr[pat, out_rows, :] = jnp.where(first_head, r[0:BLOCK, HEAD_PAIR:], r[BLOCK:, HEAD_PAIR:])
            max_scr[pat, out_rows, :] = jnp.where(first_head, mb[0:BLOCK], mb[BLOCK:])

        def trip(it, carry, unit=unit):
            for j in range(DIL_UNROLL):
                unit(it * DIL_UNROLL + j)
            return carry

        lax.fori_loop(0, DIL_UNITS // DIL_UNROLL, trip, 0)

    def mix(ci, carry):
        rows = pl.ds(pl.multiple_of(ci * MIX_ROWS, MIX_ROWS), MIX_ROWS)
        ms = [max_scr[p, rows, :] for p in range(len(DIL_PATTERNS))]
        mx = jnp.maximum(jnp.maximum(ms[0], ms[1]), ms[2])
        cs = [jnp.exp2(m - mx) for m in ms]
        num = cs[0] * num_scr[0, rows, :] + cs[1] * num_scr[1, rows, :] + cs[2] * num_scr[2, rows, :]
        den = cs[0] * den_scr[0, rows, :] + cs[1] * den_scr[1, rows, :] + cs[2] * den_scr[2, rows, :]
        o_ref[rows, :] = (num / den * z_ref[rows, :].astype(f32)).astype(o_ref.dtype)
        return carry

    lax.fori_loop(0, DIL_SUPER // MIX_ROWS, mix, 0)


def _dil_mix(qb, kb, vb, szb, batch):
    rows = qb.shape[0]
    nsb = rows // batch // DIL_SUPER
    cur = pl.BlockSpec((DIL_SUPER, HEAD_PAIR), lambda b, s, h: (b * nsb + s, h))
    prev = pl.BlockSpec((DIL_SUPER, HEAD_PAIR), lambda b, s, h: (b * nsb + jnp.maximum(s - 1, 0), h))
    f32 = jnp.float32
    return pl.pallas_call(
        _dil_mix_kernel,
        grid=(batch, nsb, W_DIL // HEAD_PAIR),
        in_specs=[cur, prev, cur, prev, cur, cur],
        out_specs=cur,
        out_shape=jax.ShapeDtypeStruct((rows, W_DIL), jnp.bfloat16),
        scratch_shapes=[pltpu.VMEM((DIL_SUPER, HEAD_PAIR), f32),
                        pltpu.VMEM((2 * DIL_SUPER, HEAD_PAIR), f32),
                        pltpu.VMEM((2 * DIL_SUPER, HEAD_PAIR), f32),
                        pltpu.VMEM((len(DIL_PATTERNS), DIL_SUPER, HEAD_PAIR), f32),
                        pltpu.VMEM((len(DIL_PATTERNS), DIL_SUPER, HEAD_PAIR), f32),
                        pltpu.VMEM((len(DIL_PATTERNS), DIL_SUPER, HEAD_PAIR), f32),
                        pltpu.VMEM((2, 2 * BLOCK, 2 * BLOCK), f32)],
        compiler_params=pltpu.CompilerParams(
            dimension_semantics=("arbitrary", "arbitrary", "arbitrary"),
            vmem_limit_bytes=VMEM_LIMIT),
        name="dil_mix",
    )(qb, kb, kb, vb, vb, szb)


def _out_proj_kernel(x_ref, ma_ref, mb_ref, w_ref, g_ref, out_ref):
    y = lax.dot_general(ma_ref[0], w_ref[0:W_NSA, :], _TN, preferred_element_type=jnp.float32)
    y = y + _dot(mb_ref[...], w_ref[W_NSA:W_NSA + W_DIL, :])
    ms = jnp.mean(y * y, axis=-1, keepdims=True)
    out_ref[...] = x_ref[...] + y * lax.rsqrt(ms + RMS_EPS) * g_ref[...]


def _out_proj(x2, mixed_a_t, mixed_b, w, g):
    rows, d_model = x2.shape
    tm = PROJ_ROWS
    per_batch = mixed_a_t.shape[2] // tm
    row_spec = lambda wd: pl.BlockSpec((tm, wd), lambda i: (i, 0))
    return pl.pallas_call(
        _out_proj_kernel,
        grid=(rows // tm,),
        in_specs=[row_spec(d_model),
                  pl.BlockSpec((1, W_NSA, tm), lambda i: (i // per_batch, 0, i % per_batch)),
                  row_spec(W_DIL),
                  pl.BlockSpec(w.shape, lambda i: (0, 0)),
                  pl.BlockSpec((1, d_model), lambda i: (0, 0))],
        out_specs=row_spec(d_model),
        out_shape=jax.ShapeDtypeStruct((rows, d_model), jnp.float32),
        compiler_params=pltpu.CompilerParams(dimension_semantics=("arbitrary",)),
        name="out_proj",
    )(x2, mixed_a_t, mixed_b, w, g)


def _rope_tables(positions):
    inv = 1.0 / (ROPE_THETA ** (jnp.arange(0, ROPE_DIMS, 2, dtype=jnp.float32) / ROPE_DIMS))
    ang = positions.astype(jnp.float32).reshape(-1)[:, None] * inv
    cos, sin = jnp.cos(ang), jnp.sin(ang)
    rest = HEAD_DIM - ROPE_DIMS
    one = jnp.ones((cos.shape[0], rest), jnp.float32)
    zero = jnp.zeros((cos.shape[0], rest), jnp.float32)
    zh = jnp.zeros_like(sin)
    per_head = lambda lo, hi, fill: jnp.tile(jnp.concatenate([lo, hi, fill], axis=1), (1, 2))
    return per_head(cos, cos, one), per_head(-sin, zh, zero), per_head(zh, sin, zero)


def _in_proj_weights(w_in):
    scale = HEAD_DIM ** -0.5 * LOG2_E
    bf16 = jnp.bfloat16
    n_head, n_gate = W_NSA + 6 * W_KV, 3 * NSA_HEADS
    ones = lambda n: jnp.ones((n,), jnp.float32)
    head_scale = jnp.concatenate([scale * ones(W_NSA), ones(6 * W_KV)])
    tail_scale = jnp.concatenate([ones(W_NSA), scale * ones(W_DIL), ones(3 * W_DIL)])
    w_head = (w_in[:, :n_head] * head_scale).astype(bf16)
    w_gate = jnp.pad(w_in[:, n_head:n_head + n_gate], ((0, 0), (0, LANES - n_gate))).astype(bf16)
    w_tail = (w_in[:, n_head + n_gate:] * tail_scale).astype(bf16)
    return w_head, w_gate, w_tail


def kernel(x, positions, pre_norm_g, w_in, cmp_k_pos, cmp_k_w1, cmp_k_w2,
           cmp_v_pos, cmp_v_w1, cmp_v_w2, w_out, post_norm_g):
    B, S, d_model = x.shape
    depth = w_in.shape[0]
    n_sel = S // SEL_BLOCK
    assert S % KEY_TILE == 0 and n_sel == MAX_SEL_BLOCKS and S >= WIN_BLOCKS * BLOCK
    assert S % DIL_SUPER == 0 and all(win // dil == BLOCK for win, dil in DIL_PATTERNS)
    bf16 = jnp.bfloat16

    rope_c, rope_a, rope_b = _rope_tables(positions)
    onehot = (jnp.arange(S)[:, None] // SEL_BLOCK == jnp.arange(LANES)[None, :]).astype(bf16)
    x2 = x.reshape(B * S, d_model)

    for layer in range(depth):
        qa, kvc, ks, kw, vst, vwt, gate, sza, qb_, kb_, vb_, szb = _in_proj(
            x2, pre_norm_g[layer][None, :], _in_proj_weights(w_in[layer]), rope_c, rope_a, rope_b, B)

        kvc = _compress(kvc, *_compress_weights(jnp.stack([cmp_k_pos[layer], cmp_v_pos[layer]]),
                                                jnp.stack([cmp_k_w1[layer], cmp_v_w1[layer]]),
                                                jnp.stack([cmp_k_w2[layer], cmp_v_w2[layer]])), B)
        mixed_a = _nsa_attn(qa, kvc[0], kvc[1], onehot, ks, vst, kw, vwt, gate, sza, B)
        mixed_b = _dil_mix(qb_, kb_, vb_, szb, B)
        x2 = _out_proj(x2, mixed_a, mixed_b, w_out[layer].astype(bf16), post_norm_g[layer][None, :])
    return x2.reshape(B, S, d_model)
```

```python
import jax
import jax.numpy as jnp
from jax import lax
from jax.experimental import pallas as pl
from jax.experimental.pallas import tpu as pltpu

HEAD_DIM = 64
NSA_HEADS = 8
NSA_KV_HEADS = 2
NSA_Q_PER_KV = NSA_HEADS // NSA_KV_HEADS
DIL_HEADS = 8
W_NSA = NSA_HEADS * HEAD_DIM
W_KV = NSA_KV_HEADS * HEAD_DIM
W_DIL = DIL_HEADS * HEAD_DIM
CMP_LEN = 32
CMP_STRIDE = 16
CMP_HIDDEN = 256
SEL_BLOCK = 64
SEL_TOP_N = 16
N_FORCED = 3
SWA_WINDOW = 512
DIL_PATTERNS = ((128, 1), (512, 4), (2048, 16))
BLOCK = 128
ROPE_THETA = 500000.0
ROPE_DIMS = HEAD_DIM // 4
RMS_EPS = 1e-6
NEG_INF = -1e30
FORCE_SCORE = 1e4
LOG2_E = 1.4426950408889634

LANES = 128
VMEM_LIMIT = 56 * 1024 * 1024
MAX_SEL_BLOCKS = LANES
CMP_PER_SEL = SEL_BLOCK // CMP_STRIDE
KEY_TILE = 512
PROJ_ROWS = 512
ONES_ROWS = 16

_NT = (((1,), (1,)), ((), ()))
_TN = (((0,), (0,)), ((), ()))


def _dot(a, b):
    return jnp.dot(a, b, preferred_element_type=jnp.float32)


def _dot_nt(a, b):
    return lax.dot_general(a, b, _NT, preferred_element_type=jnp.float32)


def _sigmoid(x):
    return 1.0 / (1.0 + jnp.exp(-x))


def _rope(x, c, a, b):
    width = x.shape[1]
    reps = width // LANES
    ct = jnp.tile(c, (1, reps))
    at = jnp.tile(a, (1, reps))
    bt = jnp.tile(b, (1, reps))
    half = ROPE_DIMS // 2
    return x * ct + pltpu.roll(x, width - half, 1) * at + pltpu.roll(x, half, 1) * bt


def _in_proj_kernel(x_ref, g_ref, w_head_ref, w_gate_ref, w_tail_ref, c_ref, a_ref, b_ref,
                    qa_ref, kvc_ref, ks_ref, kw_ref, vst_ref, vwt_ref, gate_ref, sza_ref,
                    qb_ref, kb_ref, vb_ref, szb_ref):
    x = x_ref[...]
    ms = jnp.mean(x * x, axis=-1, keepdims=True)
    h = (x * lax.rsqrt(ms + RMS_EPS) * g_ref[...]).astype(jnp.bfloat16)
    c = c_ref[...]
    a = a_ref[...]
    b = b_ref[...]
    def columns_of(w_ref):
        off = 0

        def take(width):
            nonlocal off
            off += width
            return _dot(h, w_ref[:, off - width:off])
        return take

    head, gates, proj = columns_of(w_head_ref), columns_of(w_gate_ref), columns_of(w_tail_ref)
    qa_ref[0] = _rope(head(W_NSA), c, a, b).T.astype(qa_ref.dtype)
    kva = head(6 * W_KV)
    part = lambda i: kva[:, i * W_KV:(i + 1) * W_KV]
    kvc_ref[...] = jnp.concatenate([_rope(part(0), c, a, b), part(1)], axis=1).astype(kvc_ref.dtype)
    ks_ref[...] = _rope(part(2), c, a, b).astype(ks_ref.dtype)
    kw_ref[...] = _rope(part(4), c, a, b).astype(kw_ref.dtype)
    vst_ref[0] = part(3).T.astype(vst_ref.dtype)
    vwt_ref[0] = part(5).T.astype(vwt_ref.dtype)
    gate_ref[0] = _sigmoid(gates(LANES)).T
    za = proj(W_NSA)
    sza_ref[0] = (za * _sigmoid(za)).T.astype(sza_ref.dtype)
    qb_ref[...] = _rope(proj(W_DIL), c, a, b).astype(qb_ref.dtype)
    kb_ref[...] = _rope(proj(W_DIL), c, a, b).astype(kb_ref.dtype)
    vb_ref[...] = proj(W_DIL).astype(vb_ref.dtype)
    zb = proj(W_DIL)
    szb_ref[...] = (zb * _sigmoid(zb)).astype(szb_ref.dtype)


def _in_proj(x2, g, weights, c, a, b, batch):
    rows, d_model = x2.shape
    tm = PROJ_ROWS
    seq = rows // batch
    per_batch = seq // tm
    bf16 = jnp.bfloat16
    row_spec = lambda wd: pl.BlockSpec((tm, wd), lambda i: (i, 0))
    row_out = lambda wd: (row_spec(wd), jax.ShapeDtypeStruct((rows, wd), bf16))
    t_out = lambda wd, dt=bf16: (pl.BlockSpec((1, wd, tm), lambda i: (i // per_batch, 0, i % per_batch)),
                                 jax.ShapeDtypeStruct((batch, wd, seq), dt))
    outs = [t_out(W_NSA), row_out(2 * W_KV), row_out(W_KV), row_out(W_KV), t_out(W_KV), t_out(W_KV),
            t_out(LANES, jnp.float32), t_out(W_NSA),
            row_out(W_DIL), row_out(W_DIL), row_out(W_DIL), row_out(W_DIL)]
    return pl.pallas_call(
        _in_proj_kernel,
        grid=(rows // tm,),
        in_specs=[row_spec(d_model),
                  pl.BlockSpec((1, d_model), lambda i: (0, 0))]
                 + [pl.BlockSpec(w.shape, lambda i: (0, 0)) for w in weights]
                 + [row_spec(LANES), row_spec(LANES), row_spec(LANES)],
        out_specs=[o[0] for o in outs],
        out_shape=[o[1] for o in outs],
        compiler_params=pltpu.CompilerParams(dimension_semantics=("arbitrary",),
                                             vmem_limit_bytes=VMEM_LIMIT),
        name="in_proj",
    )(x2, g, *weights, c, a, b)


CMP_PAIRS = CMP_STRIDE // 2


def _compress_kernel(x_ref, pos_ref, wt_ref, wb_ref, w2_ref, o_ref, stage):
    f32, bf16 = jnp.float32, jnp.bfloat16
    stage[...] = x_ref[...].astype(f32)
    nj = MAX_SEL_BLOCKS
    hidden = 2 * CMP_HIDDEN

    def offset_rows(l):
        return jnp.concatenate([stage[pl.ds(CMP_STRIDE * m + l, nj, stride=SEL_BLOCK), :]
                                for m in range(CMP_PER_SEL)], axis=0)

    top = jnp.zeros((CMP_PER_SEL * nj, hidden), f32)
    bot = jnp.zeros((CMP_PER_SEL * nj, hidden), f32)
    bias = jnp.zeros((16, hidden), f32)
    for i in range(CMP_PAIRS):
        x = jnp.concatenate([offset_rows(2 * i), offset_rows(2 * i + 1)], axis=1).astype(bf16)
        top = top + _dot(x, wt_ref[0, i])
        bot = bot + _dot(x, wb_ref[0, i])
        p_top = jnp.concatenate([pos_ref[0, 2 * i], pos_ref[0, 2 * i + 1]], axis=1)
        p_bot = jnp.concatenate([pos_ref[0, CMP_STRIDE + 2 * i], pos_ref[0, CMP_STRIDE + 2 * i + 1]], axis=1)
        bias = bias + _dot(p_top, wt_ref[0, i]) + _dot(p_bot, wb_ref[0, i])
    nxt = jnp.concatenate([bot[nj:], pltpu.roll(bot[0:nj], nj - 1, 0)], axis=0)
    hid = top + nxt + bias[0:1, :]
    act = (hid * _sigmoid(hid)).astype(bf16)
    for g in range(NSA_KV_HEADS):
        o_ref[0, g] = _dot(act[:, g * CMP_HIDDEN:(g + 1) * CMP_HIDDEN], w2_ref[0]).astype(o_ref.dtype)


def _compress(kvc, pos, wt, wb, w2, batch):
    seq = kvc.shape[0] // batch
    G = NSA_KV_HEADS
    ncmp = CMP_PER_SEL * MAX_SEL_BLOCKS
    whole = lambda arr: pl.BlockSpec((1,) + arr.shape[1:], lambda s, b: (s,) + (0,) * (arr.ndim - 1))
    return pl.pallas_call(
        _compress_kernel,
        grid=(2, batch),
        in_specs=[pl.BlockSpec((seq, W_KV), lambda s, b: (b, s)), whole(pos), whole(wt), whole(wb), whole(w2)],
        out_specs=pl.BlockSpec((1, G, ncmp, HEAD_DIM), lambda s, b: (s, b, 0, 0)),
        out_shape=jax.ShapeDtypeStruct((2, batch * G, ncmp, HEAD_DIM), jnp.bfloat16),
        scratch_shapes=[pltpu.VMEM((seq, W_KV), jnp.float32)],
        compiler_params=pltpu.CompilerParams(dimension_semantics=("arbitrary", "arbitrary"),
                                             vmem_limit_bytes=VMEM_LIMIT),
        name="compress",
    )(kvc, pos, wt, wb, w2)


def _compress_weights(pos, w1, w2):
    bf16 = jnp.bfloat16
    pos2 = jnp.broadcast_to(jnp.tile(pos, (1, 1, 2))[:, :, None, :], (2, CMP_LEN, 16, W_KV)).astype(bf16)
    w = w1.astype(bf16).reshape(2, 2, CMP_STRIDE, HEAD_DIM, CMP_HIDDEN)
    lead = ((0, 0),) * 3
    bd = (jnp.pad(w, lead + ((0, HEAD_DIM), (0, CMP_HIDDEN)))
          + jnp.pad(w, lead + ((HEAD_DIM, 0), (CMP_HIDDEN, 0))))
    bd = bd.reshape(2, 2, CMP_PAIRS, 2 * W_KV, 2 * CMP_HIDDEN)
    return pos2, bd[:, 0], bd[:, 1], w2.astype(bf16)


WIN_BLOCKS = -(-(SWA_WINDOW - 1) // BLOCK) + 1


def _nsa_attn_kernel(q_ref, kc_ref, vc_ref, et_ref, ks_ref, vst_ref, kw_ref, vwt_ref,
                     gate_ref, sza_ref, o_ref, s_a, s_b, p_a, p_b):
    f32, bf16 = jnp.float32, jnp.bfloat16
    qb = pl.program_id(1)
    G, R = NSA_KV_HEADS, NSA_Q_PER_KV
    grows = R * BLOCK
    rows = G * grows
    gcols = lambda g: slice(g * grows, (g + 1) * grows)
    heads_t = [q_ref[0, h * HEAD_DIM:(h + 1) * HEAD_DIM, :] for h in range(NSA_HEADS)]
    q_cmp_t = jnp.concatenate(heads_t, axis=1)
    zero_t = jnp.zeros((HEAD_DIM, BLOCK), bf16)
    q_kv_t = jnp.concatenate([jnp.concatenate([qh, zero_t] if h < R else [zero_t, qh], axis=0)
                              for h, qh in enumerate(heads_t)], axis=1)
    t_1 = qb * BLOCK + lax.broadcasted_iota(jnp.int32, (1, BLOCK), 1)
    head_bias = lambda ok, reps: jnp.tile(jnp.where(ok, 0.0, NEG_INF), (1, reps))

    ncmp = kc_ref.shape[1]
    pos = lax.broadcasted_iota(jnp.int32, (ncmp, BLOCK), 0)
    cmp_end = (pos & (LANES - 1)) * SEL_BLOCK + (pos >> 7) * CMP_STRIDE + (CMP_LEN - 1)
    cmp_bias = head_bias(cmp_end <= t_1, R)
    seen = jnp.tile(jnp.where(t_1 >= CMP_LEN - 1, 1.0, 0.0), (1, R))
    blk = lax.broadcasted_iota(jnp.int32, (LANES, BLOCK), 0)
    o_cmp, imp_sel = [], []
    for g in range(G):
        st = _dot(kc_ref[g], q_cmp_t[:, gcols(g)]) + cmp_bias
        mx = jnp.max(st, axis=0, keepdims=True)
        e = jnp.exp2(st - mx)
        den = jnp.maximum(jnp.sum(e, axis=0, keepdims=True), 1e-30)
        p = e * (seen / den)
        o_cmp.append(lax.dot_general(vc_ref[g], p.astype(bf16), _TN,
                                     preferred_element_type=f32))
        imp = p[:, 0:BLOCK]
        for r in range(1, R):
            imp = imp + p[:, r * BLOCK:(r + 1) * BLOCK]
        q4 = [imp[i * LANES:(i + 1) * LANES] for i in range(CMP_PER_SEL)]
        prev_last = jnp.where(blk == 0, 0.0, pltpu.roll(q4[3], 1, 0))
        imp_sel.append(prev_last + 2.0 * (q4[0] + q4[1] + q4[2]) + q4[3])

    span = WIN_BLOCKS * BLOCK
    first_blk = jnp.maximum(qb - (WIN_BLOCKS - 1), 0)
    win = pl.ds(pl.multiple_of(first_blk * BLOCK, BLOCK), span)
    dist = t_1 - (first_blk * BLOCK + lax.broadcasted_iota(jnp.int32, (span, BLOCK), 0))
    st = _dot(kw_ref[win, :], q_kv_t) + head_bias((dist >= 0) & (dist <= SWA_WINDOW - 1), G * R)
    mw = jnp.max(st, axis=0, keepdims=True)
    e = jnp.exp2(st - mw).astype(bf16)
    ones_win = jnp.ones((ONES_ROWS, span), bf16)
    o_win = []
    for g in range(G):
        r = _dot(jnp.concatenate([vwt_ref[0, g * HEAD_DIM:(g + 1) * HEAD_DIM, win], ones_win], axis=0),
                 e[:, gcols(g)])
        o_win.append(r[0:HEAD_DIM] * (1.0 / r[HEAD_DIM:HEAD_DIM + 1]))

    cur = (qb * BLOCK + lax.broadcasted_iota(jnp.int32, (LANES, BLOCK), 1)) >> 6
    forced = (blk == 0) | (blk == cur) | (blk == cur - 1)
    blk_f = blk.astype(f32)
    bias = []
    for g in range(G):
        sc = jnp.where(forced, -2.0, jnp.where(blk <= cur, imp_sel[g], -1.0))
        for _ in range(SEL_TOP_N - N_FORCED):
            best = jnp.max(sc, axis=0, keepdims=True)
            first = jnp.min(jnp.where(sc == best, blk_f, float(LANES)), axis=0, keepdims=True)
            sc = jnp.where(blk_f == first, -2.0, sc)
        taken = (sc == -2.0) & (blk <= cur)
        bias.append(jnp.tile(jnp.where(taken, 0.0, NEG_INF).astype(bf16), (1, R)))

    qa_t = jnp.concatenate([jnp.concatenate(bias, axis=1), q_kv_t], axis=0)
    n = (qb * BLOCK) // KEY_TILE
    last = jnp.maximum(n - 1, 0)
    ones = jnp.ones((ONES_ROWS, KEY_TILE), bf16)

    def keys_aug(kt):
        tile = pl.ds(pl.multiple_of(kt * KEY_TILE, KEY_TILE), KEY_TILE)
        return jnp.concatenate([et_ref[tile, :], ks_ref[tile, :]], axis=1)

    def qk(kt, s_ref):
        st = _dot(keys_aug(kt), qa_t)
        s_ref[...] = st
        return jnp.max(st, axis=0, keepdims=True)

    def pv(kt, p_ref):
        tile = pl.ds(pl.multiple_of(kt * KEY_TILE, KEY_TILE), KEY_TILE)
        return tuple(_dot(jnp.concatenate([vst_ref[0, g * HEAD_DIM:(g + 1) * HEAD_DIM, tile], ones], axis=0),
                          p_ref[:, gcols(g)]) for g in range(G))

    def softmax(m_old, mx, s_ref, p_ref):
        m_new = jnp.maximum(m_old, mx)
        p_ref[...] = jnp.exp2(s_ref[...] - m_new).astype(bf16)
        return m_new, jnp.exp2(m_old - m_new)

    def accumulate(acc, alpha, weight, contrib):
        return tuple(alpha[:, gcols(g)] * acc[g] + weight * contrib[g] for g in range(G))

    key = n * KEY_TILE + lax.broadcasted_iota(jnp.int32, (KEY_TILE, BLOCK), 0)
    st = _dot(keys_aug(n), qa_t) + head_bias(key <= t_1, G * R)
    m0 = jnp.max(st, axis=0, keepdims=True)
    p_b[...] = jnp.exp2(st - m0).astype(bf16)
    mx0 = qk(0, s_a)
    acc0 = tuple(jnp.zeros((HEAD_DIM + ONES_ROWS, grows), f32) for _ in range(G))
    one = jnp.ones_like(m0)

    def body(i, carry):
        m, acc, alpha_prev, w_prev, kt_prev, mx = carry
        first, second = 2 * i, 2 * i + 1
        w_second = jnp.where(second < n, 1.0, 0.0)
        kt_second = jnp.minimum(second, last)
        acc = accumulate(acc, alpha_prev, w_prev, pv(kt_prev, p_b))
        m, alpha = softmax(m, mx, s_a, p_a)
        mx = qk(kt_second, s_b)
        acc = accumulate(acc, alpha, 1.0, pv(first, p_a))
        m, alpha = softmax(m, mx, s_b, p_b)
        mx = qk(jnp.minimum(second + 1, last), s_a)
        return m, acc, alpha, w_second, kt_second, mx

    init = (m0, acc0, one, jnp.float32(1.0), n, mx0)
    _, acc, alpha_prev, w_prev, kt_prev, _ = lax.fori_loop(0, (n + 1) // 2, body, init)
    acc = accumulate(acc, alpha_prev, w_prev, pv(kt_prev, p_b))
    o_slc = [a[0:HEAD_DIM] * (1.0 / a[HEAD_DIM:HEAD_DIM + 1]) for a in acc]

    for g in range(G):
        for r in range(R):
            h = g * R + r
            cols = slice(r * BLOCK, (r + 1) * BLOCK)
            gate_row = lambda branch: gate_ref[0, branch * NSA_HEADS + h:branch * NSA_HEADS + h + 1, :]
            o_t = (gate_row(0) * o_cmp[g][:, cols] + gate_row(1) * o_slc[g][:, cols]
                   + gate_row(2) * o_win[g][:, cols])
            dims = slice(h * HEAD_DIM, (h + 1) * HEAD_DIM)
            o_ref[0, dims, :] = (o_t * sza_ref[0, dims, :].astype(f32)).astype(o_ref.dtype)


def _nsa_attn(qa, kcp, vcp, onehot, ks, vst, kw, vwt, gate, sza, batch):
    seq = qa.shape[2]
    nqb = seq // BLOCK
    ncmp = kcp.shape[1]
    G = NSA_KV_HEADS
    rows = NSA_HEADS * BLOCK
    t_block = lambda wd: pl.BlockSpec((1, wd, BLOCK), lambda b, j: (b, 0, j))
    return pl.pallas_call(
        _nsa_attn_kernel,
        grid=(batch, nqb),
        in_specs=[t_block(W_NSA),
                  pl.BlockSpec((G, ncmp, HEAD_DIM), lambda b, j: (b, 0, 0)),
                  pl.BlockSpec((G, ncmp, HEAD_DIM), lambda b, j: (b, 0, 0)),
                  pl.BlockSpec((seq, LANES), lambda b, j: (0, 0)),
                  pl.BlockSpec((seq, W_KV), lambda b, j: (b, 0)),
                  pl.BlockSpec((1, W_KV, seq), lambda b, j: (b, 0, 0)),
                  pl.BlockSpec((seq, W_KV), lambda b, j: (b, 0)),
                  pl.BlockSpec((1, W_KV, seq), lambda b, j: (b, 0, 0)),
                  t_block(LANES),
                  t_block(W_NSA)],
        out_specs=t_block(W_NSA),
        out_shape=jax.ShapeDtypeStruct((batch, W_NSA, seq), jnp.bfloat16),
        scratch_shapes=[pltpu.VMEM((KEY_TILE, rows), jnp.float32),
                        pltpu.VMEM((KEY_TILE, rows), jnp.float32),
                        pltpu.VMEM((KEY_TILE, rows), jnp.bfloat16),
                        pltpu.VMEM((KEY_TILE, rows), jnp.bfloat16)],
        compiler_params=pltpu.CompilerParams(dimension_semantics=("arbitrary", "arbitrary"),
                                             vmem_limit_bytes=VMEM_LIMIT),
        name="nsa_attn",
    )(qa, kcp, vcp, onehot, ks, vst, kw, vwt, gate, sza)


DIL_MAX = max(d for _, d in DIL_PATTERNS)
DIL_SUPER = BLOCK * DIL_MAX
DIL_UNITS = DIL_SUPER // BLOCK
DIL_UNROLL = 16
HEAD_PAIR = 2 * HEAD_DIM
MIX_ROWS = 256


def _dil_mix_kernel(q_ref, kp_ref, kc_ref, vp_ref, vc_ref, z_ref, o_ref,
                    qf, kf, vf, num_scr, den_scr, max_scr, bias_scr):
    f32, bf16 = jnp.float32, jnp.bfloat16
    sb = pl.program_id(1)
    qf[...] = q_ref[...].astype(f32)
    kf[0:DIL_SUPER] = kp_ref[...].astype(f32)
    kf[DIL_SUPER:2 * DIL_SUPER] = kc_ref[...].astype(f32)
    vf[0:DIL_SUPER] = vp_ref[...].astype(f32)
    vf[DIL_SUPER:2 * DIL_SUPER] = vc_ref[...].astype(f32)

    row = lax.broadcasted_iota(jnp.int32, (2 * BLOCK, 2 * BLOCK), 0)
    col = lax.broadcasted_iota(jnp.int32, (2 * BLOCK, 2 * BLOCK), 1)
    dist = BLOCK + (row & (BLOCK - 1)) - col
    band = (dist >= 0) & (dist <= BLOCK)
    bias_scr[0] = jnp.where(band, 0.0, NEG_INF)
    bias_scr[1] = jnp.where(band & (col >= BLOCK), 0.0, NEG_INF)
    first_head = lax.broadcasted_iota(jnp.int32, (BLOCK, HEAD_PAIR), 1) < HEAD_DIM
    ones = jnp.ones((2 * BLOCK, HEAD_PAIR), bf16)

    for pat, (window, dil) in enumerate(DIL_PATTERNS):
        shift = dil.bit_length() - 1

        def unit(u, pat=pat, dil=dil, shift=shift):
            cls = u & (dil - 1)
            blk = u >> shift
            q_start = cls + blk * (BLOCK * dil)
            k_start = DIL_SUPER + q_start - BLOCK * dil
            q2 = qf[pl.ds(q_start, BLOCK, stride=dil), :]
            k2 = kf[pl.ds(k_start, 2 * BLOCK, stride=dil), :]
            v2 = vf[pl.ds(k_start, 2 * BLOCK, stride=dil), :]
            qm = jnp.concatenate([jnp.where(first_head, q2, 0.0),
                                  jnp.where(first_head, 0.0, q2)], axis=0).astype(bf16)
            s = _dot_nt(qm, k2.astype(bf16))
            no_prev = jnp.where((sb == 0) & (blk == 0), 1, 0)
            s = s + bias_scr[no_prev]
            m = jnp.max(s, axis=1, keepdims=True)
            e = jnp.exp2(s - m).astype(bf16)
            r = _dot(e, jnp.concatenate([v2.astype(bf16), ones], axis=1))
            mb = jnp.broadcast_to(m, (2 * BLOCK, HEAD_PAIR))
            out_rows = pl.ds(q_start, BLOCK, stride=dil)
            num_scr[pat, out_rows, :] = jnp.where(first_head, r[0:BLOCK, 0:HEAD_PAIR], r[BLOCK:, 0:HEAD_PAIR])
            den_scr[pat, out_rows, :] = jnp.where(first_head, r[0:BLOCK, HEAD_PAIR:], r[BLOCK:, HEAD_PAIR:])
            max_scr[pat, out_rows, :] = jnp.where(first_head, mb[0:BLOCK], mb[BLOCK:])

        def trip(it, carry, unit=unit):
            for j in range(DIL_UNROLL):
                unit(it * DIL_UNROLL + j)
            return carry

        lax.fori_loop(0, DIL_UNITS // DIL_UNROLL, trip, 0)

    def mix(ci, carry):
        rows = pl.ds(pl.multiple_of(ci * MIX_ROWS, MIX_ROWS), MIX_ROWS)
        ms = [max_scr[p, rows, :] for p in range(len(DIL_PATTERNS))]
        mx = jnp.maximum(jnp.maximum(ms[0], ms[1]), ms[2])
        cs = [jnp.exp2(m - mx) for m in ms]
        num = cs[0] * num_scr[0, rows, :] + cs[1] * num_scr[1, rows, :] + cs[2] * num_scr[2, rows, :]
        den = cs[0] * den_scr[0, rows, :] + cs[1] * den_scr[1, rows, :] + cs[2] * den_scr[2, rows, :]
        o_ref[rows, :] = (num / den * z_ref[rows, :].astype(f32)).astype(o_ref.dtype)
        return carry

    lax.fori_loop(0, DIL_SUPER // MIX_ROWS, mix, 0)


def _dil_mix(qb, kb, vb, szb, batch):
    rows = qb.shape[0]
    nsb = rows // batch // DIL_SUPER
    cur = pl.BlockSpec((DIL_SUPER, HEAD_PAIR), lambda b, s, h: (b * nsb + s, h))
    prev = pl.BlockSpec((DIL_SUPER, HEAD_PAIR), lambda b, s, h: (b * nsb + jnp.maximum(s - 1, 0), h))
    f32 = jnp.float32
    return pl.pallas_call(
        _dil_mix_kernel,
        grid=(batch, nsb, W_DIL // HEAD_PAIR),
        in_specs=[cur, prev, cur, prev, cur, cur],
        out_specs=cur,
        out_shape=jax.ShapeDtypeStruct((rows, W_DIL), jnp.bfloat16),
        scratch_shapes=[pltpu.VMEM((DIL_SUPER, HEAD_PAIR), f32),
                        pltpu.VMEM((2 * DIL_SUPER, HEAD_PAIR), f32),
                        pltpu.VMEM((2 * DIL_SUPER, HEAD_PAIR), f32),
                        pltpu.VMEM((len(DIL_PATTERNS), DIL_SUPER, HEAD_PAIR), f32),
                        pltpu.VMEM((len(DIL_PATTERNS), DIL_SUPER, HEAD_PAIR), f32),
                        pltpu.VMEM((len(DIL_PATTERNS), DIL_SUPER, HEAD_PAIR), f32),
                        pltpu.VMEM((2, 2 * BLOCK, 2 * BLOCK), f32)],
        compiler_params=pltpu.CompilerParams(
            dimension_semantics=("arbitrary", "arbitrary", "arbitrary"),
            vmem_limit_bytes=VMEM_LIMIT),
        name="dil_mix",
    )(qb, kb, kb, vb, vb, szb)


def _out_proj_kernel(x_ref, ma_ref, mb_ref, w_ref, g_ref, out_ref):
    y = lax.dot_general(ma_ref[0], w_ref[0:W_NSA, :], _TN, preferred_element_type=jnp.float32)
    y = y + _dot(mb_ref[...], w_ref[W_NSA:W_NSA + W_DIL, :])
    ms = jnp.mean(y * y, axis=-1, keepdims=True)
    out_ref[...] = x_ref[...] + y * lax.rsqrt(ms + RMS_EPS) * g_ref[...]


def _out_proj(x2, mixed_a_t, mixed_b, w, g):
    rows, d_model = x2.shape
    tm = PROJ_ROWS
    per_batch = mixed_a_t.shape[2] // tm
    row_spec = lambda wd: pl.BlockSpec((tm, wd), lambda i: (i, 0))
    return pl.pallas_call(
        _out_proj_kernel,
        grid=(rows // tm,),
        in_specs=[row_spec(d_model),
                  pl.BlockSpec((1, W_NSA, tm), lambda i: (i // per_batch, 0, i % per_batch)),
                  row_spec(W_DIL),
                  pl.BlockSpec(w.shape, lambda i: (0, 0)),
                  pl.BlockSpec((1, d_model), lambda i: (0, 0))],
        out_specs=row_spec(d_model),
        out_shape=jax.ShapeDtypeStruct((rows, d_model), jnp.float32),
        compiler_params=pltpu.CompilerParams(dimension_semantics=("arbitrary",)),
        name="out_proj",
    )(x2, mixed_a_t, mixed_b, w, g)


def _rope_tables(positions):
    inv = 1.0 / (ROPE_THETA ** (jnp.arange(0, ROPE_DIMS, 2, dtype=jnp.float32) / ROPE_DIMS))
    ang = positions.astype(jnp.float32).reshape(-1)[:, None] * inv
    cos, sin = jnp.cos(ang), jnp.sin(ang)
    rest = HEAD_DIM - ROPE_DIMS
    one = jnp.ones((cos.shape[0], rest), jnp.float32)
    zero = jnp.zeros((cos.shape[0], rest), jnp.float32)
    zh = jnp.zeros_like(sin)
    per_head = lambda lo, hi, fill: jnp.concatenate([lo, hi, fill] * (LANES // HEAD_DIM), axis=1)
    return per_head(cos, cos, one), per_head(-sin, zh, zero), per_head(zh, sin, zero)


def _in_proj_weights(w_in):
    scale = HEAD_DIM ** -0.5 * LOG2_E
    bf16 = jnp.bfloat16
    n_head, n_gate = W_NSA + 6 * W_KV, 3 * NSA_HEADS
    ones = lambda n: jnp.ones((n,), jnp.float32)
    head_scale = jnp.concatenate([scale * ones(W_NSA), ones(6 * W_KV)])
    tail_scale = jnp.concatenate([ones(W_NSA), scale * ones(W_DIL), ones(3 * W_DIL)])
    w_head = (w_in[:, :n_head] * head_scale).astype(bf16)
    w_gate = jnp.pad(w_in[:, n_head:n_head + n_gate], ((0, 0), (0, LANES - n_gate))).astype(bf16)
    w_tail = (w_in[:, n_head + n_gate:] * tail_scale).astype(bf16)
    return w_head, w_gate, w_tail


def kernel(x, positions, pre_norm_g, w_in, cmp_k_pos, cmp_k_w1, cmp_k_w2,
           cmp_v_pos, cmp_v_w1, cmp_v_w2, w_out, post_norm_g):
    B, S, d_model = x.shape
    depth = w_in.shape[0]
    n_sel = S // SEL_BLOCK
    assert S % KEY_TILE == 0 and n_sel == MAX_SEL_BLOCKS and S >= WIN_BLOCKS * BLOCK
    assert S % DIL_SUPER == 0 and all(win // dil == BLOCK for win, dil in DIL_PATTERNS)
    bf16 = jnp.bfloat16

    rope_c, rope_a, rope_b = _rope_tables(positions)
    onehot = (jnp.arange(S)[:, None] // SEL_BLOCK == jnp.arange(LANES)[None, :]).astype(bf16)
    x2 = x.reshape(B * S, d_model)

    for layer in range(depth):
        qa, kvc, ks, kw, vst, vwt, gate, sza, qb_, kb_, vb_, szb = _in_proj(
            x2, pre_norm_g[layer][None, :], _in_proj_weights(w_in[layer]), rope_c, rope_a, rope_b, B)

        kvc = _compress(kvc, *_compress_weights(jnp.stack([cmp_k_pos[layer], cmp_v_pos[layer]]),
                                                jnp.stack([cmp_k_w1[layer], cmp_v_w1[layer]]),
                                                jnp.stack([cmp_k_w2[layer], cmp_v_w2[layer]])), B)
        mixed_a = _nsa_attn(qa, kvc[0], kvc[1], onehot, ks, vst, kw, vwt, gate, sza, B)
        mixed_b = _dil_mix(qb_, kb_, vb_, szb, B)
        x2 = _out_proj(x2, mixed_a, mixed_b, w_out[layer].astype(bf16), post_norm_g[layer][None, :])
    return x2.reshape(B, S, d_model)
```

```python
import jax
import jax.numpy as jnp
from jax import lax
from jax.experimental import pallas as pl
from jax.experimental.pallas import tpu as pltpu

HEAD_DIM = 64
NSA_HEADS = 8
NSA_KV_HEADS = 2
NSA_Q_PER_KV = NSA_HEADS // NSA_KV_HEADS
DIL_HEADS = 8
W_NSA = NSA_HEADS * HEAD_DIM
W_KV = NSA_KV_HEADS * HEAD_DIM
W_DIL = DIL_HEADS * HEAD_DIM
CMP_LEN = 32
CMP_STRIDE = 16
CMP_HIDDEN = 256
SEL_BLOCK = 64
SEL_TOP_N = 16
N_FORCED = 3
SWA_WINDOW = 512
DIL_PATTERNS = ((128, 1), (512, 4), (2048, 16))
BLOCK = 128
ROPE_THETA = 500000.0
ROPE_DIMS = HEAD_DIM // 4
RMS_EPS = 1e-6
NEG_INF = -1e30
FORCE_SCORE = 1e4
LOG2_E = 1.4426950408889634

LANES = 128
VMEM_LIMIT = 56 * 1024 * 1024
MAX_SEL_BLOCKS = LANES
CMP_PER_SEL = SEL_BLOCK // CMP_STRIDE
KEY_TILE = 512
PROJ_ROWS = 512
ONES_ROWS = 16

_NT = (((1,), (1,)), ((), ()))
_TN = (((0,), (0,)), ((), ()))


def _dot(a, b):
    return jnp.dot(a, b, preferred_element_type=jnp.float32)


def _dot_nt(a, b):
    return lax.dot_general(a, b, _NT, preferred_element_type=jnp.float32)


def _sigmoid(x):
    return 1.0 / (1.0 + jnp.exp(-x))


def _rope(x, c, a, b):
    width = x.shape[1]
    reps = width // LANES
    ct = jnp.tile(c, (1, reps))
    at = jnp.tile(a, (1, reps))
    bt = jnp.tile(b, (1, reps))
    half = ROPE_DIMS // 2
    return x * ct + pltpu.roll(x, width - half, 1) * at + pltpu.roll(x, half, 1) * bt


def _in_proj_kernel(x_ref, g_ref, w_head_ref, w_gate_ref, w_tail_ref, c_ref, a_ref, b_ref,
                    qa_ref, kvc_ref, ks_ref, kw_ref, vst_ref, vwt_ref, gate_ref, sza_ref,
                    qb_ref, kb_ref, vb_ref, szb_ref):
    x = x_ref[...]
    ms = jnp.mean(x * x, axis=-1, keepdims=True)
    h = (x * lax.rsqrt(ms + RMS_EPS) * g_ref[...]).astype(jnp.bfloat16)
    c = c_ref[...]
    a = a_ref[...]
    b = b_ref[...]
    def columns_of(w_ref):
        off = 0

        def take(width):
            nonlocal off
            off += width
            return _dot(h, w_ref[:, off - width:off])
        return take

    head, gates, proj = columns_of(w_head_ref), columns_of(w_gate_ref), columns_of(w_tail_ref)
    qa_ref[0] = _rope(head(W_NSA), c, a, b).T.astype(qa_ref.dtype)
    kva = head(6 * W_KV)
    part = lambda i: kva[:, i * W_KV:(i + 1) * W_KV]
    kvc_ref[...] = jnp.concatenate([_rope(part(0), c, a, b), part(1)], axis=1).astype(kvc_ref.dtype)
    ks_ref[...] = _rope(part(2), c, a, b).astype(ks_ref.dtype)
    kw_ref[...] = _rope(part(4), c, a, b).astype(kw_ref.dtype)
    vst_ref[0] = part(3).T.astype(vst_ref.dtype)
    vwt_ref[0] = part(5).T.astype(vwt_ref.dtype)
    gate_ref[0] = _sigmoid(gates(LANES)).T
    za = proj(W_NSA)
    sza_ref[0] = (za * _sigmoid(za)).T.astype(sza_ref.dtype)
    qb_ref[...] = _rope(proj(W_DIL), c, a, b).astype(qb_ref.dtype)
    kb_ref[...] = _rope(proj(W_DIL), c, a, b).astype(kb_ref.dtype)
    vb_ref[...] = proj(W_DIL).astype(vb_ref.dtype)
    zb = proj(W_DIL)
    szb_ref[...] = (zb * _sigmoid(zb)).astype(szb_ref.dtype)


def _in_proj(x2, g, weights, c, a, b, batch):
    rows, d_model = x2.shape
    tm = PROJ_ROWS
    seq = rows // batch
    per_batch = seq // tm
    bf16 = jnp.bfloat16
    row_spec = lambda wd: pl.BlockSpec((tm, wd), lambda i: (i, 0))
    row_out = lambda wd: (row_spec(wd), jax.ShapeDtypeStruct((rows, wd), bf16))
    t_out = lambda wd, dt=bf16: (pl.BlockSpec((1, wd, tm), lambda i: (i // per_batch, 0, i % per_batch)),
                                 jax.ShapeDtypeStruct((batch, wd, seq), dt))
    outs = [t_out(W_NSA), row_out(2 * W_KV), row_out(W_KV), row_out(W_KV), t_out(W_KV), t_out(W_KV),
            t_out(LANES, jnp.float32), t_out(W_NSA),
            row_out(W_DIL), row_out(W_DIL), row_out(W_DIL), row_out(W_DIL)]
    return pl.pallas_call(
        _in_proj_kernel,
        grid=(rows // tm,),
        in_specs=[row_spec(d_model),
                  pl.BlockSpec((1, d_model), lambda i: (0, 0))]
                 + [pl.BlockSpec(w.shape, lambda i: (0, 0)) for w in weights]
                 + [row_spec(LANES), row_spec(LANES), row_spec(LANES)],
        out_specs=[o[0] for o in outs],
        out_shape=[o[1] for o in outs],
        compiler_params=pltpu.CompilerParams(dimension_semantics=("arbitrary",),
                                             vmem_limit_bytes=VMEM_LIMIT),
        name="in_proj",
    )(x2, g, *weights, c, a, b)


CMP_PAIRS = CMP_STRIDE // 2


def _compress_kernel(x_ref, pos_ref, wt_ref, wb_ref, w2_ref, o_ref, stage):
    f32, bf16 = jnp.float32, jnp.bfloat16
    stage[...] = x_ref[...].astype(f32)
    nj = MAX_SEL_BLOCKS
    hidden = 2 * CMP_HIDDEN

    def offset_rows(l):
        return jnp.concatenate([stage[pl.ds(CMP_STRIDE * m + l, nj, stride=SEL_BLOCK), :]
                                for m in range(CMP_PER_SEL)], axis=0)

    top = jnp.zeros((CMP_PER_SEL * nj, hidden), f32)
    bot = jnp.zeros((CMP_PER_SEL * nj, hidden), f32)
    bias = jnp.zeros((16, hidden), f32)
    for i in range(CMP_PAIRS):
        x = jnp.concatenate([offset_rows(2 * i), offset_rows(2 * i + 1)], axis=1).astype(bf16)
        top = top + _dot(x, wt_ref[0, i])
        bot = bot + _dot(x, wb_ref[0, i])
        p_top = jnp.concatenate([pos_ref[0, 2 * i], pos_ref[0, 2 * i + 1]], axis=1)
        p_bot = jnp.concatenate([pos_ref[0, CMP_STRIDE + 2 * i], pos_ref[0, CMP_STRIDE + 2 * i + 1]], axis=1)
        bias = bias + _dot(p_top, wt_ref[0, i]) + _dot(p_bot, wb_ref[0, i])
    nxt = jnp.concatenate([bot[nj:], pltpu.roll(bot[0:nj], nj - 1, 0)], axis=0)
    hid = top + nxt + bias[0:1, :]
    act = (hid * _sigmoid(hid)).astype(bf16)
    for g in range(NSA_KV_HEADS):
        o_ref[0, g] = _dot(act[:, g * CMP_HIDDEN:(g + 1) * CMP_HIDDEN], w2_ref[0]).astype(o_ref.dtype)


def _compress(kvc, pos, wt, wb, w2, batch):
    seq = kvc.shape[0] // batch
    G = NSA_KV_HEADS
    ncmp = CMP_PER_SEL * MAX_SEL_BLOCKS
    whole = lambda arr: pl.BlockSpec((1,) + arr.shape[1:], lambda s, b: (s,) + (0,) * (arr.ndim - 1))
    return pl.pallas_call(
        _compress_kernel,
        grid=(2, batch),
        in_specs=[pl.BlockSpec((seq, W_KV), lambda s, b: (b, s)), whole(pos), whole(wt), whole(wb), whole(w2)],
        out_specs=pl.BlockSpec((1, G, ncmp, HEAD_DIM), lambda s, b: (s, b, 0, 0)),
        out_shape=jax.ShapeDtypeStruct((2, batch * G, ncmp, HEAD_DIM), jnp.bfloat16),
        scratch_shapes=[pltpu.VMEM((seq, W_KV), jnp.float32)],
        compiler_params=pltpu.CompilerParams(dimension_semantics=("arbitrary", "arbitrary"),
                                             vmem_limit_bytes=VMEM_LIMIT),
        name="compress",
    )(kvc, pos, wt, wb, w2)


def _compress_weights(pos, w1, w2):
    bf16 = jnp.bfloat16
    pos2 = jnp.broadcast_to(jnp.tile(pos, (1, 1, 2))[:, :, None, :], (2, CMP_LEN, 16, W_KV)).astype(bf16)
    w = w1.astype(bf16).reshape(2, 2, CMP_STRIDE, HEAD_DIM, CMP_HIDDEN)
    lead = ((0, 0),) * 3
    bd = (jnp.pad(w, lead + ((0, HEAD_DIM), (0, CMP_HIDDEN)))
          + jnp.pad(w, lead + ((HEAD_DIM, 0), (CMP_HIDDEN, 0))))
    bd = bd.reshape(2, 2, CMP_PAIRS, 2 * W_KV, 2 * CMP_HIDDEN)
    return pos2, bd[:, 0], bd[:, 1], w2.astype(bf16)


WIN_BLOCKS = -(-(SWA_WINDOW - 1) // BLOCK) + 1


def _nsa_attn_kernel(q_ref, kc_ref, vc_ref, et_ref, ks_ref, vst_ref, kw_ref, vwt_ref,
                     gate_ref, sza_ref, o_ref, s_a, s_b, p_a, p_b):
    f32, bf16 = jnp.float32, jnp.bfloat16
    qb = pl.program_id(1)
    G, R = NSA_KV_HEADS, NSA_Q_PER_KV
    grows = R * BLOCK
    rows = G * grows
    gcols = lambda g: slice(g * grows, (g + 1) * grows)
    heads_t = [q_ref[0, h * HEAD_DIM:(h + 1) * HEAD_DIM, :] for h in range(NSA_HEADS)]
    q_cmp_t = jnp.concatenate(heads_t, axis=1)
    zero_t = jnp.zeros((HEAD_DIM, BLOCK), bf16)
    q_kv_t = jnp.concatenate([jnp.concatenate([qh, zero_t] if h < R else [zero_t, qh], axis=0)
                              for h, qh in enumerate(heads_t)], axis=1)
    t_1 = qb * BLOCK + lax.broadcasted_iota(jnp.int32, (1, BLOCK), 1)
    head_bias = lambda ok, reps: jnp.tile(jnp.where(ok, 0.0, NEG_INF), (1, reps))

    ncmp = kc_ref.shape[1]
    pos = lax.broadcasted_iota(jnp.int32, (ncmp, BLOCK), 0)
    cmp_end = (pos & (LANES - 1)) * SEL_BLOCK + (pos >> 7) * CMP_STRIDE + (CMP_LEN - 1)
    cmp_bias = head_bias(cmp_end <= t_1, R)
    seen = jnp.tile(jnp.where(t_1 >= CMP_LEN - 1, 1.0, 0.0), (1, R))
    blk = lax.broadcasted_iota(jnp.int32, (LANES, BLOCK), 0)
    o_cmp, imp_sel = [], []
    for g in range(G):
        st = _dot(kc_ref[g], q_cmp_t[:, gcols(g)]) + cmp_bias
        mx = jnp.max(st, axis=0, keepdims=True)
        e = jnp.exp2(st - mx)
        den = jnp.maximum(jnp.sum(e, axis=0, keepdims=True), 1e-30)
        p = e * (seen / den)
        o_cmp.append(lax.dot_general(vc_ref[g], p.astype(bf16), _TN,
                                     preferred_element_type=f32))
        imp = p[:, 0:BLOCK]
        for r in range(1, R):
            imp = imp + p[:, r * BLOCK:(r + 1) * BLOCK]
        q4 = [imp[i * LANES:(i + 1) * LANES] for i in range(CMP_PER_SEL)]
        prev_last = jnp.where(blk == 0, 0.0, pltpu.roll(q4[3], 1, 0))
        imp_sel.append(prev_last + 2.0 * (q4[0] + q4[1] + q4[2]) + q4[3])

    span = WIN_BLOCKS * BLOCK
    first_blk = jnp.maximum(qb - (WIN_BLOCKS - 1), 0)
    win = pl.ds(pl.multiple_of(first_blk * BLOCK, BLOCK), span)
    dist = t_1 - (first_blk * BLOCK + lax.broadcasted_iota(jnp.int32, (span, BLOCK), 0))
    st = _dot(kw_ref[win, :], q_kv_t) + head_bias((dist >= 0) & (dist <= SWA_WINDOW - 1), G * R)
    mw = jnp.max(st, axis=0, keepdims=True)
    e = jnp.exp2(st - mw).astype(bf16)
    ones_win = jnp.ones((ONES_ROWS, span), bf16)
    o_win = []
    for g in range(G):
        r = _dot(jnp.concatenate([vwt_ref[0, g * HEAD_DIM:(g + 1) * HEAD_DIM, win], ones_win], axis=0),
                 e[:, gcols(g)])
        o_win.append(r[0:HEAD_DIM] * (1.0 / r[HEAD_DIM:HEAD_DIM + 1]))

    cur = (qb * BLOCK + lax.broadcasted_iota(jnp.int32, (LANES, BLOCK), 1)) >> 6
    forced = (blk == 0) | (blk == cur) | (blk == cur - 1)
    blk_f = blk.astype(f32)
    bias = []
    for g in range(G):
        sc = jnp.where(forced, -2.0, jnp.where(blk <= cur, imp_sel[g], -1.0))
        for _ in range(SEL_TOP_N - N_FORCED):
            best = jnp.max(sc, axis=0, keepdims=True)
            first = jnp.min(jnp.where(sc == best, blk_f, float(LANES)), axis=0, keepdims=True)
            sc = jnp.where(blk_f == first, -2.0, sc)
        taken = (sc == -2.0) & (blk <= cur)
        bias.append(jnp.tile(jnp.where(taken, 0.0, NEG_INF).astype(bf16), (1, R)))

    qa_t = jnp.concatenate([jnp.concatenate(bias, axis=1), q_kv_t], axis=0)
    n = (qb * BLOCK) // KEY_TILE
    last = jnp.maximum(n - 1, 0)
    ones = jnp.ones((ONES_ROWS, KEY_TILE), bf16)

    def keys_aug(kt):
        tile = pl.ds(pl.multiple_of(kt * KEY_TILE, KEY_TILE), KEY_TILE)
        return jnp.concatenate([et_ref[tile, :], ks_ref[tile, :]], axis=1)

    def qk(kt, s_ref):
        st = _dot(keys_aug(kt), qa_t)
        s_ref[...] = st
        return jnp.max(st, axis=0, keepdims=True)

    def pv(kt, p_ref):
        tile = pl.ds(pl.multiple_of(kt * KEY_TILE, KEY_TILE), KEY_TILE)
        return tuple(_dot(jnp.concatenate([vst_ref[0, g * HEAD_DIM:(g + 1) * HEAD_DIM, tile], ones], axis=0),
                          p_ref[:, gcols(g)]) for g in range(G))

    def softmax(m_old, mx, s_ref, p_ref):
        m_new = jnp.maximum(m_old, mx)
        p_ref[...] = jnp.exp2(s_ref[...] - m_new).astype(bf16)
        return m_new, jnp.exp2(m_old - m_new)

    def accumulate(acc, alpha, weight, contrib):
        return tuple(alpha[:, gcols(g)] * acc[g] + weight * contrib[g] for g in range(G))

    key = n * KEY_TILE + lax.broadcasted_iota(jnp.int32, (KEY_TILE, BLOCK), 0)
    st = _dot(keys_aug(n), qa_t) + head_bias(key <= t_1, G * R)
    m0 = jnp.max(st, axis=0, keepdims=True)
    p_b[...] = jnp.exp2(st - m0).astype(bf16)
    mx0 = qk(0, s_a)
    acc0 = tuple(jnp.zeros((HEAD_DIM + ONES_ROWS, grows), f32) for _ in range(G))
    one = jnp.ones_like(m0)

    def body(i, carry):
        m, acc, alpha_prev, w_prev, kt_prev, mx = carry
        first, second = 2 * i, 2 * i + 1
        w_second = jnp.where(second < n, 1.0, 0.0)
        kt_second = jnp.minimum(second, last)
        acc = accumulate(acc, alpha_prev, w_prev, pv(kt_prev, p_b))
        m, alpha = softmax(m, mx, s_a, p_a)
        mx = qk(kt_second, s_b)
        acc = accumulate(acc, alpha, 1.0, pv(first, p_a))
        m, alpha = softmax(m, mx, s_b, p_b)
        mx = qk(jnp.minimum(second + 1, last), s_a)
        return m, acc, alpha, w_second, kt_second, mx

    init = (m0, acc0, one, jnp.float32(1.0), n, mx0)
    _, acc, alpha_prev, w_prev, kt_prev, _ = lax.fori_loop(0, (n + 1) // 2, body, init)
    acc = accumulate(acc, alpha_prev, w_prev, pv(kt_prev, p_b))
    o_slc = [a[0:HEAD_DIM] * (1.0 / a[HEAD_DIM:HEAD_DIM + 1]) for a in acc]

    for g in range(G):
        for r in range(R):
            h = g * R + r
            cols = slice(r * BLOCK, (r + 1) * BLOCK)
            gate_row = lambda branch: gate_ref[0, branch * NSA_HEADS + h:branch * NSA_HEADS + h + 1, :]
            o_t = (gate_row(0) * o_cmp[g][:, cols] + gate_row(1) * o_slc[g][:, cols]
                   + gate_row(2) * o_win[g][:, cols])
            dims = slice(h * HEAD_DIM, (h + 1) * HEAD_DIM)
            o_ref[0, dims, :] = (o_t * sza_ref[0, dims, :].astype(f32)).astype(o_ref.dtype)


def _nsa_attn(qa, kcp, vcp, onehot, ks, vst, kw, vwt, gate, sza, batch):
    seq = qa.shape[2]
    nqb = seq // BLOCK
    ncmp = kcp.shape[1]
    G = NSA_KV_HEADS
    rows = NSA_HEADS * BLOCK
    t_block = lambda wd: pl.BlockSpec((1, wd, BLOCK), lambda b, j: (b, 0, j))
    return pl.pallas_call(
        _nsa_attn_kernel,
        grid=(batch, nqb),
        in_specs=[t_block(W_NSA),
                  pl.BlockSpec((G, ncmp, HEAD_DIM), lambda b, j: (b, 0, 0)),
                  pl.BlockSpec((G, ncmp, HEAD_DIM), lambda b, j: (b, 0, 0)),
                  pl.BlockSpec((seq, LANES), lambda b, j: (0, 0)),
                  pl.BlockSpec((seq, W_KV), lambda b, j: (b, 0)),
                  pl.BlockSpec((1, W_KV, seq), lambda b, j: (b, 0, 0)),
                  pl.BlockSpec((seq, W_KV), lambda b, j: (b, 0)),
                  pl.BlockSpec((1, W_KV, seq), lambda b, j: (b, 0, 0)),
                  t_block(LANES),
                  t_block(W_NSA)],
        out_specs=t_block(W_NSA),
        out_shape=jax.ShapeDtypeStruct((batch, W_NSA, seq), jnp.bfloat16),
        scratch_shapes=[pltpu.VMEM((KEY_TILE, rows), jnp.float32),
                        pltpu.VMEM((KEY_TILE, rows), jnp.float32),
                        pltpu.VMEM((KEY_TILE, rows), jnp.bfloat16),
                        pltpu.VMEM((KEY_TILE, rows), jnp.bfloat16)],
        compiler_params=pltpu.CompilerParams(dimension_semantics=("arbitrary", "arbitrary"),
                                             vmem_limit_bytes=VMEM_LIMIT),
        name="nsa_attn",
    )(qa, kcp, vcp, onehot, ks, vst, kw, vwt, gate, sza)


DIL_MAX = max(d for _, d in DIL_PATTERNS)
DIL_SUPER = BLOCK * DIL_MAX
DIL_UNITS = DIL_SUPER // BLOCK
DIL_UNROLL = 16
HEAD_PAIR = 2 * HEAD_DIM
MIX_ROWS = 256


def _dil_mix_kernel(q_ref, kp_ref, kc_ref, vp_ref, vc_ref, z_ref, o_ref,
                    qf, kf, vf, num_scr, den_scr, max_scr, bias_scr):
    f32, bf16 = jnp.float32, jnp.bfloat16
    sb = pl.program_id(1)
    qf[...] = q_ref[...].astype(f32)
    kf[0:DIL_SUPER] = kp_ref[...].astype(f32)
    kf[DIL_SUPER:2 * DIL_SUPER] = kc_ref[...].astype(f32)
    vf[0:DIL_SUPER] = vp_ref[...].astype(f32)
    vf[DIL_SUPER:2 * DIL_SUPER] = vc_ref[...].astype(f32)

    row = lax.broadcasted_iota(jnp.int32, (2 * BLOCK, 2 * BLOCK), 0)
    col = lax.broadcasted_iota(jnp.int32, (2 * BLOCK, 2 * BLOCK), 1)
    dist = BLOCK + (row & (BLOCK - 1)) - col
    band = (dist >= 0) & (dist <= BLOCK)
    bias_scr[0] = jnp.where(band, 0.0, NEG_INF)
    bias_scr[1] = jnp.where(band & (col >= BLOCK), 0.0, NEG_INF)
    first_head = lax.broadcasted_iota(jnp.int32, (BLOCK, HEAD_PAIR), 1) < HEAD_DIM
    ones = jnp.ones((2 * BLOCK, HEAD_PAIR), bf16)

    for pat, (window, dil) in enumerate(DIL_PATTERNS):
        shift = dil.bit_length() - 1

        def unit(u, pat=pat, dil=dil, shift=shift):
            cls = u & (dil - 1)
            blk = u >> shift
            q_start = cls + blk * (BLOCK * dil)
            k_start = DIL_SUPER + q_start - BLOCK * dil
            q2 = qf[pl.ds(q_start, BLOCK, stride=dil), :]
            k2 = kf[pl.ds(k_start, 2 * BLOCK, stride=dil), :]
            v2 = vf[pl.ds(k_start, 2 * BLOCK, stride=dil), :]
            qm = jnp.concatenate([jnp.where(first_head, q2, 0.0),
                                  jnp.where(first_head, 0.0, q2)], axis=0).astype(bf16)
            s = _dot_nt(qm, k2.astype(bf16))
            no_prev = jnp.where((sb == 0) & (blk == 0), 1, 0)
            s = s + bias_scr[no_prev]
            m = jnp.max(s, axis=1, keepdims=True)
            e = jnp.exp2(s - m).astype(bf16)
            r = _dot(e, jnp.concatenate([v2.astype(bf16), ones], axis=1))
            mb = jnp.broadcast_to(m, (2 * BLOCK, HEAD_PAIR))
            out_rows = pl.ds(q_start, BLOCK, stride=dil)
            num_scr[pat, out_rows, :] = jnp.where(first_head, r[0:BLOCK, 0:HEAD_PAIR], r[BLOCK:, 0:HEAD_PAIR])
            den_scr[pat, out_rows, :] = jnp.where(first_head, r[0:BLOCK, HEAD_PAIR:], r[BLOCK:, HEAD_PAIR:])
            max_scr[pat, out_rows, :] = jnp.where(first_head, mb[0:BLOCK], mb[BLOCK:])

        def trip(it, carry, unit=unit):
            for j in range(DIL_UNROLL):
                unit(it * DIL_UNROLL + j)
            return carry

        lax.fori_loop(0, DIL_UNITS // DIL_UNROLL, trip, 0)

    def mix(ci, carry):
        rows = pl.ds(pl.multiple_of(ci * MIX_ROWS, MIX_ROWS), MIX_ROWS)
        ms = [max_scr[p, rows, :] for p in range(len(DIL_PATTERNS))]
        mx = jnp.maximum(jnp.maximum(ms[0], ms[1]), ms[2])
        cs = [jnp.exp2(m - mx) for m in ms]
        num = cs[0] * num_scr[0, rows, :] + cs[1] * num_scr[1, rows, :] + cs[2] * num_scr[2, rows, :]
        den = cs[0] * den_scr[0, rows, :] + cs[1] * den_scr[1, rows, :] + cs[2] * den_scr[2, rows, :]
        o_ref[rows, :] = (num / den * z_ref[rows, :].astype(f32)).astype(o_ref.dtype)
        return carry

    lax.fori_loop(0, DIL_SUPER // MIX_ROWS, mix, 0)


def _dil_mix(qb, kb, vb, szb, batch):
    rows = qb.shape[0]
    nsb = rows // batch // DIL_SUPER
    cur = pl.BlockSpec((DIL_SUPER, HEAD_PAIR), lambda b, s, h: (b * nsb + s, h))
    prev = pl.BlockSpec((DIL_SUPER, HEAD_PAIR), lambda b, s, h: (b * nsb + jnp.maximum(s - 1, 0), h))
    f32 = jnp.float32
    return pl.pallas_call(
        _dil_mix_kernel,
        grid=(batch, nsb, W_DIL // HEAD_PAIR),
        in_specs=[cur, prev, cur, prev, cur, cur],
        out_specs=cur,
        out_shape=jax.ShapeDtypeStruct((rows, W_DIL), jnp.bfloat16),
        scratch_shapes=[pltpu.VMEM((DIL_SUPER, HEAD_PAIR), f32),
                        pltpu.VMEM((2 * DIL_SUPER, HEAD_PAIR), f32),
                        pltpu.VMEM((2 * DIL_SUPER, HEAD_PAIR), f32),
                        pltpu.VMEM((len(DIL_PATTERNS), DIL_SUPER, HEAD_PAIR), f32),
                        pltpu.VMEM((len(DIL_PATTERNS), DIL_SUPER, HEAD_PAIR), f32),
                        pltpu.VMEM((len(DIL_PATTERNS), DIL_SUPER, HEAD_PAIR), f32),
                        pltpu.VMEM((2, 2 * BLOCK, 2 * BLOCK), f32)],
        compiler_params=pltpu.CompilerParams(
            dimension_semantics=("arbitrary", "arbitrary", "arbitrary"),
            vmem_limit_bytes=VMEM_LIMIT),
        name="dil_mix",
    )(qb, kb, kb, vb, vb, szb)


def _out_proj_kernel(x_ref, ma_ref, mb_ref, w_ref, g_ref, out_ref):
    y = lax.dot_general(ma_ref[0], w_ref[0:W_NSA, :], _TN, preferred_element_type=jnp.float32)
    y = y + _dot(mb_ref[...], w_ref[W_NSA:W_NSA + W_DIL, :])
    ms = jnp.mean(y * y, axis=-1, keepdims=True)
    out_ref[...] = x_ref[...] + y * lax.rsqrt(ms + RMS_EPS) * g_ref[...]


def _out_proj(x2, mixed_a_t, mixed_b, w, g):
    rows, d_model = x2.shape
    tm = PROJ_ROWS
    per_batch = mixed_a_t.shape[2] // tm
    row_spec = lambda wd: pl.BlockSpec((tm, wd), lambda i: (i, 0))
    return pl.pallas_call(
        _out_proj_kernel,
        grid=(rows // tm,),
        in_specs=[row_spec(d_model),
                  pl.BlockSpec((1, W_NSA, tm), lambda i: (i // per_batch, 0, i % per_batch)),
                  row_spec(W_DIL),
                  pl.BlockSpec(w.shape, lambda i: (0, 0)),
                  pl.BlockSpec((1, d_model), lambda i: (0, 0))],
        out_specs=row_spec(d_model),
        out_shape=jax.ShapeDtypeStruct((rows, d_model), jnp.float32),
        compiler_params=pltpu.CompilerParams(dimension_semantics=("arbitrary",)),
        name="out_proj",
    )(x2, mixed_a_t, mixed_b, w, g)


def _rope_tables(positions):
    inv = 1.0 / (ROPE_THETA ** (jnp.arange(0, ROPE_DIMS, 2, dtype=jnp.float32) / ROPE_DIMS))
    ang = positions.astype(jnp.float32).reshape(-1)[:, None] * inv
    cos, sin = jnp.cos(ang), jnp.sin(ang)
    half = ROPE_DIMS // 2
    k = jnp.arange(LANES) % HEAD_DIM
    freq = jnp.arange(half)[:, None]
    first = (k[None, :] == freq).astype(jnp.float32)
    second = (k[None, :] == freq + half).astype(jnp.float32)
    spread = lambda t, m: jnp.dot(t, m, precision=lax.Precision.HIGHEST)
    unrotated = (k >= ROPE_DIMS).astype(jnp.float32)[None, :]
    return spread(cos, first + second) + unrotated, spread(-sin, first), spread(sin, second)


def _in_proj_weights(w_in):
    scale = HEAD_DIM ** -0.5 * LOG2_E
    bf16 = jnp.bfloat16
    n_head, n_gate = W_NSA + 6 * W_KV, 3 * NSA_HEADS
    ones = lambda n: jnp.ones((n,), jnp.float32)
    head_scale = jnp.concatenate([scale * ones(W_NSA), ones(6 * W_KV)])
    tail_scale = jnp.concatenate([ones(W_NSA), scale * ones(W_DIL), ones(3 * W_DIL)])
    w_head = (w_in[:, :n_head] * head_scale).astype(bf16)
    w_gate = jnp.pad(w_in[:, n_head:n_head + n_gate], ((0, 0), (0, LANES - n_gate))).astype(bf16)
    w_tail = (w_in[:, n_head + n_gate:] * tail_scale).astype(bf16)
    return w_head, w_gate, w_tail


def kernel(x, positions, pre_norm_g, w_in, cmp_k_pos, cmp_k_w1, cmp_k_w2,
           cmp_v_pos, cmp_v_w1, cmp_v_w2, w_out, post_norm_g):
    B, S, d_model = x.shape
    depth = w_in.shape[0]
    n_sel = S // SEL_BLOCK
    assert S % KEY_TILE == 0 and n_sel == MAX_SEL_BLOCKS and S >= WIN_BLOCKS * BLOCK
    assert S % DIL_SUPER == 0 and all(win // dil == BLOCK for win, dil in DIL_PATTERNS)
    bf16 = jnp.bfloat16

    rope_c, rope_a, rope_b = _rope_tables(positions)
    onehot = (jnp.arange(S)[:, None] // SEL_BLOCK == jnp.arange(LANES)[None, :]).astype(bf16)
    x2 = x.reshape(B * S, d_model)

    for layer in range(depth):
        qa, kvc, ks, kw, vst, vwt, gate, sza, qb_, kb_, vb_, szb = _in_proj(
            x2, pre_norm_g[layer][None, :], _in_proj_weights(w_in[layer]), rope_c, rope_a, rope_b, B)

        kvc = _compress(kvc, *_compress_weights(jnp.stack([cmp_k_pos[layer], cmp_v_pos[layer]]),
                                                jnp.stack([cmp_k_w1[layer], cmp_v_w1[layer]]),
                                                jnp.stack([cmp_k_w2[layer], cmp_v_w2[layer]])), B)
        mixed_a = _nsa_attn(qa, kvc[0], kvc[1], onehot, ks, vst, kw, vwt, gate, sza, B)
        mixed_b = _dil_mix(qb_, kb_, vb_, szb, B)
        x2 = _out_proj(x2, mixed_a, mixed_b, w_out[layer].astype(bf16), post_norm_g[layer][None, :])
    return x2.reshape(B, S, d_model)
```

```python
import jax
import jax.numpy as jnp
from jax import lax
from jax.experimental import pallas as pl
from jax.experimental.pallas import tpu as pltpu

HEAD_DIM = 64
NSA_HEADS = 8
NSA_KV_HEADS = 2
NSA_Q_PER_KV = NSA_HEADS // NSA_KV_HEADS
DIL_HEADS = 8
W_NSA = NSA_HEADS * HEAD_DIM
W_KV = NSA_KV_HEADS * HEAD_DIM
W_DIL = DIL_HEADS * HEAD_DIM
CMP_LEN = 32
CMP_STRIDE = 16
CMP_HIDDEN = 256
SEL_BLOCK = 64
SEL_TOP_N = 16
N_FORCED = 3
SWA_WINDOW = 512
DIL_PATTERNS = ((128, 1), (512, 4), (2048, 16))
BLOCK = 128
ROPE_THETA = 500000.0
ROPE_DIMS = HEAD_DIM // 4
RMS_EPS = 1e-6
NEG_INF = -1e30
FORCE_SCORE = 1e4
LOG2_E = 1.4426950408889634

LANES = 128
VMEM_LIMIT = 56 * 1024 * 1024
MAX_SEL_BLOCKS = LANES
CMP_PER_SEL = SEL_BLOCK // CMP_STRIDE
KEY_TILE = 512
PROJ_ROWS = 512
BF16_ROWS = 16
ONES_ROWS = BF16_ROWS
SEL_SHIFT = SEL_BLOCK.bit_length() - 1
LANE_SHIFT = LANES.bit_length() - 1

_NT = (((1,), (1,)), ((), ()))
_TN = (((0,), (0,)), ((), ()))


def _dot(a, b):
    return jnp.dot(a, b, preferred_element_type=jnp.float32)


def _dot_nt(a, b):
    return lax.dot_general(a, b, _NT, preferred_element_type=jnp.float32)


def _sigmoid(x):
    return 1.0 / (1.0 + jnp.exp(-x))


def _rope(x, c, a, b):
    width = x.shape[1]
    reps = width // LANES
    ct = jnp.tile(c, (1, reps))
    at = jnp.tile(a, (1, reps))
    bt = jnp.tile(b, (1, reps))
    half = ROPE_DIMS // 2
    return x * ct + pltpu.roll(x, width - half, 1) * at + pltpu.roll(x, half, 1) * bt


def _in_proj_kernel(x_ref, g_ref, w_head_ref, w_gate_ref, w_tail_ref, c_ref, a_ref, b_ref,
                    qa_ref, kvc_ref, ks_ref, kw_ref, vst_ref, vwt_ref, gate_ref, sza_ref,
                    qb_ref, kb_ref, vb_ref, szb_ref):
    x = x_ref[...]
    ms = jnp.mean(x * x, axis=-1, keepdims=True)
    h = (x * lax.rsqrt(ms + RMS_EPS) * g_ref[...]).astype(jnp.bfloat16)
    c = c_ref[...]
    a = a_ref[...]
    b = b_ref[...]

    def columns_of(w_ref):
        off = 0

        def take(width):
            nonlocal off
            off += width
            return _dot(h, w_ref[:, off - width:off])
        return take

    head, gates, proj = columns_of(w_head_ref), columns_of(w_gate_ref), columns_of(w_tail_ref)
    qa_ref[0] = _rope(head(W_NSA), c, a, b).T.astype(qa_ref.dtype)
    kva = head(6 * W_KV)
    part = lambda i: kva[:, i * W_KV:(i + 1) * W_KV]
    kvc_ref[...] = jnp.concatenate([_rope(part(0), c, a, b), part(1)], axis=1).astype(kvc_ref.dtype)
    ks_ref[...] = _rope(part(2), c, a, b).astype(ks_ref.dtype)
    kw_ref[...] = _rope(part(4), c, a, b).astype(kw_ref.dtype)
    vst_ref[0] = part(3).T.astype(vst_ref.dtype)
    vwt_ref[0] = part(5).T.astype(vwt_ref.dtype)
    gate_ref[0] = _sigmoid(gates(LANES)).T
    za = proj(W_NSA)
    sza_ref[0] = (za * _sigmoid(za)).T.astype(sza_ref.dtype)
    qb_ref[...] = _rope(proj(W_DIL), c, a, b).astype(qb_ref.dtype)
    kb_ref[...] = _rope(proj(W_DIL), c, a, b).astype(kb_ref.dtype)
    vb_ref[...] = proj(W_DIL).astype(vb_ref.dtype)
    zb = proj(W_DIL)
    szb_ref[...] = (zb * _sigmoid(zb)).astype(szb_ref.dtype)


def _in_proj(x2, g, weights, c, a, b, batch):
    rows, d_model = x2.shape
    tm = PROJ_ROWS
    seq = rows // batch
    per_batch = seq // tm
    bf16 = jnp.bfloat16
    row_spec = lambda wd: pl.BlockSpec((tm, wd), lambda i: (i, 0))
    row_out = lambda wd: (row_spec(wd), jax.ShapeDtypeStruct((rows, wd), bf16))
    t_out = lambda wd, dt=bf16: (pl.BlockSpec((1, wd, tm), lambda i: (i // per_batch, 0, i % per_batch)),
                                 jax.ShapeDtypeStruct((batch, wd, seq), dt))
    outs = [t_out(W_NSA), row_out(2 * W_KV), row_out(W_KV), row_out(W_KV), t_out(W_KV), t_out(W_KV),
            t_out(LANES, jnp.float32), t_out(W_NSA),
            row_out(W_DIL), row_out(W_DIL), row_out(W_DIL), row_out(W_DIL)]
    return pl.pallas_call(
        _in_proj_kernel,
        grid=(rows // tm,),
        in_specs=[row_spec(d_model),
                  pl.BlockSpec((1, d_model), lambda i: (0, 0))]
                 + [pl.BlockSpec(w.shape, lambda i: (0, 0)) for w in weights]
                 + [row_spec(LANES), row_spec(LANES), row_spec(LANES)],
        out_specs=[o[0] for o in outs],
        out_shape=[o[1] for o in outs],
        compiler_params=pltpu.CompilerParams(dimension_semantics=("arbitrary",),
                                             vmem_limit_bytes=VMEM_LIMIT),
        name="in_proj",
    )(x2, g, *weights, c, a, b)


CMP_PAIRS = CMP_STRIDE // 2


def _compress_kernel(x_ref, pos_ref, wt_ref, wb_ref, w2_ref, o_ref, stage):
    f32, bf16 = jnp.float32, jnp.bfloat16
    stage[...] = x_ref[...].astype(f32)
    nj = MAX_SEL_BLOCKS
    hidden = 2 * CMP_HIDDEN

    def offset_rows(l):
        return jnp.concatenate([stage[pl.ds(CMP_STRIDE * m + l, nj, stride=SEL_BLOCK), :]
                                for m in range(CMP_PER_SEL)], axis=0)

    top = jnp.zeros((CMP_PER_SEL * nj, hidden), f32)
    bot = jnp.zeros((CMP_PER_SEL * nj, hidden), f32)
    bias = jnp.zeros((BF16_ROWS, hidden), f32)
    for i in range(CMP_PAIRS):
        x = jnp.concatenate([offset_rows(2 * i), offset_rows(2 * i + 1)], axis=1).astype(bf16)
        top = top + _dot(x, wt_ref[0, i])
        bot = bot + _dot(x, wb_ref[0, i])
        p_top = jnp.concatenate([pos_ref[0, 2 * i], pos_ref[0, 2 * i + 1]], axis=1)
        p_bot = jnp.concatenate([pos_ref[0, CMP_STRIDE + 2 * i], pos_ref[0, CMP_STRIDE + 2 * i + 1]], axis=1)
        bias = bias + _dot(p_top, wt_ref[0, i]) + _dot(p_bot, wb_ref[0, i])
    nxt = jnp.concatenate([bot[nj:], pltpu.roll(bot[0:nj], nj - 1, 0)], axis=0)
    hid = top + nxt + bias[0:1, :]
    act = (hid * _sigmoid(hid)).astype(bf16)
    for g in range(NSA_KV_HEADS):
        o_ref[0, g] = _dot(act[:, g * CMP_HIDDEN:(g + 1) * CMP_HIDDEN], w2_ref[0]).astype(o_ref.dtype)


def _compress(kvc, pos, wt, wb, w2, batch):
    seq = kvc.shape[0] // batch
    G = NSA_KV_HEADS
    ncmp = CMP_PER_SEL * MAX_SEL_BLOCKS
    whole = lambda arr: pl.BlockSpec((1,) + arr.shape[1:], lambda s, b: (s,) + (0,) * (arr.ndim - 1))
    return pl.pallas_call(
        _compress_kernel,
        grid=(2, batch),
        in_specs=[pl.BlockSpec((seq, W_KV), lambda s, b: (b, s)), whole(pos), whole(wt), whole(wb), whole(w2)],
        out_specs=pl.BlockSpec((1, G, ncmp, HEAD_DIM), lambda s, b: (s, b, 0, 0)),
        out_shape=jax.ShapeDtypeStruct((2, batch * G, ncmp, HEAD_DIM), jnp.bfloat16),
        scratch_shapes=[pltpu.VMEM((seq, W_KV), jnp.float32)],
        compiler_params=pltpu.CompilerParams(dimension_semantics=("arbitrary", "arbitrary"),
                                             vmem_limit_bytes=VMEM_LIMIT),
        name="compress",
    )(kvc, pos, wt, wb, w2)


def _compress_weights(pos, w1, w2):
    bf16 = jnp.bfloat16
    pos2 = jnp.broadcast_to(jnp.tile(pos, (1, 1, 2))[:, :, None, :], (2, CMP_LEN, BF16_ROWS, W_KV)).astype(bf16)
    w = w1.astype(bf16).reshape(2, 2, CMP_STRIDE, HEAD_DIM, CMP_HIDDEN)
    lead = ((0, 0),) * 3
    bd = (jnp.pad(w, lead + ((0, HEAD_DIM), (0, CMP_HIDDEN)))
          + jnp.pad(w, lead + ((HEAD_DIM, 0), (CMP_HIDDEN, 0))))
    bd = bd.reshape(2, 2, CMP_PAIRS, 2 * W_KV, 2 * CMP_HIDDEN)
    return pos2, bd[:, 0], bd[:, 1], w2.astype(bf16)


WIN_BLOCKS = -(-(SWA_WINDOW - 1) // BLOCK) + 1


def _nsa_attn_kernel(q_ref, kc_ref, vc_ref, et_ref, ks_ref, vst_ref, kw_ref, vwt_ref,
                     gate_ref, sza_ref, o_ref, s_a, s_b, p_a, p_b):
    f32, bf16 = jnp.float32, jnp.bfloat16
    qb = pl.program_id(1)
    G, R = NSA_KV_HEADS, NSA_Q_PER_KV
    grows = R * BLOCK
    rows = G * grows
    gcols = lambda g: slice(g * grows, (g + 1) * grows)
    heads_t = [q_ref[0, h * HEAD_DIM:(h + 1) * HEAD_DIM, :] for h in range(NSA_HEADS)]
    q_cmp_t = jnp.concatenate(heads_t, axis=1)
    zero_t = jnp.zeros((HEAD_DIM, BLOCK), bf16)
    q_kv_t = jnp.concatenate([jnp.concatenate([qh, zero_t] if h < R else [zero_t, qh], axis=0)
                              for h, qh in enumerate(heads_t)], axis=1)
    t_1 = qb * BLOCK + lax.broadcasted_iota(jnp.int32, (1, BLOCK), 1)
    head_bias = lambda ok, reps: jnp.tile(jnp.where(ok, 0.0, NEG_INF), (1, reps))

    ncmp = kc_ref.shape[1]
    pos = lax.broadcasted_iota(jnp.int32, (ncmp, BLOCK), 0)
    cmp_end = (pos & (LANES - 1)) * SEL_BLOCK + (pos >> LANE_SHIFT) * CMP_STRIDE + (CMP_LEN - 1)
    cmp_bias = head_bias(cmp_end <= t_1, R)
    seen = jnp.tile(jnp.where(t_1 >= CMP_LEN - 1, 1.0, 0.0), (1, R))
    blk = lax.broadcasted_iota(jnp.int32, (LANES, BLOCK), 0)
    o_cmp, imp_sel = [], []
    for g in range(G):
        st = _dot(kc_ref[g], q_cmp_t[:, gcols(g)]) + cmp_bias
        mx = jnp.max(st, axis=0, keepdims=True)
        e = jnp.exp2(st - mx)
        den = jnp.maximum(jnp.sum(e, axis=0, keepdims=True), 1e-30)
        p = e * (seen / den)
        o_cmp.append(lax.dot_general(vc_ref[g], p.astype(bf16), _TN,
                                     preferred_element_type=f32))
        imp = p[:, 0:BLOCK]
        for r in range(1, R):
            imp = imp + p[:, r * BLOCK:(r + 1) * BLOCK]
        q4 = [imp[i * LANES:(i + 1) * LANES] for i in range(CMP_PER_SEL)]
        prev_last = jnp.where(blk == 0, 0.0, pltpu.roll(q4[3], 1, 0))
        imp_sel.append(prev_last + 2.0 * (q4[0] + q4[1] + q4[2]) + q4[3])

    span = WIN_BLOCKS * BLOCK
    first_blk = jnp.maximum(qb - (WIN_BLOCKS - 1), 0)
    win = pl.ds(pl.multiple_of(first_blk * BLOCK, BLOCK), span)
    dist = t_1 - (first_blk * BLOCK + lax.broadcasted_iota(jnp.int32, (span, BLOCK), 0))
    st = _dot(kw_ref[win, :], q_kv_t) + head_bias((dist >= 0) & (dist <= SWA_WINDOW - 1), G * R)
    mw = jnp.max(st, axis=0, keepdims=True)
    e = jnp.exp2(st - mw).astype(bf16)
    ones_win = jnp.ones((ONES_ROWS, span), bf16)
    o_win = []
    for g in range(G):
        r = _dot(jnp.concatenate([vwt_ref[0, g * HEAD_DIM:(g + 1) * HEAD_DIM, win], ones_win], axis=0),
                 e[:, gcols(g)])
        o_win.append(r[0:HEAD_DIM] * (1.0 / r[HEAD_DIM:HEAD_DIM + 1]))

    cur = (qb * BLOCK + lax.broadcasted_iota(jnp.int32, (LANES, BLOCK), 1)) >> SEL_SHIFT
    forced = (blk == 0) | (blk == cur) | (blk == cur - 1)
    blk_f = blk.astype(f32)
    bias = []
    for g in range(G):
        sc = jnp.where(forced, -2.0, jnp.where(blk <= cur, imp_sel[g], -1.0))
        for _ in range(SEL_TOP_N - N_FORCED):
            best = jnp.max(sc, axis=0, keepdims=True)
            first = jnp.min(jnp.where(sc == best, blk_f, float(LANES)), axis=0, keepdims=True)
            sc = jnp.where(blk_f == first, -2.0, sc)
        taken = (sc == -2.0) & (blk <= cur)
        bias.append(jnp.tile(jnp.where(taken, 0.0, NEG_INF).astype(bf16), (1, R)))

    qa_t = jnp.concatenate([jnp.concatenate(bias, axis=1), q_kv_t], axis=0)
    n = (qb * BLOCK) // KEY_TILE
    last = jnp.maximum(n - 1, 0)
    ones = jnp.ones((ONES_ROWS, KEY_TILE), bf16)

    def keys_aug(kt):
        tile = pl.ds(pl.multiple_of(kt * KEY_TILE, KEY_TILE), KEY_TILE)
        return jnp.concatenate([et_ref[tile, :], ks_ref[tile, :]], axis=1)

    def qk(kt, s_ref):
        st = _dot(keys_aug(kt), qa_t)
        s_ref[...] = st
        return jnp.max(st, axis=0, keepdims=True)

    def pv(kt, p_ref):
        tile = pl.ds(pl.multiple_of(kt * KEY_TILE, KEY_TILE), KEY_TILE)
        return tuple(_dot(jnp.concatenate([vst_ref[0, g * HEAD_DIM:(g + 1) * HEAD_DIM, tile], ones], axis=0),
                          p_ref[:, gcols(g)]) for g in range(G))

    def softmax(m_old, mx, s_ref, p_ref):
        m_new = jnp.maximum(m_old, mx)
        p_ref[...] = jnp.exp2(s_ref[...] - m_new).astype(bf16)
        return m_new, jnp.exp2(m_old - m_new)

    def accumulate(acc, alpha, weight, contrib):
        return tuple(alpha[:, gcols(g)] * acc[g] + weight * contrib[g] for g in range(G))

    key = n * KEY_TILE + lax.broadcasted_iota(jnp.int32, (KEY_TILE, BLOCK), 0)
    st = _dot(keys_aug(n), qa_t) + head_bias(key <= t_1, G * R)
    m0 = jnp.max(st, axis=0, keepdims=True)
    p_b[...] = jnp.exp2(st - m0).astype(bf16)
    mx0 = qk(0, s_a)
    acc0 = tuple(jnp.zeros((HEAD_DIM + ONES_ROWS, grows), f32) for _ in range(G))
    one = jnp.ones_like(m0)

    def body(i, carry):
        m, acc, alpha_prev, w_prev, kt_prev, mx = carry
        first, second = 2 * i, 2 * i + 1
        w_second = jnp.where(second < n, 1.0, 0.0)
        kt_second = jnp.minimum(second, last)
        acc = accumulate(acc, alpha_prev, w_prev, pv(kt_prev, p_b))
        m, alpha = softmax(m, mx, s_a, p_a)
        mx = qk(kt_second, s_b)
        acc = accumulate(acc, alpha, 1.0, pv(first, p_a))
        m, alpha = softmax(m, mx, s_b, p_b)
        mx = qk(jnp.minimum(second + 1, last), s_a)
        return m, acc, alpha, w_second, kt_second, mx

    init = (m0, acc0, one, jnp.float32(1.0), n, mx0)
    _, acc, alpha_prev, w_prev, kt_prev, _ = lax.fori_loop(0, (n + 1) // 2, body, init)
    acc = accumulate(acc, alpha_prev, w_prev, pv(kt_prev, p_b))
    o_slc = [a[0:HEAD_DIM] * (1.0 / a[HEAD_DIM:HEAD_DIM + 1]) for a in acc]

    for g in range(G):
        for r in range(R):
            h = g * R + r
            cols = slice(r * BLOCK, (r + 1) * BLOCK)
            gate_row = lambda branch: gate_ref[0, branch * NSA_HEADS + h:branch * NSA_HEADS + h + 1, :]
            o_t = (gate_row(0) * o_cmp[g][:, cols] + gate_row(1) * o_slc[g][:, cols]
                   + gate_row(2) * o_win[g][:, cols])
            dims = slice(h * HEAD_DIM, (h + 1) * HEAD_DIM)
            o_ref[0, dims, :] = (o_t * sza_ref[0, dims, :].astype(f32)).astype(o_ref.dtype)


def _nsa_attn(qa, kcp, vcp, onehot, ks, vst, kw, vwt, gate, sza, batch):
    seq = qa.shape[2]
    nqb = seq // BLOCK
    ncmp = kcp.shape[1]
    G = NSA_KV_HEADS
    rows = NSA_HEADS * BLOCK
    t_block = lambda wd: pl.BlockSpec((1, wd, BLOCK), lambda b, j: (b, 0, j))
    return pl.pallas_call(
        _nsa_attn_kernel,
        grid=(batch, nqb),
        in_specs=[t_block(W_NSA),
                  pl.BlockSpec((G, ncmp, HEAD_DIM), lambda b, j: (b, 0, 0)),
                  pl.BlockSpec((G, ncmp, HEAD_DIM), lambda b, j: (b, 0, 0)),
                  pl.BlockSpec((seq, LANES), lambda b, j: (0, 0)),
                  pl.BlockSpec((seq, W_KV), lambda b, j: (b, 0)),
                  pl.BlockSpec((1, W_KV, seq), lambda b, j: (b, 0, 0)),
                  pl.BlockSpec((seq, W_KV), lambda b, j: (b, 0)),
                  pl.BlockSpec((1, W_KV, seq), lambda b, j: (b, 0, 0)),
                  t_block(LANES),
                  t_block(W_NSA)],
        out_specs=t_block(W_NSA),
        out_shape=jax.ShapeDtypeStruct((batch, W_NSA, seq), jnp.bfloat16),
        scratch_shapes=[pltpu.VMEM((KEY_TILE, rows), jnp.float32),
                        pltpu.VMEM((KEY_TILE, rows), jnp.float32),
                        pltpu.VMEM((KEY_TILE, rows), jnp.bfloat16),
                        pltpu.VMEM((KEY_TILE, rows), jnp.bfloat16)],
        compiler_params=pltpu.CompilerParams(dimension_semantics=("arbitrary", "arbitrary"),
                                             vmem_limit_bytes=VMEM_LIMIT),
        name="nsa_attn",
    )(qa, kcp, vcp, onehot, ks, vst, kw, vwt, gate, sza)


DIL_MAX = max(d for _, d in DIL_PATTERNS)
DIL_SUPER = BLOCK * DIL_MAX
DIL_UNITS = DIL_SUPER // BLOCK
DIL_UNROLL = 16
HEAD_PAIR = 2 * HEAD_DIM
MIX_ROWS = 256


def _dil_mix_kernel(q_ref, kp_ref, kc_ref, vp_ref, vc_ref, z_ref, o_ref,
                    qf, kf, vf, num_scr, den_scr, max_scr, bias_scr):
    f32, bf16 = jnp.float32, jnp.bfloat16
    sb = pl.program_id(1)
    qf[...] = q_ref[...].astype(f32)
    kf[0:DIL_SUPER] = kp_ref[...].astype(f32)
    kf[DIL_SUPER:2 * DIL_SUPER] = kc_ref[...].astype(f32)
    vf[0:DIL_SUPER] = vp_ref[...].astype(f32)
    vf[DIL_SUPER:2 * DIL_SUPER] = vc_ref[...].astype(f32)

    row = lax.broadcasted_iota(jnp.int32, (2 * BLOCK, 2 * BLOCK), 0)
    col = lax.broadcasted_iota(jnp.int32, (2 * BLOCK, 2 * BLOCK), 1)
    dist = BLOCK + (row & (BLOCK - 1)) - col
    band = (dist >= 0) & (dist <= BLOCK)
    bias_scr[0] = jnp.where(band, 0.0, NEG_INF)
    bias_scr[1] = jnp.where(band & (col >= BLOCK), 0.0, NEG_INF)
    first_head = lax.broadcasted_iota(jnp.int32, (BLOCK, HEAD_PAIR), 1) < HEAD_DIM
    ones = jnp.ones((2 * BLOCK, HEAD_PAIR), bf16)

    for pat, (window, dil) in enumerate(DIL_PATTERNS):
        shift = dil.bit_length() - 1

        def unit(u, pat=pat, dil=dil, shift=shift):
            cls = u & (dil - 1)
            blk = u >> shift
            q_start = cls + blk * (BLOCK * dil)
            k_start = DIL_SUPER + q_start - BLOCK * dil
            q2 = qf[pl.ds(q_start, BLOCK, stride=dil), :]
            k2 = kf[pl.ds(k_start, 2 * BLOCK, stride=dil), :]
            v2 = vf[pl.ds(k_start, 2 * BLOCK, stride=dil), :]
            qm = jnp.concatenate([jnp.where(first_head, q2, 0.0),
                                  jnp.where(first_head, 0.0, q2)], axis=0).astype(bf16)
            s = _dot_nt(qm, k2.astype(bf16))
            no_prev = jnp.where((sb == 0) & (blk == 0), 1, 0)
            s = s + bias_scr[no_prev]
            m = jnp.max(s, axis=1, keepdims=True)
            e = jnp.exp2(s - m).astype(bf16)
            r = _dot(e, jnp.concatenate([v2.astype(bf16), ones], axis=1))
            mb = jnp.broadcast_to(m, (2 * BLOCK, HEAD_PAIR))
            out_rows = pl.ds(q_start, BLOCK, stride=dil)
            num_scr[pat, out_rows, :] = jnp.where(first_head, r[0:BLOCK, 0:HEAD_PAIR], r[BLOCK:, 0:HEAD_PAIR])
            den_scr[pat, out_rows, :] = jnp.where(first_head, r[0:BLOCK, HEAD_PAIR:], r[BLOCK:, HEAD_PAIR:])
            max_scr[pat, out_rows, :] = jnp.where(first_head, mb[0:BLOCK], mb[BLOCK:])

        def trip(it, carry, unit=unit):
            for j in range(DIL_UNROLL):
                unit(it * DIL_UNROLL + j)
            return carry

        lax.fori_loop(0, DIL_UNITS // DIL_UNROLL, trip, 0)

    def mix(ci, carry):
        rows = pl.ds(pl.multiple_of(ci * MIX_ROWS, MIX_ROWS), MIX_ROWS)
        ms = [max_scr[p, rows, :] for p in range(len(DIL_PATTERNS))]
        mx = jnp.maximum(jnp.maximum(ms[0], ms[1]), ms[2])
        cs = [jnp.exp2(m - mx) for m in ms]
        num = cs[0] * num_scr[0, rows, :] + cs[1] * num_scr[1, rows, :] + cs[2] * num_scr[2, rows, :]
        den = cs[0] * den_scr[0, rows, :] + cs[1] * den_scr[1, rows, :] + cs[2] * den_scr[2, rows, :]
        o_ref[rows, :] = (num / den * z_ref[rows, :].astype(f32)).astype(o_ref.dtype)
        return carry

    lax.fori_loop(0, DIL_SUPER // MIX_ROWS, mix, 0)


def _dil_mix(qb, kb, vb, szb, batch):
    rows = qb.shape[0]
    nsb = rows // batch // DIL_SUPER
    cur = pl.BlockSpec((DIL_SUPER, HEAD_PAIR), lambda b, s, h: (b * nsb + s, h))
    prev = pl.BlockSpec((DIL_SUPER, HEAD_PAIR), lambda b, s, h: (b * nsb + jnp.maximum(s - 1, 0), h))
    f32 = jnp.float32
    return pl.pallas_call(
        _dil_mix_kernel,
        grid=(batch, nsb, W_DIL // HEAD_PAIR),
        in_specs=[cur, prev, cur, prev, cur, cur],
        out_specs=cur,
        out_shape=jax.ShapeDtypeStruct((rows, W_DIL), jnp.bfloat16),
        scratch_shapes=[pltpu.VMEM((DIL_SUPER, HEAD_PAIR), f32),
                        pltpu.VMEM((2 * DIL_SUPER, HEAD_PAIR), f32),
                        pltpu.VMEM((2 * DIL_SUPER, HEAD_PAIR), f32),
                        pltpu.VMEM((len(DIL_PATTERNS), DIL_SUPER, HEAD_PAIR), f32),
                        pltpu.VMEM((len(DIL_PATTERNS), DIL_SUPER, HEAD_PAIR), f32),
                        pltpu.VMEM((len(DIL_PATTERNS), DIL_SUPER, HEAD_PAIR), f32),
                        pltpu.VMEM((2, 2 * BLOCK, 2 * BLOCK), f32)],
        compiler_params=pltpu.CompilerParams(
            dimension_semantics=("arbitrary", "arbitrary", "arbitrary"),
            vmem_limit_bytes=VMEM_LIMIT),
        name="dil_mix",
    )(qb, kb, kb, vb, vb, szb)


def _out_proj_kernel(x_ref, ma_ref, mb_ref, w_ref, g_ref, out_ref):
    y = lax.dot_general(ma_ref[0], w_ref[0:W_NSA, :], _TN, preferred_element_type=jnp.float32)
    y = y + _dot(mb_ref[...], w_ref[W_NSA:W_NSA + W_DIL, :])
    ms = jnp.mean(y * y, axis=-1, keepdims=True)
    out_ref[...] = x_ref[...] + y * lax.rsqrt(ms + RMS_EPS) * g_ref[...]


def _out_proj(x2, mixed_a_t, mixed_b, w, g):
    rows, d_model = x2.shape
    tm = PROJ_ROWS
    per_batch = mixed_a_t.shape[2] // tm
    row_spec = lambda wd: pl.BlockSpec((tm, wd), lambda i: (i, 0))
    return pl.pallas_call(
        _out_proj_kernel,
        grid=(rows // tm,),
        in_specs=[row_spec(d_model),
                  pl.BlockSpec((1, W_NSA, tm), lambda i: (i // per_batch, 0, i % per_batch)),
                  row_spec(W_DIL),
                  pl.BlockSpec(w.shape, lambda i: (0, 0)),
                  pl.BlockSpec((1, d_model), lambda i: (0, 0))],
        out_specs=row_spec(d_model),
        out_shape=jax.ShapeDtypeStruct((rows, d_model), jnp.float32),
        compiler_params=pltpu.CompilerParams(dimension_semantics=("arbitrary",)),
        name="out_proj",
    )(x2, mixed_a_t, mixed_b, w, g)


def _rope_tables(positions):
    inv = 1.0 / (ROPE_THETA ** (jnp.arange(0, ROPE_DIMS, 2, dtype=jnp.float32) / ROPE_DIMS))
    ang = positions.astype(jnp.float32).reshape(-1)[:, None] * inv
    cos, sin = jnp.cos(ang), jnp.sin(ang)
    half = ROPE_DIMS // 2
    k = jnp.arange(LANES) % HEAD_DIM
    freq = jnp.arange(half)[:, None]
    first = (k[None, :] == freq).astype(jnp.float32)
    second = (k[None, :] == freq + half).astype(jnp.float32)
    spread = lambda t, m: jnp.dot(t, m, precision=lax.Precision.HIGHEST)
    unrotated = (k >= ROPE_DIMS).astype(jnp.float32)[None, :]
    return spread(cos, first + second) + unrotated, spread(-sin, first), spread(sin, second)


def _in_proj_weights(w_in):
    scale = HEAD_DIM ** -0.5 * LOG2_E
    bf16 = jnp.bfloat16
    n_head, n_gate = W_NSA + 6 * W_KV, 3 * NSA_HEADS
    ones = lambda n: jnp.ones((n,), jnp.float32)
    head_scale = jnp.concatenate([scale * ones(W_NSA), ones(6 * W_KV)])
    tail_scale = jnp.concatenate([ones(W_NSA), scale * ones(W_DIL), ones(3 * W_DIL)])
    w_head = (w_in[:, :n_head] * head_scale).astype(bf16)
    w_gate = jnp.pad(w_in[:, n_head:n_head + n_gate], ((0, 0), (0, LANES - n_gate))).astype(bf16)
    w_tail = (w_in[:, n_head + n_gate:] * tail_scale).astype(bf16)
    return w_head, w_gate, w_tail


def kernel(x, positions, pre_norm_g, w_in, cmp_k_pos, cmp_k_w1, cmp_k_w2,
           cmp_v_pos, cmp_v_w1, cmp_v_w2, w_out, post_norm_g):
    B, S, d_model = x.shape
    depth = w_in.shape[0]
    n_sel = S // SEL_BLOCK
    assert S % KEY_TILE == 0 and n_sel == MAX_SEL_BLOCKS and S >= WIN_BLOCKS * BLOCK
    assert S % DIL_SUPER == 0 and all(win // dil == BLOCK for win, dil in DIL_PATTERNS)
    bf16 = jnp.bfloat16

    rope_c, rope_a, rope_b = _rope_tables(positions)
    onehot = (jnp.arange(S)[:, None] // SEL_BLOCK == jnp.arange(LANES)[None, :]).astype(bf16)
    x2 = x.reshape(B * S, d_model)

    for layer in range(depth):
        qa, kvc, ks, kw, vst, vwt, gate, sza, qb_, kb_, vb_, szb = _in_proj(
            x2, pre_norm_g[layer][None, :], _in_proj_weights(w_in[layer]), rope_c, rope_a, rope_b, B)

        kvc = _compress(kvc, *_compress_weights(jnp.stack([cmp_k_pos[layer], cmp_v_pos[layer]]),
                                                jnp.stack([cmp_k_w1[layer], cmp_v_w1[layer]]),
                                                jnp.stack([cmp_k_w2[layer], cmp_v_w2[layer]])), B)
        mixed_a = _nsa_attn(qa, kvc[0], kvc[1], onehot, ks, vst, kw, vwt, gate, sza, B)
        mixed_b = _dil_mix(qb_, kb_, vb_, szb, B)
        x2 = _out_proj(x2, mixed_a, mixed_b, w_out[layer].astype(bf16), post_norm_g[layer][None, :])
    return x2.reshape(B, S, d_model)
```

```python
import jax
import jax.numpy as jnp
from jax import lax
from jax.experimental import pallas as pl
from jax.experimental.pallas import tpu as pltpu

HEAD_DIM = 64
NSA_HEADS = 8
NSA_KV_HEADS = 2
NSA_Q_PER_KV = NSA_HEADS // NSA_KV_HEADS
DIL_HEADS = 8
W_NSA = NSA_HEADS * HEAD_DIM
W_KV = NSA_KV_HEADS * HEAD_DIM
W_DIL = DIL_HEADS * HEAD_DIM
CMP_LEN = 32
CMP_STRIDE = 16
CMP_HIDDEN = 256
SEL_BLOCK = 64
SEL_TOP_N = 16
N_FORCED = 3
SWA_WINDOW = 512
DIL_PATTERNS = ((128, 1), (512, 4), (2048, 16))
BLOCK = 128
ROPE_THETA = 500000.0
ROPE_DIMS = HEAD_DIM // 4
RMS_EPS = 1e-6
NEG_INF = -1e30
FORCE_SCORE = 1e4
LOG2_E = 1.4426950408889634

LANES = 128
VMEM_LIMIT = 56 * 1024 * 1024
MAX_SEL_BLOCKS = LANES
CMP_PER_SEL = SEL_BLOCK // CMP_STRIDE
KEY_TILE = 512
PROJ_ROWS = 512
OUT_ROWS = 1024
BF16_ROWS = 16
ONES_ROWS = BF16_ROWS
SEL_SHIFT = SEL_BLOCK.bit_length() - 1
LANE_SHIFT = LANES.bit_length() - 1

_NT = (((1,), (1,)), ((), ()))
_TN = (((0,), (0,)), ((), ()))


def _dot(a, b):
    return jnp.dot(a, b, preferred_element_type=jnp.float32)


def _dot_nt(a, b):
    return lax.dot_general(a, b, _NT, preferred_element_type=jnp.float32)


def _sigmoid(x):
    return 1.0 / (1.0 + jnp.exp(-x))


def _rope(x, c, a, b):
    width = x.shape[1]
    reps = width // LANES
    ct = jnp.tile(c, (1, reps))
    at = jnp.tile(a, (1, reps))
    bt = jnp.tile(b, (1, reps))
    half = ROPE_DIMS // 2
    return x * ct + pltpu.roll(x, width - half, 1) * at + pltpu.roll(x, half, 1) * bt


def _in_proj_kernel(x_ref, g_ref, w_head_ref, w_gate_ref, w_tail_ref, c_ref, a_ref, b_ref,
                    qa_ref, kvc_ref, ks_ref, kw_ref, vst_ref, vwt_ref, gate_ref, sza_ref,
                    qb_ref, kb_ref, vb_ref, szb_ref):
    x = x_ref[...]
    ms = jnp.mean(x * x, axis=-1, keepdims=True)
    h = (x * lax.rsqrt(ms + RMS_EPS) * g_ref[...]).astype(jnp.bfloat16)
    c = c_ref[...]
    a = a_ref[...]
    b = b_ref[...]

    def columns_of(w_ref):
        off = 0

        def take(width):
            nonlocal off
            off += width
            return _dot(h, w_ref[:, off - width:off])
        return take

    head, gates, proj = columns_of(w_head_ref), columns_of(w_gate_ref), columns_of(w_tail_ref)
    qa_ref[0] = _rope(head(W_NSA), c, a, b).T.astype(qa_ref.dtype)
    kva = head(6 * W_KV)
    part = lambda i: kva[:, i * W_KV:(i + 1) * W_KV]
    kvc_ref[...] = jnp.concatenate([_rope(part(0), c, a, b), part(1)], axis=1).astype(kvc_ref.dtype)
    ks_ref[...] = _rope(part(2), c, a, b).astype(ks_ref.dtype)
    kw_ref[...] = _rope(part(4), c, a, b).astype(kw_ref.dtype)
    vst_ref[0] = part(3).T.astype(vst_ref.dtype)
    vwt_ref[0] = part(5).T.astype(vwt_ref.dtype)
    gate_ref[0] = _sigmoid(gates(LANES)).T
    za = proj(W_NSA)
    sza_ref[0] = (za * _sigmoid(za)).T.astype(sza_ref.dtype)
    qb_ref[...] = _rope(proj(W_DIL), c, a, b).astype(qb_ref.dtype)
    kb_ref[...] = _rope(proj(W_DIL), c, a, b).astype(kb_ref.dtype)
    vb_ref[...] = proj(W_DIL).astype(vb_ref.dtype)
    zb = proj(W_DIL)
    szb_ref[...] = (zb * _sigmoid(zb)).astype(szb_ref.dtype)


def _in_proj(x2, g, weights, c, a, b, batch):
    rows, d_model = x2.shape
    tm = PROJ_ROWS
    seq = rows // batch
    per_batch = seq // tm
    bf16 = jnp.bfloat16
    row_spec = lambda wd: pl.BlockSpec((tm, wd), lambda i: (i, 0))
    row_out = lambda wd: (row_spec(wd), jax.ShapeDtypeStruct((rows, wd), bf16))
    t_out = lambda wd, dt=bf16: (pl.BlockSpec((1, wd, tm), lambda i: (i // per_batch, 0, i % per_batch)),
                                 jax.ShapeDtypeStruct((batch, wd, seq), dt))
    outs = [t_out(W_NSA), row_out(2 * W_KV), row_out(W_KV), row_out(W_KV), t_out(W_KV), t_out(W_KV),
            t_out(LANES, jnp.float32), t_out(W_NSA),
            row_out(W_DIL), row_out(W_DIL), row_out(W_DIL), row_out(W_DIL)]
    return pl.pallas_call(
        _in_proj_kernel,
        grid=(rows // tm,),
        in_specs=[row_spec(d_model),
                  pl.BlockSpec((1, d_model), lambda i: (0, 0))]
                 + [pl.BlockSpec(w.shape, lambda i: (0, 0)) for w in weights]
                 + [row_spec(LANES), row_spec(LANES), row_spec(LANES)],
        out_specs=[o[0] for o in outs],
        out_shape=[o[1] for o in outs],
        compiler_params=pltpu.CompilerParams(dimension_semantics=("arbitrary",),
                                             vmem_limit_bytes=VMEM_LIMIT),
        name="in_proj",
    )(x2, g, *weights, c, a, b)


CMP_PAIRS = CMP_STRIDE // 2


def _compress_kernel(x_ref, pos_ref, wt_ref, wb_ref, w2_ref, o_ref, stage):
    f32, bf16 = jnp.float32, jnp.bfloat16
    stage[...] = x_ref[...].astype(f32)
    nj = MAX_SEL_BLOCKS
    hidden = 2 * CMP_HIDDEN

    def offset_rows(l):
        return jnp.concatenate([stage[pl.ds(CMP_STRIDE * m + l, nj, stride=SEL_BLOCK), :]
                                for m in range(CMP_PER_SEL)], axis=0)

    top = jnp.zeros((CMP_PER_SEL * nj, hidden), f32)
    bot = jnp.zeros((CMP_PER_SEL * nj, hidden), f32)
    bias = jnp.zeros((BF16_ROWS, hidden), f32)
    for i in range(CMP_PAIRS):
        x = jnp.concatenate([offset_rows(2 * i), offset_rows(2 * i + 1)], axis=1).astype(bf16)
        top = top + _dot(x, wt_ref[0, i])
        bot = bot + _dot(x, wb_ref[0, i])
        p_top = jnp.concatenate([pos_ref[0, 2 * i], pos_ref[0, 2 * i + 1]], axis=1)
        p_bot = jnp.concatenate([pos_ref[0, CMP_STRIDE + 2 * i], pos_ref[0, CMP_STRIDE + 2 * i + 1]], axis=1)
        bias = bias + _dot(p_top, wt_ref[0, i]) + _dot(p_bot, wb_ref[0, i])
    nxt = jnp.concatenate([bot[nj:], pltpu.roll(bot[0:nj], nj - 1, 0)], axis=0)
    hid = top + nxt + bias[0:1, :]
    act = (hid * _sigmoid(hid)).astype(bf16)
    for g in range(NSA_KV_HEADS):
        o_ref[0, g] = _dot(act[:, g * CMP_HIDDEN:(g + 1) * CMP_HIDDEN], w2_ref[0]).astype(o_ref.dtype)


def _compress(kvc, pos, wt, wb, w2, batch):
    seq = kvc.shape[0] // batch
    G = NSA_KV_HEADS
    ncmp = CMP_PER_SEL * MAX_SEL_BLOCKS
    whole = lambda arr: pl.BlockSpec((1,) + arr.shape[1:], lambda s, b: (s,) + (0,) * (arr.ndim - 1))
    return pl.pallas_call(
        _compress_kernel,
        grid=(2, batch),
        in_specs=[pl.BlockSpec((seq, W_KV), lambda s, b: (b, s)), whole(pos), whole(wt), whole(wb), whole(w2)],
        out_specs=pl.BlockSpec((1, G, ncmp, HEAD_DIM), lambda s, b: (s, b, 0, 0)),
        out_shape=jax.ShapeDtypeStruct((2, batch * G, ncmp, HEAD_DIM), jnp.bfloat16),
        scratch_shapes=[pltpu.VMEM((seq, W_KV), jnp.float32)],
        compiler_params=pltpu.CompilerParams(dimension_semantics=("arbitrary", "arbitrary"),
                                             vmem_limit_bytes=VMEM_LIMIT),
        name="compress",
    )(kvc, pos, wt, wb, w2)


def _compress_weights(pos, w1, w2):
    bf16 = jnp.bfloat16
    pos2 = jnp.broadcast_to(jnp.tile(pos, (1, 1, 2))[:, :, None, :], (2, CMP_LEN, BF16_ROWS, W_KV)).astype(bf16)
    w = w1.astype(bf16).reshape(2, 2, CMP_STRIDE, HEAD_DIM, CMP_HIDDEN)
    lead = ((0, 0),) * 3
    bd = (jnp.pad(w, lead + ((0, HEAD_DIM), (0, CMP_HIDDEN)))
          + jnp.pad(w, lead + ((HEAD_DIM, 0), (CMP_HIDDEN, 0))))
    bd = bd.reshape(2, 2, CMP_PAIRS, 2 * W_KV, 2 * CMP_HIDDEN)
    return pos2, bd[:, 0], bd[:, 1], w2.astype(bf16)


WIN_BLOCKS = -(-(SWA_WINDOW - 1) // BLOCK) + 1


def _nsa_attn_kernel(q_ref, kc_ref, vc_ref, et_ref, ks_ref, vst_ref, kw_ref, vwt_ref,
                     gate_ref, sza_ref, o_ref, s_a, s_b, p_a, p_b):
    f32, bf16 = jnp.float32, jnp.bfloat16
    qb = pl.program_id(1)
    G, R = NSA_KV_HEADS, NSA_Q_PER_KV
    grows = R * BLOCK
    rows = G * grows
    gcols = lambda g: slice(g * grows, (g + 1) * grows)
    heads_t = [q_ref[0, h * HEAD_DIM:(h + 1) * HEAD_DIM, :] for h in range(NSA_HEADS)]
    q_cmp_t = jnp.concatenate(heads_t, axis=1)
    zero_t = jnp.zeros((HEAD_DIM, BLOCK), bf16)
    q_kv_t = jnp.concatenate([jnp.concatenate([qh, zero_t] if h < R else [zero_t, qh], axis=0)
                              for h, qh in enumerate(heads_t)], axis=1)
    t_1 = qb * BLOCK + lax.broadcasted_iota(jnp.int32, (1, BLOCK), 1)
    head_bias = lambda ok, reps: jnp.tile(jnp.where(ok, 0.0, NEG_INF), (1, reps))

    ncmp = kc_ref.shape[1]
    pos = lax.broadcasted_iota(jnp.int32, (ncmp, BLOCK), 0)
    cmp_end = (pos & (LANES - 1)) * SEL_BLOCK + (pos >> LANE_SHIFT) * CMP_STRIDE + (CMP_LEN - 1)
    cmp_bias = head_bias(cmp_end <= t_1, R)
    seen = jnp.tile(jnp.where(t_1 >= CMP_LEN - 1, 1.0, 0.0), (1, R))
    blk = lax.broadcasted_iota(jnp.int32, (LANES, BLOCK), 0)
    o_cmp, imp_sel = [], []
    for g in range(G):
        st = _dot(kc_ref[g], q_cmp_t[:, gcols(g)]) + cmp_bias
        mx = jnp.max(st, axis=0, keepdims=True)
        e = jnp.exp2(st - mx)
        den = jnp.maximum(jnp.sum(e, axis=0, keepdims=True), 1e-30)
        p = e * (seen / den)
        o_cmp.append(lax.dot_general(vc_ref[g], p.astype(bf16), _TN,
                                     preferred_element_type=f32))
        imp = p[:, 0:BLOCK]
        for r in range(1, R):
            imp = imp + p[:, r * BLOCK:(r + 1) * BLOCK]
        q4 = [imp[i * LANES:(i + 1) * LANES] for i in range(CMP_PER_SEL)]
        prev_last = jnp.where(blk == 0, 0.0, pltpu.roll(q4[3], 1, 0))
        imp_sel.append(prev_last + 2.0 * (q4[0] + q4[1] + q4[2]) + q4[3])

    span = WIN_BLOCKS * BLOCK
    first_blk = jnp.maximum(qb - (WIN_BLOCKS - 1), 0)
    win = pl.ds(pl.multiple_of(first_blk * BLOCK, BLOCK), span)
    dist = t_1 - (first_blk * BLOCK + lax.broadcasted_iota(jnp.int32, (span, BLOCK), 0))
    st = _dot(kw_ref[win, :], q_kv_t) + head_bias((dist >= 0) & (dist <= SWA_WINDOW - 1), G * R)
    mw = jnp.max(st, axis=0, keepdims=True)
    e = jnp.exp2(st - mw).astype(bf16)
    ones_win = jnp.ones((ONES_ROWS, span), bf16)
    o_win = []
    for g in range(G):
        r = _dot(jnp.concatenate([vwt_ref[0, g * HEAD_DIM:(g + 1) * HEAD_DIM, win], ones_win], axis=0),
                 e[:, gcols(g)])
        o_win.append(r[0:HEAD_DIM] * (1.0 / r[HEAD_DIM:HEAD_DIM + 1]))

    cur = (qb * BLOCK + lax.broadcasted_iota(jnp.int32, (LANES, BLOCK), 1)) >> SEL_SHIFT
    forced = (blk == 0) | (blk == cur) | (blk == cur - 1)
    blk_f = blk.astype(f32)
    bias = []
    for g in range(G):
        sc = jnp.where(forced, -2.0, jnp.where(blk <= cur, imp_sel[g], -1.0))
        for _ in range(SEL_TOP_N - N_FORCED):
            best = jnp.max(sc, axis=0, keepdims=True)
            first = jnp.min(jnp.where(sc == best, blk_f, float(LANES)), axis=0, keepdims=True)
            sc = jnp.where(blk_f == first, -2.0, sc)
        taken = (sc == -2.0) & (blk <= cur)
        bias.append(jnp.tile(jnp.where(taken, 0.0, NEG_INF).astype(bf16), (1, R)))

    qa_t = jnp.concatenate([jnp.concatenate(bias, axis=1), q_kv_t], axis=0)
    n = (qb * BLOCK) // KEY_TILE
    last = jnp.maximum(n - 1, 0)
    ones = jnp.ones((ONES_ROWS, KEY_TILE), bf16)

    def keys_aug(kt):
        tile = pl.ds(pl.multiple_of(kt * KEY_TILE, KEY_TILE), KEY_TILE)
        return jnp.concatenate([et_ref[tile, :], ks_ref[tile, :]], axis=1)

    def qk(kt, s_ref):
        st = _dot(keys_aug(kt), qa_t)
        s_ref[...] = st
        return jnp.max(st, axis=0, keepdims=True)

    def pv(kt, p_ref):
        tile = pl.ds(pl.multiple_of(kt * KEY_TILE, KEY_TILE), KEY_TILE)
        return tuple(_dot(jnp.concatenate([vst_ref[0, g * HEAD_DIM:(g + 1) * HEAD_DIM, tile], ones], axis=0),
                          p_ref[:, gcols(g)]) for g in range(G))

    def softmax(m_old, mx, s_ref, p_ref):
        m_new = jnp.maximum(m_old, mx)
        p_ref[...] = jnp.exp2(s_ref[...] - m_new).astype(bf16)
        return m_new, jnp.exp2(m_old - m_new)

    def accumulate(acc, alpha, weight, contrib):
        return tuple(alpha[:, gcols(g)] * acc[g] + weight * contrib[g] for g in range(G))

    key = n * KEY_TILE + lax.broadcasted_iota(jnp.int32, (KEY_TILE, BLOCK), 0)
    st = _dot(keys_aug(n), qa_t) + head_bias(key <= t_1, G * R)
    m0 = jnp.max(st, axis=0, keepdims=True)
    p_b[...] = jnp.exp2(st - m0).astype(bf16)
    mx0 = qk(0, s_a)
    acc0 = tuple(jnp.zeros((HEAD_DIM + ONES_ROWS, grows), f32) for _ in range(G))
    one = jnp.ones_like(m0)

    def body(i, carry):
        m, acc, alpha_prev, w_prev, kt_prev, mx = carry
        first, second = 2 * i, 2 * i + 1
        w_second = jnp.where(second < n, 1.0, 0.0)
        kt_second = jnp.minimum(second, last)
        acc = accumulate(acc, alpha_prev, w_prev, pv(kt_prev, p_b))
        m, alpha = softmax(m, mx, s_a, p_a)
        mx = qk(kt_second, s_b)
        acc = accumulate(acc, alpha, 1.0, pv(first, p_a))
        m, alpha = softmax(m, mx, s_b, p_b)
        mx = qk(jnp.minimum(second + 1, last), s_a)
        return m, acc, alpha, w_second, kt_second, mx

    init = (m0, acc0, one, jnp.float32(1.0), n, mx0)
    _, acc, alpha_prev, w_prev, kt_prev, _ = lax.fori_loop(0, (n + 1) // 2, body, init)
    acc = accumulate(acc, alpha_prev, w_prev, pv(kt_prev, p_b))
    o_slc = [a[0:HEAD_DIM] * (1.0 / a[HEAD_DIM:HEAD_DIM + 1]) for a in acc]

    for g in range(G):
        for r in range(R):
            h = g * R + r
            cols = slice(r * BLOCK, (r + 1) * BLOCK)
            gate_row = lambda branch: gate_ref[0, branch * NSA_HEADS + h:branch * NSA_HEADS + h + 1, :]
            o_t = (gate_row(0) * o_cmp[g][:, cols] + gate_row(1) * o_slc[g][:, cols]
                   + gate_row(2) * o_win[g][:, cols])
            dims = slice(h * HEAD_DIM, (h + 1) * HEAD_DIM)
            o_ref[0, dims, :] = (o_t * sza_ref[0, dims, :].astype(f32)).astype(o_ref.dtype)


def _nsa_attn(qa, kcp, vcp, onehot, ks, vst, kw, vwt, gate, sza, batch):
    seq = qa.shape[2]
    nqb = seq // BLOCK
    ncmp = kcp.shape[1]
    G = NSA_KV_HEADS
    rows = NSA_HEADS * BLOCK
    t_block = lambda wd: pl.BlockSpec((1, wd, BLOCK), lambda b, j: (b, 0, j))
    return pl.pallas_call(
        _nsa_attn_kernel,
        grid=(batch, nqb),
        in_specs=[t_block(W_NSA),
                  pl.BlockSpec((G, ncmp, HEAD_DIM), lambda b, j: (b, 0, 0)),
                  pl.BlockSpec((G, ncmp, HEAD_DIM), lambda b, j: (b, 0, 0)),
                  pl.BlockSpec((seq, LANES), lambda b, j: (0, 0)),
                  pl.BlockSpec((seq, W_KV), lambda b, j: (b, 0)),
                  pl.BlockSpec((1, W_KV, seq), lambda b, j: (b, 0, 0)),
                  pl.BlockSpec((seq, W_KV), lambda b, j: (b, 0)),
                  pl.BlockSpec((1, W_KV, seq), lambda b, j: (b, 0, 0)),
                  t_block(LANES),
                  t_block(W_NSA)],
        out_specs=t_block(W_NSA),
        out_shape=jax.ShapeDtypeStruct((batch, W_NSA, seq), jnp.bfloat16),
        scratch_shapes=[pltpu.VMEM((KEY_TILE, rows), jnp.float32),
                        pltpu.VMEM((KEY_TILE, rows), jnp.float32),
                        pltpu.VMEM((KEY_TILE, rows), jnp.bfloat16),
                        pltpu.VMEM((KEY_TILE, rows), jnp.bfloat16)],
        compiler_params=pltpu.CompilerParams(dimension_semantics=("arbitrary", "arbitrary"),
                                             vmem_limit_bytes=VMEM_LIMIT),
        name="nsa_attn",
    )(qa, kcp, vcp, onehot, ks, vst, kw, vwt, gate, sza)


DIL_MAX = max(d for _, d in DIL_PATTERNS)
DIL_SUPER = BLOCK * DIL_MAX
DIL_UNITS = DIL_SUPER // BLOCK
HEAD_PAIR = 2 * HEAD_DIM
MIX_ROWS = 256


def _dil_mix_kernel(q_ref, kp_ref, kc_ref, vp_ref, vc_ref, z_ref, o_ref,
                    qf, kf, vf, num_scr, den_scr, max_scr, bias_scr):
    f32, bf16 = jnp.float32, jnp.bfloat16
    sb = pl.program_id(1)
    qf[...] = q_ref[...].astype(f32)
    kf[0:DIL_SUPER] = kp_ref[...].astype(f32)
    kf[DIL_SUPER:2 * DIL_SUPER] = kc_ref[...].astype(f32)
    vf[0:DIL_SUPER] = vp_ref[...].astype(f32)
    vf[DIL_SUPER:2 * DIL_SUPER] = vc_ref[...].astype(f32)

    row = lax.broadcasted_iota(jnp.int32, (2 * BLOCK, 2 * BLOCK), 0)
    col = lax.broadcasted_iota(jnp.int32, (2 * BLOCK, 2 * BLOCK), 1)
    dist = BLOCK + (row & (BLOCK - 1)) - col
    band = (dist >= 0) & (dist <= BLOCK)
    bias_scr[0] = jnp.where(band, 0.0, NEG_INF)
    bias_scr[1] = jnp.where(band & (col >= BLOCK), 0.0, NEG_INF)
    first_head = lax.broadcasted_iota(jnp.int32, (BLOCK, HEAD_PAIR), 1) < HEAD_DIM
    ones = jnp.ones((2 * BLOCK, HEAD_PAIR), bf16)

    for pat, (window, dil) in enumerate(DIL_PATTERNS):
        shift = dil.bit_length() - 1

        def unit(u, pat=pat, dil=dil, shift=shift):
            cls = u & (dil - 1)
            blk = u >> shift
            q_start = cls + blk * (BLOCK * dil)
            k_start = DIL_SUPER + q_start - BLOCK * dil
            q2 = qf[pl.ds(q_start, BLOCK, stride=dil), :]
            k2 = kf[pl.ds(k_start, 2 * BLOCK, stride=dil), :]
            v2 = vf[pl.ds(k_start, 2 * BLOCK, stride=dil), :]
            qm = jnp.concatenate([jnp.where(first_head, q2, 0.0),
                                  jnp.where(first_head, 0.0, q2)], axis=0).astype(bf16)
            s = _dot_nt(qm, k2.astype(bf16))
            no_prev = jnp.where((sb == 0) & (blk == 0), 1, 0)
            s = s + bias_scr[no_prev]
            m = jnp.max(s, axis=1, keepdims=True)
            e = jnp.exp2(s - m).astype(bf16)
            r = _dot(e, jnp.concatenate([v2.astype(bf16), ones], axis=1))
            mb = jnp.broadcast_to(m, (2 * BLOCK, HEAD_PAIR))
            out_rows = pl.ds(q_start, BLOCK, stride=dil)
            num_scr[pat, out_rows, :] = jnp.where(first_head, r[0:BLOCK, 0:HEAD_PAIR], r[BLOCK:, 0:HEAD_PAIR])
            den_scr[pat, out_rows, :] = jnp.where(first_head, r[0:BLOCK, HEAD_PAIR:], r[BLOCK:, HEAD_PAIR:])
            max_scr[pat, out_rows, :] = jnp.where(first_head, mb[0:BLOCK], mb[BLOCK:])

        for u in range(DIL_UNITS):
            unit(u)

    def mix(ci, carry):
        rows = pl.ds(pl.multiple_of(ci * MIX_ROWS, MIX_ROWS), MIX_ROWS)
        ms = [max_scr[p, rows, :] for p in range(len(DIL_PATTERNS))]
        mx = jnp.maximum(jnp.maximum(ms[0], ms[1]), ms[2])
        cs = [jnp.exp2(m - mx) for m in ms]
        num = cs[0] * num_scr[0, rows, :] + cs[1] * num_scr[1, rows, :] + cs[2] * num_scr[2, rows, :]
        den = cs[0] * den_scr[0, rows, :] + cs[1] * den_scr[1, rows, :] + cs[2] * den_scr[2, rows, :]
        o_ref[rows, :] = (num / den * z_ref[rows, :].astype(f32)).astype(o_ref.dtype)
        return carry

    lax.fori_loop(0, DIL_SUPER // MIX_ROWS, mix, 0)


def _dil_mix(qb, kb, vb, szb, batch):
    rows = qb.shape[0]
    nsb = rows // batch // DIL_SUPER
    cur = pl.BlockSpec((DIL_SUPER, HEAD_PAIR), lambda b, s, h: (b * nsb + s, h))
    prev = pl.BlockSpec((DIL_SUPER, HEAD_PAIR), lambda b, s, h: (b * nsb + jnp.maximum(s - 1, 0), h))
    f32 = jnp.float32
    return pl.pallas_call(
        _dil_mix_kernel,
        grid=(batch, nsb, W_DIL // HEAD_PAIR),
        in_specs=[cur, prev, cur, prev, cur, cur],
        out_specs=cur,
        out_shape=jax.ShapeDtypeStruct((rows, W_DIL), jnp.bfloat16),
        scratch_shapes=[pltpu.VMEM((DIL_SUPER, HEAD_PAIR), f32),
                        pltpu.VMEM((2 * DIL_SUPER, HEAD_PAIR), f32),
                        pltpu.VMEM((2 * DIL_SUPER, HEAD_PAIR), f32),
                        pltpu.VMEM((len(DIL_PATTERNS), DIL_SUPER, HEAD_PAIR), f32),
                        pltpu.VMEM((len(DIL_PATTERNS), DIL_SUPER, HEAD_PAIR), f32),
                        pltpu.VMEM((len(DIL_PATTERNS), DIL_SUPER, HEAD_PAIR), f32),
                        pltpu.VMEM((2, 2 * BLOCK, 2 * BLOCK), f32)],
        compiler_params=pltpu.CompilerParams(
            dimension_semantics=("arbitrary", "arbitrary", "arbitrary"),
            vmem_limit_bytes=VMEM_LIMIT),
        name="dil_mix",
    )(qb, kb, kb, vb, vb, szb)


def _out_proj_kernel(x_ref, ma_ref, mb_ref, w_ref, g_ref, out_ref):
    y = lax.dot_general(ma_ref[0], w_ref[0:W_NSA, :], _TN, preferred_element_type=jnp.float32)
    y = y + _dot(mb_ref[...], w_ref[W_NSA:W_NSA + W_DIL, :])
    ms = jnp.mean(y * y, axis=-1, keepdims=True)
    out_ref[...] = x_ref[...] + y * lax.rsqrt(ms + RMS_EPS) * g_ref[...]


def _out_proj(x2, mixed_a_t, mixed_b, w, g):
    rows, d_model = x2.shape
    tm = OUT_ROWS
    per_batch = mixed_a_t.shape[2] // tm
    row_spec = lambda wd: pl.BlockSpec((tm, wd), lambda i: (i, 0))
    return pl.pallas_call(
        _out_proj_kernel,
        grid=(rows // tm,),
        in_specs=[row_spec(d_model),
                  pl.BlockSpec((1, W_NSA, tm), lambda i: (i // per_batch, 0, i % per_batch)),
                  row_spec(W_DIL),
                  pl.BlockSpec(w.shape, lambda i: (0, 0)),
                  pl.BlockSpec((1, d_model), lambda i: (0, 0))],
        out_specs=row_spec(d_model),
        out_shape=jax.ShapeDtypeStruct((rows, d_model), jnp.float32),
        compiler_params=pltpu.CompilerParams(dimension_semantics=("arbitrary",),
                                             vmem_limit_bytes=VMEM_LIMIT),
        name="out_proj",
    )(x2, mixed_a_t, mixed_b, w, g)


def _rope_tables(positions):
    inv = 1.0 / (ROPE_THETA ** (jnp.arange(0, ROPE_DIMS, 2, dtype=jnp.float32) / ROPE_DIMS))
    ang = positions.astype(jnp.float32).reshape(-1)[:, None] * inv
    cos, sin = jnp.cos(ang), jnp.sin(ang)
    half = ROPE_DIMS // 2
    k = jnp.arange(LANES) % HEAD_DIM
    freq = jnp.arange(half)[:, None]
    first = (k[None, :] == freq).astype(jnp.float32)
    second = (k[None, :] == freq + half).astype(jnp.float32)
    spread = lambda t, m: jnp.dot(t, m, precision=lax.Precision.HIGHEST)
    unrotated = (k >= ROPE_DIMS).astype(jnp.float32)[None, :]
    return spread(cos, first + second) + unrotated, spread(-sin, first), spread(sin, second)


def _in_proj_weights(w_in):
    scale = HEAD_DIM ** -0.5 * LOG2_E
    bf16 = jnp.bfloat16
    n_head, n_gate = W_NSA + 6 * W_KV, 3 * NSA_HEADS
    ones = lambda n: jnp.ones((n,), jnp.float32)
    head_scale = jnp.concatenate([scale * ones(W_NSA), ones(6 * W_KV)])
    tail_scale = jnp.concatenate([ones(W_NSA), scale * ones(W_DIL), ones(3 * W_DIL)])
    w_head = (w_in[:, :n_head] * head_scale).astype(bf16)
    w_gate = jnp.pad(w_in[:, n_head:n_head + n_gate], ((0, 0), (0, LANES - n_gate))).astype(bf16)
    w_tail = (w_in[:, n_head + n_gate:] * tail_scale).astype(bf16)
    return w_head, w_gate, w_tail


def kernel(x, positions, pre_norm_g, w_in, cmp_k_pos, cmp_k_w1, cmp_k_w2,
           cmp_v_pos, cmp_v_w1, cmp_v_w2, w_out, post_norm_g):
    B, S, d_model = x.shape
    depth = w_in.shape[0]
    n_sel = S // SEL_BLOCK
    assert S % KEY_TILE == 0 and n_sel == MAX_SEL_BLOCKS and S >= WIN_BLOCKS * BLOCK
    assert S % DIL_SUPER == 0 and all(win // dil == BLOCK for win, dil in DIL_PATTERNS)
    bf16 = jnp.bfloat16

    rope_c, rope_a, rope_b = _rope_tables(positions)
    onehot = (jnp.arange(S)[:, None] // SEL_BLOCK == jnp.arange(LANES)[None, :]).astype(bf16)
    x2 = x.reshape(B * S, d_model)

    for layer in range(depth):
        qa, kvc, ks, kw, vst, vwt, gate, sza, qb_, kb_, vb_, szb = _in_proj(
            x2, pre_norm_g[layer][None, :], _in_proj_weights(w_in[layer]), rope_c, rope_a, rope_b, B)

        kvc = _compress(kvc, *_compress_weights(jnp.stack([cmp_k_pos[layer], cmp_v_pos[layer]]),
                                                jnp.stack([cmp_k_w1[layer], cmp_v_w1[layer]]),
                                                jnp.stack([cmp_k_w2[layer], cmp_v_w2[layer]])), B)
        mixed_a = _nsa_attn(qa, kvc[0], kvc[1], onehot, ks, vst, kw, vwt, gate, sza, B)
        mixed_b = _dil_mix(qb_, kb_, vb_, szb, B)
        x2 = _out_proj(x2, mixed_a, mixed_b, w_out[layer].astype(bf16), post_norm_g[layer][None, :])
    return x2.reshape(B, S, d_model)
```

```python
import jax
import jax.numpy as jnp
from jax import lax
from jax.experimental import pallas as pl
from jax.experimental.pallas import tpu as pltpu

HEAD_DIM = 64
NSA_HEADS = 8
NSA_KV_HEADS = 2
NSA_Q_PER_KV = NSA_HEADS // NSA_KV_HEADS
DIL_HEADS = 8
W_NSA = NSA_HEADS * HEAD_DIM
W_KV = NSA_KV_HEADS * HEAD_DIM
W_DIL = DIL_HEADS * HEAD_DIM
CMP_LEN = 32
CMP_STRIDE = 16
CMP_HIDDEN = 256
SEL_BLOCK = 64
SEL_TOP_N = 16
N_FORCED = 3
SWA_WINDOW = 512
DIL_PATTERNS = ((128, 1), (512, 4), (2048, 16))
BLOCK = 128
ROPE_THETA = 500000.0
ROPE_DIMS = HEAD_DIM // 4
RMS_EPS = 1e-6
NEG_INF = -1e30
FORCE_SCORE = 1e4
LOG2_E = 1.4426950408889634

LANES = 128
VMEM_LIMIT = 56 * 1024 * 1024
MAX_SEL_BLOCKS = LANES
CMP_PER_SEL = SEL_BLOCK // CMP_STRIDE
KEY_TILE = 512
PROJ_ROWS = 512
OUT_ROWS = 1024
BF16_ROWS = 16
ONES_ROWS = BF16_ROWS
SEL_SHIFT = SEL_BLOCK.bit_length() - 1
LANE_SHIFT = LANES.bit_length() - 1

_NT = (((1,), (1,)), ((), ()))
_TN = (((0,), (0,)), ((), ()))


def _dot(a, b):
    return jnp.dot(a, b, preferred_element_type=jnp.float32)


def _dot_nt(a, b):
    return lax.dot_general(a, b, _NT, preferred_element_type=jnp.float32)


def _sigmoid(x):
    return 1.0 / (1.0 + jnp.exp(-x))


def _rope(x, c, a, b):
    width = x.shape[1]
    reps = width // LANES
    ct = jnp.tile(c, (1, reps))
    at = jnp.tile(a, (1, reps))
    bt = jnp.tile(b, (1, reps))
    half = ROPE_DIMS // 2
    return x * ct + pltpu.roll(x, width - half, 1) * at + pltpu.roll(x, half, 1) * bt


def _in_proj_kernel(x_ref, g_ref, w_head_ref, w_gate_ref, w_tail_ref, c_ref, a_ref, b_ref,
                    qa_ref, kvc_ref, ks_ref, kw_ref, vst_ref, vwt_ref, gate_ref, sza_ref,
                    qb_ref, kb_ref, vb_ref, szb_ref):
    x = x_ref[...]
    ms = jnp.mean(x * x, axis=-1, keepdims=True)
    h = (x * lax.rsqrt(ms + RMS_EPS) * g_ref[...]).astype(jnp.bfloat16)
    c = c_ref[...]
    a = a_ref[...]
    b = b_ref[...]

    def columns_of(w_ref):
        off = 0

        def take(width):
            nonlocal off
            off += width
            return _dot(h, w_ref[:, off - width:off])
        return take

    head, gates, proj = columns_of(w_head_ref), columns_of(w_gate_ref), columns_of(w_tail_ref)
    qa_ref[0] = _rope(head(W_NSA), c, a, b).T.astype(qa_ref.dtype)
    kva = head(6 * W_KV)
    part = lambda i: kva[:, i * W_KV:(i + 1) * W_KV]
    kvc_ref[...] = jnp.concatenate([_rope(part(0), c, a, b), part(1)], axis=1).astype(kvc_ref.dtype)
    ks_ref[...] = _rope(part(2), c, a, b).astype(ks_ref.dtype)
    kw_ref[...] = _rope(part(4), c, a, b).astype(kw_ref.dtype)
    vst_ref[0] = part(3).T.astype(vst_ref.dtype)
    vwt_ref[0] = part(5).T.astype(vwt_ref.dtype)
    gate_ref[0] = _sigmoid(gates(LANES)).T
    za = proj(W_NSA)
    sza_ref[0] = (za * _sigmoid(za)).T.astype(sza_ref.dtype)
    qb_ref[...] = _rope(proj(W_DIL), c, a, b).astype(qb_ref.dtype)
    kb_ref[...] = _rope(proj(W_DIL), c, a, b).astype(kb_ref.dtype)
    vb_ref[...] = proj(W_DIL).astype(vb_ref.dtype)
    zb = proj(W_DIL)
    szb_ref[...] = (zb * _sigmoid(zb)).astype(szb_ref.dtype)


def _in_proj(x2, g, weights, c, a, b, batch):
    rows, d_model = x2.shape
    tm = PROJ_ROWS
    seq = rows // batch
    per_batch = seq // tm
    bf16 = jnp.bfloat16
    row_spec = lambda wd: pl.BlockSpec((tm, wd), lambda i: (i, 0))
    row_out = lambda wd: (row_spec(wd), jax.ShapeDtypeStruct((rows, wd), bf16))
    t_out = lambda wd, dt=bf16: (pl.BlockSpec((1, wd, tm), lambda i: (i // per_batch, 0, i % per_batch)),
                                 jax.ShapeDtypeStruct((batch, wd, seq), dt))
    outs = [t_out(W_NSA), row_out(2 * W_KV), row_out(W_KV), row_out(W_KV), t_out(W_KV), t_out(W_KV),
            t_out(LANES, jnp.float32), t_out(W_NSA),
            row_out(W_DIL), row_out(W_DIL), row_out(W_DIL), row_out(W_DIL)]
    return pl.pallas_call(
        _in_proj_kernel,
        grid=(rows // tm,),
        in_specs=[row_spec(d_model),
                  pl.BlockSpec((1, d_model), lambda i: (0, 0))]
                 + [pl.BlockSpec(w.shape, lambda i: (0, 0)) for w in weights]
                 + [row_spec(LANES), row_spec(LANES), row_spec(LANES)],
        out_specs=[o[0] for o in outs],
        out_shape=[o[1] for o in outs],
        compiler_params=pltpu.CompilerParams(dimension_semantics=("arbitrary",),
                                             vmem_limit_bytes=VMEM_LIMIT),
        name="in_proj",
    )(x2, g, *weights, c, a, b)


CMP_PAIRS = CMP_STRIDE // 2


def _compress_kernel(x_ref, pos_ref, wt_ref, wb_ref, w2_ref, o_ref, stage):
    f32, bf16 = jnp.float32, jnp.bfloat16
    stage[...] = x_ref[...].astype(f32)
    nj = MAX_SEL_BLOCKS
    hidden = 2 * CMP_HIDDEN

    def offset_rows(l):
        return jnp.concatenate([stage[pl.ds(CMP_STRIDE * m + l, nj, stride=SEL_BLOCK), :]
                                for m in range(CMP_PER_SEL)], axis=0)

    top = jnp.zeros((CMP_PER_SEL * nj, hidden), f32)
    bot = jnp.zeros((CMP_PER_SEL * nj, hidden), f32)
    bias = jnp.zeros((BF16_ROWS, hidden), f32)
    for i in range(CMP_PAIRS):
        x = jnp.concatenate([offset_rows(2 * i), offset_rows(2 * i + 1)], axis=1).astype(bf16)
        top = top + _dot(x, wt_ref[0, i])
        bot = bot + _dot(x, wb_ref[0, i])
        p_top = jnp.concatenate([pos_ref[0, 2 * i], pos_ref[0, 2 * i + 1]], axis=1)
        p_bot = jnp.concatenate([pos_ref[0, CMP_STRIDE + 2 * i], pos_ref[0, CMP_STRIDE + 2 * i + 1]], axis=1)
        bias = bias + _dot(p_top, wt_ref[0, i]) + _dot(p_bot, wb_ref[0, i])
    nxt = jnp.concatenate([bot[nj:], pltpu.roll(bot[0:nj], nj - 1, 0)], axis=0)
    hid = top + nxt + bias[0:1, :]
    act = (hid * _sigmoid(hid)).astype(bf16)
    for g in range(NSA_KV_HEADS):
        o_ref[0, g] = _dot(act[:, g * CMP_HIDDEN:(g + 1) * CMP_HIDDEN], w2_ref[0]).astype(o_ref.dtype)


def _compress(kvc, pos, wt, wb, w2, batch):
    seq = kvc.shape[0] // batch
    G = NSA_KV_HEADS
    ncmp = CMP_PER_SEL * MAX_SEL_BLOCKS
    whole = lambda arr: pl.BlockSpec((1,) + arr.shape[1:], lambda s, b: (s,) + (0,) * (arr.ndim - 1))
    return pl.pallas_call(
        _compress_kernel,
        grid=(2, batch),
        in_specs=[pl.BlockSpec((seq, W_KV), lambda s, b: (b, s)), whole(pos), whole(wt), whole(wb), whole(w2)],
        out_specs=pl.BlockSpec((1, G, ncmp, HEAD_DIM), lambda s, b: (s, b, 0, 0)),
        out_shape=jax.ShapeDtypeStruct((2, batch * G, ncmp, HEAD_DIM), jnp.bfloat16),
        scratch_shapes=[pltpu.VMEM((seq, W_KV), jnp.float32)],
        compiler_params=pltpu.CompilerParams(dimension_semantics=("arbitrary", "arbitrary"),
                                             vmem_limit_bytes=VMEM_LIMIT),
        name="compress",
    )(kvc, pos, wt, wb, w2)


def _compress_weights(pos, w1, w2):
    bf16 = jnp.bfloat16
    pos2 = jnp.broadcast_to(jnp.tile(pos, (1, 1, 2))[:, :, None, :], (2, CMP_LEN, BF16_ROWS, W_KV)).astype(bf16)
    w = w1.astype(bf16).reshape(2, 2, CMP_STRIDE, HEAD_DIM, CMP_HIDDEN)
    lead = ((0, 0),) * 3
    bd = (jnp.pad(w, lead + ((0, HEAD_DIM), (0, CMP_HIDDEN)))
          + jnp.pad(w, lead + ((HEAD_DIM, 0), (CMP_HIDDEN, 0))))
    bd = bd.reshape(2, 2, CMP_PAIRS, 2 * W_KV, 2 * CMP_HIDDEN)
    return pos2, bd[:, 0], bd[:, 1], w2.astype(bf16)


WIN_BLOCKS = -(-(SWA_WINDOW - 1) // BLOCK) + 1


def _nsa_attn_kernel(q_ref, kc_ref, vc_ref, et_ref, ks_ref, vst_ref, kw_ref, vwt_ref,
                     gate_ref, sza_ref, o_ref, s_a, s_b, p_a, p_b):
    f32, bf16 = jnp.float32, jnp.bfloat16
    qb = pl.program_id(1)
    G, R = NSA_KV_HEADS, NSA_Q_PER_KV
    grows = R * BLOCK
    rows = G * grows
    gcols = lambda g: slice(g * grows, (g + 1) * grows)
    heads_t = [q_ref[0, h * HEAD_DIM:(h + 1) * HEAD_DIM, :] for h in range(NSA_HEADS)]
    q_cmp_t = jnp.concatenate(heads_t, axis=1)
    zero_t = jnp.zeros((HEAD_DIM, BLOCK), bf16)
    q_kv_t = jnp.concatenate([jnp.concatenate([qh, zero_t] if h < R else [zero_t, qh], axis=0)
                              for h, qh in enumerate(heads_t)], axis=1)
    t_1 = qb * BLOCK + lax.broadcasted_iota(jnp.int32, (1, BLOCK), 1)
    head_bias = lambda ok, reps: jnp.tile(jnp.where(ok, 0.0, NEG_INF), (1, reps))

    ncmp = kc_ref.shape[1]
    pos = lax.broadcasted_iota(jnp.int32, (ncmp, BLOCK), 0)
    cmp_end = (pos & (LANES - 1)) * SEL_BLOCK + (pos >> LANE_SHIFT) * CMP_STRIDE + (CMP_LEN - 1)
    cmp_bias = head_bias(cmp_end <= t_1, R)
    seen = jnp.tile(jnp.where(t_1 >= CMP_LEN - 1, 1.0, 0.0), (1, R))
    blk = lax.broadcasted_iota(jnp.int32, (LANES, BLOCK), 0)
    o_cmp, imp_sel = [], []
    for g in range(G):
        st = _dot(kc_ref[g], q_cmp_t[:, gcols(g)]) + cmp_bias
        mx = jnp.max(st, axis=0, keepdims=True)
        e = jnp.exp2(st - mx)
        den = jnp.maximum(jnp.sum(e, axis=0, keepdims=True), 1e-30)
        p = e * (seen / den)
        o_cmp.append(lax.dot_general(vc_ref[g], p.astype(bf16), _TN,
                                     preferred_element_type=f32))
        imp = p[:, 0:BLOCK]
        for r in range(1, R):
            imp = imp + p[:, r * BLOCK:(r + 1) * BLOCK]
        q4 = [imp[i * LANES:(i + 1) * LANES] for i in range(CMP_PER_SEL)]
        prev_last = jnp.where(blk == 0, 0.0, pltpu.roll(q4[3], 1, 0))
        imp_sel.append(prev_last + 2.0 * (q4[0] + q4[1] + q4[2]) + q4[3])

    span = WIN_BLOCKS * BLOCK
    first_blk = jnp.maximum(qb - (WIN_BLOCKS - 1), 0)
    win = pl.ds(pl.multiple_of(first_blk * BLOCK, BLOCK), span)
    dist = t_1 - (first_blk * BLOCK + lax.broadcasted_iota(jnp.int32, (span, BLOCK), 0))
    st = _dot(kw_ref[win, :], q_kv_t) + head_bias((dist >= 0) & (dist <= SWA_WINDOW - 1), G * R)
    mw = jnp.max(st, axis=0, keepdims=True)
    e = jnp.exp2(st - mw).astype(bf16)
    ones_win = jnp.ones((ONES_ROWS, span), bf16)
    o_win = []
    for g in range(G):
        r = _dot(jnp.concatenate([vwt_ref[0, g * HEAD_DIM:(g + 1) * HEAD_DIM, win], ones_win], axis=0),
                 e[:, gcols(g)])
        o_win.append(r[0:HEAD_DIM] * (1.0 / r[HEAD_DIM:HEAD_DIM + 1]))

    cur = (qb * BLOCK + lax.broadcasted_iota(jnp.int32, (LANES, BLOCK), 1)) >> SEL_SHIFT
    forced = (blk == 0) | (blk == cur) | (blk == cur - 1)
    blk_f = blk.astype(f32)
    bias = []
    for g in range(G):
        sc = jnp.where(forced, -2.0, jnp.where(blk <= cur, imp_sel[g], -1.0))
        for _ in range(SEL_TOP_N - N_FORCED):
            best = jnp.max(sc, axis=0, keepdims=True)
            first = jnp.min(jnp.where(sc == best, blk_f, float(LANES)), axis=0, keepdims=True)
            sc = jnp.where(blk_f == first, -2.0, sc)
        taken = (sc == -2.0) & (blk <= cur)
        bias.append(jnp.tile(jnp.where(taken, 0.0, NEG_INF).astype(bf16), (1, R)))

    qa_t = jnp.concatenate([jnp.concatenate(bias, axis=1), q_kv_t], axis=0)
    n = (qb * BLOCK) // KEY_TILE
    last = jnp.maximum(n - 1, 0)
    ones = jnp.ones((ONES_ROWS, KEY_TILE), bf16)

    def keys_aug(kt):
        tile = pl.ds(pl.multiple_of(kt * KEY_TILE, KEY_TILE), KEY_TILE)
        return jnp.concatenate([et_ref[tile, :], ks_ref[tile, :]], axis=1)

    def qk(kt, s_ref):
        st = _dot(keys_aug(kt), qa_t)
        s_ref[...] = st
        return jnp.max(st, axis=0, keepdims=True)

    def pv(kt, p_ref):
        tile = pl.ds(pl.multiple_of(kt * KEY_TILE, KEY_TILE), KEY_TILE)
        return tuple(_dot(jnp.concatenate([vst_ref[0, g * HEAD_DIM:(g + 1) * HEAD_DIM, tile], ones], axis=0),
                          p_ref[:, gcols(g)]) for g in range(G))

    def softmax(m_old, mx, s_ref, p_ref):
        m_new = jnp.maximum(m_old, mx)
        p_ref[...] = jnp.exp2(s_ref[...] - m_new).astype(bf16)
        return m_new, jnp.exp2(m_old - m_new)

    def accumulate(acc, alpha, weight, contrib):
        return tuple(alpha[:, gcols(g)] * acc[g] + weight * contrib[g] for g in range(G))

    key = n * KEY_TILE + lax.broadcasted_iota(jnp.int32, (KEY_TILE, BLOCK), 0)
    st = _dot(keys_aug(n), qa_t) + head_bias(key <= t_1, G * R)
    m0 = jnp.max(st, axis=0, keepdims=True)
    p_b[...] = jnp.exp2(st - m0).astype(bf16)
    mx0 = qk(0, s_a)
    acc0 = tuple(jnp.zeros((HEAD_DIM + ONES_ROWS, grows), f32) for _ in range(G))
    one = jnp.ones_like(m0)

    def body(i, carry):
        m, acc, alpha_prev, w_prev, kt_prev, mx = carry
        first, second = 2 * i, 2 * i + 1
        w_second = jnp.where(second < n, 1.0, 0.0)
        kt_second = jnp.minimum(second, last)
        acc = accumulate(acc, alpha_prev, w_prev, pv(kt_prev, p_b))
        m, alpha = softmax(m, mx, s_a, p_a)
        mx = qk(kt_second, s_b)
        acc = accumulate(acc, alpha, 1.0, pv(first, p_a))
        m, alpha = softmax(m, mx, s_b, p_b)
        mx = qk(jnp.minimum(second + 1, last), s_a)
        return m, acc, alpha, w_second, kt_second, mx

    init = (m0, acc0, one, jnp.float32(1.0), n, mx0)
    _, acc, alpha_prev, w_prev, kt_prev, _ = lax.fori_loop(0, (n + 1) // 2, body, init)
    acc = accumulate(acc, alpha_prev, w_prev, pv(kt_prev, p_b))
    o_slc = [a[0:HEAD_DIM] * (1.0 / a[HEAD_DIM:HEAD_DIM + 1]) for a in acc]

    for g in range(G):
        for r in range(R):
            h = g * R + r
            cols = slice(r * BLOCK, (r + 1) * BLOCK)
            gate_row = lambda branch: gate_ref[0, branch * NSA_HEADS + h:branch * NSA_HEADS + h + 1, :]
            o_t = (gate_row(0) * o_cmp[g][:, cols] + gate_row(1) * o_slc[g][:, cols]
                   + gate_row(2) * o_win[g][:, cols])
            dims = slice(h * HEAD_DIM, (h + 1) * HEAD_DIM)
            o_ref[0, dims, :] = (o_t * sza_ref[0, dims, :].astype(f32)).astype(o_ref.dtype)


def _nsa_attn(qa, kcp, vcp, onehot, ks, vst, kw, vwt, gate, sza, batch):
    seq = qa.shape[2]
    nqb = seq // BLOCK
    ncmp = kcp.shape[1]
    G = NSA_KV_HEADS
    rows = NSA_HEADS * BLOCK
    t_block = lambda wd: pl.BlockSpec((1, wd, BLOCK), lambda b, j: (b, 0, j))
    return pl.pallas_call(
        _nsa_attn_kernel,
        grid=(batch, nqb),
        in_specs=[t_block(W_NSA),
                  pl.BlockSpec((G, ncmp, HEAD_DIM), lambda b, j: (b, 0, 0)),
                  pl.BlockSpec((G, ncmp, HEAD_DIM), lambda b, j: (b, 0, 0)),
                  pl.BlockSpec((seq, LANES), lambda b, j: (0, 0)),
                  pl.BlockSpec((seq, W_KV), lambda b, j: (b, 0)),
                  pl.BlockSpec((1, W_KV, seq), lambda b, j: (b, 0, 0)),
                  pl.BlockSpec((seq, W_KV), lambda b, j: (b, 0)),
                  pl.BlockSpec((1, W_KV, seq), lambda b, j: (b, 0, 0)),
                  t_block(LANES),
                  t_block(W_NSA)],
        out_specs=t_block(W_NSA),
        out_shape=jax.ShapeDtypeStruct((batch, W_NSA, seq), jnp.bfloat16),
        scratch_shapes=[pltpu.VMEM((KEY_TILE, rows), jnp.float32),
                        pltpu.VMEM((KEY_TILE, rows), jnp.float32),
                        pltpu.VMEM((KEY_TILE, rows), jnp.bfloat16),
                        pltpu.VMEM((KEY_TILE, rows), jnp.bfloat16)],
        compiler_params=pltpu.CompilerParams(dimension_semantics=("arbitrary", "arbitrary"),
                                             vmem_limit_bytes=VMEM_LIMIT),
        name="nsa_attn",
    )(qa, kcp, vcp, onehot, ks, vst, kw, vwt, gate, sza)


DIL_MAX = max(d for _, d in DIL_PATTERNS)
DIL_SUPER = BLOCK * DIL_MAX
DIL_UNITS = DIL_SUPER // BLOCK
HEAD_PAIR = 2 * HEAD_DIM
MIX_ROWS = 256


def _dil_mix_kernel(q_ref, kp_ref, kc_ref, vp_ref, vc_ref, z_ref, o_ref,
                    qf, kf, vf, num_scr, den_scr, max_scr, bias_scr):
    f32, bf16 = jnp.float32, jnp.bfloat16
    sb = pl.program_id(1)
    qf[...] = q_ref[...].astype(f32)
    kf[0:DIL_SUPER] = kp_ref[...].astype(f32)
    kf[DIL_SUPER:2 * DIL_SUPER] = kc_ref[...].astype(f32)
    vf[0:DIL_SUPER] = vp_ref[...].astype(f32)
    vf[DIL_SUPER:2 * DIL_SUPER] = vc_ref[...].astype(f32)

    row = lax.broadcasted_iota(jnp.int32, (2 * BLOCK, 2 * BLOCK), 0)
    col = lax.broadcasted_iota(jnp.int32, (2 * BLOCK, 2 * BLOCK), 1)
    dist = BLOCK + (row & (BLOCK - 1)) - col
    band = (dist >= 0) & (dist <= BLOCK)
    bias_scr[0] = jnp.where(band, 0.0, NEG_INF)
    bias_scr[1] = jnp.where(band & (col >= BLOCK), 0.0, NEG_INF)
    first_head = lax.broadcasted_iota(jnp.int32, (BLOCK, HEAD_PAIR), 1) < HEAD_DIM
    ones = jnp.ones((2 * BLOCK, HEAD_PAIR), bf16)

    for pat, (window, dil) in enumerate(DIL_PATTERNS):
        shift = dil.bit_length() - 1

        def unit(u, pat=pat, dil=dil, shift=shift):
            cls = u & (dil - 1)
            blk = u >> shift
            q_start = cls + blk * (BLOCK * dil)
            k_start = DIL_SUPER + q_start - BLOCK * dil
            q2 = qf[pl.ds(q_start, BLOCK, stride=dil), :]
            k2 = kf[pl.ds(k_start, 2 * BLOCK, stride=dil), :]
            v2 = vf[pl.ds(k_start, 2 * BLOCK, stride=dil), :]
            qm = jnp.concatenate([jnp.where(first_head, q2, 0.0),
                                  jnp.where(first_head, 0.0, q2)], axis=0).astype(bf16)
            s = _dot_nt(qm, k2.astype(bf16))
            no_prev = jnp.where((sb == 0) & (blk == 0), 1, 0)
            s = s + bias_scr[no_prev]
            m = jnp.max(s, axis=1, keepdims=True)
            e = jnp.exp2(s - m).astype(bf16)
            r = _dot(e, jnp.concatenate([v2.astype(bf16), ones], axis=1))
            mb = jnp.broadcast_to(m, (2 * BLOCK, HEAD_PAIR))
            out_rows = pl.ds(q_start, BLOCK, stride=dil)
            num_scr[pat, out_rows, :] = jnp.where(first_head, r[0:BLOCK, 0:HEAD_PAIR], r[BLOCK:, 0:HEAD_PAIR])
            den_scr[pat, out_rows, :] = jnp.where(first_head, r[0:BLOCK, HEAD_PAIR:], r[BLOCK:, HEAD_PAIR:])
            max_scr[pat, out_rows, :] = jnp.where(first_head, mb[0:BLOCK], mb[BLOCK:])

        for u in range(DIL_UNITS):
            unit(u)

    def mix(ci, carry):
        rows = pl.ds(pl.multiple_of(ci * MIX_ROWS, MIX_ROWS), MIX_ROWS)
        ms = [max_scr[p, rows, :] for p in range(len(DIL_PATTERNS))]
        mx = jnp.maximum(jnp.maximum(ms[0], ms[1]), ms[2])
        cs = [jnp.exp2(m - mx) for m in ms]
        num = cs[0] * num_scr[0, rows, :] + cs[1] * num_scr[1, rows, :] + cs[2] * num_scr[2, rows, :]
        den = cs[0] * den_scr[0, rows, :] + cs[1] * den_scr[1, rows, :] + cs[2] * den_scr[2, rows, :]
        o_ref[rows, :] = (num / den * z_ref[rows, :].astype(f32)).astype(o_ref.dtype)
        return carry

    lax.fori_loop(0, DIL_SUPER // MIX_ROWS, mix, 0)


def _dil_mix(qb, kb, vb, szb, batch):
    rows = qb.shape[0]
    nsb = rows // batch // DIL_SUPER
    cur = pl.BlockSpec((DIL_SUPER, HEAD_PAIR), lambda b, s, h: (b * nsb + s, h))
    prev = pl.BlockSpec((DIL_SUPER, HEAD_PAIR), lambda b, s, h: (b * nsb + jnp.maximum(s - 1, 0), h))
    f32 = jnp.float32
    return pl.pallas_call(
        _dil_mix_kernel,
        grid=(batch, nsb, W_DIL // HEAD_PAIR),
        in_specs=[cur, prev, cur, prev, cur, cur],
        out_specs=cur,
        out_shape=jax.ShapeDtypeStruct((rows, W_DIL), jnp.bfloat16),
        scratch_shapes=[pltpu.VMEM((DIL_SUPER, HEAD_PAIR), f32),
                        pltpu.VMEM((2 * DIL_SUPER, HEAD_PAIR), f32),
                        pltpu.VMEM((2 * DIL_SUPER, HEAD_PAIR), f32),
                        pltpu.VMEM((len(DIL_PATTERNS), DIL_SUPER, HEAD_PAIR), f32),
                        pltpu.VMEM((len(DIL_PATTERNS), DIL_SUPER, HEAD_PAIR), f32),
                        pltpu.VMEM((len(DIL_PATTERNS), DIL_SUPER, HEAD_PAIR), f32),
                        pltpu.VMEM((2, 2 * BLOCK, 2 * BLOCK), f32)],
        compiler_params=pltpu.CompilerParams(
            dimension_semantics=("arbitrary", "arbitrary", "arbitrary"),
            vmem_limit_bytes=VMEM_LIMIT),
        name="dil_mix",
    )(qb, kb, kb, vb, vb, szb)


def _out_proj_kernel(x_ref, ma_ref, mb_ref, w_ref, g_ref, out_ref):
    y = lax.dot_general(ma_ref[0], w_ref[0:W_NSA, :], _TN, preferred_element_type=jnp.float32)
    y = y + _dot(mb_ref[...], w_ref[W_NSA:W_NSA + W_DIL, :])
    ms = jnp.mean(y * y, axis=-1, keepdims=True)
    out_ref[...] = x_ref[...] + y * lax.rsqrt(ms + RMS_EPS) * g_ref[...]


def _out_proj(x2, mixed_a_t, mixed_b, w, g):
    rows, d_model = x2.shape
    tm = OUT_ROWS
    per_batch = mixed_a_t.shape[2] // tm
    row_spec = lambda wd: pl.BlockSpec((tm, wd), lambda i: (i, 0))
    return pl.pallas_call(
        _out_proj_kernel,
        grid=(rows // tm,),
        in_specs=[row_spec(d_model),
                  pl.BlockSpec((1, W_NSA, tm), lambda i: (i // per_batch, 0, i % per_batch)),
                  row_spec(W_DIL),
                  pl.BlockSpec(w.shape, lambda i: (0, 0)),
                  pl.BlockSpec((1, d_model), lambda i: (0, 0))],
        out_specs=row_spec(d_model),
        out_shape=jax.ShapeDtypeStruct((rows, d_model), jnp.float32),
        compiler_params=pltpu.CompilerParams(dimension_semantics=("arbitrary",),
                                             vmem_limit_bytes=VMEM_LIMIT),
        name="out_proj",
    )(x2, mixed_a_t, mixed_b, w, g)


def _rope_tables(positions):
    inv = 1.0 / (ROPE_THETA ** (jnp.arange(0, ROPE_DIMS, 2, dtype=jnp.float32) / ROPE_DIMS))
    ang = positions.astype(jnp.float32).reshape(-1)[:, None] * inv
    cos, sin = jnp.cos(ang), jnp.sin(ang)
    half = ROPE_DIMS // 2
    k = jnp.arange(LANES) % HEAD_DIM
    freq = jnp.arange(half)[:, None]
    first = (k[None, :] == freq).astype(jnp.float32)
    second = (k[None, :] == freq + half).astype(jnp.float32)
    spread = lambda t, m: jnp.dot(t, m, precision=lax.Precision.HIGHEST)
    unrotated = (k >= ROPE_DIMS).astype(jnp.float32)[None, :]
    return spread(cos, first + second) + unrotated, spread(-sin, first), spread(sin, second)


W_PREP_ROWS = 256


def _in_proj_weights_kernel(w_ref, head_ref, gate_ref, tail_ref):
    scale = HEAD_DIM ** -0.5 * LOG2_E
    n_head, n_gate = W_NSA + 6 * W_KV, 3 * NSA_HEADS
    w = w_ref[...]
    rows = w.shape[0]
    head_ref[:, 0:W_NSA] = (w[:, 0:W_NSA] * scale).astype(head_ref.dtype)
    head_ref[:, W_NSA:] = w[:, W_NSA:n_head].astype(head_ref.dtype)
    gate = jnp.concatenate([w[:, n_head:n_head + n_gate], jnp.zeros((rows, LANES - n_gate), w.dtype)], axis=1)
    gate_ref[...] = gate.astype(gate_ref.dtype)
    tail = w[:, n_head + n_gate:]
    tail_ref[:, 0:W_NSA] = tail[:, 0:W_NSA].astype(tail_ref.dtype)
    tail_ref[:, W_NSA:W_NSA + W_DIL] = (tail[:, W_NSA:W_NSA + W_DIL] * scale).astype(tail_ref.dtype)
    tail_ref[:, W_NSA + W_DIL:] = tail[:, W_NSA + W_DIL:].astype(tail_ref.dtype)


def _in_proj_weights(w_in):
    d_model, n_in = w_in.shape
    n_head, n_gate = W_NSA + 6 * W_KV, 3 * NSA_HEADS
    n_tail = n_in - n_head - n_gate
    tm = W_PREP_ROWS
    spec = lambda wd: pl.BlockSpec((tm, wd), lambda i: (i, 0))
    return pl.pallas_call(
        _in_proj_weights_kernel,
        grid=(d_model // tm,),
        in_specs=[spec(n_in)],
        out_specs=[spec(n_head), spec(LANES), spec(n_tail)],
        out_shape=[jax.ShapeDtypeStruct((d_model, wd), jnp.bfloat16) for wd in (n_head, LANES, n_tail)],
        compiler_params=pltpu.CompilerParams(dimension_semantics=("arbitrary",)),
        name="in_proj_weights",
    )(w_in)


def kernel(x, positions, pre_norm_g, w_in, cmp_k_pos, cmp_k_w1, cmp_k_w2,
           cmp_v_pos, cmp_v_w1, cmp_v_w2, w_out, post_norm_g):
    B, S, d_model = x.shape
    depth = w_in.shape[0]
    n_sel = S // SEL_BLOCK
    assert S % KEY_TILE == 0 and n_sel == MAX_SEL_BLOCKS and S >= WIN_BLOCKS * BLOCK
    assert S % DIL_SUPER == 0 and all(win // dil == BLOCK for win, dil in DIL_PATTERNS)
    bf16 = jnp.bfloat16

    rope_c, rope_a, rope_b = _rope_tables(positions)
    onehot = (jnp.arange(S)[:, None] // SEL_BLOCK == jnp.arange(LANES)[None, :]).astype(bf16)
    x2 = x.reshape(B * S, d_model)

    for layer in range(depth):
        qa, kvc, ks, kw, vst, vwt, gate, sza, qb_, kb_, vb_, szb = _in_proj(
            x2, pre_norm_g[layer][None, :], _in_proj_weights(w_in[layer]), rope_c, rope_a, rope_b, B)

        kvc = _compress(kvc, *_compress_weights(jnp.stack([cmp_k_pos[layer], cmp_v_pos[layer]]),
                                                jnp.stack([cmp_k_w1[layer], cmp_v_w1[layer]]),
                                                jnp.stack([cmp_k_w2[layer], cmp_v_w2[layer]])), B)
        mixed_a = _nsa_attn(qa, kvc[0], kvc[1], onehot, ks, vst, kw, vwt, gate, sza, B)
        mixed_b = _dil_mix(qb_, kb_, vb_, szb, B)
        x2 = _out_proj(x2, mixed_a, mixed_b, w_out[layer].astype(bf16), post_norm_g[layer][None, :])
    return x2.reshape(B, S, d_model)
```

```python
import jax
import jax.numpy as jnp
from jax import lax
from jax.experimental import pallas as pl
from jax.experimental.pallas import tpu as pltpu

HEAD_DIM = 64
NSA_HEADS = 8
NSA_KV_HEADS = 2
NSA_Q_PER_KV = NSA_HEADS // NSA_KV_HEADS
DIL_HEADS = 8
W_NSA = NSA_HEADS * HEAD_DIM
W_KV = NSA_KV_HEADS * HEAD_DIM
W_DIL = DIL_HEADS * HEAD_DIM
CMP_LEN = 32
CMP_STRIDE = 16
CMP_HIDDEN = 256
SEL_BLOCK = 64
SEL_TOP_N = 16
N_FORCED = 3
SWA_WINDOW = 512
DIL_PATTERNS = ((128, 1), (512, 4), (2048, 16))
BLOCK = 128
ROPE_THETA = 500000.0
ROPE_DIMS = HEAD_DIM // 4
RMS_EPS = 1e-6
NEG_INF = -1e30
FORCE_SCORE = 1e4
LOG2_E = 1.4426950408889634

LANES = 128
VMEM_LIMIT = 56 * 1024 * 1024
MAX_SEL_BLOCKS = LANES
CMP_PER_SEL = SEL_BLOCK // CMP_STRIDE
KEY_TILE = 512
PROJ_ROWS = 512
OUT_ROWS = 1024
BF16_ROWS = 16
ONES_ROWS = BF16_ROWS
SEL_SHIFT = SEL_BLOCK.bit_length() - 1
LANE_SHIFT = LANES.bit_length() - 1

_NT = (((1,), (1,)), ((), ()))
_TN = (((0,), (0,)), ((), ()))


def _dot(a, b):
    return jnp.dot(a, b, preferred_element_type=jnp.float32)


def _dot_nt(a, b):
    return lax.dot_general(a, b, _NT, preferred_element_type=jnp.float32)


def _sigmoid(x):
    return 1.0 / (1.0 + jnp.exp(-x))


def _rope(x, c, a, b):
    width = x.shape[1]
    reps = width // LANES
    ct = jnp.tile(c, (1, reps))
    at = jnp.tile(a, (1, reps))
    bt = jnp.tile(b, (1, reps))
    half = ROPE_DIMS // 2
    return x * ct + pltpu.roll(x, width - half, 1) * at + pltpu.roll(x, half, 1) * bt


def _in_proj_kernel(x_ref, g_ref, w_head_ref, w_gate_ref, w_tail_ref, c_ref, a_ref, b_ref,
                    qa_ref, kvc_ref, ks_ref, kw_ref, vst_ref, vwt_ref, gate_ref, sza_ref,
                    qb_ref, kb_ref, vb_ref, szb_ref):
    x = x_ref[...]
    ms = jnp.mean(x * x, axis=-1, keepdims=True)
    h = (x * lax.rsqrt(ms + RMS_EPS) * g_ref[...]).astype(jnp.bfloat16)
    c = c_ref[...]
    a = a_ref[...]
    b = b_ref[...]

    def columns_of(w_ref):
        off = 0

        def take(width):
            nonlocal off
            off += width
            return _dot(h, w_ref[:, off - width:off])
        return take

    def store_blocks_t(ref, value):
        t = value.T.astype(ref.dtype)
        for j in range(ref.shape[1]):
            ref[0, j] = t[:, j * BLOCK:(j + 1) * BLOCK]

    head, gates, proj = columns_of(w_head_ref), columns_of(w_gate_ref), columns_of(w_tail_ref)
    store_blocks_t(qa_ref, _rope(head(W_NSA), c, a, b))
    kva = head(6 * W_KV)
    part = lambda i: kva[:, i * W_KV:(i + 1) * W_KV]
    kvc_ref[...] = jnp.concatenate([_rope(part(0), c, a, b), part(1)], axis=1).astype(kvc_ref.dtype)
    ks_ref[...] = _rope(part(2), c, a, b).astype(ks_ref.dtype)
    kw_ref[...] = _rope(part(4), c, a, b).astype(kw_ref.dtype)
    vst_ref[0] = part(3).T.astype(vst_ref.dtype)
    vwt_ref[0] = part(5).T.astype(vwt_ref.dtype)
    store_blocks_t(gate_ref, _sigmoid(gates(LANES)))
    za = proj(W_NSA)
    store_blocks_t(sza_ref, za * _sigmoid(za))
    qb_ref[...] = _rope(proj(W_DIL), c, a, b).astype(qb_ref.dtype)
    kb_ref[...] = _rope(proj(W_DIL), c, a, b).astype(kb_ref.dtype)
    vb_ref[...] = proj(W_DIL).astype(vb_ref.dtype)
    zb = proj(W_DIL)
    szb_ref[...] = (zb * _sigmoid(zb)).astype(szb_ref.dtype)


def _in_proj(x2, g, weights, c, a, b, batch):
    rows, d_model = x2.shape
    tm = PROJ_ROWS
    seq = rows // batch
    per_batch = seq // tm
    bf16 = jnp.bfloat16
    row_spec = lambda wd: pl.BlockSpec((tm, wd), lambda i: (i, 0))
    row_out = lambda wd: (row_spec(wd), jax.ShapeDtypeStruct((rows, wd), bf16))
    t_out = lambda wd, dt=bf16: (pl.BlockSpec((1, wd, tm), lambda i: (i // per_batch, 0, i % per_batch)),
                                 jax.ShapeDtypeStruct((batch, wd, seq), dt))
    blocks_out = lambda wd, dt=bf16: (
        pl.BlockSpec((1, tm // BLOCK, wd, BLOCK), lambda i: (i // per_batch, i % per_batch, 0, 0)),
        jax.ShapeDtypeStruct((batch, seq // BLOCK, wd, BLOCK), dt))
    outs = [blocks_out(W_NSA), row_out(2 * W_KV), row_out(W_KV), row_out(W_KV), t_out(W_KV), t_out(W_KV),
            blocks_out(LANES, jnp.float32), blocks_out(W_NSA),
            row_out(W_DIL), row_out(W_DIL), row_out(W_DIL), row_out(W_DIL)]
    return pl.pallas_call(
        _in_proj_kernel,
        grid=(rows // tm,),
        in_specs=[row_spec(d_model),
                  pl.BlockSpec((1, d_model), lambda i: (0, 0))]
                 + [pl.BlockSpec(w.shape, lambda i: (0, 0)) for w in weights]
                 + [row_spec(LANES), row_spec(LANES), row_spec(LANES)],
        out_specs=[o[0] for o in outs],
        out_shape=[o[1] for o in outs],
        compiler_params=pltpu.CompilerParams(dimension_semantics=("arbitrary",),
                                             vmem_limit_bytes=VMEM_LIMIT),
        name="in_proj",
    )(x2, g, *weights, c, a, b)


CMP_PAIRS = CMP_STRIDE // 2


def _compress_kernel(x_ref, pos_ref, wt_ref, wb_ref, w2_ref, o_ref, stage):
    f32, bf16 = jnp.float32, jnp.bfloat16
    stage[...] = x_ref[...].astype(f32)
    nj = MAX_SEL_BLOCKS
    hidden = 2 * CMP_HIDDEN

    def offset_rows(l):
        return jnp.concatenate([stage[pl.ds(CMP_STRIDE * m + l, nj, stride=SEL_BLOCK), :]
                                for m in range(CMP_PER_SEL)], axis=0)

    top = jnp.zeros((CMP_PER_SEL * nj, hidden), f32)
    bot = jnp.zeros((CMP_PER_SEL * nj, hidden), f32)
    bias = jnp.zeros((BF16_ROWS, hidden), f32)
    for i in range(CMP_PAIRS):
        x = jnp.concatenate([offset_rows(2 * i), offset_rows(2 * i + 1)], axis=1).astype(bf16)
        top = top + _dot(x, wt_ref[0, i])
        bot = bot + _dot(x, wb_ref[0, i])
        p_top = jnp.concatenate([pos_ref[0, 2 * i], pos_ref[0, 2 * i + 1]], axis=1)
        p_bot = jnp.concatenate([pos_ref[0, CMP_STRIDE + 2 * i], pos_ref[0, CMP_STRIDE + 2 * i + 1]], axis=1)
        bias = bias + _dot(p_top, wt_ref[0, i]) + _dot(p_bot, wb_ref[0, i])
    nxt = jnp.concatenate([bot[nj:], pltpu.roll(bot[0:nj], nj - 1, 0)], axis=0)
    hid = top + nxt + bias[0:1, :]
    act = (hid * _sigmoid(hid)).astype(bf16)
    for g in range(NSA_KV_HEADS):
        o_ref[0, g] = _dot(act[:, g * CMP_HIDDEN:(g + 1) * CMP_HIDDEN], w2_ref[0]).astype(o_ref.dtype)


def _compress(kvc, pos, wt, wb, w2, batch):
    seq = kvc.shape[0] // batch
    G = NSA_KV_HEADS
    ncmp = CMP_PER_SEL * MAX_SEL_BLOCKS
    whole = lambda arr: pl.BlockSpec((1,) + arr.shape[1:], lambda s, b: (s,) + (0,) * (arr.ndim - 1))
    return pl.pallas_call(
        _compress_kernel,
        grid=(2, batch),
        in_specs=[pl.BlockSpec((seq, W_KV), lambda s, b: (b, s)), whole(pos), whole(wt), whole(wb), whole(w2)],
        out_specs=pl.BlockSpec((1, G, ncmp, HEAD_DIM), lambda s, b: (s, b, 0, 0)),
        out_shape=jax.ShapeDtypeStruct((2, batch * G, ncmp, HEAD_DIM), jnp.bfloat16),
        scratch_shapes=[pltpu.VMEM((seq, W_KV), jnp.float32)],
        compiler_params=pltpu.CompilerParams(dimension_semantics=("arbitrary", "arbitrary"),
                                             vmem_limit_bytes=VMEM_LIMIT),
        name="compress",
    )(kvc, pos, wt, wb, w2)


def _compress_weights(pos, w1, w2):
    bf16 = jnp.bfloat16
    pos2 = jnp.broadcast_to(jnp.tile(pos, (1, 1, 2))[:, :, None, :], (2, CMP_LEN, BF16_ROWS, W_KV)).astype(bf16)
    w = w1.astype(bf16).reshape(2, 2, CMP_STRIDE, HEAD_DIM, CMP_HIDDEN)
    lead = ((0, 0),) * 3
    bd = (jnp.pad(w, lead + ((0, HEAD_DIM), (0, CMP_HIDDEN)))
          + jnp.pad(w, lead + ((HEAD_DIM, 0), (CMP_HIDDEN, 0))))
    bd = bd.reshape(2, 2, CMP_PAIRS, 2 * W_KV, 2 * CMP_HIDDEN)
    return pos2, bd[:, 0], bd[:, 1], w2.astype(bf16)


WIN_BLOCKS = -(-(SWA_WINDOW - 1) // BLOCK) + 1


def _nsa_attn_kernel(q_ref, kc_ref, vc_ref, et_ref, ks_ref, vst_ref, kw_ref, vwt_ref,
                     gate_ref, sza_ref, o_ref, s_a, s_b, p_a, p_b):
    f32, bf16 = jnp.float32, jnp.bfloat16
    qb = pl.program_id(1)
    G, R = NSA_KV_HEADS, NSA_Q_PER_KV
    grows = R * BLOCK
    rows = G * grows
    gcols = lambda g: slice(g * grows, (g + 1) * grows)
    heads_t = [q_ref[0, 0, h * HEAD_DIM:(h + 1) * HEAD_DIM, :] for h in range(NSA_HEADS)]
    q_cmp_t = jnp.concatenate(heads_t, axis=1)
    zero_t = jnp.zeros((HEAD_DIM, BLOCK), bf16)
    q_kv_t = jnp.concatenate([jnp.concatenate([qh, zero_t] if h < R else [zero_t, qh], axis=0)
                              for h, qh in enumerate(heads_t)], axis=1)
    t_1 = qb * BLOCK + lax.broadcasted_iota(jnp.int32, (1, BLOCK), 1)
    head_bias = lambda ok, reps: jnp.tile(jnp.where(ok, 0.0, NEG_INF), (1, reps))

    ncmp = kc_ref.shape[1]
    pos = lax.broadcasted_iota(jnp.int32, (ncmp, BLOCK), 0)
    cmp_end = (pos & (LANES - 1)) * SEL_BLOCK + (pos >> LANE_SHIFT) * CMP_STRIDE + (CMP_LEN - 1)
    cmp_bias = head_bias(cmp_end <= t_1, R)
    seen = jnp.tile(jnp.where(t_1 >= CMP_LEN - 1, 1.0, 0.0), (1, R))
    blk = lax.broadcasted_iota(jnp.int32, (LANES, BLOCK), 0)
    o_cmp, imp_sel = [], []
    for g in range(G):
        st = _dot(kc_ref[g], q_cmp_t[:, gcols(g)]) + cmp_bias
        mx = jnp.max(st, axis=0, keepdims=True)
        e = jnp.exp2(st - mx)
        den = jnp.maximum(jnp.sum(e, axis=0, keepdims=True), 1e-30)
        p = e * (seen / den)
        o_cmp.append(lax.dot_general(vc_ref[g], p.astype(bf16), _TN,
                                     preferred_element_type=f32))
        imp = p[:, 0:BLOCK]
        for r in range(1, R):
            imp = imp + p[:, r * BLOCK:(r + 1) * BLOCK]
        q4 = [imp[i * LANES:(i + 1) * LANES] for i in range(CMP_PER_SEL)]
        prev_last = jnp.where(blk == 0, 0.0, pltpu.roll(q4[3], 1, 0))
        imp_sel.append(prev_last + 2.0 * (q4[0] + q4[1] + q4[2]) + q4[3])

    span = WIN_BLOCKS * BLOCK
    first_blk = jnp.maximum(qb - (WIN_BLOCKS - 1), 0)
    win = pl.ds(pl.multiple_of(first_blk * BLOCK, BLOCK), span)
    dist = t_1 - (first_blk * BLOCK + lax.broadcasted_iota(jnp.int32, (span, BLOCK), 0))
    st = _dot(kw_ref[win, :], q_kv_t) + head_bias((dist >= 0) & (dist <= SWA_WINDOW - 1), G * R)
    mw = jnp.max(st, axis=0, keepdims=True)
    e = jnp.exp2(st - mw).astype(bf16)
    ones_win = jnp.ones((ONES_ROWS, span), bf16)
    o_win = []
    for g in range(G):
        r = _dot(jnp.concatenate([vwt_ref[0, g * HEAD_DIM:(g + 1) * HEAD_DIM, win], ones_win], axis=0),
                 e[:, gcols(g)])
        o_win.append(r[0:HEAD_DIM] * (1.0 / r[HEAD_DIM:HEAD_DIM + 1]))

    cur = (qb * BLOCK + lax.broadcasted_iota(jnp.int32, (LANES, BLOCK), 1)) >> SEL_SHIFT
    forced = (blk == 0) | (blk == cur) | (blk == cur - 1)
    blk_f = blk.astype(f32)
    bias = []
    for g in range(G):
        sc = jnp.where(forced, -2.0, jnp.where(blk <= cur, imp_sel[g], -1.0))
        for _ in range(SEL_TOP_N - N_FORCED):
            best = jnp.max(sc, axis=0, keepdims=True)
            first = jnp.min(jnp.where(sc == best, blk_f, float(LANES)), axis=0, keepdims=True)
            sc = jnp.where(blk_f == first, -2.0, sc)
        taken = (sc == -2.0) & (blk <= cur)
        bias.append(jnp.tile(jnp.where(taken, 0.0, NEG_INF).astype(bf16), (1, R)))

    qa_t = jnp.concatenate([jnp.concatenate(bias, axis=1), q_kv_t], axis=0)
    n = (qb * BLOCK) // KEY_TILE
    last = jnp.maximum(n - 1, 0)
    ones = jnp.ones((ONES_ROWS, KEY_TILE), bf16)

    def keys_aug(kt):
        tile = pl.ds(pl.multiple_of(kt * KEY_TILE, KEY_TILE), KEY_TILE)
        return jnp.concatenate([et_ref[tile, :], ks_ref[tile, :]], axis=1)

    def qk(kt, s_ref):
        st = _dot(keys_aug(kt), qa_t)
        s_ref[...] = st
        return jnp.max(st, axis=0, keepdims=True)

    def pv(kt, p_ref):
        tile = pl.ds(pl.multiple_of(kt * KEY_TILE, KEY_TILE), KEY_TILE)
        return tuple(_dot(jnp.concatenate([vst_ref[0, g * HEAD_DIM:(g + 1) * HEAD_DIM, tile], ones], axis=0),
                          p_ref[:, gcols(g)]) for g in range(G))

    def softmax(m_old, mx, s_ref, p_ref):
        m_new = jnp.maximum(m_old, mx)
        p_ref[...] = jnp.exp2(s_ref[...] - m_new).astype(bf16)
        return m_new, jnp.exp2(m_old - m_new)

    def accumulate(acc, alpha, weight, contrib):
        return tuple(alpha[:, gcols(g)] * acc[g] + weight * contrib[g] for g in range(G))

    key = n * KEY_TILE + lax.broadcasted_iota(jnp.int32, (KEY_TILE, BLOCK), 0)
    st = _dot(keys_aug(n), qa_t) + head_bias(key <= t_1, G * R)
    m0 = jnp.max(st, axis=0, keepdims=True)
    p_b[...] = jnp.exp2(st - m0).astype(bf16)
    mx0 = qk(0, s_a)
    acc0 = tuple(jnp.zeros((HEAD_DIM + ONES_ROWS, grows), f32) for _ in range(G))
    one = jnp.ones_like(m0)

    def body(i, carry):
        m, acc, alpha_prev, w_prev, kt_prev, mx = carry
        first, second = 2 * i, 2 * i + 1
        w_second = jnp.where(second < n, 1.0, 0.0)
        kt_second = jnp.minimum(second, last)
        acc = accumulate(acc, alpha_prev, w_prev, pv(kt_prev, p_b))
        m, alpha = softmax(m, mx, s_a, p_a)
        mx = qk(kt_second, s_b)
        acc = accumulate(acc, alpha, 1.0, pv(first, p_a))
        m, alpha = softmax(m, mx, s_b, p_b)
        mx = qk(jnp.minimum(second + 1, last), s_a)
        return m, acc, alpha, w_second, kt_second, mx

    init = (m0, acc0, one, jnp.float32(1.0), n, mx0)
    _, acc, alpha_prev, w_prev, kt_prev, _ = lax.fori_loop(0, (n + 1) // 2, body, init)
    acc = accumulate(acc, alpha_prev, w_prev, pv(kt_prev, p_b))
    o_slc = [a[0:HEAD_DIM] * (1.0 / a[HEAD_DIM:HEAD_DIM + 1]) for a in acc]

    for g in range(G):
        for r in range(R):
            h = g * R + r
            cols = slice(r * BLOCK, (r + 1) * BLOCK)
            gate_row = lambda branch: gate_ref[0, 0, branch * NSA_HEADS + h:branch * NSA_HEADS + h + 1, :]
            o_t = (gate_row(0) * o_cmp[g][:, cols] + gate_row(1) * o_slc[g][:, cols]
                   + gate_row(2) * o_win[g][:, cols])
            dims = slice(h * HEAD_DIM, (h + 1) * HEAD_DIM)
            o_ref[0, 0, dims, :] = (o_t * sza_ref[0, 0, dims, :].astype(f32)).astype(o_ref.dtype)


def _nsa_attn(qa, kcp, vcp, onehot, ks, vst, kw, vwt, gate, sza, batch):
    nqb = qa.shape[1]
    seq = nqb * BLOCK
    ncmp = kcp.shape[1]
    G = NSA_KV_HEADS
    rows = NSA_HEADS * BLOCK
    t_block = lambda wd: pl.BlockSpec((1, 1, wd, BLOCK), lambda b, j: (b, j, 0, 0))
    return pl.pallas_call(
        _nsa_attn_kernel,
        grid=(batch, nqb),
        in_specs=[t_block(W_NSA),
                  pl.BlockSpec((G, ncmp, HEAD_DIM), lambda b, j: (b, 0, 0)),
                  pl.BlockSpec((G, ncmp, HEAD_DIM), lambda b, j: (b, 0, 0)),
                  pl.BlockSpec((seq, LANES), lambda b, j: (0, 0)),
                  pl.BlockSpec((seq, W_KV), lambda b, j: (b, 0)),
                  pl.BlockSpec((1, W_KV, seq), lambda b, j: (b, 0, 0)),
                  pl.BlockSpec((seq, W_KV), lambda b, j: (b, 0)),
                  pl.BlockSpec((1, W_KV, seq), lambda b, j: (b, 0, 0)),
                  t_block(LANES),
                  t_block(W_NSA)],
        out_specs=t_block(W_NSA),
        out_shape=jax.ShapeDtypeStruct((batch, nqb, W_NSA, BLOCK), jnp.bfloat16),
        scratch_shapes=[pltpu.VMEM((KEY_TILE, rows), jnp.float32),
                        pltpu.VMEM((KEY_TILE, rows), jnp.float32),
                        pltpu.VMEM((KEY_TILE, rows), jnp.bfloat16),
                        pltpu.VMEM((KEY_TILE, rows), jnp.bfloat16)],
        compiler_params=pltpu.CompilerParams(dimension_semantics=("arbitrary", "arbitrary"),
                                             vmem_limit_bytes=VMEM_LIMIT),
        name="nsa_attn",
    )(qa, kcp, vcp, onehot, ks, vst, kw, vwt, gate, sza)


DIL_MAX = max(d for _, d in DIL_PATTERNS)
DIL_SUPER = BLOCK * DIL_MAX
DIL_UNITS = DIL_SUPER // BLOCK
HEAD_PAIR = 2 * HEAD_DIM
MIX_ROWS = 256


def _dil_mix_kernel(q_ref, kp_ref, kc_ref, vp_ref, vc_ref, z_ref, o_ref,
                    qf, kf, vf, num_scr, den_scr, max_scr, bias_scr):
    f32, bf16 = jnp.float32, jnp.bfloat16
    sb = pl.program_id(1)
    qf[...] = q_ref[...].astype(f32)
    kf[0:DIL_SUPER] = kp_ref[...].astype(f32)
    kf[DIL_SUPER:2 * DIL_SUPER] = kc_ref[...].astype(f32)
    vf[0:DIL_SUPER] = vp_ref[...].astype(f32)
    vf[DIL_SUPER:2 * DIL_SUPER] = vc_ref[...].astype(f32)

    row = lax.broadcasted_iota(jnp.int32, (2 * BLOCK, 2 * BLOCK), 0)
    col = lax.broadcasted_iota(jnp.int32, (2 * BLOCK, 2 * BLOCK), 1)
    dist = BLOCK + (row & (BLOCK - 1)) - col
    band = (dist >= 0) & (dist <= BLOCK)
    bias_scr[0] = jnp.where(band, 0.0, NEG_INF)
    bias_scr[1] = jnp.where(band & (col >= BLOCK), 0.0, NEG_INF)
    first_head = lax.broadcasted_iota(jnp.int32, (BLOCK, HEAD_PAIR), 1) < HEAD_DIM
    ones = jnp.ones((2 * BLOCK, HEAD_PAIR), bf16)

    for pat, (window, dil) in enumerate(DIL_PATTERNS):
        shift = dil.bit_length() - 1

        def unit(u, pat=pat, dil=dil, shift=shift):
            cls = u & (dil - 1)
            blk = u >> shift
            q_start = cls + blk * (BLOCK * dil)
            k_start = DIL_SUPER + q_start - BLOCK * dil
            q2 = qf[pl.ds(q_start, BLOCK, stride=dil), :]
            k2 = kf[pl.ds(k_start, 2 * BLOCK, stride=dil), :]
            v2 = vf[pl.ds(k_start, 2 * BLOCK, stride=dil), :]
            qm = jnp.concatenate([jnp.where(first_head, q2, 0.0),
                                  jnp.where(first_head, 0.0, q2)], axis=0).astype(bf16)
            s = _dot_nt(qm, k2.astype(bf16))
            no_prev = jnp.where((sb == 0) & (blk == 0), 1, 0)
            s = s + bias_scr[no_prev]
            m = jnp.max(s, axis=1, keepdims=True)
            e = jnp.exp2(s - m).astype(bf16)
            r = _dot(e, jnp.concatenate([v2.astype(bf16), ones], axis=1))
            mb = jnp.broadcast_to(m, (2 * BLOCK, HEAD_PAIR))
            out_rows = pl.ds(q_start, BLOCK, stride=dil)
            num_scr[pat, out_rows, :] = jnp.where(first_head, r[0:BLOCK, 0:HEAD_PAIR], r[BLOCK:, 0:HEAD_PAIR])
            den_scr[pat, out_rows, :] = jnp.where(first_head, r[0:BLOCK, HEAD_PAIR:], r[BLOCK:, HEAD_PAIR:])
            max_scr[pat, out_rows, :] = jnp.where(first_head, mb[0:BLOCK], mb[BLOCK:])

        for u in range(DIL_UNITS):
            unit(u)

    def mix(ci, carry):
        rows = pl.ds(pl.multiple_of(ci * MIX_ROWS, MIX_ROWS), MIX_ROWS)
        ms = [max_scr[p, rows, :] for p in range(len(DIL_PATTERNS))]
        mx = jnp.maximum(jnp.maximum(ms[0], ms[1]), ms[2])
        cs = [jnp.exp2(m - mx) for m in ms]
        num = cs[0] * num_scr[0, rows, :] + cs[1] * num_scr[1, rows, :] + cs[2] * num_scr[2, rows, :]
        den = cs[0] * den_scr[0, rows, :] + cs[1] * den_scr[1, rows, :] + cs[2] * den_scr[2, rows, :]
        o_ref[rows, :] = (num / den * z_ref[rows, :].astype(f32)).astype(o_ref.dtype)
        return carry

    lax.fori_loop(0, DIL_SUPER // MIX_ROWS, mix, 0)


def _dil_mix(qb, kb, vb, szb, batch):
    rows = qb.shape[0]
    nsb = rows // batch // DIL_SUPER
    cur = pl.BlockSpec((DIL_SUPER, HEAD_PAIR), lambda b, s, h: (b * nsb + s, h))
    prev = pl.BlockSpec((DIL_SUPER, HEAD_PAIR), lambda b, s, h: (b * nsb + jnp.maximum(s - 1, 0), h))
    f32 = jnp.float32
    return pl.pallas_call(
        _dil_mix_kernel,
        grid=(batch, nsb, W_DIL // HEAD_PAIR),
        in_specs=[cur, prev, cur, prev, cur, cur],
        out_specs=cur,
        out_shape=jax.ShapeDtypeStruct((rows, W_DIL), jnp.bfloat16),
        scratch_shapes=[pltpu.VMEM((DIL_SUPER, HEAD_PAIR), f32),
                        pltpu.VMEM((2 * DIL_SUPER, HEAD_PAIR), f32),
                        pltpu.VMEM((2 * DIL_SUPER, HEAD_PAIR), f32),
                        pltpu.VMEM((len(DIL_PATTERNS), DIL_SUPER, HEAD_PAIR), f32),
                        pltpu.VMEM((len(DIL_PATTERNS), DIL_SUPER, HEAD_PAIR), f32),
                        pltpu.VMEM((len(DIL_PATTERNS), DIL_SUPER, HEAD_PAIR), f32),
                        pltpu.VMEM((2, 2 * BLOCK, 2 * BLOCK), f32)],
        compiler_params=pltpu.CompilerParams(
            dimension_semantics=("arbitrary", "arbitrary", "arbitrary"),
            vmem_limit_bytes=VMEM_LIMIT),
        name="dil_mix",
    )(qb, kb, kb, vb, vb, szb)


def _out_proj_kernel(x_ref, ma_ref, mb_ref, w_ref, g_ref, out_ref):
    ma_t = jnp.concatenate([ma_ref[0, j] for j in range(ma_ref.shape[1])], axis=1)
    y = lax.dot_general(ma_t, w_ref[0:W_NSA, :], _TN, preferred_element_type=jnp.float32)
    y = y + _dot(mb_ref[...], w_ref[W_NSA:W_NSA + W_DIL, :])
    ms = jnp.mean(y * y, axis=-1, keepdims=True)
    out_ref[...] = x_ref[...] + y * lax.rsqrt(ms + RMS_EPS) * g_ref[...]


def _out_proj(x2, mixed_a_t, mixed_b, w, g):
    rows, d_model = x2.shape
    tm = OUT_ROWS
    per_batch = mixed_a_t.shape[1] * BLOCK // tm
    row_spec = lambda wd: pl.BlockSpec((tm, wd), lambda i: (i, 0))
    return pl.pallas_call(
        _out_proj_kernel,
        grid=(rows // tm,),
        in_specs=[row_spec(d_model),
                  pl.BlockSpec((1, tm // BLOCK, W_NSA, BLOCK), lambda i: (i // per_batch, i % per_batch, 0, 0)),
                  row_spec(W_DIL),
                  pl.BlockSpec(w.shape, lambda i: (0, 0)),
                  pl.BlockSpec((1, d_model), lambda i: (0, 0))],
        out_specs=row_spec(d_model),
        out_shape=jax.ShapeDtypeStruct((rows, d_model), jnp.float32),
        compiler_params=pltpu.CompilerParams(dimension_semantics=("arbitrary",),
                                             vmem_limit_bytes=VMEM_LIMIT),
        name="out_proj",
    )(x2, mixed_a_t, mixed_b, w, g)


def _rope_tables(positions):
    inv = 1.0 / (ROPE_THETA ** (jnp.arange(0, ROPE_DIMS, 2, dtype=jnp.float32) / ROPE_DIMS))
    ang = positions.astype(jnp.float32).reshape(-1)[:, None] * inv
    cos, sin = jnp.cos(ang), jnp.sin(ang)
    half = ROPE_DIMS // 2
    k = jnp.arange(LANES) % HEAD_DIM
    freq = jnp.arange(half)[:, None]
    first = (k[None, :] == freq).astype(jnp.float32)
    second = (k[None, :] == freq + half).astype(jnp.float32)
    spread = lambda t, m: jnp.dot(t, m, precision=lax.Precision.HIGHEST)
    unrotated = (k >= ROPE_DIMS).astype(jnp.float32)[None, :]
    return spread(cos, first + second) + unrotated, spread(-sin, first), spread(sin, second)


def _in_proj_weights(w_in):
    scale = HEAD_DIM ** -0.5 * LOG2_E
    bf16 = jnp.bfloat16
    n_head, n_gate = W_NSA + 6 * W_KV, 3 * NSA_HEADS
    ones = lambda n: jnp.ones((n,), jnp.float32)
    head_scale = jnp.concatenate([scale * ones(W_NSA), ones(6 * W_KV)])
    tail_scale = jnp.concatenate([ones(W_NSA), scale * ones(W_DIL), ones(3 * W_DIL)])
    w_head = (w_in[:, :n_head] * head_scale).astype(bf16)
    w_gate = jnp.pad(w_in[:, n_head:n_head + n_gate], ((0, 0), (0, LANES - n_gate))).astype(bf16)
    w_tail = (w_in[:, n_head + n_gate:] * tail_scale).astype(bf16)
    return w_head, w_gate, w_tail


def kernel(x, positions, pre_norm_g, w_in, cmp_k_pos, cmp_k_w1, cmp_k_w2,
           cmp_v_pos, cmp_v_w1, cmp_v_w2, w_out, post_norm_g):
    B, S, d_model = x.shape
    depth = w_in.shape[0]
    n_sel = S // SEL_BLOCK
    assert S % KEY_TILE == 0 and n_sel == MAX_SEL_BLOCKS and S >= WIN_BLOCKS * BLOCK
    assert S % DIL_SUPER == 0 and all(win // dil == BLOCK for win, dil in DIL_PATTERNS)
    bf16 = jnp.bfloat16

    rope_c, rope_a, rope_b = _rope_tables(positions)
    onehot = (jnp.arange(S)[:, None] // SEL_BLOCK == jnp.arange(LANES)[None, :]).astype(bf16)
    x2 = x.reshape(B * S, d_model)

    for layer in range(depth):
        qa, kvc, ks, kw, vst, vwt, gate, sza, qb_, kb_, vb_, szb = _in_proj(
            x2, pre_norm_g[layer][None, :], _in_proj_weights(w_in[layer]), rope_c, rope_a, rope_b, B)

        kvc = _compress(kvc, *_compress_weights(jnp.stack([cmp_k_pos[layer], cmp_v_pos[layer]]),
                                                jnp.stack([cmp_k_w1[layer], cmp_v_w1[layer]]),
                                                jnp.stack([cmp_k_w2[layer], cmp_v_w2[layer]])), B)
        mixed_a = _nsa_attn(qa, kvc[0], kvc[1], onehot, ks, vst, kw, vwt, gate, sza, B)
        mixed_b = _dil_mix(qb_, kb_, vb_, szb, B)
        x2 = _out_proj(x2, mixed_a, mixed_b, w_out[layer].astype(bf16), post_norm_g[layer][None, :])
    return x2.reshape(B, S, d_model)
```

```python
import jax
import jax.numpy as jnp
from jax import lax
from jax.experimental import pallas as pl
from jax.experimental.pallas import tpu as pltpu

HEAD_DIM = 64
NSA_HEADS = 8
NSA_KV_HEADS = 2
NSA_Q_PER_KV = NSA_HEADS // NSA_KV_HEADS
DIL_HEADS = 8
W_NSA = NSA_HEADS * HEAD_DIM
W_KV = NSA_KV_HEADS * HEAD_DIM
W_DIL = DIL_HEADS * HEAD_DIM
CMP_LEN = 32
CMP_STRIDE = 16
CMP_HIDDEN = 256
SEL_BLOCK = 64
SEL_TOP_N = 16
N_FORCED = 3
SWA_WINDOW = 512
DIL_PATTERNS = ((128, 1), (512, 4), (2048, 16))
BLOCK = 128
ROPE_THETA = 500000.0
ROPE_DIMS = HEAD_DIM // 4
RMS_EPS = 1e-6
NEG_INF = -1e30
FORCE_SCORE = 1e4
LOG2_E = 1.4426950408889634

LANES = 128
VMEM_LIMIT = 56 * 1024 * 1024
MAX_SEL_BLOCKS = LANES
CMP_PER_SEL = SEL_BLOCK // CMP_STRIDE
KEY_TILE = 512
PROJ_ROWS = 512
OUT_ROWS = 1024
BF16_ROWS = 16
ONES_ROWS = BF16_ROWS
SEL_SHIFT = SEL_BLOCK.bit_length() - 1
LANE_SHIFT = LANES.bit_length() - 1

_NT = (((1,), (1,)), ((), ()))
_TN = (((0,), (0,)), ((), ()))


def _dot(a, b):
    return jnp.dot(a, b, preferred_element_type=jnp.float32)


def _dot_nt(a, b):
    return lax.dot_general(a, b, _NT, preferred_element_type=jnp.float32)


def _sigmoid(x):
    return 1.0 / (1.0 + jnp.exp(-x))


def _rope(x, c, a, b):
    width = x.shape[1]
    reps = width // LANES
    ct = jnp.tile(c, (1, reps))
    at = jnp.tile(a, (1, reps))
    bt = jnp.tile(b, (1, reps))
    half = ROPE_DIMS // 2
    return x * ct + pltpu.roll(x, width - half, 1) * at + pltpu.roll(x, half, 1) * bt


def _in_proj_kernel(x_ref, g_ref, w_head_ref, w_gate_ref, w_tail_ref, c_ref, a_ref, b_ref,
                    qa_ref, kvc_ref, ks_ref, kw_ref, vst_ref, vwt_ref, gate_ref, sza_ref,
                    qb_ref, kb_ref, vb_ref, szb_ref):
    x = x_ref[...]
    ms = jnp.mean(x * x, axis=-1, keepdims=True)
    h = (x * lax.rsqrt(ms + RMS_EPS) * g_ref[...]).astype(jnp.bfloat16)
    c = c_ref[...]
    a = a_ref[...]
    b = b_ref[...]

    def columns_of(w_ref):
        off = 0

        def take(width):
            nonlocal off
            off += width
            return _dot(h, w_ref[:, off - width:off])
        return take

    head, gates, proj = columns_of(w_head_ref), columns_of(w_gate_ref), columns_of(w_tail_ref)
    qa_ref[0] = _rope(head(W_NSA), c, a, b).T.astype(qa_ref.dtype)
    kva = head(6 * W_KV)
    part = lambda i: kva[:, i * W_KV:(i + 1) * W_KV]
    kvc_ref[...] = jnp.concatenate([_rope(part(0), c, a, b), part(1)], axis=1).astype(kvc_ref.dtype)
    ks_ref[...] = _rope(part(2), c, a, b).astype(ks_ref.dtype)
    kw_ref[...] = _rope(part(4), c, a, b).astype(kw_ref.dtype)
    vst_ref[0] = part(3).T.astype(vst_ref.dtype)
    vwt_ref[0] = part(5).T.astype(vwt_ref.dtype)
    gate_ref[0] = _sigmoid(gates(LANES)).T
    za = proj(W_NSA)
    sza_ref[0] = (za * _sigmoid(za)).T.astype(sza_ref.dtype)
    qb_ref[...] = _rope(proj(W_DIL), c, a, b).astype(qb_ref.dtype)
    kb_ref[...] = _rope(proj(W_DIL), c, a, b).astype(kb_ref.dtype)
    vb_ref[...] = proj(W_DIL).astype(vb_ref.dtype)
    zb = proj(W_DIL)
    szb_ref[...] = (zb * _sigmoid(zb)).astype(szb_ref.dtype)


def _in_proj(x2, g, weights, c, a, b, batch):
    rows, d_model = x2.shape
    tm = PROJ_ROWS
    seq = rows // batch
    per_batch = seq // tm
    bf16 = jnp.bfloat16
    row_spec = lambda wd: pl.BlockSpec((tm, wd), lambda i: (i, 0))
    row_out = lambda wd: (row_spec(wd), jax.ShapeDtypeStruct((rows, wd), bf16))
    t_out = lambda wd, dt=bf16: (pl.BlockSpec((1, wd, tm), lambda i: (i // per_batch, 0, i % per_batch)),
                                 jax.ShapeDtypeStruct((batch, wd, seq), dt))
    outs = [t_out(W_NSA), row_out(2 * W_KV), row_out(W_KV), row_out(W_KV), t_out(W_KV), t_out(W_KV),
            t_out(LANES, jnp.float32), t_out(W_NSA),
            row_out(W_DIL), row_out(W_DIL), row_out(W_DIL), row_out(W_DIL)]
    return pl.pallas_call(
        _in_proj_kernel,
        grid=(rows // tm,),
        in_specs=[row_spec(d_model),
                  pl.BlockSpec((1, d_model), lambda i: (0, 0))]
                 + [pl.BlockSpec(w.shape, lambda i: (0, 0)) for w in weights]
                 + [row_spec(LANES), row_spec(LANES), row_spec(LANES)],
        out_specs=[o[0] for o in outs],
        out_shape=[o[1] for o in outs],
        compiler_params=pltpu.CompilerParams(dimension_semantics=("arbitrary",),
                                             vmem_limit_bytes=VMEM_LIMIT),
        name="in_proj",
    )(x2, g, *weights, c, a, b)


CMP_PAIRS = CMP_STRIDE // 2


def _compress_kernel(x_ref, pos_ref, wt_ref, wb_ref, w2_ref, o_ref, stage):
    f32, bf16 = jnp.float32, jnp.bfloat16
    stage[...] = x_ref[...].astype(f32)
    nj = MAX_SEL_BLOCKS
    hidden = 2 * CMP_HIDDEN

    def offset_rows(l):
        return jnp.concatenate([stage[pl.ds(CMP_STRIDE * m + l, nj, stride=SEL_BLOCK), :]
                                for m in range(CMP_PER_SEL)], axis=0)

    top = jnp.zeros((CMP_PER_SEL * nj, hidden), f32)
    bot = jnp.zeros((CMP_PER_SEL * nj, hidden), f32)
    bias = jnp.zeros((BF16_ROWS, hidden), f32)
    for i in range(CMP_PAIRS):
        x = jnp.concatenate([offset_rows(2 * i), offset_rows(2 * i + 1)], axis=1).astype(bf16)
        top = top + _dot(x, wt_ref[0, i])
        bot = bot + _dot(x, wb_ref[0, i])
        p_top = jnp.concatenate([pos_ref[0, 2 * i], pos_ref[0, 2 * i + 1]], axis=1)
        p_bot = jnp.concatenate([pos_ref[0, CMP_STRIDE + 2 * i], pos_ref[0, CMP_STRIDE + 2 * i + 1]], axis=1)
        bias = bias + _dot(p_top, wt_ref[0, i]) + _dot(p_bot, wb_ref[0, i])
    nxt = jnp.concatenate([bot[nj:], pltpu.roll(bot[0:nj], nj - 1, 0)], axis=0)
    hid = top + nxt + bias[0:1, :]
    act = (hid * _sigmoid(hid)).astype(bf16)
    for g in range(NSA_KV_HEADS):
        o_ref[0, g] = _dot(act[:, g * CMP_HIDDEN:(g + 1) * CMP_HIDDEN], w2_ref[0]).astype(o_ref.dtype)


def _compress(kvc, pos, wt, wb, w2, batch):
    seq = kvc.shape[0] // batch
    G = NSA_KV_HEADS
    ncmp = CMP_PER_SEL * MAX_SEL_BLOCKS
    whole = lambda arr: pl.BlockSpec((1,) + arr.shape[1:], lambda s, b: (s,) + (0,) * (arr.ndim - 1))
    return pl.pallas_call(
        _compress_kernel,
        grid=(2, batch),
        in_specs=[pl.BlockSpec((seq, W_KV), lambda s, b: (b, s)), whole(pos), whole(wt), whole(wb), whole(w2)],
        out_specs=pl.BlockSpec((1, G, ncmp, HEAD_DIM), lambda s, b: (s, b, 0, 0)),
        out_shape=jax.ShapeDtypeStruct((2, batch * G, ncmp, HEAD_DIM), jnp.bfloat16),
        scratch_shapes=[pltpu.VMEM((seq, W_KV), jnp.float32)],
        compiler_params=pltpu.CompilerParams(dimension_semantics=("arbitrary", "arbitrary"),
                                             vmem_limit_bytes=VMEM_LIMIT),
        name="compress",
    )(kvc, pos, wt, wb, w2)


def _compress_weights(pos, w1, w2):
    bf16 = jnp.bfloat16
    pos2 = jnp.broadcast_to(jnp.tile(pos, (1, 1, 2))[:, :, None, :], (2, CMP_LEN, BF16_ROWS, W_KV)).astype(bf16)
    w = w1.astype(bf16).reshape(2, 2, CMP_STRIDE, HEAD_DIM, CMP_HIDDEN)
    lead = ((0, 0),) * 3
    bd = (jnp.pad(w, lead + ((0, HEAD_DIM), (0, CMP_HIDDEN)))
          + jnp.pad(w, lead + ((HEAD_DIM, 0), (CMP_HIDDEN, 0))))
    bd = bd.reshape(2, 2, CMP_PAIRS, 2 * W_KV, 2 * CMP_HIDDEN)
    return pos2, bd[:, 0], bd[:, 1], w2.astype(bf16)


WIN_BLOCKS = -(-(SWA_WINDOW - 1) // BLOCK) + 1


PAIR = 2


def _nsa_attn_kernel(q_ref, kc_ref, vc_ref, et_ref, ks_ref, vst_ref, kw_ref, vwt_ref,
                     gate_ref, sza_ref, o_ref, s_a, s_b, p_a, p_b):
    f32, bf16 = jnp.float32, jnp.bfloat16
    pair = pl.program_id(1)
    G, R = NSA_KV_HEADS, NSA_Q_PER_KV
    grows = R * BLOCK
    brows = G * grows
    rows = PAIR * brows
    gcols = lambda g: slice(g * grows, (g + 1) * grows)
    bgcols = lambda blk, g: slice(blk * brows + g * grows, blk * brows + (g + 1) * grows)
    qcols = lambda blk: slice(blk * BLOCK, (blk + 1) * BLOCK)
    head_bias = lambda ok, reps: jnp.tile(jnp.where(ok, 0.0, NEG_INF), (1, reps))
    zero_t = jnp.zeros((HEAD_DIM, BLOCK), bf16)
    ncmp = kc_ref.shape[1]
    pos = lax.broadcasted_iota(jnp.int32, (ncmp, BLOCK), 0)
    cmp_end = (pos & (LANES - 1)) * SEL_BLOCK + (pos >> LANE_SHIFT) * CMP_STRIDE + (CMP_LEN - 1)
    blk_id = lax.broadcasted_iota(jnp.int32, (LANES, BLOCK), 0)
    blk_f = blk_id.astype(f32)
    span = WIN_BLOCKS * BLOCK
    ones_win = jnp.ones((ONES_ROWS, span), bf16)

    q_kv_t, t_1, o_cmp, o_win, bias = [], [], [], [], []
    for b in range(PAIR):
        qb = pair * PAIR + b
        heads_t = [q_ref[0, h * HEAD_DIM:(h + 1) * HEAD_DIM, qcols(b)] for h in range(NSA_HEADS)]
        q_cmp_t = jnp.concatenate(heads_t, axis=1)
        q_kv = jnp.concatenate([jnp.concatenate([qh, zero_t] if h < R else [zero_t, qh], axis=0)
                                for h, qh in enumerate(heads_t)], axis=1)
        q_kv_t.append(q_kv)
        t_b = qb * BLOCK + lax.broadcasted_iota(jnp.int32, (1, BLOCK), 1)
        t_1.append(t_b)

        cmp_bias = head_bias(cmp_end <= t_b, R)
        seen = jnp.tile(jnp.where(t_b >= CMP_LEN - 1, 1.0, 0.0), (1, R))
        o_cmp_b, imp_sel = [], []
        for g in range(G):
            st = _dot(kc_ref[g], q_cmp_t[:, gcols(g)]) + cmp_bias
            mx = jnp.max(st, axis=0, keepdims=True)
            e = jnp.exp2(st - mx)
            den = jnp.maximum(jnp.sum(e, axis=0, keepdims=True), 1e-30)
            p = e * (seen / den)
            o_cmp_b.append(lax.dot_general(vc_ref[g], p.astype(bf16), _TN,
                                           preferred_element_type=f32))
            imp = p[:, 0:BLOCK]
            for r in range(1, R):
                imp = imp + p[:, r * BLOCK:(r + 1) * BLOCK]
            q4 = [imp[i * LANES:(i + 1) * LANES] for i in range(CMP_PER_SEL)]
            prev_last = jnp.where(blk_id == 0, 0.0, pltpu.roll(q4[3], 1, 0))
            imp_sel.append(prev_last + 2.0 * (q4[0] + q4[1] + q4[2]) + q4[3])
        o_cmp.append(o_cmp_b)

        first_blk = jnp.maximum(qb - (WIN_BLOCKS - 1), 0)
        win = pl.ds(pl.multiple_of(first_blk * BLOCK, BLOCK), span)
        dist = t_b - (first_blk * BLOCK + lax.broadcasted_iota(jnp.int32, (span, BLOCK), 0))
        st = _dot(kw_ref[win, :], q_kv) + head_bias((dist >= 0) & (dist <= SWA_WINDOW - 1), G * R)
        mw = jnp.max(st, axis=0, keepdims=True)
        e = jnp.exp2(st - mw).astype(bf16)
        o_win_b = []
        for g in range(G):
            r = _dot(jnp.concatenate([vwt_ref[0, g * HEAD_DIM:(g + 1) * HEAD_DIM, win], ones_win], axis=0),
                     e[:, gcols(g)])
            o_win_b.append(r[0:HEAD_DIM] * (1.0 / r[HEAD_DIM:HEAD_DIM + 1]))
        o_win.append(o_win_b)

        cur = (qb * BLOCK + lax.broadcasted_iota(jnp.int32, (LANES, BLOCK), 1)) >> SEL_SHIFT
        forced = (blk_id == 0) | (blk_id == cur) | (blk_id == cur - 1)
        for g in range(G):
            sc = jnp.where(forced, -2.0, jnp.where(blk_id <= cur, imp_sel[g], -1.0))
            for _ in range(SEL_TOP_N - N_FORCED):
                best = jnp.max(sc, axis=0, keepdims=True)
                first = jnp.min(jnp.where(sc == best, blk_f, float(LANES)), axis=0, keepdims=True)
                sc = jnp.where(blk_f == first, -2.0, sc)
            taken = (sc == -2.0) & (blk_id <= cur)
            bias.append(jnp.tile(jnp.where(taken, 0.0, NEG_INF).astype(bf16), (1, R)))

    qa_t = jnp.concatenate([jnp.concatenate(bias, axis=1), jnp.concatenate(q_kv_t, axis=1)], axis=0)
    n = (pair * PAIR * BLOCK) // KEY_TILE
    last = jnp.maximum(n - 1, 0)
    ones = jnp.ones((ONES_ROWS, KEY_TILE), bf16)
    parts = [(b, g) for b in range(PAIR) for g in range(G)]

    def keys_aug(kt):
        tile = pl.ds(pl.multiple_of(kt * KEY_TILE, KEY_TILE), KEY_TILE)
        return jnp.concatenate([et_ref[tile, :], ks_ref[tile, :]], axis=1)

    def qk(kt, s_ref):
        st = _dot(keys_aug(kt), qa_t)
        s_ref[...] = st
        return jnp.max(st, axis=0, keepdims=True)

    def pv(kt, p_ref):
        tile = pl.ds(pl.multiple_of(kt * KEY_TILE, KEY_TILE), KEY_TILE)
        return tuple(_dot(jnp.concatenate([vst_ref[0, g * HEAD_DIM:(g + 1) * HEAD_DIM, tile], ones], axis=0),
                          p_ref[:, bgcols(b, g)]) for b, g in parts)

    def softmax(m_old, mx, s_ref, p_ref):
        m_new = jnp.maximum(m_old, mx)
        p_ref[...] = jnp.exp2(s_ref[...] - m_new).astype(bf16)
        return m_new, jnp.exp2(m_old - m_new)

    def accumulate(acc, alpha, weight, contrib):
        return tuple(alpha[:, bgcols(b, g)] * acc[i] + weight * contrib[i] for i, (b, g) in enumerate(parts))

    key = n * KEY_TILE + lax.broadcasted_iota(jnp.int32, (KEY_TILE, BLOCK), 0)
    causal = jnp.concatenate([head_bias(key <= t_1[b], G * R) for b in range(PAIR)], axis=1)
    st = _dot(keys_aug(n), qa_t) + causal
    m0 = jnp.max(st, axis=0, keepdims=True)
    p_b[...] = jnp.exp2(st - m0).astype(bf16)
    mx0 = qk(0, s_a)
    acc0 = tuple(jnp.zeros((HEAD_DIM + ONES_ROWS, grows), f32) for _ in parts)
    one = jnp.ones_like(m0)

    def body(i, carry):
        m, acc, alpha_prev, w_prev, kt_prev, mx = carry
        first, second = 2 * i, 2 * i + 1
        w_second = jnp.where(second < n, 1.0, 0.0)
        kt_second = jnp.minimum(second, last)
        acc = accumulate(acc, alpha_prev, w_prev, pv(kt_prev, p_b))
        m, alpha = softmax(m, mx, s_a, p_a)
        mx = qk(kt_second, s_b)
        acc = accumulate(acc, alpha, 1.0, pv(first, p_a))
        m, alpha = softmax(m, mx, s_b, p_b)
        mx = qk(jnp.minimum(second + 1, last), s_a)
        return m, acc, alpha, w_second, kt_second, mx

    init = (m0, acc0, one, jnp.float32(1.0), n, mx0)
    _, acc, alpha_prev, w_prev, kt_prev, _ = lax.fori_loop(0, (n + 1) // 2, body, init)
    acc = accumulate(acc, alpha_prev, w_prev, pv(kt_prev, p_b))
    o_slc = [a[0:HEAD_DIM] * (1.0 / a[HEAD_DIM:HEAD_DIM + 1]) for a in acc]

    for i, (b, g) in enumerate(parts):
        for r in range(R):
            h = g * R + r
            cols = slice(r * BLOCK, (r + 1) * BLOCK)
            gate_row = lambda branch: gate_ref[0, branch * NSA_HEADS + h:branch * NSA_HEADS + h + 1, qcols(b)]
            o_t = (gate_row(0) * o_cmp[b][g][:, cols] + gate_row(1) * o_slc[i][:, cols]
                   + gate_row(2) * o_win[b][g][:, cols])
            dims = slice(h * HEAD_DIM, (h + 1) * HEAD_DIM)
            o_ref[0, dims, qcols(b)] = (o_t * sza_ref[0, dims, qcols(b)].astype(f32)).astype(o_ref.dtype)


def _nsa_attn(qa, kcp, vcp, onehot, ks, vst, kw, vwt, gate, sza, batch):
    seq = qa.shape[2]
    npairs = seq // (PAIR * BLOCK)
    ncmp = kcp.shape[1]
    G = NSA_KV_HEADS
    rows = PAIR * NSA_HEADS * BLOCK
    t_block = lambda wd: pl.BlockSpec((1, wd, PAIR * BLOCK), lambda b, j: (b, 0, j))
    return pl.pallas_call(
        _nsa_attn_kernel,
        grid=(batch, npairs),
        in_specs=[t_block(W_NSA),
                  pl.BlockSpec((G, ncmp, HEAD_DIM), lambda b, j: (b, 0, 0)),
                  pl.BlockSpec((G, ncmp, HEAD_DIM), lambda b, j: (b, 0, 0)),
                  pl.BlockSpec((seq, LANES), lambda b, j: (0, 0)),
                  pl.BlockSpec((seq, W_KV), lambda b, j: (b, 0)),
                  pl.BlockSpec((1, W_KV, seq), lambda b, j: (b, 0, 0)),
                  pl.BlockSpec((seq, W_KV), lambda b, j: (b, 0)),
                  pl.BlockSpec((1, W_KV, seq), lambda b, j: (b, 0, 0)),
                  t_block(LANES),
                  t_block(W_NSA)],
        out_specs=t_block(W_NSA),
        out_shape=jax.ShapeDtypeStruct((batch, W_NSA, seq), jnp.bfloat16),
        scratch_shapes=[pltpu.VMEM((KEY_TILE, rows), jnp.float32),
                        pltpu.VMEM((KEY_TILE, rows), jnp.float32),
                        pltpu.VMEM((KEY_TILE, rows), jnp.bfloat16),
                        pltpu.VMEM((KEY_TILE, rows), jnp.bfloat16)],
        compiler_params=pltpu.CompilerParams(dimension_semantics=("arbitrary", "arbitrary"),
                                             vmem_limit_bytes=VMEM_LIMIT),
        name="nsa_attn",
    )(qa, kcp, vcp, onehot, ks, vst, kw, vwt, gate, sza)


DIL_MAX = max(d for _, d in DIL_PATTERNS)
DIL_SUPER = BLOCK * DIL_MAX
DIL_UNITS = DIL_SUPER // BLOCK
HEAD_PAIR = 2 * HEAD_DIM
MIX_ROWS = 256


def _dil_mix_kernel(q_ref, kp_ref, kc_ref, vp_ref, vc_ref, z_ref, o_ref,
                    qf, kf, vf, num_scr, den_scr, max_scr, bias_scr):
    f32, bf16 = jnp.float32, jnp.bfloat16
    sb = pl.program_id(1)
    qf[...] = q_ref[...].astype(f32)
    kf[0:DIL_SUPER] = kp_ref[...].astype(f32)
    kf[DIL_SUPER:2 * DIL_SUPER] = kc_ref[...].astype(f32)
    vf[0:DIL_SUPER] = vp_ref[...].astype(f32)
    vf[DIL_SUPER:2 * DIL_SUPER] = vc_ref[...].astype(f32)

    row = lax.broadcasted_iota(jnp.int32, (2 * BLOCK, 2 * BLOCK), 0)
    col = lax.broadcasted_iota(jnp.int32, (2 * BLOCK, 2 * BLOCK), 1)
    dist = BLOCK + (row & (BLOCK - 1)) - col
    band = (dist >= 0) & (dist <= BLOCK)
    bias_scr[0] = jnp.where(band, 0.0, NEG_INF)
    bias_scr[1] = jnp.where(band & (col >= BLOCK), 0.0, NEG_INF)
    first_head = lax.broadcasted_iota(jnp.int32, (BLOCK, HEAD_PAIR), 1) < HEAD_DIM
    ones = jnp.ones((2 * BLOCK, HEAD_PAIR), bf16)

    for pat, (window, dil) in enumerate(DIL_PATTERNS):
        shift = dil.bit_length() - 1

        def unit(u, pat=pat, dil=dil, shift=shift):
            cls = u & (dil - 1)
            blk = u >> shift
            q_start = cls + blk * (BLOCK * dil)
            k_start = DIL_SUPER + q_start - BLOCK * dil
            q2 = qf[pl.ds(q_start, BLOCK, stride=dil), :]
            k2 = kf[pl.ds(k_start, 2 * BLOCK, stride=dil), :]
            v2 = vf[pl.ds(k_start, 2 * BLOCK, stride=dil), :]
            qm = jnp.concatenate([jnp.where(first_head, q2, 0.0),
                                  jnp.where(first_head, 0.0, q2)], axis=0).astype(bf16)
            s = _dot_nt(qm, k2.astype(bf16))
            no_prev = jnp.where((sb == 0) & (blk == 0), 1, 0)
            s = s + bias_scr[no_prev]
            m = jnp.max(s, axis=1, keepdims=True)
            e = jnp.exp2(s - m).astype(bf16)
            r = _dot(e, jnp.concatenate([v2.astype(bf16), ones], axis=1))
            mb = jnp.broadcast_to(m, (2 * BLOCK, HEAD_PAIR))
            out_rows = pl.ds(q_start, BLOCK, stride=dil)
            num_scr[pat, out_rows, :] = jnp.where(first_head, r[0:BLOCK, 0:HEAD_PAIR], r[BLOCK:, 0:HEAD_PAIR])
            den_scr[pat, out_rows, :] = jnp.where(first_head, r[0:BLOCK, HEAD_PAIR:], r[BLOCK:, HEAD_PAIR:])
            max_scr[pat, out_rows, :] = jnp.where(first_head, mb[0:BLOCK], mb[BLOCK:])

        for u in range(DIL_UNITS):
            unit(u)

    def mix(ci, carry):
        rows = pl.ds(pl.multiple_of(ci * MIX_ROWS, MIX_ROWS), MIX_ROWS)
        ms = [max_scr[p, rows, :] for p in range(len(DIL_PATTERNS))]
        mx = jnp.maximum(jnp.maximum(ms[0], ms[1]), ms[2])
        cs = [jnp.exp2(m - mx) for m in ms]
        num = cs[0] * num_scr[0, rows, :] + cs[1] * num_scr[1, rows, :] + cs[2] * num_scr[2, rows, :]
        den = cs[0] * den_scr[0, rows, :] + cs[1] * den_scr[1, rows, :] + cs[2] * den_scr[2, rows, :]
        o_ref[rows, :] = (num / den * z_ref[rows, :].astype(f32)).astype(o_ref.dtype)
        return carry

    lax.fori_loop(0, DIL_SUPER // MIX_ROWS, mix, 0)


def _dil_mix(qb, kb, vb, szb, batch):
    rows = qb.shape[0]
    nsb = rows // batch // DIL_SUPER
    cur = pl.BlockSpec((DIL_SUPER, HEAD_PAIR), lambda b, s, h: (b * nsb + s, h))
    prev = pl.BlockSpec((DIL_SUPER, HEAD_PAIR), lambda b, s, h: (b * nsb + jnp.maximum(s - 1, 0), h))
    f32 = jnp.float32
    return pl.pallas_call(
        _dil_mix_kernel,
        grid=(batch, nsb, W_DIL // HEAD_PAIR),
        in_specs=[cur, prev, cur, prev, cur, cur],
        out_specs=cur,
        out_shape=jax.ShapeDtypeStruct((rows, W_DIL), jnp.bfloat16),
        scratch_shapes=[pltpu.VMEM((DIL_SUPER, HEAD_PAIR), f32),
                        pltpu.VMEM((2 * DIL_SUPER, HEAD_PAIR), f32),
                        pltpu.VMEM((2 * DIL_SUPER, HEAD_PAIR), f32),
                        pltpu.VMEM((len(DIL_PATTERNS), DIL_SUPER, HEAD_PAIR), f32),
                        pltpu.VMEM((len(DIL_PATTERNS), DIL_SUPER, HEAD_PAIR), f32),
                        pltpu.VMEM((len(DIL_PATTERNS), DIL_SUPER, HEAD_PAIR), f32),
                        pltpu.VMEM((2, 2 * BLOCK, 2 * BLOCK), f32)],
        compiler_params=pltpu.CompilerParams(
            dimension_semantics=("arbitrary", "arbitrary", "arbitrary"),
            vmem_limit_bytes=VMEM_LIMIT),
        name="dil_mix",
    )(qb, kb, kb, vb, vb, szb)


def _out_proj_kernel(x_ref, ma_ref, mb_ref, w_ref, g_ref, out_ref):
    y = lax.dot_general(ma_ref[0], w_ref[0:W_NSA, :], _TN, preferred_element_type=jnp.float32)
    y = y + _dot(mb_ref[...], w_ref[W_NSA:W_NSA + W_DIL, :])
    ms = jnp.mean(y * y, axis=-1, keepdims=True)
    out_ref[...] = x_ref[...] + y * lax.rsqrt(ms + RMS_EPS) * g_ref[...]


def _out_proj(x2, mixed_a_t, mixed_b, w, g):
    rows, d_model = x2.shape
    tm = OUT_ROWS
    per_batch = mixed_a_t.shape[2] // tm
    row_spec = lambda wd: pl.BlockSpec((tm, wd), lambda i: (i, 0))
    return pl.pallas_call(
        _out_proj_kernel,
        grid=(rows // tm,),
        in_specs=[row_spec(d_model),
                  pl.BlockSpec((1, W_NSA, tm), lambda i: (i // per_batch, 0, i % per_batch)),
                  row_spec(W_DIL),
                  pl.BlockSpec(w.shape, lambda i: (0, 0)),
                  pl.BlockSpec((1, d_model), lambda i: (0, 0))],
        out_specs=row_spec(d_model),
        out_shape=jax.ShapeDtypeStruct((rows, d_model), jnp.float32),
        compiler_params=pltpu.CompilerParams(dimension_semantics=("arbitrary",),
                                             vmem_limit_bytes=VMEM_LIMIT),
        name="out_proj",
    )(x2, mixed_a_t, mixed_b, w, g)


def _rope_tables(positions):
    inv = 1.0 / (ROPE_THETA ** (jnp.arange(0, ROPE_DIMS, 2, dtype=jnp.float32) / ROPE_DIMS))
    ang = positions.astype(jnp.float32).reshape(-1)[:, None] * inv
    cos, sin = jnp.cos(ang), jnp.sin(ang)
    half = ROPE_DIMS // 2
    k = jnp.arange(LANES) % HEAD_DIM
    freq = jnp.arange(half)[:, None]
    first = (k[None, :] == freq).astype(jnp.float32)
    second = (k[None, :] == freq + half).astype(jnp.float32)
    spread = lambda t, m: jnp.dot(t, m, precision=lax.Precision.HIGHEST)
    unrotated = (k >= ROPE_DIMS).astype(jnp.float32)[None, :]
    return spread(cos, first + second) + unrotated, spread(-sin, first), spread(sin, second)


def _in_proj_weights(w_in):
    scale = HEAD_DIM ** -0.5 * LOG2_E
    bf16 = jnp.bfloat16
    n_head, n_gate = W_NSA + 6 * W_KV, 3 * NSA_HEADS
    ones = lambda n: jnp.ones((n,), jnp.float32)
    head_scale = jnp.concatenate([scale * ones(W_NSA), ones(6 * W_KV)])
    tail_scale = jnp.concatenate([ones(W_NSA), scale * ones(W_DIL), ones(3 * W_DIL)])
    w_head = (w_in[:, :n_head] * head_scale).astype(bf16)
    w_gate = jnp.pad(w_in[:, n_head:n_head + n_gate], ((0, 0), (0, LANES - n_gate))).astype(bf16)
    w_tail = (w_in[:, n_head + n_gate:] * tail_scale).astype(bf16)
    return w_head, w_gate, w_tail


def kernel(x, positions, pre_norm_g, w_in, cmp_k_pos, cmp_k_w1, cmp_k_w2,
           cmp_v_pos, cmp_v_w1, cmp_v_w2, w_out, post_norm_g):
    B, S, d_model = x.shape
    depth = w_in.shape[0]
    n_sel = S // SEL_BLOCK
    assert S % KEY_TILE == 0 and n_sel == MAX_SEL_BLOCKS and S >= WIN_BLOCKS * BLOCK
    assert S % DIL_SUPER == 0 and all(win // dil == BLOCK for win, dil in DIL_PATTERNS)
    bf16 = jnp.bfloat16

    rope_c, rope_a, rope_b = _rope_tables(positions)
    onehot = (jnp.arange(S)[:, None] // SEL_BLOCK == jnp.arange(LANES)[None, :]).astype(bf16)
    x2 = x.reshape(B * S, d_model)

    for layer in range(depth):
        qa, kvc, ks, kw, vst, vwt, gate, sza, qb_, kb_, vb_, szb = _in_proj(
            x2, pre_norm_g[layer][None, :], _in_proj_weights(w_in[layer]), rope_c, rope_a, rope_b, B)

        kvc = _compress(kvc, *_compress_weights(jnp.stack([cmp_k_pos[layer], cmp_v_pos[layer]]),
                                                jnp.stack([cmp_k_w1[layer], cmp_v_w1[layer]]),
                                                jnp.stack([cmp_k_w2[layer], cmp_v_w2[layer]])), B)
        mixed_a = _nsa_attn(qa, kvc[0], kvc[1], onehot, ks, vst, kw, vwt, gate, sza, B)
        mixed_b = _dil_mix(qb_, kb_, vb_, szb, B)
        x2 = _out_proj(x2, mixed_a, mixed_b, w_out[layer].astype(bf16), post_norm_g[layer][None, :])
    return x2.reshape(B, S, d_model)
```

```python
import jax
import jax.numpy as jnp
from jax import lax
from jax.experimental import pallas as pl
from jax.experimental.pallas import tpu as pltpu

HEAD_DIM = 64
NSA_HEADS = 8
NSA_KV_HEADS = 2
NSA_Q_PER_KV = NSA_HEADS // NSA_KV_HEADS
DIL_HEADS = 8
W_NSA = NSA_HEADS * HEAD_DIM
W_KV = NSA_KV_HEADS * HEAD_DIM
W_DIL = DIL_HEADS * HEAD_DIM
CMP_LEN = 32
CMP_STRIDE = 16
CMP_HIDDEN = 256
SEL_BLOCK = 64
SEL_TOP_N = 16
N_FORCED = 3
SWA_WINDOW = 512
DIL_PATTERNS = ((128, 1), (512, 4), (2048, 16))
BLOCK = 128
ROPE_THETA = 500000.0
ROPE_DIMS = HEAD_DIM // 4
RMS_EPS = 1e-6
NEG_INF = -1e30
FORCE_SCORE = 1e4
LOG2_E = 1.4426950408889634

LANES = 128
VMEM_LIMIT = 56 * 1024 * 1024
MAX_SEL_BLOCKS = LANES
CMP_PER_SEL = SEL_BLOCK // CMP_STRIDE
KEY_TILE = 512
PROJ_ROWS = 512
OUT_ROWS = 1024
BF16_ROWS = 16
ONES_ROWS = BF16_ROWS
SEL_SHIFT = SEL_BLOCK.bit_length() - 1
LANE_SHIFT = LANES.bit_length() - 1

_NT = (((1,), (1,)), ((), ()))
_TN = (((0,), (0,)), ((), ()))


def _dot(a, b):
    return jnp.dot(a, b, preferred_element_type=jnp.float32)


def _dot_nt(a, b):
    return lax.dot_general(a, b, _NT, preferred_element_type=jnp.float32)


def _sigmoid(x):
    return 1.0 / (1.0 + jnp.exp(-x))


def _rope(x, c, a, b):
    width = x.shape[1]
    reps = width // LANES
    ct = jnp.tile(c, (1, reps))
    at = jnp.tile(a, (1, reps))
    bt = jnp.tile(b, (1, reps))
    half = ROPE_DIMS // 2
    return x * ct + pltpu.roll(x, width - half, 1) * at + pltpu.roll(x, half, 1) * bt


def _in_proj_kernel(x_ref, g_ref, w_head_ref, w_gate_ref, w_tail_ref, c_ref, a_ref, b_ref,
                    qa_ref, kvc_ref, ks_ref, kw_ref, vst_ref, vwt_ref, gate_ref, sza_ref,
                    qb_ref, kb_ref, vb_ref, szb_ref):
    x = x_ref[...]
    ms = jnp.mean(x * x, axis=-1, keepdims=True)
    h = (x * lax.rsqrt(ms + RMS_EPS) * g_ref[...]).astype(jnp.bfloat16)
    c = c_ref[...]
    a = a_ref[...]
    b = b_ref[...]

    def columns_of(w_ref):
        off = 0

        def take(width):
            nonlocal off
            off += width
            return _dot(h, w_ref[:, off - width:off])
        return take

    head, gates, proj = columns_of(w_head_ref), columns_of(w_gate_ref), columns_of(w_tail_ref)
    qa_ref[0] = _rope(head(W_NSA), c, a, b).T.astype(qa_ref.dtype)
    kva = head(6 * W_KV)
    part = lambda i: kva[:, i * W_KV:(i + 1) * W_KV]
    kvc_ref[...] = jnp.concatenate([_rope(part(0), c, a, b), part(1)], axis=1).astype(kvc_ref.dtype)
    ks_ref[...] = _rope(part(2), c, a, b).astype(ks_ref.dtype)
    kw_ref[...] = _rope(part(4), c, a, b).astype(kw_ref.dtype)
    vst_ref[0] = part(3).T.astype(vst_ref.dtype)
    vwt_ref[0] = part(5).T.astype(vwt_ref.dtype)
    gate_ref[0] = _sigmoid(gates(LANES)).T
    za = proj(W_NSA)
    sza_ref[0] = (za * _sigmoid(za)).T.astype(sza_ref.dtype)
    qb_ref[...] = _rope(proj(W_DIL), c, a, b).astype(qb_ref.dtype)
    kb_ref[...] = _rope(proj(W_DIL), c, a, b).astype(kb_ref.dtype)
    vb_ref[...] = proj(W_DIL).astype(vb_ref.dtype)
    zb = proj(W_DIL)
    szb_ref[...] = (zb * _sigmoid(zb)).astype(szb_ref.dtype)


def _in_proj(x2, g, weights, c, a, b, batch):
    rows, d_model = x2.shape
    tm = PROJ_ROWS
    seq = rows // batch
    per_batch = seq // tm
    bf16 = jnp.bfloat16
    row_spec = lambda wd: pl.BlockSpec((tm, wd), lambda i: (i, 0))
    row_out = lambda wd: (row_spec(wd), jax.ShapeDtypeStruct((rows, wd), bf16))
    t_out = lambda wd, dt=bf16: (pl.BlockSpec((1, wd, tm), lambda i: (i // per_batch, 0, i % per_batch)),
                                 jax.ShapeDtypeStruct((batch, wd, seq), dt))
    outs = [t_out(W_NSA), row_out(2 * W_KV), row_out(W_KV), row_out(W_KV), t_out(W_KV), t_out(W_KV),
            t_out(LANES, jnp.float32), t_out(W_NSA),
            row_out(W_DIL), row_out(W_DIL), row_out(W_DIL), row_out(W_DIL)]
    return pl.pallas_call(
        _in_proj_kernel,
        grid=(rows // tm,),
        in_specs=[row_spec(d_model),
                  pl.BlockSpec((1, d_model), lambda i: (0, 0))]
                 + [pl.BlockSpec(w.shape, lambda i: (0, 0)) for w in weights]
                 + [row_spec(LANES), row_spec(LANES), row_spec(LANES)],
        out_specs=[o[0] for o in outs],
        out_shape=[o[1] for o in outs],
        compiler_params=pltpu.CompilerParams(dimension_semantics=("arbitrary",),
                                             vmem_limit_bytes=VMEM_LIMIT),
        name="in_proj",
    )(x2, g, *weights, c, a, b)


CMP_PAIRS = CMP_STRIDE // 2


def _compress_kernel(x_ref, pos_ref, wt_ref, wb_ref, w2_ref, o_ref, stage):
    f32, bf16 = jnp.float32, jnp.bfloat16
    stage[...] = x_ref[...].astype(f32)
    nj = MAX_SEL_BLOCKS
    hidden = 2 * CMP_HIDDEN

    def offset_rows(l):
        return jnp.concatenate([stage[pl.ds(CMP_STRIDE * m + l, nj, stride=SEL_BLOCK), :]
                                for m in range(CMP_PER_SEL)], axis=0)

    top = jnp.zeros((CMP_PER_SEL * nj, hidden), f32)
    bot = jnp.zeros((CMP_PER_SEL * nj, hidden), f32)
    bias = jnp.zeros((BF16_ROWS, hidden), f32)
    for i in range(CMP_PAIRS):
        x = jnp.concatenate([offset_rows(2 * i), offset_rows(2 * i + 1)], axis=1).astype(bf16)
        top = top + _dot(x, wt_ref[0, i])
        bot = bot + _dot(x, wb_ref[0, i])
        p_top = jnp.concatenate([pos_ref[0, 2 * i], pos_ref[0, 2 * i + 1]], axis=1)
        p_bot = jnp.concatenate([pos_ref[0, CMP_STRIDE + 2 * i], pos_ref[0, CMP_STRIDE + 2 * i + 1]], axis=1)
        bias = bias + _dot(p_top, wt_ref[0, i]) + _dot(p_bot, wb_ref[0, i])
    nxt = jnp.concatenate([bot[nj:], pltpu.roll(bot[0:nj], nj - 1, 0)], axis=0)
    hid = top + nxt + bias[0:1, :]
    act = (hid * _sigmoid(hid)).astype(bf16)
    for g in range(NSA_KV_HEADS):
        o_ref[0, g] = _dot(act[:, g * CMP_HIDDEN:(g + 1) * CMP_HIDDEN], w2_ref[0]).astype(o_ref.dtype)


def _compress(kvc, pos, wt, wb, w2, batch):
    seq = kvc.shape[0] // batch
    G = NSA_KV_HEADS
    ncmp = CMP_PER_SEL * MAX_SEL_BLOCKS
    whole = lambda arr: pl.BlockSpec((1,) + arr.shape[1:], lambda s, b: (s,) + (0,) * (arr.ndim - 1))
    return pl.pallas_call(
        _compress_kernel,
        grid=(2, batch),
        in_specs=[pl.BlockSpec((seq, W_KV), lambda s, b: (b, s)), whole(pos), whole(wt), whole(wb), whole(w2)],
        out_specs=pl.BlockSpec((1, G, ncmp, HEAD_DIM), lambda s, b: (s, b, 0, 0)),
        out_shape=jax.ShapeDtypeStruct((2, batch * G, ncmp, HEAD_DIM), jnp.bfloat16),
        scratch_shapes=[pltpu.VMEM((seq, W_KV), jnp.float32)],
        compiler_params=pltpu.CompilerParams(dimension_semantics=("arbitrary", "arbitrary"),
                                             vmem_limit_bytes=VMEM_LIMIT),
        name="compress",
    )(kvc, pos, wt, wb, w2)


def _compress_weights(pos, w1, w2):
    bf16 = jnp.bfloat16
    pos2 = jnp.broadcast_to(jnp.tile(pos, (1, 1, 2))[:, :, None, :], (2, CMP_LEN, BF16_ROWS, W_KV)).astype(bf16)
    w = w1.astype(bf16).reshape(2, 2, CMP_STRIDE, HEAD_DIM, CMP_HIDDEN)
    lead = ((0, 0),) * 3
    bd = (jnp.pad(w, lead + ((0, HEAD_DIM), (0, CMP_HIDDEN)))
          + jnp.pad(w, lead + ((HEAD_DIM, 0), (CMP_HIDDEN, 0))))
    bd = bd.reshape(2, 2, CMP_PAIRS, 2 * W_KV, 2 * CMP_HIDDEN)
    return pos2, bd[:, 0], bd[:, 1], w2.astype(bf16)


WIN_BLOCKS = -(-(SWA_WINDOW - 1) // BLOCK) + 1


PAIR = 4


def _nsa_attn_kernel(q_ref, kc_ref, vc_ref, et_ref, ks_ref, vst_ref, kw_ref, vwt_ref,
                     gate_ref, sza_ref, o_ref, s_a, s_b, p_a, p_b):
    f32, bf16 = jnp.float32, jnp.bfloat16
    pair = pl.program_id(1)
    G, R = NSA_KV_HEADS, NSA_Q_PER_KV
    grows = R * BLOCK
    brows = G * grows
    rows = PAIR * brows
    gcols = lambda g: slice(g * grows, (g + 1) * grows)
    bgcols = lambda blk, g: slice(blk * brows + g * grows, blk * brows + (g + 1) * grows)
    qcols = lambda blk: slice(blk * BLOCK, (blk + 1) * BLOCK)
    head_bias = lambda ok, reps: jnp.tile(jnp.where(ok, 0.0, NEG_INF), (1, reps))
    zero_t = jnp.zeros((HEAD_DIM, BLOCK), bf16)
    ncmp = kc_ref.shape[1]
    pos = lax.broadcasted_iota(jnp.int32, (ncmp, BLOCK), 0)
    cmp_end = (pos & (LANES - 1)) * SEL_BLOCK + (pos >> LANE_SHIFT) * CMP_STRIDE + (CMP_LEN - 1)
    blk_id = lax.broadcasted_iota(jnp.int32, (LANES, BLOCK), 0)
    blk_f = blk_id.astype(f32)
    span = WIN_BLOCKS * BLOCK
    ones_win = jnp.ones((ONES_ROWS, span), bf16)

    q_kv_t, t_1, o_cmp, o_win, bias = [], [], [], [], []
    for b in range(PAIR):
        qb = pair * PAIR + b
        heads_t = [q_ref[0, h * HEAD_DIM:(h + 1) * HEAD_DIM, qcols(b)] for h in range(NSA_HEADS)]
        q_cmp_t = jnp.concatenate(heads_t, axis=1)
        q_kv = jnp.concatenate([jnp.concatenate([qh, zero_t] if h < R else [zero_t, qh], axis=0)
                                for h, qh in enumerate(heads_t)], axis=1)
        q_kv_t.append(q_kv)
        t_b = qb * BLOCK + lax.broadcasted_iota(jnp.int32, (1, BLOCK), 1)
        t_1.append(t_b)

        cmp_bias = head_bias(cmp_end <= t_b, R)
        seen = jnp.tile(jnp.where(t_b >= CMP_LEN - 1, 1.0, 0.0), (1, R))
        o_cmp_b, imp_sel = [], []
        for g in range(G):
            st = _dot(kc_ref[g], q_cmp_t[:, gcols(g)]) + cmp_bias
            mx = jnp.max(st, axis=0, keepdims=True)
            e = jnp.exp2(st - mx)
            den = jnp.maximum(jnp.sum(e, axis=0, keepdims=True), 1e-30)
            p = e * (seen / den)
            o_cmp_b.append(lax.dot_general(vc_ref[g], p.astype(bf16), _TN,
                                           preferred_element_type=f32))
            imp = p[:, 0:BLOCK]
            for r in range(1, R):
                imp = imp + p[:, r * BLOCK:(r + 1) * BLOCK]
            q4 = [imp[i * LANES:(i + 1) * LANES] for i in range(CMP_PER_SEL)]
            prev_last = jnp.where(blk_id == 0, 0.0, pltpu.roll(q4[3], 1, 0))
            imp_sel.append(prev_last + 2.0 * (q4[0] + q4[1] + q4[2]) + q4[3])
        o_cmp.append(o_cmp_b)

        first_blk = jnp.maximum(qb - (WIN_BLOCKS - 1), 0)
        win = pl.ds(pl.multiple_of(first_blk * BLOCK, BLOCK), span)
        dist = t_b - (first_blk * BLOCK + lax.broadcasted_iota(jnp.int32, (span, BLOCK), 0))
        st = _dot(kw_ref[win, :], q_kv) + head_bias((dist >= 0) & (dist <= SWA_WINDOW - 1), G * R)
        mw = jnp.max(st, axis=0, keepdims=True)
        e = jnp.exp2(st - mw).astype(bf16)
        o_win_b = []
        for g in range(G):
            r = _dot(jnp.concatenate([vwt_ref[0, g * HEAD_DIM:(g + 1) * HEAD_DIM, win], ones_win], axis=0),
                     e[:, gcols(g)])
            o_win_b.append(r[0:HEAD_DIM] * (1.0 / r[HEAD_DIM:HEAD_DIM + 1]))
        o_win.append(o_win_b)

        cur = (qb * BLOCK + lax.broadcasted_iota(jnp.int32, (LANES, BLOCK), 1)) >> SEL_SHIFT
        forced = (blk_id == 0) | (blk_id == cur) | (blk_id == cur - 1)
        for g in range(G):
            sc = jnp.where(forced, -2.0, jnp.where(blk_id <= cur, imp_sel[g], -1.0))
            for _ in range(SEL_TOP_N - N_FORCED):
                best = jnp.max(sc, axis=0, keepdims=True)
                first = jnp.min(jnp.where(sc == best, blk_f, float(LANES)), axis=0, keepdims=True)
                sc = jnp.where(blk_f == first, -2.0, sc)
            taken = (sc == -2.0) & (blk_id <= cur)
            bias.append(jnp.tile(jnp.where(taken, 0.0, NEG_INF).astype(bf16), (1, R)))

    qa_t = jnp.concatenate([jnp.concatenate(bias, axis=1), jnp.concatenate(q_kv_t, axis=1)], axis=0)
    n = (pair * PAIR * BLOCK) // KEY_TILE
    last = jnp.maximum(n - 1, 0)
    ones = jnp.ones((ONES_ROWS, KEY_TILE), bf16)
    parts = [(b, g) for b in range(PAIR) for g in range(G)]

    def keys_aug(kt):
        tile = pl.ds(pl.multiple_of(kt * KEY_TILE, KEY_TILE), KEY_TILE)
        return jnp.concatenate([et_ref[tile, :], ks_ref[tile, :]], axis=1)

    def qk(kt, s_ref):
        st = _dot(keys_aug(kt), qa_t)
        s_ref[...] = st
        return jnp.max(st, axis=0, keepdims=True)

    def pv(kt, p_ref):
        tile = pl.ds(pl.multiple_of(kt * KEY_TILE, KEY_TILE), KEY_TILE)
        return tuple(_dot(jnp.concatenate([vst_ref[0, g * HEAD_DIM:(g + 1) * HEAD_DIM, tile], ones], axis=0),
                          p_ref[:, bgcols(b, g)]) for b, g in parts)

    def softmax(m_old, mx, s_ref, p_ref):
        m_new = jnp.maximum(m_old, mx)
        p_ref[...] = jnp.exp2(s_ref[...] - m_new).astype(bf16)
        return m_new, jnp.exp2(m_old - m_new)

    def accumulate(acc, alpha, weight, contrib):
        return tuple(alpha[:, bgcols(b, g)] * acc[i] + weight * contrib[i] for i, (b, g) in enumerate(parts))

    key = n * KEY_TILE + lax.broadcasted_iota(jnp.int32, (KEY_TILE, BLOCK), 0)
    causal = jnp.concatenate([head_bias(key <= t_1[b], G * R) for b in range(PAIR)], axis=1)
    st = _dot(keys_aug(n), qa_t) + causal
    m0 = jnp.max(st, axis=0, keepdims=True)
    p_b[...] = jnp.exp2(st - m0).astype(bf16)
    mx0 = qk(0, s_a)
    acc0 = tuple(jnp.zeros((HEAD_DIM + ONES_ROWS, grows), f32) for _ in parts)
    one = jnp.ones_like(m0)

    def body(i, carry):
        m, acc, alpha_prev, w_prev, kt_prev, mx = carry
        first, second = 2 * i, 2 * i + 1
        w_second = jnp.where(second < n, 1.0, 0.0)
        kt_second = jnp.minimum(second, last)
        acc = accumulate(acc, alpha_prev, w_prev, pv(kt_prev, p_b))
        m, alpha = softmax(m, mx, s_a, p_a)
        mx = qk(kt_second, s_b)
        acc = accumulate(acc, alpha, 1.0, pv(first, p_a))
        m, alpha = softmax(m, mx, s_b, p_b)
        mx = qk(jnp.minimum(second + 1, last), s_a)
        return m, acc, alpha, w_second, kt_second, mx

    init = (m0, acc0, one, jnp.float32(1.0), n, mx0)
    _, acc, alpha_prev, w_prev, kt_prev, _ = lax.fori_loop(0, (n + 1) // 2, body, init)
    acc = accumulate(acc, alpha_prev, w_prev, pv(kt_prev, p_b))
    o_slc = [a[0:HEAD_DIM] * (1.0 / a[HEAD_DIM:HEAD_DIM + 1]) for a in acc]

    for i, (b, g) in enumerate(parts):
        for r in range(R):
            h = g * R + r
            cols = slice(r * BLOCK, (r + 1) * BLOCK)
            gate_row = lambda branch: gate_ref[0, branch * NSA_HEADS + h:branch * NSA_HEADS + h + 1, qcols(b)]
            o_t = (gate_row(0) * o_cmp[b][g][:, cols] + gate_row(1) * o_slc[i][:, cols]
                   + gate_row(2) * o_win[b][g][:, cols])
            dims = slice(h * HEAD_DIM, (h + 1) * HEAD_DIM)
            o_ref[0, dims, qcols(b)] = (o_t * sza_ref[0, dims, qcols(b)].astype(f32)).astype(o_ref.dtype)


def _nsa_attn(qa, kcp, vcp, onehot, ks, vst, kw, vwt, gate, sza, batch):
    seq = qa.shape[2]
    npairs = seq // (PAIR * BLOCK)
    ncmp = kcp.shape[1]
    G = NSA_KV_HEADS
    rows = PAIR * NSA_HEADS * BLOCK
    t_block = lambda wd: pl.BlockSpec((1, wd, PAIR * BLOCK), lambda b, j: (b, 0, j))
    return pl.pallas_call(
        _nsa_attn_kernel,
        grid=(batch, npairs),
        in_specs=[t_block(W_NSA),
                  pl.BlockSpec((G, ncmp, HEAD_DIM), lambda b, j: (b, 0, 0)),
                  pl.BlockSpec((G, ncmp, HEAD_DIM), lambda b, j: (b, 0, 0)),
                  pl.BlockSpec((seq, LANES), lambda b, j: (0, 0)),
                  pl.BlockSpec((seq, W_KV), lambda b, j: (b, 0)),
                  pl.BlockSpec((1, W_KV, seq), lambda b, j: (b, 0, 0)),
                  pl.BlockSpec((seq, W_KV), lambda b, j: (b, 0)),
                  pl.BlockSpec((1, W_KV, seq), lambda b, j: (b, 0, 0)),
                  t_block(LANES),
                  t_block(W_NSA)],
        out_specs=t_block(W_NSA),
        out_shape=jax.ShapeDtypeStruct((batch, W_NSA, seq), jnp.bfloat16),
        scratch_shapes=[pltpu.VMEM((KEY_TILE, rows), jnp.float32),
                        pltpu.VMEM((KEY_TILE, rows), jnp.float32),
                        pltpu.VMEM((KEY_TILE, rows), jnp.bfloat16),
                        pltpu.VMEM((KEY_TILE, rows), jnp.bfloat16)],
        compiler_params=pltpu.CompilerParams(dimension_semantics=("arbitrary", "arbitrary"),
                                             vmem_limit_bytes=VMEM_LIMIT),
        name="nsa_attn",
    )(qa, kcp, vcp, onehot, ks, vst, kw, vwt, gate, sza)


DIL_MAX = max(d for _, d in DIL_PATTERNS)
DIL_SUPER = BLOCK * DIL_MAX
DIL_UNITS = DIL_SUPER // BLOCK
HEAD_PAIR = 2 * HEAD_DIM
MIX_ROWS = 256


def _dil_mix_kernel(q_ref, kp_ref, kc_ref, vp_ref, vc_ref, z_ref, o_ref,
                    qf, kf, vf, num_scr, den_scr, max_scr, bias_scr):
    f32, bf16 = jnp.float32, jnp.bfloat16
    sb = pl.program_id(1)
    qf[...] = q_ref[...].astype(f32)
    kf[0:DIL_SUPER] = kp_ref[...].astype(f32)
    kf[DIL_SUPER:2 * DIL_SUPER] = kc_ref[...].astype(f32)
    vf[0:DIL_SUPER] = vp_ref[...].astype(f32)
    vf[DIL_SUPER:2 * DIL_SUPER] = vc_ref[...].astype(f32)

    row = lax.broadcasted_iota(jnp.int32, (2 * BLOCK, 2 * BLOCK), 0)
    col = lax.broadcasted_iota(jnp.int32, (2 * BLOCK, 2 * BLOCK), 1)
    dist = BLOCK + (row & (BLOCK - 1)) - col
    band = (dist >= 0) & (dist <= BLOCK)
    bias_scr[0] = jnp.where(band, 0.0, NEG_INF)
    bias_scr[1] = jnp.where(band & (col >= BLOCK), 0.0, NEG_INF)
    first_head = lax.broadcasted_iota(jnp.int32, (BLOCK, HEAD_PAIR), 1) < HEAD_DIM
    ones = jnp.ones((2 * BLOCK, HEAD_PAIR), bf16)

    for pat, (window, dil) in enumerate(DIL_PATTERNS):
        shift = dil.bit_length() - 1

        def unit(u, pat=pat, dil=dil, shift=shift):
            cls = u & (dil - 1)
            blk = u >> shift
            q_start = cls + blk * (BLOCK * dil)
            k_start = DIL_SUPER + q_start - BLOCK * dil
            q2 = qf[pl.ds(q_start, BLOCK, stride=dil), :]
            k2 = kf[pl.ds(k_start, 2 * BLOCK, stride=dil), :]
            v2 = vf[pl.ds(k_start, 2 * BLOCK, stride=dil), :]
            qm = jnp.concatenate([jnp.where(first_head, q2, 0.0),
                                  jnp.where(first_head, 0.0, q2)], axis=0).astype(bf16)
            s = _dot_nt(qm, k2.astype(bf16))
            no_prev = jnp.where((sb == 0) & (blk == 0), 1, 0)
            s = s + bias_scr[no_prev]
            m = jnp.max(s, axis=1, keepdims=True)
            e = jnp.exp2(s - m).astype(bf16)
            r = _dot(e, jnp.concatenate([v2.astype(bf16), ones], axis=1))
            mb = jnp.broadcast_to(m, (2 * BLOCK, HEAD_PAIR))
            out_rows = pl.ds(q_start, BLOCK, stride=dil)
            num_scr[pat, out_rows, :] = jnp.where(first_head, r[0:BLOCK, 0:HEAD_PAIR], r[BLOCK:, 0:HEAD_PAIR])
            den_scr[pat, out_rows, :] = jnp.where(first_head, r[0:BLOCK, HEAD_PAIR:], r[BLOCK:, HEAD_PAIR:])
            max_scr[pat, out_rows, :] = jnp.where(first_head, mb[0:BLOCK], mb[BLOCK:])

        for u in range(DIL_UNITS):
            unit(u)

    def mix(ci, carry):
        rows = pl.ds(pl.multiple_of(ci * MIX_ROWS, MIX_ROWS), MIX_ROWS)
        ms = [max_scr[p, rows, :] for p in range(len(DIL_PATTERNS))]
        mx = jnp.maximum(jnp.maximum(ms[0], ms[1]), ms[2])
        cs = [jnp.exp2(m - mx) for m in ms]
        num = cs[0] * num_scr[0, rows, :] + cs[1] * num_scr[1, rows, :] + cs[2] * num_scr[2, rows, :]
        den = cs[0] * den_scr[0, rows, :] + cs[1] * den_scr[1, rows, :] + cs[2] * den_scr[2, rows, :]
        o_ref[rows, :] = (num / den * z_ref[rows, :].astype(f32)).astype(o_ref.dtype)
        return carry

    lax.fori_loop(0, DIL_SUPER // MIX_ROWS, mix, 0)


def _dil_mix(qb, kb, vb, szb, batch):
    rows = qb.shape[0]
    nsb = rows // batch // DIL_SUPER
    cur = pl.BlockSpec((DIL_SUPER, HEAD_PAIR), lambda b, s, h: (b * nsb + s, h))
    prev = pl.BlockSpec((DIL_SUPER, HEAD_PAIR), lambda b, s, h: (b * nsb + jnp.maximum(s - 1, 0), h))
    f32 = jnp.float32
    return pl.pallas_call(
        _dil_mix_kernel,
        grid=(batch, nsb, W_DIL // HEAD_PAIR),
        in_specs=[cur, prev, cur, prev, cur, cur],
        out_specs=cur,
        out_shape=jax.ShapeDtypeStruct((rows, W_DIL), jnp.bfloat16),
        scratch_shapes=[pltpu.VMEM((DIL_SUPER, HEAD_PAIR), f32),
                        pltpu.VMEM((2 * DIL_SUPER, HEAD_PAIR), f32),
                        pltpu.VMEM((2 * DIL_SUPER, HEAD_PAIR), f32),
                        pltpu.VMEM((len(DIL_PATTERNS), DIL_SUPER, HEAD_PAIR), f32),
                        pltpu.VMEM((len(DIL_PATTERNS), DIL_SUPER, HEAD_PAIR), f32),
                        pltpu.VMEM((len(DIL_PATTERNS), DIL_SUPER, HEAD_PAIR), f32),
                        pltpu.VMEM((2, 2 * BLOCK, 2 * BLOCK), f32)],
        compiler_params=pltpu.CompilerParams(
            dimension_semantics=("arbitrary", "arbitrary", "arbitrary"),
            vmem_limit_bytes=VMEM_LIMIT),
        name="dil_mix",
    )(qb, kb, kb, vb, vb, szb)


def _out_proj_kernel(x_ref, ma_ref, mb_ref, w_ref, g_ref, out_ref):
    y = lax.dot_general(ma_ref[0], w_ref[0:W_NSA, :], _TN, preferred_element_type=jnp.float32)
    y = y + _dot(mb_ref[...], w_ref[W_NSA:W_NSA + W_DIL, :])
    ms = jnp.mean(y * y, axis=-1, keepdims=True)
    out_ref[...] = x_ref[...] + y * lax.rsqrt(ms + RMS_EPS) * g_ref[...]


def _out_proj(x2, mixed_a_t, mixed_b, w, g):
    rows, d_model = x2.shape
    tm = OUT_ROWS
    per_batch = mixed_a_t.shape[2] // tm
    row_spec = lambda wd: pl.BlockSpec((tm, wd), lambda i: (i, 0))
    return pl.pallas_call(
        _out_proj_kernel,
        grid=(rows // tm,),
        in_specs=[row_spec(d_model),
                  pl.BlockSpec((1, W_NSA, tm), lambda i: (i // per_batch, 0, i % per_batch)),
                  row_spec(W_DIL),
                  pl.BlockSpec(w.shape, lambda i: (0, 0)),
                  pl.BlockSpec((1, d_model), lambda i: (0, 0))],
        out_specs=row_spec(d_model),
        out_shape=jax.ShapeDtypeStruct((rows, d_model), jnp.float32),
        compiler_params=pltpu.CompilerParams(dimension_semantics=("arbitrary",),
                                             vmem_limit_bytes=VMEM_LIMIT),
        name="out_proj",
    )(x2, mixed_a_t, mixed_b, w, g)


def _rope_tables(positions):
    inv = 1.0 / (ROPE_THETA ** (jnp.arange(0, ROPE_DIMS, 2, dtype=jnp.float32) / ROPE_DIMS))
    ang = positions.astype(jnp.float32).reshape(-1)[:, None] * inv
    cos, sin = jnp.cos(ang), jnp.sin(ang)
    half = ROPE_DIMS // 2
    k = jnp.arange(LANES) % HEAD_DIM
    freq = jnp.arange(half)[:, None]
    first = (k[None, :] == freq).astype(jnp.float32)
    second = (k[None, :] == freq + half).astype(jnp.float32)
    spread = lambda t, m: jnp.dot(t, m, precision=lax.Precision.HIGHEST)
    unrotated = (k >= ROPE_DIMS).astype(jnp.float32)[None, :]
    return spread(cos, first + second) + unrotated, spread(-sin, first), spread(sin, second)


def _in_proj_weights(w_in):
    scale = HEAD_DIM ** -0.5 * LOG2_E
    bf16 = jnp.bfloat16
    n_head, n_gate = W_NSA + 6 * W_KV, 3 * NSA_HEADS
    ones = lambda n: jnp.ones((n,), jnp.float32)
    head_scale = jnp.concatenate([scale * ones(W_NSA), ones(6 * W_KV)])
    tail_scale = jnp.concatenate([ones(W_NSA), scale * ones(W_DIL), ones(3 * W_DIL)])
    w_head = (w_in[:, :n_head] * head_scale).astype(bf16)
    w_gate = jnp.pad(w_in[:, n_head:n_head + n_gate], ((0, 0), (0, LANES - n_gate))).astype(bf16)
    w_tail = (w_in[:, n_head + n_gate:] * tail_scale).astype(bf16)
    return w_head, w_gate, w_tail


def kernel(x, positions, pre_norm_g, w_in, cmp_k_pos, cmp_k_w1, cmp_k_w2,
           cmp_v_pos, cmp_v_w1, cmp_v_w2, w_out, post_norm_g):
    B, S, d_model = x.shape
    depth = w_in.shape[0]
    n_sel = S // SEL_BLOCK
    assert S % KEY_TILE == 0 and n_sel == MAX_SEL_BLOCKS and S >= WIN_BLOCKS * BLOCK
    assert S % DIL_SUPER == 0 and all(win // dil == BLOCK for win, dil in DIL_PATTERNS)
    bf16 = jnp.bfloat16

    rope_c, rope_a, rope_b = _rope_tables(positions)
    onehot = (jnp.arange(S)[:, None] // SEL_BLOCK == jnp.arange(LANES)[None, :]).astype(bf16)
    x2 = x.reshape(B * S, d_model)

    for layer in range(depth):
        qa, kvc, ks, kw, vst, vwt, gate, sza, qb_, kb_, vb_, szb = _in_proj(
            x2, pre_norm_g[layer][None, :], _in_proj_weights(w_in[layer]), rope_c, rope_a, rope_b, B)

        kvc = _compress(kvc, *_compress_weights(jnp.stack([cmp_k_pos[layer], cmp_v_pos[layer]]),
                                                jnp.stack([cmp_k_w1[layer], cmp_v_w1[layer]]),
                                                jnp.stack([cmp_k_w2[layer], cmp_v_w2[layer]])), B)
        mixed_a = _nsa_attn(qa, kvc[0], kvc[1], onehot, ks, vst, kw, vwt, gate, sza, B)
        mixed_b = _dil_mix(qb_, kb_, vb_, szb, B)
        x2 = _out_proj(x2, mixed_a, mixed_b, w_out[layer].astype(bf16), post_norm_g[layer][None, :])
    return x2.reshape(B, S, d_model)
```

```python
import jax
import jax.numpy as jnp
from jax import lax
from jax.experimental import pallas as pl
from jax.experimental.pallas import tpu as pltpu

HEAD_DIM = 64
NSA_HEADS = 8
NSA_KV_HEADS = 2
NSA_Q_PER_KV = NSA_HEADS // NSA_KV_HEADS
DIL_HEADS = 8
W_NSA = NSA_HEADS * HEAD_DIM
W_KV = NSA_KV_HEADS * HEAD_DIM
W_DIL = DIL_HEADS * HEAD_DIM
CMP_LEN = 32
CMP_STRIDE = 16
CMP_HIDDEN = 256
SEL_BLOCK = 64
SEL_TOP_N = 16
N_FORCED = 3
SWA_WINDOW = 512
DIL_PATTERNS = ((128, 1), (512, 4), (2048, 16))
BLOCK = 128
ROPE_THETA = 500000.0
ROPE_DIMS = HEAD_DIM // 4
RMS_EPS = 1e-6
NEG_INF = -1e30
FORCE_SCORE = 1e4
LOG2_E = 1.4426950408889634

LANES = 128
VMEM_LIMIT = 56 * 1024 * 1024
MAX_SEL_BLOCKS = LANES
CMP_PER_SEL = SEL_BLOCK // CMP_STRIDE
KEY_TILE = 512
PROJ_ROWS = 512
OUT_ROWS = 1024
BF16_ROWS = 16
ONES_ROWS = BF16_ROWS
SEL_SHIFT = SEL_BLOCK.bit_length() - 1
LANE_SHIFT = LANES.bit_length() - 1

_NT = (((1,), (1,)), ((), ()))
_TN = (((0,), (0,)), ((), ()))


def _dot(a, b):
    return jnp.dot(a, b, preferred_element_type=jnp.float32)


def _dot_nt(a, b):
    return lax.dot_general(a, b, _NT, preferred_element_type=jnp.float32)


def _sigmoid(x):
    return 1.0 / (1.0 + jnp.exp(-x))


def _rope(x, c, a, b):
    width = x.shape[1]
    reps = width // LANES
    ct = jnp.tile(c, (1, reps))
    at = jnp.tile(a, (1, reps))
    bt = jnp.tile(b, (1, reps))
    half = ROPE_DIMS // 2
    return x * ct + pltpu.roll(x, width - half, 1) * at + pltpu.roll(x, half, 1) * bt


def _in_proj_kernel(x_ref, g_ref, w_head_ref, w_gate_ref, w_tail_ref, c_ref, a_ref, b_ref,
                    qa_ref, kvc_ref, ks_ref, kw_ref, vst_ref, vwt_ref, gate_ref, sza_ref,
                    qb_ref, kb_ref, vb_ref, szb_ref):
    x = x_ref[...]
    ms = jnp.mean(x * x, axis=-1, keepdims=True)
    h = (x * lax.rsqrt(ms + RMS_EPS) * g_ref[...]).astype(jnp.bfloat16)
    c = c_ref[...]
    a = a_ref[...]
    b = b_ref[...]

    def columns_of(w_ref):
        off = 0

        def take(width):
            nonlocal off
            off += width
            return _dot(h, w_ref[:, off - width:off])
        return take

    head, gates, proj = columns_of(w_head_ref), columns_of(w_gate_ref), columns_of(w_tail_ref)
    qa_ref[0] = _rope(head(W_NSA), c, a, b).T.astype(qa_ref.dtype)
    kva = head(6 * W_KV)
    part = lambda i: kva[:, i * W_KV:(i + 1) * W_KV]
    kvc_ref[...] = jnp.concatenate([_rope(part(0), c, a, b), part(1)], axis=1).astype(kvc_ref.dtype)
    ks_ref[...] = _rope(part(2), c, a, b).astype(ks_ref.dtype)
    kw_ref[...] = _rope(part(4), c, a, b).astype(kw_ref.dtype)
    vst_ref[0] = part(3).T.astype(vst_ref.dtype)
    vwt_ref[0] = part(5).T.astype(vwt_ref.dtype)
    gate_ref[0] = _sigmoid(gates(LANES)).T
    za = proj(W_NSA)
    sza_ref[0] = (za * _sigmoid(za)).T.astype(sza_ref.dtype)
    qb_ref[...] = _rope(proj(W_DIL), c, a, b).astype(qb_ref.dtype)
    kb_ref[...] = _rope(proj(W_DIL), c, a, b).astype(kb_ref.dtype)
    vb_ref[...] = proj(W_DIL).astype(vb_ref.dtype)
    zb = proj(W_DIL)
    szb_ref[...] = (zb * _sigmoid(zb)).astype(szb_ref.dtype)


def _in_proj(x2, g, weights, c, a, b, batch):
    rows, d_model = x2.shape
    tm = PROJ_ROWS
    seq = rows // batch
    per_batch = seq // tm
    bf16 = jnp.bfloat16
    row_spec = lambda wd: pl.BlockSpec((tm, wd), lambda i: (i, 0))
    row_out = lambda wd: (row_spec(wd), jax.ShapeDtypeStruct((rows, wd), bf16))
    t_out = lambda wd, dt=bf16: (pl.BlockSpec((1, wd, tm), lambda i: (i // per_batch, 0, i % per_batch)),
                                 jax.ShapeDtypeStruct((batch, wd, seq), dt))
    outs = [t_out(W_NSA), row_out(2 * W_KV), row_out(W_KV), row_out(W_KV), t_out(W_KV), t_out(W_KV),
            t_out(LANES, jnp.float32), t_out(W_NSA),
            row_out(W_DIL), row_out(W_DIL), row_out(W_DIL), row_out(W_DIL)]
    return pl.pallas_call(
        _in_proj_kernel,
        grid=(rows // tm,),
        in_specs=[row_spec(d_model),
                  pl.BlockSpec((1, d_model), lambda i: (0, 0))]
                 + [pl.BlockSpec(w.shape, lambda i: (0, 0)) for w in weights]
                 + [row_spec(LANES), row_spec(LANES), row_spec(LANES)],
        out_specs=[o[0] for o in outs],
        out_shape=[o[1] for o in outs],
        compiler_params=pltpu.CompilerParams(dimension_semantics=("arbitrary",),
                                             vmem_limit_bytes=VMEM_LIMIT),
        name="in_proj",
    )(x2, g, *weights, c, a, b)


CMP_PAIRS = CMP_STRIDE // 2


def _compress_kernel(x_ref, pos_ref, wt_ref, wb_ref, w2_ref, o_ref, stage):
    f32, bf16 = jnp.float32, jnp.bfloat16
    stage[...] = x_ref[...].astype(f32)
    nj = MAX_SEL_BLOCKS
    hidden = 2 * CMP_HIDDEN

    def offset_rows(l):
        return jnp.concatenate([stage[pl.ds(CMP_STRIDE * m + l, nj, stride=SEL_BLOCK), :]
                                for m in range(CMP_PER_SEL)], axis=0)

    top = jnp.zeros((CMP_PER_SEL * nj, hidden), f32)
    bot = jnp.zeros((CMP_PER_SEL * nj, hidden), f32)
    bias = jnp.zeros((BF16_ROWS, hidden), f32)
    for i in range(CMP_PAIRS):
        x = jnp.concatenate([offset_rows(2 * i), offset_rows(2 * i + 1)], axis=1).astype(bf16)
        top = top + _dot(x, wt_ref[0, i])
        bot = bot + _dot(x, wb_ref[0, i])
        p_top = jnp.concatenate([pos_ref[0, 2 * i], pos_ref[0, 2 * i + 1]], axis=1)
        p_bot = jnp.concatenate([pos_ref[0, CMP_STRIDE + 2 * i], pos_ref[0, CMP_STRIDE + 2 * i + 1]], axis=1)
        bias = bias + _dot(p_top, wt_ref[0, i]) + _dot(p_bot, wb_ref[0, i])
    nxt = jnp.concatenate([bot[nj:], pltpu.roll(bot[0:nj], nj - 1, 0)], axis=0)
    hid = top + nxt + bias[0:1, :]
    act = (hid * _sigmoid(hid)).astype(bf16)
    for g in range(NSA_KV_HEADS):
        o_ref[0, g] = _dot(act[:, g * CMP_HIDDEN:(g + 1) * CMP_HIDDEN], w2_ref[0]).astype(o_ref.dtype)


def _compress(kvc, pos, wt, wb, w2, batch):
    seq = kvc.shape[0] // batch
    G = NSA_KV_HEADS
    ncmp = CMP_PER_SEL * MAX_SEL_BLOCKS
    whole = lambda arr: pl.BlockSpec((1,) + arr.shape[1:], lambda s, b: (s,) + (0,) * (arr.ndim - 1))
    return pl.pallas_call(
        _compress_kernel,
        grid=(2, batch),
        in_specs=[pl.BlockSpec((seq, W_KV), lambda s, b: (b, s)), whole(pos), whole(wt), whole(wb), whole(w2)],
        out_specs=pl.BlockSpec((1, G, ncmp, HEAD_DIM), lambda s, b: (s, b, 0, 0)),
        out_shape=jax.ShapeDtypeStruct((2, batch * G, ncmp, HEAD_DIM), jnp.bfloat16),
        scratch_shapes=[pltpu.VMEM((seq, W_KV), jnp.float32)],
        compiler_params=pltpu.CompilerParams(dimension_semantics=("arbitrary", "arbitrary"),
                                             vmem_limit_bytes=VMEM_LIMIT),
        name="compress",
    )(kvc, pos, wt, wb, w2)


def _compress_weights(pos, w1, w2):
    bf16 = jnp.bfloat16
    pos2 = jnp.broadcast_to(jnp.tile(pos, (1, 1, 2))[:, :, None, :], (2, CMP_LEN, BF16_ROWS, W_KV)).astype(bf16)
    w = w1.astype(bf16).reshape(2, 2, CMP_STRIDE, HEAD_DIM, CMP_HIDDEN)
    lead = ((0, 0),) * 3
    bd = (jnp.pad(w, lead + ((0, HEAD_DIM), (0, CMP_HIDDEN)))
          + jnp.pad(w, lead + ((HEAD_DIM, 0), (CMP_HIDDEN, 0))))
    bd = bd.reshape(2, 2, CMP_PAIRS, 2 * W_KV, 2 * CMP_HIDDEN)
    return pos2, bd[:, 0], bd[:, 1], w2.astype(bf16)


WIN_BLOCKS = -(-(SWA_WINDOW - 1) // BLOCK) + 1


STEP_BLOCKS = KEY_TILE // BLOCK


def _nsa_attn_kernel(q_ref, kc_ref, vc_ref, et_ref, ks_ref, vst_ref, kw_ref, vwt_ref,
                     gate_ref, sza_ref, o_ref, s_a, s_b, p_a, p_b):
    f32, bf16 = jnp.float32, jnp.bfloat16
    n = pl.program_id(1)
    G, R = NSA_KV_HEADS, NSA_Q_PER_KV
    grows = R * BLOCK
    brows = G * grows
    rows = STEP_BLOCKS * brows
    gcols = lambda g: slice(g * grows, (g + 1) * grows)
    bgcols = lambda blk, g: slice(blk * brows + g * grows, blk * brows + (g + 1) * grows)
    qcols = lambda blk: slice(blk * BLOCK, (blk + 1) * BLOCK)
    head_bias = lambda ok, reps: jnp.tile(jnp.where(ok, 0.0, NEG_INF), (1, reps))
    zero_t = jnp.zeros((HEAD_DIM, BLOCK), bf16)
    ncmp = kc_ref.shape[1]
    pos = lax.broadcasted_iota(jnp.int32, (ncmp, BLOCK), 0)
    cmp_end = (pos & (LANES - 1)) * SEL_BLOCK + (pos >> LANE_SHIFT) * CMP_STRIDE + (CMP_LEN - 1)
    blk_id = lax.broadcasted_iota(jnp.int32, (LANES, BLOCK), 0)
    blk_f = blk_id.astype(f32)
    span = WIN_BLOCKS * BLOCK
    ones_win = jnp.ones((ONES_ROWS, span), bf16)

    q_kv_t, t_1, o_cmp, o_win, bias = [], [], [], [], []
    for b in range(STEP_BLOCKS):
        qb = n * STEP_BLOCKS + b
        heads_t = [q_ref[0, h * HEAD_DIM:(h + 1) * HEAD_DIM, qcols(b)] for h in range(NSA_HEADS)]
        q_cmp_t = jnp.concatenate(heads_t, axis=1)
        q_kv = jnp.concatenate([jnp.concatenate([qh, zero_t] if h < R else [zero_t, qh], axis=0)
                                for h, qh in enumerate(heads_t)], axis=1)
        q_kv_t.append(q_kv)
        t_b = qb * BLOCK + lax.broadcasted_iota(jnp.int32, (1, BLOCK), 1)
        t_1.append(t_b)

        cmp_bias = head_bias(cmp_end <= t_b, R)
        seen = jnp.tile(jnp.where(t_b >= CMP_LEN - 1, 1.0, 0.0), (1, R))
        o_cmp_b, imp_sel = [], []
        for g in range(G):
            st = _dot(kc_ref[g], q_cmp_t[:, gcols(g)]) + cmp_bias
            mx = jnp.max(st, axis=0, keepdims=True)
            e = jnp.exp2(st - mx)
            den = jnp.maximum(jnp.sum(e, axis=0, keepdims=True), 1e-30)
            p = e * (seen / den)
            o_cmp_b.append(lax.dot_general(vc_ref[g], p.astype(bf16), _TN,
                                           preferred_element_type=f32))
            imp = p[:, 0:BLOCK]
            for r in range(1, R):
                imp = imp + p[:, r * BLOCK:(r + 1) * BLOCK]
            q4 = [imp[i * LANES:(i + 1) * LANES] for i in range(CMP_PER_SEL)]
            prev_last = jnp.where(blk_id == 0, 0.0, pltpu.roll(q4[3], 1, 0))
            imp_sel.append(prev_last + 2.0 * (q4[0] + q4[1] + q4[2]) + q4[3])
        o_cmp.append(o_cmp_b)

        first_blk = jnp.maximum(qb - (WIN_BLOCKS - 1), 0)
        win = pl.ds(pl.multiple_of(first_blk * BLOCK, BLOCK), span)
        dist = t_b - (first_blk * BLOCK + lax.broadcasted_iota(jnp.int32, (span, BLOCK), 0))
        st = _dot(kw_ref[win, :], q_kv) + head_bias((dist >= 0) & (dist <= SWA_WINDOW - 1), G * R)
        mw = jnp.max(st, axis=0, keepdims=True)
        e = jnp.exp2(st - mw).astype(bf16)
        o_win_b = []
        for g in range(G):
            r = _dot(jnp.concatenate([vwt_ref[0, g * HEAD_DIM:(g + 1) * HEAD_DIM, win], ones_win], axis=0),
                     e[:, gcols(g)])
            o_win_b.append(r[0:HEAD_DIM] * (1.0 / r[HEAD_DIM:HEAD_DIM + 1]))
        o_win.append(o_win_b)

        cur = (qb * BLOCK + lax.broadcasted_iota(jnp.int32, (LANES, BLOCK), 1)) >> SEL_SHIFT
        forced = (blk_id == 0) | (blk_id == cur) | (blk_id == cur - 1)
        for g in range(G):
            sc = jnp.where(forced, -2.0, jnp.where(blk_id <= cur, imp_sel[g], -1.0))
            for _ in range(SEL_TOP_N - N_FORCED):
                best = jnp.max(sc, axis=0, keepdims=True)
                first = jnp.min(jnp.where(sc == best, blk_f, float(LANES)), axis=0, keepdims=True)
                sc = jnp.where(blk_f == first, -2.0, sc)
            taken = (sc == -2.0) & (blk_id <= cur)
            bias.append(jnp.tile(jnp.where(taken, 0.0, NEG_INF).astype(bf16), (1, R)))

    qa_t = jnp.concatenate([jnp.concatenate(bias, axis=1), jnp.concatenate(q_kv_t, axis=1)], axis=0)
    last = jnp.maximum(n - 1, 0)
    ones = jnp.ones((ONES_ROWS, KEY_TILE), bf16)
    parts = [(b, g) for b in range(STEP_BLOCKS) for g in range(G)]

    def keys_aug(kt):
        tile = pl.ds(pl.multiple_of(kt * KEY_TILE, KEY_TILE), KEY_TILE)
        return jnp.concatenate([et_ref[tile, :], ks_ref[tile, :]], axis=1)

    def qk(kt, s_ref):
        st = _dot(keys_aug(kt), qa_t)
        s_ref[...] = st
        return jnp.max(st, axis=0, keepdims=True)

    def pv(kt, p_ref):
        tile = pl.ds(pl.multiple_of(kt * KEY_TILE, KEY_TILE), KEY_TILE)
        return tuple(_dot(jnp.concatenate([vst_ref[0, g * HEAD_DIM:(g + 1) * HEAD_DIM, tile], ones], axis=0),
                          p_ref[:, bgcols(b, g)]) for b, g in parts)

    def softmax(m_old, mx, s_ref, p_ref):
        m_new = jnp.maximum(m_old, mx)
        p_ref[...] = jnp.exp2(s_ref[...] - m_new).astype(bf16)
        return m_new, jnp.exp2(m_old - m_new)

    def accumulate(acc, alpha, weight, contrib):
        return tuple(alpha[:, bgcols(b, g)] * acc[i] + weight * contrib[i] for i, (b, g) in enumerate(parts))

    own = keys_aug(n)
    key = n * KEY_TILE + lax.broadcasted_iota(jnp.int32, (KEY_TILE, BLOCK), 0)
    m0 = []
    for b in range(STEP_BLOCKS):
        live = (b + 1) * BLOCK
        bcols = slice(b * brows, (b + 1) * brows)
        st = _dot(own[0:live], qa_t[:, bcols]) + head_bias(key[0:live] <= t_1[b], G * R)
        m0.append(jnp.max(st, axis=0, keepdims=True))
        p_b[0:live, bcols] = jnp.exp2(st - m0[b]).astype(bf16)
        if live < KEY_TILE:
            p_b[live:KEY_TILE, bcols] = jnp.zeros((KEY_TILE - live, brows), bf16)
    m0 = jnp.concatenate(m0, axis=1)
    mx0 = qk(0, s_a)
    acc0 = tuple(jnp.zeros((HEAD_DIM + ONES_ROWS, grows), f32) for _ in parts)
    one = jnp.ones_like(m0)

    def body(i, carry):
        m, acc, alpha_prev, w_prev, kt_prev, mx = carry
        first, second = 2 * i, 2 * i + 1
        w_second = jnp.where(second < n, 1.0, 0.0)
        kt_second = jnp.minimum(second, last)
        acc = accumulate(acc, alpha_prev, w_prev, pv(kt_prev, p_b))
        m, alpha = softmax(m, mx, s_a, p_a)
        mx = qk(kt_second, s_b)
        acc = accumulate(acc, alpha, 1.0, pv(first, p_a))
        m, alpha = softmax(m, mx, s_b, p_b)
        mx = qk(jnp.minimum(second + 1, last), s_a)
        return m, acc, alpha, w_second, kt_second, mx

    init = (m0, acc0, one, jnp.float32(1.0), n, mx0)
    _, acc, alpha_prev, w_prev, kt_prev, _ = lax.fori_loop(0, (n + 1) // 2, body, init)
    acc = accumulate(acc, alpha_prev, w_prev, pv(kt_prev, p_b))
    o_slc = [a[0:HEAD_DIM] * (1.0 / a[HEAD_DIM:HEAD_DIM + 1]) for a in acc]

    for i, (b, g) in enumerate(parts):
        for r in range(R):
            h = g * R + r
            cols = slice(r * BLOCK, (r + 1) * BLOCK)
            gate_row = lambda branch: gate_ref[0, branch * NSA_HEADS + h:branch * NSA_HEADS + h + 1, qcols(b)]
            o_t = (gate_row(0) * o_cmp[b][g][:, cols] + gate_row(1) * o_slc[i][:, cols]
                   + gate_row(2) * o_win[b][g][:, cols])
            dims = slice(h * HEAD_DIM, (h + 1) * HEAD_DIM)
            o_ref[0, dims, qcols(b)] = (o_t * sza_ref[0, dims, qcols(b)].astype(f32)).astype(o_ref.dtype)


def _nsa_attn(qa, kcp, vcp, onehot, ks, vst, kw, vwt, gate, sza, batch):
    seq = qa.shape[2]
    ncmp = kcp.shape[1]
    G = NSA_KV_HEADS
    rows = STEP_BLOCKS * NSA_HEADS * BLOCK
    t_block = lambda wd: pl.BlockSpec((1, wd, KEY_TILE), lambda b, j: (b, 0, j))
    return pl.pallas_call(
        _nsa_attn_kernel,
        grid=(batch, seq // KEY_TILE),
        in_specs=[t_block(W_NSA),
                  pl.BlockSpec((G, ncmp, HEAD_DIM), lambda b, j: (b, 0, 0)),
                  pl.BlockSpec((G, ncmp, HEAD_DIM), lambda b, j: (b, 0, 0)),
                  pl.BlockSpec((seq, LANES), lambda b, j: (0, 0)),
                  pl.BlockSpec((seq, W_KV), lambda b, j: (b, 0)),
                  pl.BlockSpec((1, W_KV, seq), lambda b, j: (b, 0, 0)),
                  pl.BlockSpec((seq, W_KV), lambda b, j: (b, 0)),
                  pl.BlockSpec((1, W_KV, seq), lambda b, j: (b, 0, 0)),
                  t_block(LANES),
                  t_block(W_NSA)],
        out_specs=t_block(W_NSA),
        out_shape=jax.ShapeDtypeStruct((batch, W_NSA, seq), jnp.bfloat16),
        scratch_shapes=[pltpu.VMEM((KEY_TILE, rows), jnp.float32),
                        pltpu.VMEM((KEY_TILE, rows), jnp.float32),
                        pltpu.VMEM((KEY_TILE, rows), jnp.bfloat16),
                        pltpu.VMEM((KEY_TILE, rows), jnp.bfloat16)],
        compiler_params=pltpu.CompilerParams(dimension_semantics=("arbitrary", "arbitrary"),
                                             vmem_limit_bytes=VMEM_LIMIT),
        name="nsa_attn",
    )(qa, kcp, vcp, onehot, ks, vst, kw, vwt, gate, sza)


DIL_MAX = max(d for _, d in DIL_PATTERNS)
DIL_SUPER = BLOCK * DIL_MAX
DIL_UNITS = DIL_SUPER // BLOCK
HEAD_PAIR = 2 * HEAD_DIM
MIX_ROWS = 256


def _dil_mix_kernel(q_ref, kp_ref, kc_ref, vp_ref, vc_ref, z_ref, o_ref,
                    qf, kf, vf, num_scr, den_scr, max_scr, bias_scr):
    f32, bf16 = jnp.float32, jnp.bfloat16
    sb = pl.program_id(1)
    qf[...] = q_ref[...].astype(f32)
    kf[0:DIL_SUPER] = kp_ref[...].astype(f32)
    kf[DIL_SUPER:2 * DIL_SUPER] = kc_ref[...].astype(f32)
    vf[0:DIL_SUPER] = vp_ref[...].astype(f32)
    vf[DIL_SUPER:2 * DIL_SUPER] = vc_ref[...].astype(f32)

    row = lax.broadcasted_iota(jnp.int32, (2 * BLOCK, 2 * BLOCK), 0)
    col = lax.broadcasted_iota(jnp.int32, (2 * BLOCK, 2 * BLOCK), 1)
    dist = BLOCK + (row & (BLOCK - 1)) - col
    band = (dist >= 0) & (dist <= BLOCK)
    bias_scr[0] = jnp.where(band, 0.0, NEG_INF)
    bias_scr[1] = jnp.where(band & (col >= BLOCK), 0.0, NEG_INF)
    first_head = lax.broadcasted_iota(jnp.int32, (BLOCK, HEAD_PAIR), 1) < HEAD_DIM
    ones = jnp.ones((2 * BLOCK, HEAD_PAIR), bf16)

    for pat, (window, dil) in enumerate(DIL_PATTERNS):
        shift = dil.bit_length() - 1

        def unit(u, pat=pat, dil=dil, shift=shift):
            cls = u & (dil - 1)
            blk = u >> shift
            q_start = cls + blk * (BLOCK * dil)
            k_start = DIL_SUPER + q_start - BLOCK * dil
            q2 = qf[pl.ds(q_start, BLOCK, stride=dil), :]
            k2 = kf[pl.ds(k_start, 2 * BLOCK, stride=dil), :]
            v2 = vf[pl.ds(k_start, 2 * BLOCK, stride=dil), :]
            qm = jnp.concatenate([jnp.where(first_head, q2, 0.0),
                                  jnp.where(first_head, 0.0, q2)], axis=0).astype(bf16)
            s = _dot_nt(qm, k2.astype(bf16))
            no_prev = jnp.where((sb == 0) & (blk == 0), 1, 0)
            s = s + bias_scr[no_prev]
            m = jnp.max(s, axis=1, keepdims=True)
            e = jnp.exp2(s - m).astype(bf16)
            r = _dot(e, jnp.concatenate([v2.astype(bf16), ones], axis=1))
            mb = jnp.broadcast_to(m, (2 * BLOCK, HEAD_PAIR))
            out_rows = pl.ds(q_start, BLOCK, stride=dil)
            num_scr[pat, out_rows, :] = jnp.where(first_head, r[0:BLOCK, 0:HEAD_PAIR], r[BLOCK:, 0:HEAD_PAIR])
            den_scr[pat, out_rows, :] = jnp.where(first_head, r[0:BLOCK, HEAD_PAIR:], r[BLOCK:, HEAD_PAIR:])
            max_scr[pat, out_rows, :] = jnp.where(first_head, mb[0:BLOCK], mb[BLOCK:])

        for u in range(DIL_UNITS):
            unit(u)

    def mix(ci, carry):
        rows = pl.ds(pl.multiple_of(ci * MIX_ROWS, MIX_ROWS), MIX_ROWS)
        ms = [max_scr[p, rows, :] for p in range(len(DIL_PATTERNS))]
        mx = jnp.maximum(jnp.maximum(ms[0], ms[1]), ms[2])
        cs = [jnp.exp2(m - mx) for m in ms]
        num = cs[0] * num_scr[0, rows, :] + cs[1] * num_scr[1, rows, :] + cs[2] * num_scr[2, rows, :]
        den = cs[0] * den_scr[0, rows, :] + cs[1] * den_scr[1, rows, :] + cs[2] * den_scr[2, rows, :]
        o_ref[rows, :] = (num / den * z_ref[rows, :].astype(f32)).astype(o_ref.dtype)
        return carry

    lax.fori_loop(0, DIL_SUPER // MIX_ROWS, mix, 0)


def _dil_mix(qb, kb, vb, szb, batch):
    rows = qb.shape[0]
    nsb = rows // batch // DIL_SUPER
    cur = pl.BlockSpec((DIL_SUPER, HEAD_PAIR), lambda b, s, h: (b * nsb + s, h))
    prev = pl.BlockSpec((DIL_SUPER, HEAD_PAIR), lambda b, s, h: (b * nsb + jnp.maximum(s - 1, 0), h))
    f32 = jnp.float32
    return pl.pallas_call(
        _dil_mix_kernel,
        grid=(batch, nsb, W_DIL // HEAD_PAIR),
        in_specs=[cur, prev, cur, prev, cur, cur],
        out_specs=cur,
        out_shape=jax.ShapeDtypeStruct((rows, W_DIL), jnp.bfloat16),
        scratch_shapes=[pltpu.VMEM((DIL_SUPER, HEAD_PAIR), f32),
                        pltpu.VMEM((2 * DIL_SUPER, HEAD_PAIR), f32),
                        pltpu.VMEM((2 * DIL_SUPER, HEAD_PAIR), f32),
                        pltpu.VMEM((len(DIL_PATTERNS), DIL_SUPER, HEAD_PAIR), f32),
                        pltpu.VMEM((len(DIL_PATTERNS), DIL_SUPER, HEAD_PAIR), f32),
                        pltpu.VMEM((len(DIL_PATTERNS), DIL_SUPER, HEAD_PAIR), f32),
                        pltpu.VMEM((2, 2 * BLOCK, 2 * BLOCK), f32)],
        compiler_params=pltpu.CompilerParams(
            dimension_semantics=("arbitrary", "arbitrary", "arbitrary"),
            vmem_limit_bytes=VMEM_LIMIT),
        name="dil_mix",
    )(qb, kb, kb, vb, vb, szb)


def _out_proj_kernel(x_ref, ma_ref, mb_ref, w_ref, g_ref, out_ref):
    y = lax.dot_general(ma_ref[0], w_ref[0:W_NSA, :], _TN, preferred_element_type=jnp.float32)
    y = y + _dot(mb_ref[...], w_ref[W_NSA:W_NSA + W_DIL, :])
    ms = jnp.mean(y * y, axis=-1, keepdims=True)
    out_ref[...] = x_ref[...] + y * lax.rsqrt(ms + RMS_EPS) * g_ref[...]


def _out_proj(x2, mixed_a_t, mixed_b, w, g):
    rows, d_model = x2.shape
    tm = OUT_ROWS
    per_batch = mixed_a_t.shape[2] // tm
    row_spec = lambda wd: pl.BlockSpec((tm, wd), lambda i: (i, 0))
    return pl.pallas_call(
        _out_proj_kernel,
        grid=(rows // tm,),
        in_specs=[row_spec(d_model),
                  pl.BlockSpec((1, W_NSA, tm), lambda i: (i // per_batch, 0, i % per_batch)),
                  row_spec(W_DIL),
                  pl.BlockSpec(w.shape, lambda i: (0, 0)),
                  pl.BlockSpec((1, d_model), lambda i: (0, 0))],
        out_specs=row_spec(d_model),
        out_shape=jax.ShapeDtypeStruct((rows, d_model), jnp.float32),
        compiler_params=pltpu.CompilerParams(dimension_semantics=("arbitrary",),
                                             vmem_limit_bytes=VMEM_LIMIT),
        name="out_proj",
    )(x2, mixed_a_t, mixed_b, w, g)


def _rope_tables(positions):
    inv = 1.0 / (ROPE_THETA ** (jnp.arange(0, ROPE_DIMS, 2, dtype=jnp.float32) / ROPE_DIMS))
    ang = positions.astype(jnp.float32).reshape(-1)[:, None] * inv
    cos, sin = jnp.cos(ang), jnp.sin(ang)
    half = ROPE_DIMS // 2
    k = jnp.arange(LANES) % HEAD_DIM
    freq = jnp.arange(half)[:, None]
    first = (k[None, :] == freq).astype(jnp.float32)
    second = (k[None, :] == freq + half).astype(jnp.float32)
    spread = lambda t, m: jnp.dot(t, m, precision=lax.Precision.HIGHEST)
    unrotated = (k >= ROPE_DIMS).astype(jnp.float32)[None, :]
    return spread(cos, first + second) + unrotated, spread(-sin, first), spread(sin, second)


def _in_proj_weights(w_in):
    scale = HEAD_DIM ** -0.5 * LOG2_E
    bf16 = jnp.bfloat16
    n_head, n_gate = W_NSA + 6 * W_KV, 3 * NSA_HEADS
    ones = lambda n: jnp.ones((n,), jnp.float32)
    head_scale = jnp.concatenate([scale * ones(W_NSA), ones(6 * W_KV)])
    tail_scale = jnp.concatenate([ones(W_NSA), scale * ones(W_DIL), ones(3 * W_DIL)])
    w_head = (w_in[:, :n_head] * head_scale).astype(bf16)
    w_gate = jnp.pad(w_in[:, n_head:n_head + n_gate], ((0, 0), (0, LANES - n_gate))).astype(bf16)
    w_tail = (w_in[:, n_head + n_gate:] * tail_scale).astype(bf16)
    return w_head, w_gate, w_tail


def kernel(x, positions, pre_norm_g, w_in, cmp_k_pos, cmp_k_w1, cmp_k_w2,
           cmp_v_pos, cmp_v_w1, cmp_v_w2, w_out, post_norm_g):
    B, S, d_model = x.shape
    depth = w_in.shape[0]
    n_sel = S // SEL_BLOCK
    assert S % KEY_TILE == 0 and n_sel == MAX_SEL_BLOCKS and S >= WIN_BLOCKS * BLOCK
    assert S % DIL_SUPER == 0 and all(win // dil == BLOCK for win, dil in DIL_PATTERNS)
    bf16 = jnp.bfloat16

    rope_c, rope_a, rope_b = _rope_tables(positions)
    onehot = (jnp.arange(S)[:, None] // SEL_BLOCK == jnp.arange(LANES)[None, :]).astype(bf16)
    x2 = x.reshape(B * S, d_model)

    for layer in range(depth):
        qa, kvc, ks, kw, vst, vwt, gate, sza, qb_, kb_, vb_, szb = _in_proj(
            x2, pre_norm_g[layer][None, :], _in_proj_weights(w_in[layer]), rope_c, rope_a, rope_b, B)

        kvc = _compress(kvc, *_compress_weights(jnp.stack([cmp_k_pos[layer], cmp_v_pos[layer]]),
                                                jnp.stack([cmp_k_w1[layer], cmp_v_w1[layer]]),
                                                jnp.stack([cmp_k_w2[layer], cmp_v_w2[layer]])), B)
        mixed_a = _nsa_attn(qa, kvc[0], kvc[1], onehot, ks, vst, kw, vwt, gate, sza, B)
        mixed_b = _dil_mix(qb_, kb_, vb_, szb, B)
        x2 = _out_proj(x2, mixed_a, mixed_b, w_out[layer].astype(bf16), post_norm_g[layer][None, :])
    return x2.reshape(B, S, d_model)
```

```python
import jax
import jax.numpy as jnp
from jax import lax
from jax.experimental import pallas as pl
from jax.experimental.pallas import tpu as pltpu

HEAD_DIM = 64
NSA_HEADS = 8
NSA_KV_HEADS = 2
NSA_Q_PER_KV = NSA_HEADS // NSA_KV_HEADS
DIL_HEADS = 8
W_NSA = NSA_HEADS * HEAD_DIM
W_KV = NSA_KV_HEADS * HEAD_DIM
W_DIL = DIL_HEADS * HEAD_DIM
CMP_LEN = 32
CMP_STRIDE = 16
CMP_HIDDEN = 256
SEL_BLOCK = 64
SEL_TOP_N = 16
N_FORCED = 3
SWA_WINDOW = 512
DIL_PATTERNS = ((128, 1), (512, 4), (2048, 16))
BLOCK = 128
ROPE_THETA = 500000.0
ROPE_DIMS = HEAD_DIM // 4
RMS_EPS = 1e-6
NEG_INF = -1e30
FORCE_SCORE = 1e4
LOG2_E = 1.4426950408889634

LANES = 128
VMEM_LIMIT = 56 * 1024 * 1024
MAX_SEL_BLOCKS = LANES
CMP_PER_SEL = SEL_BLOCK // CMP_STRIDE
KEY_TILE = 512
PROJ_ROWS = 512
OUT_ROWS = 1024
BF16_ROWS = 16
ONES_ROWS = BF16_ROWS
SEL_SHIFT = SEL_BLOCK.bit_length() - 1
LANE_SHIFT = LANES.bit_length() - 1

_NT = (((1,), (1,)), ((), ()))
_TN = (((0,), (0,)), ((), ()))


def _dot(a, b):
    return jnp.dot(a, b, preferred_element_type=jnp.float32)


def _dot_nt(a, b):
    return lax.dot_general(a, b, _NT, preferred_element_type=jnp.float32)


def _sigmoid(x):
    return 1.0 / (1.0 + jnp.exp(-x))


def _rope(x, c, a, b):
    width = x.shape[1]
    reps = width // LANES
    ct = jnp.tile(c, (1, reps))
    at = jnp.tile(a, (1, reps))
    bt = jnp.tile(b, (1, reps))
    half = ROPE_DIMS // 2
    return x * ct + pltpu.roll(x, width - half, 1) * at + pltpu.roll(x, half, 1) * bt


def _in_proj_kernel(x_ref, g_ref, w_head_ref, w_gate_ref, w_tail_ref, c_ref, a_ref, b_ref,
                    qa_ref, kvc_ref, ks_ref, kw_ref, vst_ref, vwt_ref, gate_ref, sza_ref,
                    qb_ref, kb_ref, vb_ref, szb_ref):
    x = x_ref[...]
    ms = jnp.mean(x * x, axis=-1, keepdims=True)
    h = (x * lax.rsqrt(ms + RMS_EPS) * g_ref[...]).astype(jnp.bfloat16)
    c = c_ref[...]
    a = a_ref[...]
    b = b_ref[...]

    def columns_of(w_ref):
        off = 0

        def take(width):
            nonlocal off
            off += width
            return _dot(h, w_ref[:, off - width:off])
        return take

    head, gates, proj = columns_of(w_head_ref), columns_of(w_gate_ref), columns_of(w_tail_ref)
    qa_ref[0] = _rope(head(W_NSA), c, a, b).T.astype(qa_ref.dtype)
    kva = head(6 * W_KV)
    part = lambda i: kva[:, i * W_KV:(i + 1) * W_KV]
    kvc_ref[...] = jnp.concatenate([_rope(part(0), c, a, b), part(1)], axis=1).astype(kvc_ref.dtype)
    ks_ref[...] = _rope(part(2), c, a, b).astype(ks_ref.dtype)
    kw_ref[...] = _rope(part(4), c, a, b).astype(kw_ref.dtype)
    vst_ref[0] = part(3).T.astype(vst_ref.dtype)
    vwt_ref[0] = part(5).T.astype(vwt_ref.dtype)
    gate_ref[0] = _sigmoid(gates(LANES)).T
    za = proj(W_NSA)
    sza_ref[0] = (za * _sigmoid(za)).T.astype(sza_ref.dtype)
    qb_ref[...] = _rope(proj(W_DIL), c, a, b).astype(qb_ref.dtype)
    kb_ref[...] = _rope(proj(W_DIL), c, a, b).astype(kb_ref.dtype)
    vb_ref[...] = proj(W_DIL).astype(vb_ref.dtype)
    zb = proj(W_DIL)
    szb_ref[...] = (zb * _sigmoid(zb)).astype(szb_ref.dtype)


def _in_proj(x2, g, weights, c, a, b, batch):
    rows, d_model = x2.shape
    tm = PROJ_ROWS
    seq = rows // batch
    per_batch = seq // tm
    bf16 = jnp.bfloat16
    row_spec = lambda wd: pl.BlockSpec((tm, wd), lambda i: (i, 0))
    row_out = lambda wd: (row_spec(wd), jax.ShapeDtypeStruct((rows, wd), bf16))
    t_out = lambda wd, dt=bf16: (pl.BlockSpec((1, wd, tm), lambda i: (i // per_batch, 0, i % per_batch)),
                                 jax.ShapeDtypeStruct((batch, wd, seq), dt))
    outs = [t_out(W_NSA), row_out(2 * W_KV), row_out(W_KV), row_out(W_KV), t_out(W_KV), t_out(W_KV),
            t_out(LANES, jnp.float32), t_out(W_NSA),
            row_out(W_DIL), row_out(W_DIL), row_out(W_DIL), row_out(W_DIL)]
    return pl.pallas_call(
        _in_proj_kernel,
        grid=(rows // tm,),
        in_specs=[row_spec(d_model),
                  pl.BlockSpec((1, d_model), lambda i: (0, 0))]
                 + [pl.BlockSpec(w.shape, lambda i: (0, 0)) for w in weights]
                 + [row_spec(LANES), row_spec(LANES), row_spec(LANES)],
        out_specs=[o[0] for o in outs],
        out_shape=[o[1] for o in outs],
        compiler_params=pltpu.CompilerParams(dimension_semantics=("arbitrary",),
                                             vmem_limit_bytes=VMEM_LIMIT),
        name="in_proj",
    )(x2, g, *weights, c, a, b)


CMP_PAIRS = CMP_STRIDE // 2


def _compress_kernel(x_ref, pos_ref, wt_ref, wb_ref, w2_ref, o_ref, stage):
    f32, bf16 = jnp.float32, jnp.bfloat16
    stage[...] = x_ref[...].astype(f32)
    nj = MAX_SEL_BLOCKS
    hidden = 2 * CMP_HIDDEN

    def offset_rows(l):
        return jnp.concatenate([stage[pl.ds(CMP_STRIDE * m + l, nj, stride=SEL_BLOCK), :]
                                for m in range(CMP_PER_SEL)], axis=0)

    top = jnp.zeros((CMP_PER_SEL * nj, hidden), f32)
    bot = jnp.zeros((CMP_PER_SEL * nj, hidden), f32)
    bias = jnp.zeros((BF16_ROWS, hidden), f32)
    for i in range(CMP_PAIRS):
        x = jnp.concatenate([offset_rows(2 * i), offset_rows(2 * i + 1)], axis=1).astype(bf16)
        top = top + _dot(x, wt_ref[0, i])
        bot = bot + _dot(x, wb_ref[0, i])
        p_top = jnp.concatenate([pos_ref[0, 2 * i], pos_ref[0, 2 * i + 1]], axis=1)
        p_bot = jnp.concatenate([pos_ref[0, CMP_STRIDE + 2 * i], pos_ref[0, CMP_STRIDE + 2 * i + 1]], axis=1)
        bias = bias + _dot(p_top, wt_ref[0, i]) + _dot(p_bot, wb_ref[0, i])
    nxt = jnp.concatenate([bot[nj:], pltpu.roll(bot[0:nj], nj - 1, 0)], axis=0)
    hid = top + nxt + bias[0:1, :]
    act = (hid * _sigmoid(hid)).astype(bf16)
    for g in range(NSA_KV_HEADS):
        o_ref[0, g] = _dot(act[:, g * CMP_HIDDEN:(g + 1) * CMP_HIDDEN], w2_ref[0]).astype(o_ref.dtype)


def _compress(kvc, pos, wt, wb, w2, batch):
    seq = kvc.shape[0] // batch
    G = NSA_KV_HEADS
    ncmp = CMP_PER_SEL * MAX_SEL_BLOCKS
    whole = lambda arr: pl.BlockSpec((1,) + arr.shape[1:], lambda s, b: (s,) + (0,) * (arr.ndim - 1))
    return pl.pallas_call(
        _compress_kernel,
        grid=(2, batch),
        in_specs=[pl.BlockSpec((seq, W_KV), lambda s, b: (b, s)), whole(pos), whole(wt), whole(wb), whole(w2)],
        out_specs=pl.BlockSpec((1, G, ncmp, HEAD_DIM), lambda s, b: (s, b, 0, 0)),
        out_shape=jax.ShapeDtypeStruct((2, batch * G, ncmp, HEAD_DIM), jnp.bfloat16),
        scratch_shapes=[pltpu.VMEM((seq, W_KV), jnp.float32)],
        compiler_params=pltpu.CompilerParams(dimension_semantics=("arbitrary", "arbitrary"),
                                             vmem_limit_bytes=VMEM_LIMIT),
        name="compress",
    )(kvc, pos, wt, wb, w2)


def _compress_weights(pos, w1, w2):
    bf16 = jnp.bfloat16
    pos2 = jnp.broadcast_to(jnp.tile(pos, (1, 1, 2))[:, :, None, :], (2, CMP_LEN, BF16_ROWS, W_KV)).astype(bf16)
    w = w1.astype(bf16).reshape(2, 2, CMP_STRIDE, HEAD_DIM, CMP_HIDDEN)
    lead = ((0, 0),) * 3
    bd = (jnp.pad(w, lead + ((0, HEAD_DIM), (0, CMP_HIDDEN)))
          + jnp.pad(w, lead + ((HEAD_DIM, 0), (CMP_HIDDEN, 0))))
    bd = bd.reshape(2, 2, CMP_PAIRS, 2 * W_KV, 2 * CMP_HIDDEN)
    return pos2, bd[:, 0], bd[:, 1], w2.astype(bf16)


WIN_BLOCKS = -(-(SWA_WINDOW - 1) // BLOCK) + 1


STEP_BLOCKS = KEY_TILE // BLOCK


def _nsa_attn_kernel(q_ref, kc_ref, vc_ref, et_ref, ks_ref, vst_ref, kw_ref, vwt_ref,
                     gate_ref, sza_ref, o_ref, s_a, s_b, p_a, p_b):
    f32, bf16 = jnp.float32, jnp.bfloat16
    n = pl.program_id(1)
    G, R = NSA_KV_HEADS, NSA_Q_PER_KV
    grows = R * BLOCK
    brows = G * grows
    rows = STEP_BLOCKS * brows
    gcols = lambda g: slice(g * grows, (g + 1) * grows)
    bgcols = lambda blk, g: slice(blk * brows + g * grows, blk * brows + (g + 1) * grows)
    qcols = lambda blk: slice(blk * BLOCK, (blk + 1) * BLOCK)
    head_bias = lambda ok, reps: jnp.tile(jnp.where(ok, 0.0, NEG_INF), (1, reps))
    zero_t = jnp.zeros((HEAD_DIM, BLOCK), bf16)
    ncmp = kc_ref.shape[1]
    pos = lax.broadcasted_iota(jnp.int32, (ncmp, BLOCK), 0)
    cmp_end = (pos & (LANES - 1)) * SEL_BLOCK + (pos >> LANE_SHIFT) * CMP_STRIDE + (CMP_LEN - 1)
    blk_id = lax.broadcasted_iota(jnp.int32, (LANES, BLOCK), 0)
    blk_f = blk_id.astype(f32)
    span = WIN_BLOCKS * BLOCK
    ones_win = jnp.ones((ONES_ROWS, span), bf16)

    q_kv_t, t_1, o_cmp, o_win, bias = [], [], [], [], []
    for b in range(STEP_BLOCKS):
        qb = n * STEP_BLOCKS + b
        heads_t = [q_ref[0, h * HEAD_DIM:(h + 1) * HEAD_DIM, qcols(b)] for h in range(NSA_HEADS)]
        q_cmp_t = jnp.concatenate(heads_t, axis=1)
        q_kv = jnp.concatenate([jnp.concatenate([qh, zero_t] if h < R else [zero_t, qh], axis=0)
                                for h, qh in enumerate(heads_t)], axis=1)
        q_kv_t.append(q_kv)
        t_b = qb * BLOCK + lax.broadcasted_iota(jnp.int32, (1, BLOCK), 1)
        t_1.append(t_b)

        cmp_bias = head_bias(cmp_end <= t_b, R)
        seen = jnp.tile(jnp.where(t_b >= CMP_LEN - 1, 1.0, 0.0), (1, R))
        o_cmp_b, imp_sel = [], []
        for g in range(G):
            st = _dot(kc_ref[g], q_cmp_t[:, gcols(g)]) + cmp_bias
            mx = jnp.max(st, axis=0, keepdims=True)
            e = jnp.exp2(st - mx)
            den = jnp.maximum(jnp.sum(e, axis=0, keepdims=True), 1e-30)
            p = e * (seen / den)
            o_cmp_b.append(lax.dot_general(vc_ref[g], p.astype(bf16), _TN,
                                           preferred_element_type=f32))
            imp = p[:, 0:BLOCK]
            for r in range(1, R):
                imp = imp + p[:, r * BLOCK:(r + 1) * BLOCK]
            q4 = [imp[i * LANES:(i + 1) * LANES] for i in range(CMP_PER_SEL)]
            prev_last = jnp.where(blk_id == 0, 0.0, pltpu.roll(q4[3], 1, 0))
            imp_sel.append(prev_last + 2.0 * (q4[0] + q4[1] + q4[2]) + q4[3])
        o_cmp.append(o_cmp_b)

        first_blk = jnp.maximum(qb - (WIN_BLOCKS - 1), 0)
        win = pl.ds(pl.multiple_of(first_blk * BLOCK, BLOCK), span)
        dist = t_b - (first_blk * BLOCK + lax.broadcasted_iota(jnp.int32, (span, BLOCK), 0))
        st = _dot(kw_ref[win, :], q_kv) + head_bias((dist >= 0) & (dist <= SWA_WINDOW - 1), G * R)
        mw = jnp.max(st, axis=0, keepdims=True)
        e = jnp.exp2(st - mw).astype(bf16)
        o_win_b = []
        for g in range(G):
            r = _dot(jnp.concatenate([vwt_ref[0, g * HEAD_DIM:(g + 1) * HEAD_DIM, win], ones_win], axis=0),
                     e[:, gcols(g)])
            o_win_b.append(r[0:HEAD_DIM] * (1.0 / r[HEAD_DIM:HEAD_DIM + 1]))
        o_win.append(o_win_b)

        cur = (qb * BLOCK + lax.broadcasted_iota(jnp.int32, (LANES, BLOCK), 1)) >> SEL_SHIFT
        forced = (blk_id == 0) | (blk_id == cur) | (blk_id == cur - 1)
        for g in range(G):
            sc = jnp.where(forced, -2.0, jnp.where(blk_id <= cur, imp_sel[g], -1.0))
            for _ in range(SEL_TOP_N - N_FORCED):
                best = jnp.max(sc, axis=0, keepdims=True)
                first = jnp.min(jnp.where(sc == best, blk_f, float(LANES)), axis=0, keepdims=True)
                sc = jnp.where(blk_f == first, -2.0, sc)
            taken = (sc == -2.0) & (blk_id <= cur)
            bias.append(jnp.tile(jnp.where(taken, 0.0, NEG_INF).astype(bf16), (1, R)))

    qa_t = jnp.concatenate([jnp.concatenate(bias, axis=1), jnp.concatenate(q_kv_t, axis=1)], axis=0)
    last = jnp.maximum(n - 1, 0)
    ones = jnp.ones((ONES_ROWS, KEY_TILE), bf16)
    parts = [(b, g) for b in range(STEP_BLOCKS) for g in range(G)]

    def keys_aug(kt):
        tile = pl.ds(pl.multiple_of(kt * KEY_TILE, KEY_TILE), KEY_TILE)
        return jnp.concatenate([et_ref[tile, :], ks_ref[tile, :]], axis=1)

    def qk(kt, s_ref):
        st = _dot(keys_aug(kt), qa_t)
        s_ref[...] = st
        return jnp.max(st, axis=0, keepdims=True)

    def pv(kt, p_ref):
        tile = pl.ds(pl.multiple_of(kt * KEY_TILE, KEY_TILE), KEY_TILE)
        return tuple(_dot(jnp.concatenate([vst_ref[0, g * HEAD_DIM:(g + 1) * HEAD_DIM, tile], ones], axis=0),
                          p_ref[:, bgcols(b, g)]) for b, g in parts)

    def softmax(m_old, mx, s_ref, p_ref):
        m_new = jnp.maximum(m_old, mx)
        p_ref[...] = jnp.exp2(s_ref[...] - m_new).astype(bf16)
        return m_new, jnp.exp2(m_old - m_new)

    def accumulate(acc, alpha, weight, contrib):
        return tuple(alpha[:, bgcols(b, g)] * acc[i] + weight * contrib[i] for i, (b, g) in enumerate(parts))

    own = keys_aug(n)
    key = n * KEY_TILE + lax.broadcasted_iota(jnp.int32, (KEY_TILE, BLOCK), 0)
    m0 = []
    for b in range(STEP_BLOCKS):
        live = (b + 1) * BLOCK
        bcols = slice(b * brows, (b + 1) * brows)
        st = _dot(own[0:live], qa_t[:, bcols]) + head_bias(key[0:live] <= t_1[b], G * R)
        m0.append(jnp.max(st, axis=0, keepdims=True))
        p_b[0:live, bcols] = jnp.exp2(st - m0[b]).astype(bf16)
        if live < KEY_TILE:
            p_b[live:KEY_TILE, bcols] = jnp.zeros((KEY_TILE - live, brows), bf16)
    m0 = jnp.concatenate(m0, axis=1)
    mx0 = qk(0, s_a)
    acc0 = tuple(jnp.zeros((HEAD_DIM + ONES_ROWS, grows), f32) for _ in parts)
    one = jnp.ones_like(m0)

    def body(i, carry):
        m, acc, alpha_prev, w_prev, kt_prev, mx = carry
        first, second = 2 * i, 2 * i + 1
        w_second = jnp.where(second < n, 1.0, 0.0)
        kt_second = jnp.minimum(second, last)
        mx_b = qk(kt_second, s_b)
        acc = accumulate(acc, alpha_prev, w_prev, pv(kt_prev, p_b))
        m, alpha = softmax(m, mx, s_a, p_a)
        mx = qk(jnp.minimum(second + 1, last), s_a)
        acc = accumulate(acc, alpha, 1.0, pv(first, p_a))
        m, alpha = softmax(m, mx_b, s_b, p_b)
        return m, acc, alpha, w_second, kt_second, mx

    init = (m0, acc0, one, jnp.float32(1.0), n, mx0)
    _, acc, alpha_prev, w_prev, kt_prev, _ = lax.fori_loop(0, (n + 1) // 2, body, init)
    acc = accumulate(acc, alpha_prev, w_prev, pv(kt_prev, p_b))
    o_slc = [a[0:HEAD_DIM] * (1.0 / a[HEAD_DIM:HEAD_DIM + 1]) for a in acc]

    for i, (b, g) in enumerate(parts):
        for r in range(R):
            h = g * R + r
            cols = slice(r * BLOCK, (r + 1) * BLOCK)
            gate_row = lambda branch: gate_ref[0, branch * NSA_HEADS + h:branch * NSA_HEADS + h + 1, qcols(b)]
            o_t = (gate_row(0) * o_cmp[b][g][:, cols] + gate_row(1) * o_slc[i][:, cols]
                   + gate_row(2) * o_win[b][g][:, cols])
            dims = slice(h * HEAD_DIM, (h + 1) * HEAD_DIM)
            o_ref[0, dims, qcols(b)] = (o_t * sza_ref[0, dims, qcols(b)].astype(f32)).astype(o_ref.dtype)


def _nsa_attn(qa, kcp, vcp, onehot, ks, vst, kw, vwt, gate, sza, batch):
    seq = qa.shape[2]
    ncmp = kcp.shape[1]
    G = NSA_KV_HEADS
    rows = STEP_BLOCKS * NSA_HEADS * BLOCK
    t_block = lambda wd: pl.BlockSpec((1, wd, KEY_TILE), lambda b, j: (b, 0, j))
    return pl.pallas_call(
        _nsa_attn_kernel,
        grid=(batch, seq // KEY_TILE),
        in_specs=[t_block(W_NSA),
                  pl.BlockSpec((G, ncmp, HEAD_DIM), lambda b, j: (b, 0, 0)),
                  pl.BlockSpec((G, ncmp, HEAD_DIM), lambda b, j: (b, 0, 0)),
                  pl.BlockSpec((seq, LANES), lambda b, j: (0, 0)),
                  pl.BlockSpec((seq, W_KV), lambda b, j: (b, 0)),
                  pl.BlockSpec((1, W_KV, seq), lambda b, j: (b, 0, 0)),
                  pl.BlockSpec((seq, W_KV), lambda b, j: (b, 0)),
                  pl.BlockSpec((1, W_KV, seq), lambda b, j: (b, 0, 0)),
                  t_block(LANES),
                  t_block(W_NSA)],
        out_specs=t_block(W_NSA),
        out_shape=jax.ShapeDtypeStruct((batch, W_NSA, seq), jnp.bfloat16),
        scratch_shapes=[pltpu.VMEM((KEY_TILE, rows), jnp.float32),
                        pltpu.VMEM((KEY_TILE, rows), jnp.float32),
                        pltpu.VMEM((KEY_TILE, rows), jnp.bfloat16),
                        pltpu.VMEM((KEY_TILE, rows), jnp.bfloat16)],
        compiler_params=pltpu.CompilerParams(dimension_semantics=("arbitrary", "arbitrary"),
                                             vmem_limit_bytes=VMEM_LIMIT),
        name="nsa_attn",
    )(qa, kcp, vcp, onehot, ks, vst, kw, vwt, gate, sza)


DIL_MAX = max(d for _, d in DIL_PATTERNS)
DIL_SUPER = BLOCK * DIL_MAX
DIL_UNITS = DIL_SUPER // BLOCK
HEAD_PAIR = 2 * HEAD_DIM
MIX_ROWS = 256


def _dil_mix_kernel(q_ref, kp_ref, kc_ref, vp_ref, vc_ref, z_ref, o_ref,
                    qf, kf, vf, num_scr, den_scr, max_scr, bias_scr):
    f32, bf16 = jnp.float32, jnp.bfloat16
    sb = pl.program_id(1)
    qf[...] = q_ref[...].astype(f32)
    kf[0:DIL_SUPER] = kp_ref[...].astype(f32)
    kf[DIL_SUPER:2 * DIL_SUPER] = kc_ref[...].astype(f32)
    vf[0:DIL_SUPER] = vp_ref[...].astype(f32)
    vf[DIL_SUPER:2 * DIL_SUPER] = vc_ref[...].astype(f32)

    row = lax.broadcasted_iota(jnp.int32, (2 * BLOCK, 2 * BLOCK), 0)
    col = lax.broadcasted_iota(jnp.int32, (2 * BLOCK, 2 * BLOCK), 1)
    dist = BLOCK + (row & (BLOCK - 1)) - col
    band = (dist >= 0) & (dist <= BLOCK)
    bias_scr[0] = jnp.where(band, 0.0, NEG_INF)
    bias_scr[1] = jnp.where(band & (col >= BLOCK), 0.0, NEG_INF)
    first_head = lax.broadcasted_iota(jnp.int32, (BLOCK, HEAD_PAIR), 1) < HEAD_DIM
    ones = jnp.ones((2 * BLOCK, HEAD_PAIR), bf16)

    for pat, (window, dil) in enumerate(DIL_PATTERNS):
        shift = dil.bit_length() - 1

        def unit(u, pat=pat, dil=dil, shift=shift):
            cls = u & (dil - 1)
            blk = u >> shift
            q_start = cls + blk * (BLOCK * dil)
            k_start = DIL_SUPER + q_start - BLOCK * dil
            q2 = qf[pl.ds(q_start, BLOCK, stride=dil), :]
            k2 = kf[pl.ds(k_start, 2 * BLOCK, stride=dil), :]
            v2 = vf[pl.ds(k_start, 2 * BLOCK, stride=dil), :]
            qm = jnp.concatenate([jnp.where(first_head, q2, 0.0),
                                  jnp.where(first_head, 0.0, q2)], axis=0).astype(bf16)
            s = _dot_nt(qm, k2.astype(bf16))
            no_prev = jnp.where((sb == 0) & (blk == 0), 1, 0)
            s = s + bias_scr[no_prev]
            m = jnp.max(s, axis=1, keepdims=True)
            e = jnp.exp2(s - m).astype(bf16)
            r = _dot(e, jnp.concatenate([v2.astype(bf16), ones], axis=1))
            mb = jnp.broadcast_to(m, (2 * BLOCK, HEAD_PAIR))
            out_rows = pl.ds(q_start, BLOCK, stride=dil)
            num_scr[pat, out_rows, :] = jnp.where(first_head, r[0:BLOCK, 0:HEAD_PAIR], r[BLOCK:, 0:HEAD_PAIR])
            den_scr[pat, out_rows, :] = jnp.where(first_head, r[0:BLOCK, HEAD_PAIR:], r[BLOCK:, HEAD_PAIR:])
            max_scr[pat, out_rows, :] = jnp.where(first_head, mb[0:BLOCK], mb[BLOCK:])

        for u in range(DIL_UNITS):
            unit(u)

    def mix(ci, carry):
        rows = pl.ds(pl.multiple_of(ci * MIX_ROWS, MIX_ROWS), MIX_ROWS)
        ms = [max_scr[p, rows, :] for p in range(len(DIL_PATTERNS))]
        mx = jnp.maximum(jnp.maximum(ms[0], ms[1]), ms[2])
        cs = [jnp.exp2(m - mx) for m in ms]
        num = cs[0] * num_scr[0, rows, :] + cs[1] * num_scr[1, rows, :] + cs[2] * num_scr[2, rows, :]
        den = cs[0] * den_scr[0, rows, :] + cs[1] * den_scr[1, rows, :] + cs[2] * den_scr[2, rows, :]
        o_ref[rows, :] = (num / den * z_ref[rows, :].astype(f32)).astype(o_ref.dtype)
        return carry

    lax.fori_loop(0, DIL_SUPER // MIX_ROWS, mix, 0)


def _dil_mix(qb, kb, vb, szb, batch):
    rows = qb.shape[0]
    nsb = rows // batch // DIL_SUPER
    cur = pl.BlockSpec((DIL_SUPER, HEAD_PAIR), lambda b, s, h: (b * nsb + s, h))
    prev = pl.BlockSpec((DIL_SUPER, HEAD_PAIR), lambda b, s, h: (b * nsb + jnp.maximum(s - 1, 0), h))
    f32 = jnp.float32
    return pl.pallas_call(
        _dil_mix_kernel,
        grid=(batch, nsb, W_DIL // HEAD_PAIR),
        in_specs=[cur, prev, cur, prev, cur, cur],
        out_specs=cur,
        out_shape=jax.ShapeDtypeStruct((rows, W_DIL), jnp.bfloat16),
        scratch_shapes=[pltpu.VMEM((DIL_SUPER, HEAD_PAIR), f32),
                        pltpu.VMEM((2 * DIL_SUPER, HEAD_PAIR), f32),
                        pltpu.VMEM((2 * DIL_SUPER, HEAD_PAIR), f32),
                        pltpu.VMEM((len(DIL_PATTERNS), DIL_SUPER, HEAD_PAIR), f32),
                        pltpu.VMEM((len(DIL_PATTERNS), DIL_SUPER, HEAD_PAIR), f32),
                        pltpu.VMEM((len(DIL_PATTERNS), DIL_SUPER, HEAD_PAIR), f32),
                        pltpu.VMEM((2, 2 * BLOCK, 2 * BLOCK), f32)],
        compiler_params=pltpu.CompilerParams(
            dimension_semantics=("arbitrary", "arbitrary", "arbitrary"),
            vmem_limit_bytes=VMEM_LIMIT),
        name="dil_mix",
    )(qb, kb, kb, vb, vb, szb)


def _out_proj_kernel(x_ref, ma_ref, mb_ref, w_ref, g_ref, out_ref):
    y = lax.dot_general(ma_ref[0], w_ref[0:W_NSA, :], _TN, preferred_element_type=jnp.float32)
    y = y + _dot(mb_ref[...], w_ref[W_NSA:W_NSA + W_DIL, :])
    ms = jnp.mean(y * y, axis=-1, keepdims=True)
    out_ref[...] = x_ref[...] + y * lax.rsqrt(ms + RMS_EPS) * g_ref[...]


def _out_proj(x2, mixed_a_t, mixed_b, w, g):
    rows, d_model = x2.shape
    tm = OUT_ROWS
    per_batch = mixed_a_t.shape[2] // tm
    row_spec = lambda wd: pl.BlockSpec((tm, wd), lambda i: (i, 0))
    return pl.pallas_call(
        _out_proj_kernel,
        grid=(rows // tm,),
        in_specs=[row_spec(d_model),
                  pl.BlockSpec((1, W_NSA, tm), lambda i: (i // per_batch, 0, i % per_batch)),
                  row_spec(W_DIL),
                  pl.BlockSpec(w.shape, lambda i: (0, 0)),
                  pl.BlockSpec((1, d_model), lambda i: (0, 0))],
        out_specs=row_spec(d_model),
        out_shape=jax.ShapeDtypeStruct((rows, d_model), jnp.float32),
        compiler_params=pltpu.CompilerParams(dimension_semantics=("arbitrary",),
                                             vmem_limit_bytes=VMEM_LIMIT),
        name="out_proj",
    )(x2, mixed_a_t, mixed_b, w, g)


def _rope_tables(positions):
    inv = 1.0 / (ROPE_THETA ** (jnp.arange(0, ROPE_DIMS, 2, dtype=jnp.float32) / ROPE_DIMS))
    ang = positions.astype(jnp.float32).reshape(-1)[:, None] * inv
    cos, sin = jnp.cos(ang), jnp.sin(ang)
    half = ROPE_DIMS // 2
    k = jnp.arange(LANES) % HEAD_DIM
    freq = jnp.arange(half)[:, None]
    first = (k[None, :] == freq).astype(jnp.float32)
    second = (k[None, :] == freq + half).astype(jnp.float32)
    spread = lambda t, m: jnp.dot(t, m, precision=lax.Precision.HIGHEST)
    unrotated = (k >= ROPE_DIMS).astype(jnp.float32)[None, :]
    return spread(cos, first + second) + unrotated, spread(-sin, first), spread(sin, second)


def _in_proj_weights(w_in):
    scale = HEAD_DIM ** -0.5 * LOG2_E
    bf16 = jnp.bfloat16
    n_head, n_gate = W_NSA + 6 * W_KV, 3 * NSA_HEADS
    ones = lambda n: jnp.ones((n,), jnp.float32)
    head_scale = jnp.concatenate([scale * ones(W_NSA), ones(6 * W_KV)])
    tail_scale = jnp.concatenate([ones(W_NSA), scale * ones(W_DIL), ones(3 * W_DIL)])
    w_head = (w_in[:, :n_head] * head_scale).astype(bf16)
    w_gate = jnp.pad(w_in[:, n_head:n_head + n_gate], ((0, 0), (0, LANES - n_gate))).astype(bf16)
    w_tail = (w_in[:, n_head + n_gate:] * tail_scale).astype(bf16)
    return w_head, w_gate, w_tail


def kernel(x, positions, pre_norm_g, w_in, cmp_k_pos, cmp_k_w1, cmp_k_w2,
           cmp_v_pos, cmp_v_w1, cmp_v_w2, w_out, post_norm_g):
    B, S, d_model = x.shape
    depth = w_in.shape[0]
    n_sel = S // SEL_BLOCK
    assert S % KEY_TILE == 0 and n_sel == MAX_SEL_BLOCKS and S >= WIN_BLOCKS * BLOCK
    assert S % DIL_SUPER == 0 and all(win // dil == BLOCK for win, dil in DIL_PATTERNS)
    bf16 = jnp.bfloat16

    rope_c, rope_a, rope_b = _rope_tables(positions)
    onehot = (jnp.arange(S)[:, None] // SEL_BLOCK == jnp.arange(LANES)[None, :]).astype(bf16)
    x2 = x.reshape(B * S, d_model)

    for layer in range(depth):
        qa, kvc, ks, kw, vst, vwt, gate, sza, qb_, kb_, vb_, szb = _in_proj(
            x2, pre_norm_g[layer][None, :], _in_proj_weights(w_in[layer]), rope_c, rope_a, rope_b, B)

        kvc = _compress(kvc, *_compress_weights(jnp.stack([cmp_k_pos[layer], cmp_v_pos[layer]]),
                                                jnp.stack([cmp_k_w1[layer], cmp_v_w1[layer]]),
                                                jnp.stack([cmp_k_w2[layer], cmp_v_w2[layer]])), B)
        mixed_a = _nsa_attn(qa, kvc[0], kvc[1], onehot, ks, vst, kw, vwt, gate, sza, B)
        mixed_b = _dil_mix(qb_, kb_, vb_, szb, B)
        x2 = _out_proj(x2, mixed_a, mixed_b, w_out[layer].astype(bf16), post_norm_g[layer][None, :])
    return x2.reshape(B, S, d_model)
```

```python
import jax
import jax.numpy as jnp
from jax import lax
from jax.experimental import pallas as pl
from jax.experimental.pallas import tpu as pltpu

HEAD_DIM = 64
NSA_HEADS = 8
NSA_KV_HEADS = 2
NSA_Q_PER_KV = NSA_HEADS // NSA_KV_HEADS
DIL_HEADS = 8
W_NSA = NSA_HEADS * HEAD_DIM
W_KV = NSA_KV_HEADS * HEAD_DIM
W_DIL = DIL_HEADS * HEAD_DIM
CMP_LEN = 32
CMP_STRIDE = 16
CMP_HIDDEN = 256
SEL_BLOCK = 64
SEL_TOP_N = 16
N_FORCED = 3
SWA_WINDOW = 512
DIL_PATTERNS = ((128, 1), (512, 4), (2048, 16))
BLOCK = 128
ROPE_THETA = 500000.0
ROPE_DIMS = HEAD_DIM // 4
RMS_EPS = 1e-6
NEG_INF = -1e30
FORCE_SCORE = 1e4
LOG2_E = 1.4426950408889634

LANES = 128
VMEM_LIMIT = 56 * 1024 * 1024
MAX_SEL_BLOCKS = LANES
CMP_PER_SEL = SEL_BLOCK // CMP_STRIDE
KEY_TILE = 512
PROJ_ROWS = 512
OUT_ROWS = 1024
BF16_ROWS = 16
ONES_ROWS = BF16_ROWS
SEL_SHIFT = SEL_BLOCK.bit_length() - 1
LANE_SHIFT = LANES.bit_length() - 1

_NT = (((1,), (1,)), ((), ()))
_TN = (((0,), (0,)), ((), ()))


def _dot(a, b):
    return jnp.dot(a, b, preferred_element_type=jnp.float32)


def _dot_nt(a, b):
    return lax.dot_general(a, b, _NT, preferred_element_type=jnp.float32)


def _sigmoid(x):
    return 1.0 / (1.0 + jnp.exp(-x))


def _rope(x, c, a, b):
    width = x.shape[1]
    reps = width // LANES
    ct = jnp.tile(c, (1, reps))
    at = jnp.tile(a, (1, reps))
    bt = jnp.tile(b, (1, reps))
    half = ROPE_DIMS // 2
    return x * ct + pltpu.roll(x, width - half, 1) * at + pltpu.roll(x, half, 1) * bt


def _in_proj_kernel(x_ref, g_ref, w_head_ref, w_gate_ref, w_tail_ref, c_ref, a_ref, b_ref,
                    qa_ref, kvc_ref, ks_ref, kw_ref, vst_ref, vwt_ref, gate_ref, sza_ref,
                    qb_ref, kb_ref, vb_ref, szb_ref):
    x = x_ref[...]
    ms = jnp.mean(x * x, axis=-1, keepdims=True)
    h = (x * lax.rsqrt(ms + RMS_EPS) * g_ref[...]).astype(jnp.bfloat16)
    c = c_ref[...]
    a = a_ref[...]
    b = b_ref[...]

    def columns_of(w_ref):
        off = 0

        def take(width):
            nonlocal off
            off += width
            return _dot(h, w_ref[:, off - width:off])
        return take

    head, gates, proj = columns_of(w_head_ref), columns_of(w_gate_ref), columns_of(w_tail_ref)
    qa_ref[0] = _rope(head(W_NSA), c, a, b).T.astype(qa_ref.dtype)
    kva = head(6 * W_KV)
    part = lambda i: kva[:, i * W_KV:(i + 1) * W_KV]
    kvc_ref[...] = jnp.concatenate([_rope(part(0), c, a, b), part(1)], axis=1).astype(kvc_ref.dtype)
    ks_ref[...] = _rope(part(2), c, a, b).astype(ks_ref.dtype)
    kw_ref[...] = _rope(part(4), c, a, b).astype(kw_ref.dtype)
    vst_ref[0] = part(3).T.astype(vst_ref.dtype)
    vwt_ref[0] = part(5).T.astype(vwt_ref.dtype)
    gate_ref[0] = _sigmoid(gates(LANES)).T
    za = proj(W_NSA)
    sza_ref[0] = (za * _sigmoid(za)).T.astype(sza_ref.dtype)
    qb_ref[...] = _rope(proj(W_DIL), c, a, b).astype(qb_ref.dtype)
    kb_ref[...] = _rope(proj(W_DIL), c, a, b).astype(kb_ref.dtype)
    vb_ref[...] = proj(W_DIL).astype(vb_ref.dtype)
    zb = proj(W_DIL)
    szb_ref[...] = (zb * _sigmoid(zb)).astype(szb_ref.dtype)


def _in_proj(x2, g, weights, c, a, b, batch):
    rows, d_model = x2.shape
    tm = PROJ_ROWS
    seq = rows // batch
    per_batch = seq // tm
    bf16 = jnp.bfloat16
    row_spec = lambda wd: pl.BlockSpec((tm, wd), lambda i: (i, 0))
    row_out = lambda wd: (row_spec(wd), jax.ShapeDtypeStruct((rows, wd), bf16))
    t_out = lambda wd, dt=bf16: (pl.BlockSpec((1, wd, tm), lambda i: (i // per_batch, 0, i % per_batch)),
                                 jax.ShapeDtypeStruct((batch, wd, seq), dt))
    outs = [t_out(W_NSA), row_out(2 * W_KV), row_out(W_KV), row_out(W_KV), t_out(W_KV), t_out(W_KV),
            t_out(LANES, jnp.float32), t_out(W_NSA),
            row_out(W_DIL), row_out(W_DIL), row_out(W_DIL), row_out(W_DIL)]
    return pl.pallas_call(
        _in_proj_kernel,
        grid=(rows // tm,),
        in_specs=[row_spec(d_model),
                  pl.BlockSpec((1, d_model), lambda i: (0, 0))]
                 + [pl.BlockSpec(w.shape, lambda i: (0, 0)) for w in weights]
                 + [row_spec(LANES), row_spec(LANES), row_spec(LANES)],
        out_specs=[o[0] for o in outs],
        out_shape=[o[1] for o in outs],
        compiler_params=pltpu.CompilerParams(dimension_semantics=("arbitrary",),
                                             vmem_limit_bytes=VMEM_LIMIT),
        name="in_proj",
    )(x2, g, *weights, c, a, b)


CMP_PAIRS = CMP_STRIDE // 2


def _compress_kernel(x_ref, pos_ref, wt_ref, wb_ref, w2_ref, o_ref, stage):
    f32, bf16 = jnp.float32, jnp.bfloat16
    stage[...] = x_ref[...].astype(f32)
    nj = MAX_SEL_BLOCKS
    hidden = 2 * CMP_HIDDEN

    def offset_rows(l):
        return jnp.concatenate([stage[pl.ds(CMP_STRIDE * m + l, nj, stride=SEL_BLOCK), :]
                                for m in range(CMP_PER_SEL)], axis=0)

    top = jnp.zeros((CMP_PER_SEL * nj, hidden), f32)
    bot = jnp.zeros((CMP_PER_SEL * nj, hidden), f32)
    bias = jnp.zeros((BF16_ROWS, hidden), f32)
    for i in range(CMP_PAIRS):
        x = jnp.concatenate([offset_rows(2 * i), offset_rows(2 * i + 1)], axis=1).astype(bf16)
        top = top + _dot(x, wt_ref[0, i])
        bot = bot + _dot(x, wb_ref[0, i])
        p_top = jnp.concatenate([pos_ref[0, 2 * i], pos_ref[0, 2 * i + 1]], axis=1)
        p_bot = jnp.concatenate([pos_ref[0, CMP_STRIDE + 2 * i], pos_ref[0, CMP_STRIDE + 2 * i + 1]], axis=1)
        bias = bias + _dot(p_top, wt_ref[0, i]) + _dot(p_bot, wb_ref[0, i])
    nxt = jnp.concatenate([bot[nj:], pltpu.roll(bot[0:nj], nj - 1, 0)], axis=0)
    hid = top + nxt + bias[0:1, :]
    act = (hid * _sigmoid(hid)).astype(bf16)
    for g in range(NSA_KV_HEADS):
        o_ref[0, g] = _dot(act[:, g * CMP_HIDDEN:(g + 1) * CMP_HIDDEN], w2_ref[0]).astype(o_ref.dtype)


def _compress(kvc, pos, wt, wb, w2, batch):
    seq = kvc.shape[0] // batch
    G = NSA_KV_HEADS
    ncmp = CMP_PER_SEL * MAX_SEL_BLOCKS
    whole = lambda arr: pl.BlockSpec((1,) + arr.shape[1:], lambda s, b: (s,) + (0,) * (arr.ndim - 1))
    return pl.pallas_call(
        _compress_kernel,
        grid=(2, batch),
        in_specs=[pl.BlockSpec((seq, W_KV), lambda s, b: (b, s)), whole(pos), whole(wt), whole(wb), whole(w2)],
        out_specs=pl.BlockSpec((1, G, ncmp, HEAD_DIM), lambda s, b: (s, b, 0, 0)),
        out_shape=jax.ShapeDtypeStruct((2, batch * G, ncmp, HEAD_DIM), jnp.bfloat16),
        scratch_shapes=[pltpu.VMEM((seq, W_KV), jnp.float32)],
        compiler_params=pltpu.CompilerParams(dimension_semantics=("arbitrary", "arbitrary"),
                                             vmem_limit_bytes=VMEM_LIMIT),
        name="compress",
    )(kvc, pos, wt, wb, w2)


def _compress_weights(pos, w1, w2):
    bf16 = jnp.bfloat16
    pos2 = jnp.broadcast_to(jnp.tile(pos, (1, 1, 2))[:, :, None, :], (2, CMP_LEN, BF16_ROWS, W_KV)).astype(bf16)
    w = w1.astype(bf16).reshape(2, 2, CMP_STRIDE, HEAD_DIM, CMP_HIDDEN)
    lead = ((0, 0),) * 3
    bd = (jnp.pad(w, lead + ((0, HEAD_DIM), (0, CMP_HIDDEN)))
          + jnp.pad(w, lead + ((HEAD_DIM, 0), (CMP_HIDDEN, 0))))
    bd = bd.reshape(2, 2, CMP_PAIRS, 2 * W_KV, 2 * CMP_HIDDEN)
    return pos2, bd[:, 0], bd[:, 1], w2.astype(bf16)


WIN_BLOCKS = -(-(SWA_WINDOW - 1) // BLOCK) + 1


STEP_BLOCKS = KEY_TILE // BLOCK


def _nsa_attn_kernel(q_ref, kc_ref, vc_ref, et_ref, ks_ref, vst_ref, kw_ref, vwt_ref,
                     gate_ref, sza_ref, o_ref, s_a, s_b, p_a, p_b):
    f32, bf16 = jnp.float32, jnp.bfloat16
    n = pl.program_id(1)
    G, R = NSA_KV_HEADS, NSA_Q_PER_KV
    grows = R * BLOCK
    brows = G * grows
    rows = STEP_BLOCKS * brows
    gcols = lambda g: slice(g * grows, (g + 1) * grows)
    bgcols = lambda blk, g: slice(blk * brows + g * grows, blk * brows + (g + 1) * grows)
    qcols = lambda blk: slice(blk * BLOCK, (blk + 1) * BLOCK)
    head_bias = lambda ok, reps: jnp.tile(jnp.where(ok, 0.0, NEG_INF), (1, reps))
    zero_t = jnp.zeros((HEAD_DIM, BLOCK), bf16)
    ncmp = kc_ref.shape[1]
    pos = lax.broadcasted_iota(jnp.int32, (ncmp, BLOCK), 0)
    cmp_end = (pos & (LANES - 1)) * SEL_BLOCK + (pos >> LANE_SHIFT) * CMP_STRIDE + (CMP_LEN - 1)
    blk_id = lax.broadcasted_iota(jnp.int32, (LANES, BLOCK), 0)
    blk_f = blk_id.astype(f32)
    span = WIN_BLOCKS * BLOCK
    ones_win = jnp.ones((ONES_ROWS, span), bf16)

    q_kv_t, t_1, o_cmp, o_win, bias = [], [], [], [], []
    for b in range(STEP_BLOCKS):
        qb = n * STEP_BLOCKS + b
        heads_t = [q_ref[0, h * HEAD_DIM:(h + 1) * HEAD_DIM, qcols(b)] for h in range(NSA_HEADS)]
        q_cmp_t = jnp.concatenate(heads_t, axis=1)
        q_kv = jnp.concatenate([jnp.concatenate([qh, zero_t] if h < R else [zero_t, qh], axis=0)
                                for h, qh in enumerate(heads_t)], axis=1)
        q_kv_t.append(q_kv)
        t_b = qb * BLOCK + lax.broadcasted_iota(jnp.int32, (1, BLOCK), 1)
        t_1.append(t_b)

        cmp_bias = head_bias(cmp_end <= t_b, R)
        seen = jnp.tile(jnp.where(t_b >= CMP_LEN - 1, 1.0, 0.0), (1, R))
        o_cmp_b, imp_sel = [], []
        for g in range(G):
            st = _dot(kc_ref[g], q_cmp_t[:, gcols(g)]) + cmp_bias
            mx = jnp.max(st, axis=0, keepdims=True)
            e = jnp.exp2(st - mx)
            den = jnp.maximum(jnp.sum(e, axis=0, keepdims=True), 1e-30)
            p = e * (seen / den)
            o_cmp_b.append(lax.dot_general(vc_ref[g], p.astype(bf16), _TN,
                                           preferred_element_type=f32))
            imp = p[:, 0:BLOCK]
            for r in range(1, R):
                imp = imp + p[:, r * BLOCK:(r + 1) * BLOCK]
            q4 = [imp[i * LANES:(i + 1) * LANES] for i in range(CMP_PER_SEL)]
            prev_last = jnp.where(blk_id == 0, 0.0, pltpu.roll(q4[3], 1, 0))
            imp_sel.append(prev_last + 2.0 * (q4[0] + q4[1] + q4[2]) + q4[3])
        o_cmp.append(o_cmp_b)

        first_blk = jnp.maximum(qb - (WIN_BLOCKS - 1), 0)
        win = pl.ds(pl.multiple_of(first_blk * BLOCK, BLOCK), span)
        dist = t_b - (first_blk * BLOCK + lax.broadcasted_iota(jnp.int32, (span, BLOCK), 0))
        st = _dot(kw_ref[win, :], q_kv) + head_bias((dist >= 0) & (dist <= SWA_WINDOW - 1), G * R)
        mw = jnp.max(st, axis=0, keepdims=True)
        e = jnp.exp2(st - mw).astype(bf16)
        o_win_b = []
        for g in range(G):
            r = _dot(jnp.concatenate([vwt_ref[0, g * HEAD_DIM:(g + 1) * HEAD_DIM, win], ones_win], axis=0),
                     e[:, gcols(g)])
            o_win_b.append(r[0:HEAD_DIM] * (1.0 / r[HEAD_DIM:HEAD_DIM + 1]))
        o_win.append(o_win_b)

        cur = (qb * BLOCK + lax.broadcasted_iota(jnp.int32, (LANES, BLOCK), 1)) >> SEL_SHIFT
        forced = (blk_id == 0) | (blk_id == cur) | (blk_id == cur - 1)
        for g in range(G):
            sc = jnp.where(forced, -2.0, jnp.where(blk_id <= cur, imp_sel[g], -1.0))
            for _ in range(SEL_TOP_N - N_FORCED):
                best = jnp.max(sc, axis=0, keepdims=True)
                first = jnp.min(jnp.where(sc == best, blk_f, float(LANES)), axis=0, keepdims=True)
                sc = jnp.where(blk_f == first, -2.0, sc)
            taken = (sc == -2.0) & (blk_id <= cur)
            bias.append(jnp.tile(jnp.where(taken, 0.0, NEG_INF).astype(bf16), (1, R)))

    qa_t = jnp.concatenate([jnp.concatenate(bias, axis=1), jnp.concatenate(q_kv_t, axis=1)], axis=0)
    last = jnp.maximum(n - 1, 0)
    ones = jnp.ones((ONES_ROWS, KEY_TILE), bf16)
    parts = [(b, g) for b in range(STEP_BLOCKS) for g in range(G)]

    def keys_aug(kt):
        tile = pl.ds(pl.multiple_of(kt * KEY_TILE, KEY_TILE), KEY_TILE)
        return jnp.concatenate([et_ref[tile, :], ks_ref[tile, :]], axis=1)

    def qk(kt, s_ref):
        st = _dot(keys_aug(kt), qa_t)
        s_ref[...] = st
        return jnp.max(st, axis=0, keepdims=True)

    def pv(kt, p_ref):
        tile = pl.ds(pl.multiple_of(kt * KEY_TILE, KEY_TILE), KEY_TILE)
        return tuple(_dot(jnp.concatenate([vst_ref[0, g * HEAD_DIM:(g + 1) * HEAD_DIM, tile], ones], axis=0),
                          p_ref[:, bgcols(b, g)]) for b, g in parts)

    def softmax(m_old, mx, s_ref, p_ref):
        m_new = jnp.maximum(m_old, mx)
        p_ref[...] = jnp.exp2(s_ref[...] - m_new).astype(bf16)
        return m_new, jnp.exp2(m_old - m_new)

    def accumulate(acc, alpha, weight, contrib):
        return tuple(alpha[:, bgcols(b, g)] * acc[i] + weight * contrib[i] for i, (b, g) in enumerate(parts))

    own = keys_aug(n)
    key = n * KEY_TILE + lax.broadcasted_iota(jnp.int32, (KEY_TILE, BLOCK), 0)
    m0 = []
    for b in range(STEP_BLOCKS):
        live = (b + 1) * BLOCK
        bcols = slice(b * brows, (b + 1) * brows)
        st = _dot(own[0:live], qa_t[:, bcols]) + head_bias(key[0:live] <= t_1[b], G * R)
        m0.append(jnp.max(st, axis=0, keepdims=True))
        p_b[0:live, bcols] = jnp.exp2(st - m0[b]).astype(bf16)
        if live < KEY_TILE:
            p_b[live:KEY_TILE, bcols] = jnp.zeros((KEY_TILE - live, brows), bf16)
    m0 = jnp.concatenate(m0, axis=1)
    mx0 = qk(0, s_a)
    acc0 = tuple(jnp.zeros((HEAD_DIM + ONES_ROWS, grows), f32) for _ in parts)
    one = jnp.ones_like(m0)

    def step(kt_next, s_next, kt_prev, p_prev, w_prev, alpha_prev, s_cur, p_cur, m, mx, acc):
        keys = keys_aug(kt_next)
        tile = pl.ds(pl.multiple_of(kt_prev * KEY_TILE, KEY_TILE), KEY_TILE)
        vt = [jnp.concatenate([vst_ref[0, g * HEAD_DIM:(g + 1) * HEAD_DIM, tile], ones], axis=0) for g in range(G)]
        m_new = jnp.maximum(m, mx)
        alpha = jnp.exp2(m - m_new)
        mx_next, acc_new = [], []
        for i, (b, g) in enumerate(parts):
            cols = bgcols(b, g)
            st = _dot(keys, qa_t[:, cols])
            s_next[:, cols] = st
            mx_next.append(jnp.max(st, axis=0, keepdims=True))
            acc_new.append(alpha_prev[:, cols] * acc[i] + w_prev * _dot(vt[g], p_prev[:, cols]))
            p_cur[:, cols] = jnp.exp2(s_cur[:, cols] - m_new[:, cols]).astype(bf16)
        return m_new, alpha, jnp.concatenate(mx_next, axis=1), tuple(acc_new)

    def body(i, carry):
        m, acc, alpha_prev, w_prev, kt_prev, mx = carry
        first, second = 2 * i, 2 * i + 1
        w_second = jnp.where(second < n, 1.0, 0.0)
        kt_second = jnp.minimum(second, last)
        m, alpha, mx_b, acc = step(kt_second, s_b, kt_prev, p_b, w_prev, alpha_prev, s_a, p_a, m, mx, acc)
        m, alpha, mx, acc = step(jnp.minimum(second + 1, last), s_a, first, p_a, 1.0, alpha, s_b, p_b, m, mx_b, acc)
        return m, acc, alpha, w_second, kt_second, mx

    init = (m0, acc0, one, jnp.float32(1.0), n, mx0)
    _, acc, alpha_prev, w_prev, kt_prev, _ = lax.fori_loop(0, (n + 1) // 2, body, init)
    acc = accumulate(acc, alpha_prev, w_prev, pv(kt_prev, p_b))
    o_slc = [a[0:HEAD_DIM] * (1.0 / a[HEAD_DIM:HEAD_DIM + 1]) for a in acc]

    for i, (b, g) in enumerate(parts):
        for r in range(R):
            h = g * R + r
            cols = slice(r * BLOCK, (r + 1) * BLOCK)
            gate_row = lambda branch: gate_ref[0, branch * NSA_HEADS + h:branch * NSA_HEADS + h + 1, qcols(b)]
            o_t = (gate_row(0) * o_cmp[b][g][:, cols] + gate_row(1) * o_slc[i][:, cols]
                   + gate_row(2) * o_win[b][g][:, cols])
            dims = slice(h * HEAD_DIM, (h + 1) * HEAD_DIM)
            o_ref[0, dims, qcols(b)] = (o_t * sza_ref[0, dims, qcols(b)].astype(f32)).astype(o_ref.dtype)


def _nsa_attn(qa, kcp, vcp, onehot, ks, vst, kw, vwt, gate, sza, batch):
    seq = qa.shape[2]
    ncmp = kcp.shape[1]
    G = NSA_KV_HEADS
    rows = STEP_BLOCKS * NSA_HEADS * BLOCK
    t_block = lambda wd: pl.BlockSpec((1, wd, KEY_TILE), lambda b, j: (b, 0, j))
    return pl.pallas_call(
        _nsa_attn_kernel,
        grid=(batch, seq // KEY_TILE),
        in_specs=[t_block(W_NSA),
                  pl.BlockSpec((G, ncmp, HEAD_DIM), lambda b, j: (b, 0, 0)),
                  pl.BlockSpec((G, ncmp, HEAD_DIM), lambda b, j: (b, 0, 0)),
                  pl.BlockSpec((seq, LANES), lambda b, j: (0, 0)),
                  pl.BlockSpec((seq, W_KV), lambda b, j: (b, 0)),
                  pl.BlockSpec((1, W_KV, seq), lambda b, j: (b, 0, 0)),
                  pl.BlockSpec((seq, W_KV), lambda b, j: (b, 0)),
                  pl.BlockSpec((1, W_KV, seq), lambda b, j: (b, 0, 0)),
                  t_block(LANES),
                  t_block(W_NSA)],
        out_specs=t_block(W_NSA),
        out_shape=jax.ShapeDtypeStruct((batch, W_NSA, seq), jnp.bfloat16),
        scratch_shapes=[pltpu.VMEM((KEY_TILE, rows), jnp.float32),
                        pltpu.VMEM((KEY_TILE, rows), jnp.float32),
                        pltpu.VMEM((KEY_TILE, rows), jnp.bfloat16),
                        pltpu.VMEM((KEY_TILE, rows), jnp.bfloat16)],
        compiler_params=pltpu.CompilerParams(dimension_semantics=("arbitrary", "arbitrary"),
                                             vmem_limit_bytes=VMEM_LIMIT),
        name="nsa_attn",
    )(qa, kcp, vcp, onehot, ks, vst, kw, vwt, gate, sza)


DIL_MAX = max(d for _, d in DIL_PATTERNS)
DIL_SUPER = BLOCK * DIL_MAX
DIL_UNITS = DIL_SUPER // BLOCK
HEAD_PAIR = 2 * HEAD_DIM
MIX_ROWS = 256


def _dil_mix_kernel(q_ref, kp_ref, kc_ref, vp_ref, vc_ref, z_ref, o_ref,
                    qf, kf, vf, num_scr, den_scr, max_scr, bias_scr):
    f32, bf16 = jnp.float32, jnp.bfloat16
    sb = pl.program_id(1)
    qf[...] = q_ref[...].astype(f32)
    kf[0:DIL_SUPER] = kp_ref[...].astype(f32)
    kf[DIL_SUPER:2 * DIL_SUPER] = kc_ref[...].astype(f32)
    vf[0:DIL_SUPER] = vp_ref[...].astype(f32)
    vf[DIL_SUPER:2 * DIL_SUPER] = vc_ref[...].astype(f32)

    row = lax.broadcasted_iota(jnp.int32, (2 * BLOCK, 2 * BLOCK), 0)
    col = lax.broadcasted_iota(jnp.int32, (2 * BLOCK, 2 * BLOCK), 1)
    dist = BLOCK + (row & (BLOCK - 1)) - col
    band = (dist >= 0) & (dist <= BLOCK)
    bias_scr[0] = jnp.where(band, 0.0, NEG_INF)
    bias_scr[1] = jnp.where(band & (col >= BLOCK), 0.0, NEG_INF)
    first_head = lax.broadcasted_iota(jnp.int32, (BLOCK, HEAD_PAIR), 1) < HEAD_DIM
    ones = jnp.ones((2 * BLOCK, HEAD_PAIR), bf16)

    for pat, (window, dil) in enumerate(DIL_PATTERNS):
        shift = dil.bit_length() - 1

        def unit(u, pat=pat, dil=dil, shift=shift):
            cls = u & (dil - 1)
            blk = u >> shift
            q_start = cls + blk * (BLOCK * dil)
            k_start = DIL_SUPER + q_start - BLOCK * dil
            q2 = qf[pl.ds(q_start, BLOCK, stride=dil), :]
            k2 = kf[pl.ds(k_start, 2 * BLOCK, stride=dil), :]
            v2 = vf[pl.ds(k_start, 2 * BLOCK, stride=dil), :]
            qm = jnp.concatenate([jnp.where(first_head, q2, 0.0),
                                  jnp.where(first_head, 0.0, q2)], axis=0).astype(bf16)
            s = _dot_nt(qm, k2.astype(bf16))
            no_prev = jnp.where((sb == 0) & (blk == 0), 1, 0)
            s = s + bias_scr[no_prev]
            m = jnp.max(s, axis=1, keepdims=True)
            e = jnp.exp2(s - m).astype(bf16)
            r = _dot(e, jnp.concatenate([v2.astype(bf16), ones], axis=1))
            mb = jnp.broadcast_to(m, (2 * BLOCK, HEAD_PAIR))
            out_rows = pl.ds(q_start, BLOCK, stride=dil)
            num_scr[pat, out_rows, :] = jnp.where(first_head, r[0:BLOCK, 0:HEAD_PAIR], r[BLOCK:, 0:HEAD_PAIR])
            den_scr[pat, out_rows, :] = jnp.where(first_head, r[0:BLOCK, HEAD_PAIR:], r[BLOCK:, HEAD_PAIR:])
            max_scr[pat, out_rows, :] = jnp.where(first_head, mb[0:BLOCK], mb[BLOCK:])

        for u in range(DIL_UNITS):
            unit(u)

    def mix(ci, carry):
        rows = pl.ds(pl.multiple_of(ci * MIX_ROWS, MIX_ROWS), MIX_ROWS)
        ms = [max_scr[p, rows, :] for p in range(len(DIL_PATTERNS))]
        mx = jnp.maximum(jnp.maximum(ms[0], ms[1]), ms[2])
        cs = [jnp.exp2(m - mx) for m in ms]
        num = cs[0] * num_scr[0, rows, :] + cs[1] * num_scr[1, rows, :] + cs[2] * num_scr[2, rows, :]
        den = cs[0] * den_scr[0, rows, :] + cs[1] * den_scr[1, rows, :] + cs[2] * den_scr[2, rows, :]
        o_ref[rows, :] = (num / den * z_ref[rows, :].astype(f32)).astype(o_ref.dtype)
        return carry

    lax.fori_loop(0, DIL_SUPER // MIX_ROWS, mix, 0)


def _dil_mix(qb, kb, vb, szb, batch):
    rows = qb.shape[0]
    nsb = rows // batch // DIL_SUPER
    cur = pl.BlockSpec((DIL_SUPER, HEAD_PAIR), lambda b, s, h: (b * nsb + s, h))
    prev = pl.BlockSpec((DIL_SUPER, HEAD_PAIR), lambda b, s, h: (b * nsb + jnp.maximum(s - 1, 0), h))
    f32 = jnp.float32
    return pl.pallas_call(
        _dil_mix_kernel,
        grid=(batch, nsb, W_DIL // HEAD_PAIR),
        in_specs=[cur, prev, cur, prev, cur, cur],
        out_specs=cur,
        out_shape=jax.ShapeDtypeStruct((rows, W_DIL), jnp.bfloat16),
        scratch_shapes=[pltpu.VMEM((DIL_SUPER, HEAD_PAIR), f32),
                        pltpu.VMEM((2 * DIL_SUPER, HEAD_PAIR), f32),
                        pltpu.VMEM((2 * DIL_SUPER, HEAD_PAIR), f32),
                        pltpu.VMEM((len(DIL_PATTERNS), DIL_SUPER, HEAD_PAIR), f32),
                        pltpu.VMEM((len(DIL_PATTERNS), DIL_SUPER, HEAD_PAIR), f32),
                        pltpu.VMEM((len(DIL_PATTERNS), DIL_SUPER, HEAD_PAIR), f32),
                        pltpu.VMEM((2, 2 * BLOCK, 2 * BLOCK), f32)],
        compiler_params=pltpu.CompilerParams(
            dimension_semantics=("arbitrary", "arbitrary", "arbitrary"),
            vmem_limit_bytes=VMEM_LIMIT),
        name="dil_mix",
    )(qb, kb, kb, vb, vb, szb)


def _out_proj_kernel(x_ref, ma_ref, mb_ref, w_ref, g_ref, out_ref):
    y = lax.dot_general(ma_ref[0], w_ref[0:W_NSA, :], _TN, preferred_element_type=jnp.float32)
    y = y + _dot(mb_ref[...], w_ref[W_NSA:W_NSA + W_DIL, :])
    ms = jnp.mean(y * y, axis=-1, keepdims=True)
    out_ref[...] = x_ref[...] + y * lax.rsqrt(ms + RMS_EPS) * g_ref[...]


def _out_proj(x2, mixed_a_t, mixed_b, w, g):
    rows, d_model = x2.shape
    tm = OUT_ROWS
    per_batch = mixed_a_t.shape[2] // tm
    row_spec = lambda wd: pl.BlockSpec((tm, wd), lambda i: (i, 0))
    return pl.pallas_call(
        _out_proj_kernel,
        grid=(rows // tm,),
        in_specs=[row_spec(d_model),
                  pl.BlockSpec((1, W_NSA, tm), lambda i: (i // per_batch, 0, i % per_batch)),
                  row_spec(W_DIL),
                  pl.BlockSpec(w.shape, lambda i: (0, 0)),
                  pl.BlockSpec((1, d_model), lambda i: (0, 0))],
        out_specs=row_spec(d_model),
        out_shape=jax.ShapeDtypeStruct((rows, d_model), jnp.float32),
        compiler_params=pltpu.CompilerParams(dimension_semantics=("arbitrary",),
                                             vmem_limit_bytes=VMEM_LIMIT),
        name="out_proj",
    )(x2, mixed_a_t, mixed_b, w, g)


def _rope_tables(positions):
    inv = 1.0 / (ROPE_THETA ** (jnp.arange(0, ROPE_DIMS, 2, dtype=jnp.float32) / ROPE_DIMS))
    ang = positions.astype(jnp.float32).reshape(-1)[:, None] * inv
    cos, sin = jnp.cos(ang), jnp.sin(ang)
    half = ROPE_DIMS // 2
    k = jnp.arange(LANES) % HEAD_DIM
    freq = jnp.arange(half)[:, None]
    first = (k[None, :] == freq).astype(jnp.float32)
    second = (k[None, :] == freq + half).astype(jnp.float32)
    spread = lambda t, m: jnp.dot(t, m, precision=lax.Precision.HIGHEST)
    unrotated = (k >= ROPE_DIMS).astype(jnp.float32)[None, :]
    return spread(cos, first + second) + unrotated, spread(-sin, first), spread(sin, second)


def _in_proj_weights(w_in):
    scale = HEAD_DIM ** -0.5 * LOG2_E
    bf16 = jnp.bfloat16
    n_head, n_gate = W_NSA + 6 * W_KV, 3 * NSA_HEADS
    ones = lambda n: jnp.ones((n,), jnp.float32)
    head_scale = jnp.concatenate([scale * ones(W_NSA), ones(6 * W_KV)])
    tail_scale = jnp.concatenate([ones(W_NSA), scale * ones(W_DIL), ones(3 * W_DIL)])
    w_head = (w_in[:, :n_head] * head_scale).astype(bf16)
    w_gate = jnp.pad(w_in[:, n_head:n_head + n_gate], ((0, 0), (0, LANES - n_gate))).astype(bf16)
    w_tail = (w_in[:, n_head + n_gate:] * tail_scale).astype(bf16)
    return w_head, w_gate, w_tail


def kernel(x, positions, pre_norm_g, w_in, cmp_k_pos, cmp_k_w1, cmp_k_w2,
           cmp_v_pos, cmp_v_w1, cmp_v_w2, w_out, post_norm_g):
    B, S, d_model = x.shape
    depth = w_in.shape[0]
    n_sel = S // SEL_BLOCK
    assert S % KEY_TILE == 0 and n_sel == MAX_SEL_BLOCKS and S >= WIN_BLOCKS * BLOCK
    assert S % DIL_SUPER == 0 and all(win // dil == BLOCK for win, dil in DIL_PATTERNS)
    bf16 = jnp.bfloat16

    rope_c, rope_a, rope_b = _rope_tables(positions)
    onehot = (jnp.arange(S)[:, None] // SEL_BLOCK == jnp.arange(LANES)[None, :]).astype(bf16)
    x2 = x.reshape(B * S, d_model)

    for layer in range(depth):
        qa, kvc, ks, kw, vst, vwt, gate, sza, qb_, kb_, vb_, szb = _in_proj(
            x2, pre_norm_g[layer][None, :], _in_proj_weights(w_in[layer]), rope_c, rope_a, rope_b, B)

        kvc = _compress(kvc, *_compress_weights(jnp.stack([cmp_k_pos[layer], cmp_v_pos[layer]]),
                                                jnp.stack([cmp_k_w1[layer], cmp_v_w1[layer]]),
                                                jnp.stack([cmp_k_w2[layer], cmp_v_w2[layer]])), B)
        mixed_a = _nsa_attn(qa, kvc[0], kvc[1], onehot, ks, vst, kw, vwt, gate, sza, B)
        mixed_b = _dil_mix(qb_, kb_, vb_, szb, B)
        x2 = _out_proj(x2, mixed_a, mixed_b, w_out[layer].astype(bf16), post_norm_g[layer][None, :])
    return x2.reshape(B, S, d_model)
```
